```python
import math
import jax, jax.numpy as jnp
from jax import lax
import numpy as np

D_MODEL = 1024
BATCH = 8
SEQ = 2048
DEPTH = 1
DEC_BATCH = 8
DEC_SEQ = 64
PAST_LEN = 4096

CHUNK = 64
Q_BLOCK = 128
H_A = 4
DK_A = 64
DV_A = 2 * DK_A
W_A = H_A * DV_A
H_B = 4
DK_B = 128
DV_B = 128
W_B = H_B * DV_B
QK_A = H_A * 2 * DK_A
QK_B = H_B * DK_B
W_IN = 2 * QK_A + 2 * W_A + 2 * QK_B + 2 * W_B + 2 * D_MODEL
EPS = 1e-6

kernel_name = "hybrid_diffattn_retention_stream_step"


def _split_points():
    sizes = [QK_A, QK_A, W_A, W_A, QK_B, QK_B, W_B, W_B, D_MODEL, D_MODEL]
    pts, acc = [], 0
    for s in sizes[:-1]:
        acc += s
        pts.append(acc)
    return pts


def rmsnorm(x, g=None):
    xf = x.astype(jnp.float32)
    y = xf * lax.rsqrt(jnp.mean(xf * xf, axis=-1, keepdims=True) + EPS)
    if g is not None:
        y = y * g.astype(jnp.float32)
    return y.astype(x.dtype)


def alibi_slopes():
    return 2.0 ** (-8.0 * jnp.arange(1, H_A + 1, dtype=jnp.float32) / H_A)


def retention_log_decay():
    return jnp.log(1.0 - 2.0 ** (-5.0 - jnp.arange(H_B, dtype=jnp.float32)))


def diff_attention(q, k, v, q_pos, k_pos, lam, slopes):
    s = jnp.einsum('bqhcd,bkhcd->bhcqk', q, k).astype(jnp.float32) * (DK_A ** -0.5)
    dist = jnp.abs(q_pos[:, None] - k_pos[None, :]).astype(jnp.float32)
    s = s - slopes[None, :, None, None, None] * dist[None, None, None]
    allowed = (k_pos[None, :] // CHUNK) <= (q_pos[:, None] // CHUNK)
    s = jnp.where(allowed[None, None, None], s, -jnp.inf)
    p = jax.nn.softmax(s, axis=-1)
    a = p[:, :, 0] - lam * p[:, :, 1]
    return jnp.einsum('bhqk,bkhe->bqhe', a.astype(v.dtype), v)


def diff_attn_prompt(q, k, v, lam, slopes):
    B, T = q.shape[0], q.shape[1]
    nb = T // Q_BLOCK
    qb = q.reshape(B, nb, Q_BLOCK, H_A, 2, DK_A).swapaxes(0, 1)
    starts = jnp.arange(nb, dtype=jnp.int32) * Q_BLOCK
    k_pos = jnp.arange(T, dtype=jnp.int32)

    def block(args):
        qi, st = args
        return diff_attention(qi, k, v, st + jnp.arange(Q_BLOCK, dtype=jnp.int32), k_pos, lam, slopes)

    o = lax.map(block, (qb, starts))
    return o.swapaxes(0, 1).reshape(B, T, H_A, DV_A)


def diff_attn_step(q, k_all, v_all, past, lam, slopes):
    n = q.shape[1]
    q_pos = past + jnp.arange(n, dtype=jnp.int32)
    k_pos = jnp.arange(k_all.shape[1], dtype=jnp.int32)
    return diff_attention(q, k_all, v_all, q_pos, k_pos, lam, slopes)


def retention_prompt(q, k, v, log_g):
    B, T = q.shape[0], q.shape[1]
    nc = T // CHUNK
    qc = q.reshape(B, nc, CHUNK, H_B, DK_B)
    kc = k.reshape(B, nc, CHUNK, H_B, DK_B)
    vc = v.reshape(B, nc, CHUNK, H_B, DV_B)
    j = jnp.arange(CHUNK, dtype=jnp.float32)
    d_intra = jnp.exp(jnp.abs(j[:, None] - j[None, :])[None] * log_g[:, None, None])
    s = jnp.einsum('bnthd,bnshd->bnhts', qc, kc) * d_intra
    intra = jnp.einsum('bnhts,bnshe->bnthe', s, vc)
    zeta = jnp.exp((CHUNK - 1 - j)[None] * log_g[:, None])
    u = jnp.einsum('bnshd,bnshe,hs->nbhde', kc, vc, zeta)
    decay_c = jnp.exp(CHUNK * log_g)[None, :, None, None]

    def step(r, u_n):
        return decay_c * r + u_n, r

    r0 = jnp.zeros((B, H_B, DK_B, DV_B), jnp.float32)
    r_final, r_before = lax.scan(step, r0, u)
    xi = jnp.exp((j + 1.0)[None] * log_g[:, None])
    cross = jnp.einsum('bnthd,nbhde,ht->bnthe', qc, r_before, xi)
    o = (intra + cross).reshape(B, T, H_B, DV_B)
    return o.astype(q.dtype), r_final


def retention_step(q, k, v, r, log_g):
    n = q.shape[1]
    j = jnp.arange(n, dtype=jnp.float32)
    d = jnp.exp(jnp.abs(j[:, None] - j[None, :])[None] * log_g[:, None, None])
    s = jnp.einsum('bthd,bshd->bhts', q, k) * d
    intra = jnp.einsum('bhts,bshe->bthe', s, v)
    xi = jnp.exp((j + 1.0)[None] * log_g[:, None])
    rf = r.astype(jnp.float32)
    cross = jnp.einsum('bthd,bhde,ht->bthe', q, rf, xi)
    zeta = jnp.exp((n - 1.0 - j)[None] * log_g[:, None])
    r_new = jnp.exp(n * log_g)[None, :, None, None] * rf + jnp.einsum('bshd,bshe,hs->bhde', k, v, zeta)
    return (intra + cross).astype(q.dtype), r_new.astype(r.dtype)


def layer_inputs(x, norm_g, w_in, b_gate, qn_g, kn_g):
    B, T = x.shape[0], x.shape[1]
    h = rmsnorm(x, norm_g) @ w_in
    qa, ka, va, za, qb, kb, vb, zb, ga, gb = jnp.split(h, _split_points(), axis=-1)
    qa = rmsnorm(qa.reshape(B, T, H_A, 2, DK_A), qn_g)
    ka = rmsnorm(ka.reshape(B, T, H_A, 2, DK_A), kn_g)
    va = va.reshape(B, T, H_A, DV_A)
    qb = qb.reshape(B, T, H_B, DK_B)
    kb = kb.reshape(B, T, H_B, DK_B) * (DK_B ** -0.5)
    vb = vb.reshape(B, T, H_B, DV_B)
    ga = jax.nn.sigmoid(ga + b_gate[0])
    gb = jax.nn.sigmoid(gb + b_gate[1])
    return qa, ka, va, za, qb, kb, vb, zb, ga, gb


def layer_output(x, oa, ob, za, zb, ga, gb, subln_g, lam_init, w_oa, w_ob, w_out):
    B, T = x.shape[0], x.shape[1]
    oa = (rmsnorm(oa, subln_g) * (1.0 - lam_init)).reshape(B, T, W_A) * jax.nn.silu(za)
    ob = rmsnorm(ob).reshape(B, T, W_B) * jax.nn.silu(zb)
    m = ga * (oa @ w_oa) + gb * (ob @ w_ob)
    return x + m @ w_out


def setup_inputs(seed: int = 0) -> dict:
    key = jax.random.key(seed)
    ks = jax.random.split(key, 20)
    f32 = jnp.float32
    nrm = lambda k, shape: jax.random.normal(k, shape, f32)
    return {
        "x_prompt": nrm(ks[0], (BATCH, SEQ, D_MODEL)),
        "x_sample": nrm(ks[1], (DEC_BATCH, DEC_SEQ, D_MODEL)),
        "cache_k_diff": nrm(ks[2], (DEPTH, DEC_BATCH, PAST_LEN, H_A, 2, DK_A)),
        "cache_v_diff": nrm(ks[3], (DEPTH, DEC_BATCH, PAST_LEN, H_A, DV_A)),
        "state_ret": nrm(ks[4], (DEPTH, DEC_BATCH, H_B, DK_B, DV_B)),
        "norm_g": 1.0 + 0.02 * nrm(ks[5], (DEPTH, D_MODEL)),
        "w_in": nrm(ks[6], (DEPTH, D_MODEL, W_IN)) * D_MODEL ** -0.5,
        "b_gate": 0.01 * nrm(ks[7], (DEPTH, 2, D_MODEL)),
        "qn_g": 1.0 + 0.02 * nrm(ks[8], (DEPTH, DK_A)),
        "kn_g": 1.0 + 0.02 * nrm(ks[9], (DEPTH, DK_A)),
        "lam_q1": 0.1 * nrm(ks[10], (DEPTH, DK_A)),
        "lam_k1": 0.1 * nrm(ks[11], (DEPTH, DK_A)),
        "lam_q2": 0.1 * nrm(ks[12], (DEPTH, DK_A)),
        "lam_k2": 0.1 * nrm(ks[13], (DEPTH, DK_A)),
        "subln_g": 1.0 + 0.02 * nrm(ks[14], (DEPTH, DV_A)),
        "w_o_diff": nrm(ks[15], (DEPTH, W_A, D_MODEL)) * W_A ** -0.5,
        "w_o_ret": nrm(ks[16], (DEPTH, W_B, D_MODEL)) * W_B ** -0.5,
        "w_out": nrm(ks[17], (DEPTH, D_MODEL, D_MODEL)) * D_MODEL ** -0.5,
    }


def reference(x_prompt, x_sample, cache_k_diff, cache_v_diff, state_ret, norm_g, w_in, b_gate,
              qn_g, kn_g, lam_q1, lam_k1, lam_q2, lam_k2, subln_g, w_o_diff, w_o_ret, w_out):
    slopes = alibi_slopes()
    log_g = retention_log_decay()
    past = cache_k_diff.shape[2]
    hp, hs = x_prompt, x_sample
    kp_l, vp_l, rp_l, ks_l, vs_l, rs_l = [], [], [], [], [], []
    for l in range(DEPTH):
        lam_init = 0.8 - 0.6 * math.exp(-0.3 * l)
        lam = (jnp.exp(jnp.sum(lam_q1[l] * lam_k1[l]).astype(jnp.float32))
               - jnp.exp(jnp.sum(lam_q2[l] * lam_k2[l]).astype(jnp.float32)) + lam_init)
        qa, ka, va, za, qb, kb, vb, zb, ga, gb = layer_inputs(hp, norm_g[l], w_in[l], b_gate[l], qn_g[l], kn_g[l])
        oa = diff_attn_prompt(qa, ka, va, lam, slopes)
        ob, r_p = retention_prompt(qb, kb, vb, log_g)
        hp = layer_output(hp, oa, ob, za, zb, ga, gb, subln_g[l], lam_init, w_o_diff[l], w_o_ret[l], w_out[l])
        kp_l.append(ka)
        vp_l.append(va)
        rp_l.append(r_p)
        qa, ka, va, za, qb, kb, vb, zb, ga, gb = layer_inputs(hs, norm_g[l], w_in[l], b_gate[l], qn_g[l], kn_g[l])
        k_all = jnp.concatenate([cache_k_diff[l].astype(ka.dtype), ka], axis=1)
        v_all = jnp.concatenate([cache_v_diff[l].astype(va.dtype), va], axis=1)
        oa = diff_attn_step(qa, k_all, v_all, past, lam, slopes)
        ob, r_s = retention_step(qb, kb, vb, state_ret[l], log_g)
        hs = layer_output(hs, oa, ob, za, zb, ga, gb, subln_g[l], lam_init, w_o_diff[l], w_o_ret[l], w_out[l])
        ks_l.append(ka)
        vs_l.append(va)
        rs_l.append(r_s)
    k_prompt = jnp.stack(kp_l)
    v_prompt = jnp.stack(vp_l)
    ret_prompt = jnp.stack(rp_l)
    k_sample = jnp.stack(ks_l)
    v_sample = jnp.stack(vs_l)
    ret_sample = jnp.stack(rs_l)
    return (hp, hs, k_prompt, v_prompt, ret_prompt, k_sample, v_sample, ret_sample)
```

```python
import functools
import math

import jax
import jax.numpy as jnp
from jax import lax
from jax.experimental import pallas as pl
from jax.experimental.pallas import tpu as pltpu

F32 = jnp.float32
BF16 = jnp.bfloat16

CHUNK = 64
H_A = 4
DK_A = 64
DV_A = 2 * DK_A
H_B = 4
DK_B = 128
DV_B = 128
HEAD_W = 128
EPS = 1e-6

ROW_TILE = 256
Q_TILE = 256
SAMPLE_KEY_BLOCK = 512
VMEM_LIMIT = 56 * 1024 * 1024


def _nt_dot(a, b):
    return lax.dot_general(a, b, (((1,), (1,)), ((), ())), preferred_element_type=F32)


def _tn_dot(a, b):
    return lax.dot_general(a, b, (((0,), (0,)), ((), ())), preferred_element_type=F32)


def _dot(a, b):
    return jnp.dot(a, b, preferred_element_type=F32)


def _sigmoid(x):
    return 1.0 / (1.0 + jnp.exp(-x))


def _proj_in_kernel(x_ref, g_ref, w_ref, bg_ref, qg_ref, kg_ref, gm_ref,
                    qa_ref, kaf_ref, kab_ref, vaf_ref, vab_ref, za_ref,
                    qb_ref, kb_ref, vb_ref, zb_ref, ga_ref, gb_ref, *, wa, wb, d_model):
    x = x_ref[...]
    ms = jnp.mean(x * x, axis=-1, keepdims=True)
    xn = ((x * lax.rsqrt(ms + EPS)) * g_ref[...]).astype(BF16)

    def seg(lo, width):
        return _dot(xn, w_ref[:, lo:lo + width])

    def group_norm(h, gain):
        ss = _dot((h * h).astype(BF16), gm_ref[...]) * (1.0 / DK_A)
        return (h * lax.rsqrt(ss + EPS)) * gain

    def silu(h):
        return h * _sigmoid(h)

    off = 0
    qa_ref[...] = (group_norm(seg(off, wa), qg_ref[...]) * (DK_A ** -0.5)).astype(BF16)
    off += wa
    ka = group_norm(seg(off, wa), kg_ref[...])
    kaf_ref[...] = ka
    kab_ref[...] = ka.astype(BF16)
    off += wa
    va = seg(off, wa)
    vaf_ref[...] = va
    vab_ref[...] = va.astype(BF16)
    off += wa
    za_ref[...] = silu(seg(off, wa)).astype(BF16)
    off += wa
    qb_ref[...] = seg(off, wb).astype(BF16)
    off += wb
    kb_ref[...] = (seg(off, wb) * (DK_B ** -0.5)).astype(BF16)
    off += wb
    vb_ref[...] = seg(off, wb).astype(BF16)
    off += wb
    zb_ref[...] = silu(seg(off, wb)).astype(BF16)
    off += wb
    ga_ref[...] = _sigmoid(seg(off, d_model) + bg_ref[0:1, :]).astype(BF16)
    off += d_model
    gb_ref[...] = _sigmoid(seg(off, d_model) + bg_ref[1:2, :]).astype(BF16)


def _proj_in(x2, g, w_bf, bg, qg_t, kg_t, gm):
    n, d = x2.shape
    wa = H_A * HEAD_W
    wb = H_B * HEAD_W
    tm = ROW_TILE
    assert n % tm == 0
    row = lambda w: pl.BlockSpec((tm, w), lambda i: (i, 0))
    full = lambda a: pl.BlockSpec(a.shape, lambda i: (0,) * a.ndim)
    out_shapes = [
        jax.ShapeDtypeStruct((n, wa), BF16),
        jax.ShapeDtypeStruct((n, wa), F32),
        jax.ShapeDtypeStruct((n, wa), BF16),
        jax.ShapeDtypeStruct((n, wa), F32),
        jax.ShapeDtypeStruct((n, wa), BF16),
        jax.ShapeDtypeStruct((n, wa), BF16),
        jax.ShapeDtypeStruct((n, wb), BF16),
        jax.ShapeDtypeStruct((n, wb), BF16),
        jax.ShapeDtypeStruct((n, wb), BF16),
        jax.ShapeDtypeStruct((n, wb), BF16),
        jax.ShapeDtypeStruct((n, d), BF16),
        jax.ShapeDtypeStruct((n, d), BF16),
    ]
    out_specs = [row(s.shape[1]) for s in out_shapes]
    return pl.pallas_call(
        functools.partial(_proj_in_kernel, wa=wa, wb=wb, d_model=d),
        grid=(n // tm,),
        in_specs=[row(d), full(g), full(w_bf), full(bg), full(qg_t), full(kg_t), full(gm)],
        out_specs=out_specs,
        out_shape=out_shapes,
        compiler_params=pltpu.CompilerParams(
            dimension_semantics=("arbitrary",), vmem_limit_bytes=VMEM_LIMIT),
        name="proj_in",
    )(x2, g, w_bf, bg, qg_t, kg_t, gm)


def _split_halves(q):
    lane = lax.broadcasted_iota(jnp.int32, q.shape, 1)
    zero = jnp.zeros_like(q)
    return jnp.concatenate([jnp.where(lane < DK_A, q, zero), jnp.where(lane >= DK_A, q, zero)], axis=0)


def _softmax_first(s, v):
    m = jnp.max(s, axis=-1, keepdims=True)
    p = jnp.exp(s - m)
    l = jnp.sum(p, axis=-1, keepdims=True)
    return m, l, _dot(p.astype(BF16), v)


def _softmax_update(carry, s, v):
    m, l, acc = carry
    m_new = jnp.maximum(m, jnp.max(s, axis=-1, keepdims=True))
    alpha = jnp.exp(m - m_new)
    p = jnp.exp(s - m_new)
    l = alpha * l + jnp.sum(p, axis=-1, keepdims=True)
    acc = alpha * acc + _dot(p.astype(BF16), v)
    return m_new, l, acc


def _diff_combine(l, acc, lam, t):
    o = acc / l
    return o[:t] - lam * o[t:]


def _lane_rms(o):
    return o * lax.rsqrt(jnp.mean(o * o, axis=-1, keepdims=True) + EPS)


def _retention_block(q, k, v, r, dmat, xi, zeta, cdecay):
    s = _nt_dot(q, k) * dmat
    o = _dot(s.astype(BF16), v) + _dot(q, r.astype(BF16)) * xi
    kz = (k.astype(F32) * zeta).astype(BF16)
    r_new = cdecay * r + _tn_dot(kz, v)
    return o, r_new


def _mixer_prompt_kernel(scal_ref, qa_ref, k_ref, v_ref, za_ref, qb_ref, kb_ref, vb_ref, zb_ref,
                         abias_ref, dmat_ref, xi_ref, zeta_ref, sg_ref,
                         oa_ref, ob_ref, rout_ref, r_scr, *, tq):
    h = pl.program_id(1)
    qi = pl.program_id(2)
    lam = scal_ref[0]
    slope = scal_ref[1 + h]
    cdecay = scal_ref[1 + H_A + h]

    qs = _split_halves(qa_ref[0])
    q_start = pl.multiple_of(qi * tq, tq)
    bias = abias_ref[0]
    s = _nt_dot(qs, k_ref[0, pl.ds(q_start, tq), :])
    s = s + jnp.concatenate([bias, bias], axis=0)
    carry = _softmax_first(s, v_ref[0, pl.ds(q_start, tq), :])
    col = lax.broadcasted_iota(jnp.int32, (1, tq), 1)

    def body(kb, carry):
        k_start = pl.multiple_of(kb * tq, tq)
        s = _nt_dot(qs, k_ref[0, pl.ds(k_start, tq), :])
        s = s + (col + (kb - qi) * tq).astype(F32) * slope
        return _softmax_update(carry, s, v_ref[0, pl.ds(k_start, tq), :])

    m, l, acc = lax.fori_loop(0, qi, body, carry)
    oa = _diff_combine(l, acc, lam, tq)
    oa = (_lane_rms(oa) * sg_ref[0:1, :]) * sg_ref[1:2, :]
    oa_ref[0] = (oa * za_ref[0].astype(F32)).astype(BF16)

    @pl.when(qi == 0)
    def _():
        r_scr[...] = jnp.zeros_like(r_scr)

    ob, r_new = _retention_block(qb_ref[0], kb_ref[0], vb_ref[0], r_scr[...],
                                 dmat_ref[0], xi_ref[0], zeta_ref[0], cdecay)
    r_scr[...] = r_new
    rout_ref[0, 0] = r_new
    ob_ref[0] = (_lane_rms(ob) * zb_ref[0].astype(F32)).astype(BF16)


def _mixer_prompt(scal, qa, kab, vab, za, qb, kb, vb, zb, abias, dmat, xi, zeta, sg):
    b, t, _ = qa.shape
    tq = Q_TILE
    assert t % tq == 0 and tq % CHUNK == 0
    tile = pl.BlockSpec((1, tq, HEAD_W), lambda bi, hi, qi: (bi, qi, hi))
    whole = pl.BlockSpec((1, t, HEAD_W), lambda bi, hi, qi: (bi, 0, hi))
    per_head = lambda a: pl.BlockSpec((1,) + a.shape[1:], lambda bi, hi, qi: (hi,) + (0,) * (a.ndim - 1))
    return pl.pallas_call(
        functools.partial(_mixer_prompt_kernel, tq=tq),
        grid=(b, H_A, t // tq),
        in_specs=[pl.BlockSpec(memory_space=pltpu.SMEM),
                  tile, whole, whole, tile, tile, tile, tile, tile,
                  per_head(abias), per_head(dmat), per_head(xi), per_head(zeta),
                  pl.BlockSpec(sg.shape, lambda bi, hi, qi: (0, 0))],
        out_specs=[tile, tile,
                   pl.BlockSpec((1, 1, DK_B, DV_B), lambda bi, hi, qi: (bi, hi, 0, 0))],
        out_shape=[jax.ShapeDtypeStruct((b, t, H_A * HEAD_W), BF16),
                   jax.ShapeDtypeStruct((b, t, H_B * HEAD_W), BF16),
                   jax.ShapeDtypeStruct((b, H_B, DK_B, DV_B), F32)],
        scratch_shapes=[pltpu.VMEM((DK_B, DV_B), F32)],
        compiler_params=pltpu.CompilerParams(
            dimension_semantics=("arbitrary", "arbitrary", "arbitrary"), vmem_limit_bytes=VMEM_LIMIT),
        name="mixer_prompt",
    )(scal, qa, kab, vab, za, qb, kb, vb, zb, abias, dmat, xi, zeta, sg)


def _mixer_sample_kernel(scal_ref, qa_ref, kc_ref, vc_ref, kn_ref, vn_ref, za_ref,
                         qb_ref, kb_ref, vb_ref, zb_ref, r_ref,
                         abias_ref, dmat_ref, xi_ref, zeta_ref, sg_ref,
                         oa_ref, ob_ref, rout_ref, *, n, past, tk):
    h = pl.program_id(1)
    lam = scal_ref[0]
    slope = scal_ref[1 + h]
    cdecay = scal_ref[1 + H_A + h]

    qs = _split_halves(qa_ref[0])
    bias = abias_ref[0]
    s = _nt_dot(qs, kn_ref[0]) + jnp.concatenate([bias, bias], axis=0)
    carry = _softmax_first(s, vn_ref[0])
    col = lax.broadcasted_iota(jnp.int32, (1, tk), 1)

    def body(c, carry):
        k_start = pl.multiple_of(c * tk, tk)
        kk = kc_ref[0, pl.ds(k_start, tk), :].astype(BF16)
        vv = vc_ref[0, pl.ds(k_start, tk), :].astype(BF16)
        s = _nt_dot(qs, kk) + (col + (c * tk - past)).astype(F32) * slope
        return _softmax_update(carry, s, vv)

    m, l, acc = lax.fori_loop(0, past // tk, body, carry)
    oa = _diff_combine(l, acc, lam, n)
    oa = (_lane_rms(oa) * sg_ref[0:1, :]) * sg_ref[1:2, :]
    oa_ref[0] = (oa * za_ref[0].astype(F32)).astype(BF16)

    ob, r_new = _retention_block(qb_ref[0], kb_ref[0], vb_ref[0], r_ref[0, 0],
                                 dmat_ref[0], xi_ref[0], zeta_ref[0], cdecay)
    rout_ref[0, 0] = r_new
    ob_ref[0] = (_lane_rms(ob) * zb_ref[0].astype(F32)).astype(BF16)


def _mixer_sample(scal, qa, kc, vc, kab, vab, za, qb, kb, vb, zb, r0, abias, dmat, xi, zeta, sg):
    b, n, _ = qa.shape
    past = kc.shape[1]
    tk = SAMPLE_KEY_BLOCK
    assert past % tk == 0 and past % CHUNK == 0 and n <= CHUNK
    tile = pl.BlockSpec((1, n, HEAD_W), lambda bi, hi: (bi, 0, hi))
    cache = pl.BlockSpec((1, past, HEAD_W), lambda bi, hi: (bi, 0, hi))
    state = pl.BlockSpec((1, 1, DK_B, DV_B), lambda bi, hi: (bi, hi, 0, 0))
    per_head = lambda a: pl.BlockSpec((1,) + a.shape[1:], lambda bi, hi: (hi,) + (0,) * (a.ndim - 1))
    return pl.pallas_call(
        functools.partial(_mixer_sample_kernel, n=n, past=past, tk=tk),
        grid=(b, H_A),
        in_specs=[pl.BlockSpec(memory_space=pltpu.SMEM),
                  tile, cache, cache, tile, tile, tile, tile, tile, tile, tile, state,
                  per_head(abias), per_head(dmat), per_head(xi), per_head(zeta),
                  pl.BlockSpec(sg.shape, lambda bi, hi: (0, 0))],
        out_specs=[tile, tile, state],
        out_shape=[jax.ShapeDtypeStruct((b, n, H_A * HEAD_W), BF16),
                   jax.ShapeDtypeStruct((b, n, H_B * HEAD_W), BF16),
                   jax.ShapeDtypeStruct((b, H_B, DK_B, DV_B), F32)],
        compiler_params=pltpu.CompilerParams(
            dimension_semantics=("arbitrary", "arbitrary"), vmem_limit_bytes=VMEM_LIMIT),
        name="mixer_sample",
    )(scal, qa, kc, vc, kab, vab, za, qb, kb, vb, zb, r0, abias, dmat, xi, zeta, sg)


def _proj_out_kernel(x_ref, oa_ref, ob_ref, ga_ref, gb_ref, woa_ref, wob_ref, wout_ref, y_ref):
    m = (ga_ref[...].astype(F32) * _dot(oa_ref[...], woa_ref[...])
         + gb_ref[...].astype(F32) * _dot(ob_ref[...], wob_ref[...]))
    y_ref[...] = x_ref[...] + _dot(m.astype(BF16), wout_ref[...])


def _proj_out(x2, oa, ob, ga, gb, woa, wob, wout):
    n, d = x2.shape
    tm = ROW_TILE
    row = lambda w: pl.BlockSpec((tm, w), lambda i: (i, 0))
    full = lambda a: pl.BlockSpec(a.shape, lambda i: (0,) * a.ndim)
    return pl.pallas_call(
        _proj_out_kernel,
        grid=(n // tm,),
        in_specs=[row(d), row(oa.shape[1]), row(ob.shape[1]), row(d), row(d),
                  full(woa), full(wob), full(wout)],
        out_specs=row(d),
        out_shape=jax.ShapeDtypeStruct((n, d), F32),
        compiler_params=pltpu.CompilerParams(
            dimension_semantics=("arbitrary",), vmem_limit_bytes=VMEM_LIMIT),
        name="proj_out",
    )(x2, oa, ob, ga, gb, woa, wob, wout)


def _block_tables(t, slopes, log_g):
    i = jnp.arange(t, dtype=jnp.int32)
    allowed = (i[None, :] // CHUNK) <= (i[:, None] // CHUNK)
    dist = jnp.abs(i[:, None] - i[None, :]).astype(F32)
    jf = i.astype(F32)
    abias = -slopes[:, None, None] * dist[None] + slopes[:, None, None] * jf[None, :, None]
    abias = jnp.where(allowed[None], abias, -jnp.inf)
    dmat = jnp.where(allowed[None], jnp.exp(dist[None] * log_g[:, None, None]), 0.0)
    xi = jnp.exp((jf + 1.0)[None] * log_g[:, None])
    zeta = jnp.exp((t - 1.0 - jf)[None] * log_g[:, None])
    bc = lambda a: jnp.broadcast_to(a[:, :, None], a.shape + (HEAD_W,))
    return abias, dmat, bc(xi), bc(zeta)


def _layer(x_prompt, x_sample, ck, cv, r0, norm_g, w_in, b_gate, qn_g, kn_g,
           lam_q1, lam_k1, lam_q2, lam_k2, subln_g, w_oa, w_ob, w_out, lam_init):
    bp, tp, d = x_prompt.shape
    bs, ts, _ = x_sample.shape
    past = ck.shape[1]
    slopes = 2.0 ** (-8.0 * jnp.arange(1, H_A + 1, dtype=F32) / H_A)
    log_g = jnp.log(1.0 - 2.0 ** (-5.0 - jnp.arange(H_B, dtype=F32)))
    lam = (jnp.exp(jnp.sum(lam_q1 * lam_k1).astype(F32))
           - jnp.exp(jnp.sum(lam_q2 * lam_k2).astype(F32)) + lam_init)

    w_bf = w_in.astype(BF16)
    woa_bf, wob_bf, wout_bf = w_oa.astype(BF16), w_ob.astype(BF16), w_out.astype(BF16)
    g = norm_g.reshape(1, d)
    qg_t = jnp.tile(qn_g, 2 * H_A).reshape(1, H_A * HEAD_W)
    kg_t = jnp.tile(kn_g, 2 * H_A).reshape(1, H_A * HEAD_W)
    grp = jnp.arange(H_A * HEAD_W, dtype=jnp.int32) // DK_A
    gm = (grp[:, None] == grp[None, :]).astype(BF16)
    sg = jnp.stack([subln_g, jnp.full((DV_A,), 1.0 - lam_init, F32)])

    def scal(n_block):
        return jnp.concatenate([lam.reshape(1), slopes, jnp.exp(n_block * log_g)]).astype(F32)

    xp2 = x_prompt.reshape(bp * tp, d)
    (qa, kaf, kab, vaf, vab, za, qb, kb, vb, zb, ga, gb) = _proj_in(xp2, g, w_bf, b_gate, qg_t, kg_t, gm)
    r3 = lambda a: a.reshape(bp, tp, a.shape[-1])
    tabs = _block_tables(Q_TILE, slopes, log_g)
    oa, ob, ret_p = _mixer_prompt(scal(float(Q_TILE)), r3(qa), r3(kab), r3(vab), r3(za),
                                  r3(qb), r3(kb), r3(vb), r3(zb), *tabs, sg)
    y_p = _proj_out(xp2, oa.reshape(bp * tp, -1), ob.reshape(bp * tp, -1), ga, gb,
                    woa_bf, wob_bf, wout_bf).reshape(bp, tp, d)
    k_p = kaf.reshape(bp, tp, H_A, 2, DK_A)
    v_p = vaf.reshape(bp, tp, H_A, DV_A)

    xs2 = x_sample.reshape(bs * ts, d)
    (qa, kaf, kab, vaf, vab, za, qb, kb, vb, zb, ga, gb) = _proj_in(xs2, g, w_bf, b_gate, qg_t, kg_t, gm)
    r3 = lambda a: a.reshape(bs, ts, a.shape[-1])
    tabs = _block_tables(ts, slopes, log_g)
    oa, ob, ret_s = _mixer_sample(scal(float(ts)), r3(qa), ck.reshape(bs, past, H_A * HEAD_W),
                                  cv.reshape(bs, past, H_A * HEAD_W), r3(kab), r3(vab), r3(za),
                                  r3(qb), r3(kb), r3(vb), r3(zb), r0, *tabs, sg)
    y_s = _proj_out(xs2, oa.reshape(bs * ts, -1), ob.reshape(bs * ts, -1), ga, gb,
                    woa_bf, wob_bf, wout_bf).reshape(bs, ts, d)
    k_s = kaf.reshape(bs, ts, H_A, 2, DK_A)
    v_s = vaf.reshape(bs, ts, H_A, DV_A)
    return y_p, y_s, k_p, v_p, ret_p, k_s, v_s, ret_s


def kernel(x_prompt, x_sample, cache_k_diff, cache_v_diff, state_ret, norm_g, w_in, b_gate, qn_g, kn_g,
           lam_q1, lam_k1, lam_q2, lam_k2, subln_g, w_o_diff, w_o_ret, w_out):
    depth = w_in.shape[0]
    hp, hs = x_prompt, x_sample
    outs = [[] for _ in range(6)]
    for l in range(depth):
        lam_init = 0.8 - 0.6 * math.exp(-0.3 * l)
        hp, hs, k_p, v_p, ret_p, k_s, v_s, ret_s = _layer(
            hp, hs, cache_k_diff[l], cache_v_diff[l], state_ret[l], norm_g[l], w_in[l], b_gate[l],
            qn_g[l], kn_g[l], lam_q1[l], lam_k1[l], lam_q2[l], lam_k2[l], subln_g[l],
            w_o_diff[l], w_o_ret[l], w_out[l], lam_init)
        for lst, a in zip(outs, (k_p, v_p, ret_p, k_s, v_s, ret_s)):
            lst.append(a)
    return (hp, hs) + tuple(jnp.stack(o) for o in outs)
```

```python
import functools
import math

import jax
import jax.numpy as jnp
from jax import lax
from jax.experimental import pallas as pl
from jax.experimental.pallas import tpu as pltpu

F32 = jnp.float32
BF16 = jnp.bfloat16

CHUNK = 64
H_A = 4
DK_A = 64
DV_A = 2 * DK_A
H_B = 4
DK_B = 128
DV_B = 128
HEAD_W = 128
EPS = 1e-6

ROW_TILE = 256
Q_TILE = 256
SAMPLE_KEY_BLOCK = 512
VMEM_LIMIT = 56 * 1024 * 1024


def _nt_dot(a, b):
    return lax.dot_general(a, b, (((1,), (1,)), ((), ())), preferred_element_type=F32)


def _tn_dot(a, b):
    return lax.dot_general(a, b, (((0,), (0,)), ((), ())), preferred_element_type=F32)


def _dot(a, b):
    return jnp.dot(a, b, preferred_element_type=F32)


def _sigmoid(x):
    return 1.0 / (1.0 + jnp.exp(-x))


def _proj_in_kernel(x_ref, g_ref, w_ref, bg_ref, qg_ref, kg_ref, gm_ref,
                    qa_ref, kaf_ref, kab_ref, vaf_ref, vab_ref, za_ref,
                    qb_ref, kb_ref, vb_ref, zb_ref, ga_ref, gb_ref, *, wa, wb, d_model, k_transposed):
    x = x_ref[0]
    tm = x.shape[0]
    ms = jnp.mean(x * x, axis=-1, keepdims=True)
    xn = ((x * lax.rsqrt(ms + EPS)) * g_ref[...]).astype(BF16)

    def seg(lo, width):
        return _dot(xn, w_ref[:, lo:lo + width])

    def group_norm(h, gain):
        ss = _dot((h * h).astype(BF16), gm_ref[...]) * (1.0 / DK_A)
        return (h * lax.rsqrt(ss + EPS)) * gain

    def silu(h):
        return h * _sigmoid(h)

    off = 0
    qa_ref[0] = (group_norm(seg(off, wa), qg_ref[...]) * (DK_A ** -0.5)).astype(BF16)
    off += wa
    ka = group_norm(seg(off, wa), kg_ref[...])
    if k_transposed:
        ka = ka.T
    kaf_ref[0] = ka
    kab_ref[0] = ka.astype(BF16)
    off += wa
    va = seg(off, wa)
    vab_ref[0] = va.astype(BF16)
    for h in range(H_A):
        vaf_ref[pl.ds(h, tm, stride=H_A), :] = va[:, h * HEAD_W:(h + 1) * HEAD_W]
    off += wa
    za_ref[0] = silu(seg(off, wa)).astype(BF16)
    off += wa
    qb_ref[0] = seg(off, wb).astype(BF16)
    off += wb
    kb_ref[0] = (seg(off, wb) * (DK_B ** -0.5)).astype(BF16)
    off += wb
    vb_ref[0] = seg(off, wb).astype(BF16)
    off += wb
    zb_ref[0] = silu(seg(off, wb)).astype(BF16)
    off += wb
    ga_ref[0] = _sigmoid(seg(off, d_model) + bg_ref[0:1, :]).astype(BF16)
    off += d_model
    gb_ref[0] = _sigmoid(seg(off, d_model) + bg_ref[1:2, :]).astype(BF16)


def _proj_in(x, g, w_bf, bg, qg_t, kg_t, gm, *, tm, k_transposed):
    b, t, d = x.shape
    wa = H_A * HEAD_W
    wb = H_B * HEAD_W
    assert t % tm == 0
    nt = t // tm
    row = lambda w: pl.BlockSpec((1, tm, w), lambda bi, i: (bi, i, 0))
    full = lambda a: pl.BlockSpec(a.shape, lambda bi, i: (0,) * a.ndim)
    act = lambda w, dt: jax.ShapeDtypeStruct((b, t, w), dt)
    if k_transposed:
        k_shape = lambda dt: jax.ShapeDtypeStruct((b, wa, t), dt)
        k_spec = pl.BlockSpec((1, wa, tm), lambda bi, i: (bi, 0, i))
    else:
        k_shape = lambda dt: act(wa, dt)
        k_spec = row(wa)
    out_shapes = [
        act(wa, BF16),
        k_shape(F32), k_shape(BF16),
        jax.ShapeDtypeStruct((b * t * H_A, HEAD_W), F32),
        act(wa, BF16),
        act(wa, BF16),
        act(wb, BF16), act(wb, BF16), act(wb, BF16),
        act(wb, BF16),
        act(d, BF16), act(d, BF16),
    ]
    out_specs = [row(wa), k_spec, k_spec,
                 pl.BlockSpec((tm * H_A, HEAD_W), lambda bi, i: (bi * nt + i, 0)),
                 row(wa), row(wa), row(wb), row(wb), row(wb), row(wb), row(d), row(d)]
    return pl.pallas_call(
        functools.partial(_proj_in_kernel, wa=wa, wb=wb, d_model=d, k_transposed=k_transposed),
        grid=(b, nt),
        in_specs=[row(d), full(g), full(w_bf), full(bg), full(qg_t), full(kg_t), full(gm)],
        out_specs=out_specs,
        out_shape=out_shapes,
        compiler_params=pltpu.CompilerParams(
            dimension_semantics=("arbitrary", "arbitrary"), vmem_limit_bytes=VMEM_LIMIT),
        name="proj_in",
    )(x, g, w_bf, bg, qg_t, kg_t, gm)


def _split_halves(q):
    lane = lax.broadcasted_iota(jnp.int32, q.shape, 1)
    zero = jnp.zeros_like(q)
    return jnp.concatenate([jnp.where(lane < DK_A, q, zero), jnp.where(lane >= DK_A, q, zero)], axis=0)


def _softmax_first(s, v):
    m = jnp.max(s, axis=-1, keepdims=True)
    p = jnp.exp(s - m)
    l = jnp.sum(p, axis=-1, keepdims=True)
    return m, l, _dot(p.astype(BF16), v)


def _softmax_update(carry, s, v):
    m, l, acc = carry
    m_new = jnp.maximum(m, jnp.max(s, axis=-1, keepdims=True))
    alpha = jnp.exp(m - m_new)
    p = jnp.exp(s - m_new)
    l = alpha * l + jnp.sum(p, axis=-1, keepdims=True)
    acc = alpha * acc + _dot(p.astype(BF16), v)
    return m_new, l, acc


def _diff_combine(l, acc, lam, t):
    o = acc / l
    return o[:t] - lam * o[t:]


def _lane_rms(o):
    return o * lax.rsqrt(jnp.mean(o * o, axis=-1, keepdims=True) + EPS)


def _retention_block(q, k, v, r, dmat, xi, zeta, cdecay):
    s = _nt_dot(q, k) * dmat
    o = _dot(s.astype(BF16), v) + _dot(q, r.astype(BF16)) * xi
    kz = (k.astype(F32) * zeta).astype(BF16)
    r_new = cdecay * r + _tn_dot(kz, v)
    return o, r_new


def _mixer_prompt_kernel(scal_ref, qa_ref, k_ref, v_ref, za_ref, qb_ref, kb_ref, vb_ref, zb_ref,
                         abias_ref, dmat_ref, xi_ref, zeta_ref, sg_ref,
                         oa_ref, ob_ref, rout_ref, r_scr, *, tq):
    h = pl.program_id(1)
    qi = pl.program_id(2)
    lam = scal_ref[0]
    slope = scal_ref[1 + h]
    cdecay = scal_ref[1 + H_A + h]

    qs = _split_halves(qa_ref[0])
    q_start = pl.multiple_of(qi * tq, tq)
    bias = abias_ref[0]
    s = _dot(qs, k_ref[0, :, pl.ds(q_start, tq)])
    s = s + jnp.concatenate([bias, bias], axis=0)
    carry = _softmax_first(s, v_ref[0, pl.ds(q_start, tq), :])
    col = lax.broadcasted_iota(jnp.int32, (1, tq), 1)

    def body(kb, carry):
        k_start = pl.multiple_of(kb * tq, tq)
        s = _dot(qs, k_ref[0, :, pl.ds(k_start, tq)])
        s = s + (col + (kb - qi) * tq).astype(F32) * slope
        return _softmax_update(carry, s, v_ref[0, pl.ds(k_start, tq), :])

    m, l, acc = lax.fori_loop(0, qi, body, carry)
    oa = _diff_combine(l, acc, lam, tq)
    oa = (_lane_rms(oa) * sg_ref[0:1, :]) * sg_ref[1:2, :]
    oa_ref[0] = (oa * za_ref[0].astype(F32)).astype(BF16)

    @pl.when(qi == 0)
    def _():
        r_scr[...] = jnp.zeros_like(r_scr)

    ob, r_new = _retention_block(qb_ref[0], kb_ref[0], vb_ref[0], r_scr[...],
                                 dmat_ref[0], xi_ref[0], zeta_ref[0], cdecay)
    r_scr[...] = r_new
    rout_ref[0, 0] = r_new
    ob_ref[0] = (_lane_rms(ob) * zb_ref[0].astype(F32)).astype(BF16)


def _mixer_prompt(scal, qa, kab, vab, za, qb, kb, vb, zb, abias, dmat, xi, zeta, sg):
    b, t, _ = qa.shape
    tq = Q_TILE
    assert t % tq == 0 and tq % CHUNK == 0
    tile = pl.BlockSpec((1, tq, HEAD_W), lambda bi, hi, qi: (bi, qi, hi))
    whole = pl.BlockSpec((1, t, HEAD_W), lambda bi, hi, qi: (bi, 0, hi))
    whole_t = pl.BlockSpec((1, HEAD_W, t), lambda bi, hi, qi: (bi, hi, 0))
    per_head = lambda a: pl.BlockSpec((1,) + a.shape[1:], lambda bi, hi, qi: (hi,) + (0,) * (a.ndim - 1))
    return pl.pallas_call(
        functools.partial(_mixer_prompt_kernel, tq=tq),
        grid=(b, H_A, t // tq),
        in_specs=[pl.BlockSpec(memory_space=pltpu.SMEM),
                  tile, whole_t, whole, tile, tile, tile, tile, tile,
                  per_head(abias), per_head(dmat), per_head(xi), per_head(zeta),
                  pl.BlockSpec(sg.shape, lambda bi, hi, qi: (0, 0))],
        out_specs=[tile, tile,
                   pl.BlockSpec((1, 1, DK_B, DV_B), lambda bi, hi, qi: (bi, hi, 0, 0))],
        out_shape=[jax.ShapeDtypeStruct((b, t, H_A * HEAD_W), BF16),
                   jax.ShapeDtypeStruct((b, t, H_B * HEAD_W), BF16),
                   jax.ShapeDtypeStruct((b, H_B, DK_B, DV_B), F32)],
        scratch_shapes=[pltpu.VMEM((DK_B, DV_B), F32)],
        compiler_params=pltpu.CompilerParams(
            dimension_semantics=("arbitrary", "arbitrary", "arbitrary"), vmem_limit_bytes=VMEM_LIMIT),
        name="mixer_prompt",
    )(scal, qa, kab, vab, za, qb, kb, vb, zb, abias, dmat, xi, zeta, sg)


def _mixer_sample_kernel(scal_ref, qa_ref, kc_ref, vc_ref, kn_ref, vn_ref, za_ref,
                         qb_ref, kb_ref, vb_ref, zb_ref, r_ref,
                         abias_ref, dmat_ref, xi_ref, zeta_ref, sg_ref,
                         oa_ref, ob_ref, rout_ref, *, n, past, tk):
    h = pl.program_id(1)
    lam = scal_ref[0]
    slope = scal_ref[1 + h]
    cdecay = scal_ref[1 + H_A + h]

    qs = _split_halves(qa_ref[0])
    bias = abias_ref[0]
    s = _nt_dot(qs, kn_ref[0]) + jnp.concatenate([bias, bias], axis=0)
    carry = _softmax_first(s, vn_ref[0])
    col = lax.broadcasted_iota(jnp.int32, (1, tk), 1)

    def body(c, carry):
        k_start = pl.multiple_of(c * tk, tk)
        kk = kc_ref[0, :, pl.ds(k_start, tk)].astype(BF16)
        vv = vc_ref[0, pl.ds(k_start * H_A + h, tk, stride=H_A), :].astype(BF16)
        s = _dot(qs, kk) + (col + (c * tk - past)).astype(F32) * slope
        return _softmax_update(carry, s, vv)

    m, l, acc = lax.fori_loop(0, past // tk, body, carry)
    oa = _diff_combine(l, acc, lam, n)
    oa = (_lane_rms(oa) * sg_ref[0:1, :]) * sg_ref[1:2, :]
    oa_ref[0] = (oa * za_ref[0].astype(F32)).astype(BF16)

    ob, r_new = _retention_block(qb_ref[0], kb_ref[0], vb_ref[0], r_ref[0, 0],
                                 dmat_ref[0], xi_ref[0], zeta_ref[0], cdecay)
    rout_ref[0, 0] = r_new
    ob_ref[0] = (_lane_rms(ob) * zb_ref[0].astype(F32)).astype(BF16)


def _mixer_sample(scal, qa, kc, vc, kab, vab, za, qb, kb, vb, zb, r0, abias, dmat, xi, zeta, sg):
    b, n, _ = qa.shape
    past = kc.shape[2]
    tk = SAMPLE_KEY_BLOCK
    assert past % tk == 0 and past % CHUNK == 0 and n <= CHUNK
    tile = pl.BlockSpec((1, n, HEAD_W), lambda bi, hi: (bi, 0, hi))
    cache_k = pl.BlockSpec((1, HEAD_W, past), lambda bi, hi: (bi, hi, 0))
    cache_v = pl.BlockSpec((1, past * H_A, HEAD_W), lambda bi, hi: (bi, 0, 0))
    state = pl.BlockSpec((1, 1, DK_B, DV_B), lambda bi, hi: (bi, hi, 0, 0))
    per_head = lambda a: pl.BlockSpec((1,) + a.shape[1:], lambda bi, hi: (hi,) + (0,) * (a.ndim - 1))
    return pl.pallas_call(
        functools.partial(_mixer_sample_kernel, n=n, past=past, tk=tk),
        grid=(b, H_A),
        in_specs=[pl.BlockSpec(memory_space=pltpu.SMEM),
                  tile, cache_k, cache_v, tile, tile, tile, tile, tile, tile, tile, state,
                  per_head(abias), per_head(dmat), per_head(xi), per_head(zeta),
                  pl.BlockSpec(sg.shape, lambda bi, hi: (0, 0))],
        out_specs=[tile, tile, state],
        out_shape=[jax.ShapeDtypeStruct((b, n, H_A * HEAD_W), BF16),
                   jax.ShapeDtypeStruct((b, n, H_B * HEAD_W), BF16),
                   jax.ShapeDtypeStruct((b, H_B, DK_B, DV_B), F32)],
        compiler_params=pltpu.CompilerParams(
            dimension_semantics=("arbitrary", "arbitrary"), vmem_limit_bytes=VMEM_LIMIT),
        name="mixer_sample",
    )(scal, qa, kc, vc, kab, vab, za, qb, kb, vb, zb, r0, abias, dmat, xi, zeta, sg)


def _proj_out_kernel(x_ref, oa_ref, ob_ref, ga_ref, gb_ref, woa_ref, wob_ref, wout_ref, y_ref):
    m = (ga_ref[...].astype(F32) * _dot(oa_ref[...], woa_ref[...])
         + gb_ref[...].astype(F32) * _dot(ob_ref[...], wob_ref[...]))
    y_ref[...] = x_ref[...] + _dot(m.astype(BF16), wout_ref[...])


def _proj_out(x2, oa, ob, ga, gb, woa, wob, wout):
    n, d = x2.shape
    tm = ROW_TILE
    row = lambda w: pl.BlockSpec((tm, w), lambda i: (i, 0))
    full = lambda a: pl.BlockSpec(a.shape, lambda i: (0,) * a.ndim)
    return pl.pallas_call(
        _proj_out_kernel,
        grid=(n // tm,),
        in_specs=[row(d), row(oa.shape[1]), row(ob.shape[1]), row(d), row(d),
                  full(woa), full(wob), full(wout)],
        out_specs=row(d),
        out_shape=jax.ShapeDtypeStruct((n, d), F32),
        compiler_params=pltpu.CompilerParams(
            dimension_semantics=("arbitrary",), vmem_limit_bytes=VMEM_LIMIT),
        name="proj_out",
    )(x2, oa, ob, ga, gb, woa, wob, wout)


def _block_tables(t, slopes, log_g):
    i = jnp.arange(t, dtype=jnp.int32)
    allowed = (i[None, :] // CHUNK) <= (i[:, None] // CHUNK)
    dist = jnp.abs(i[:, None] - i[None, :]).astype(F32)
    jf = i.astype(F32)
    abias = -slopes[:, None, None] * dist[None] + slopes[:, None, None] * jf[None, :, None]
    abias = jnp.where(allowed[None], abias, -jnp.inf)
    dmat = jnp.where(allowed[None], jnp.exp(dist[None] * log_g[:, None, None]), 0.0)
    xi = jnp.exp((jf + 1.0)[None] * log_g[:, None])
    zeta = jnp.exp((t - 1.0 - jf)[None] * log_g[:, None])
    bc = lambda a: jnp.broadcast_to(a[:, :, None], a.shape + (HEAD_W,))
    return abias, dmat, bc(xi), bc(zeta)


def _layer(x_prompt, x_sample, ck, cv, r0, norm_g, w_in, b_gate, qn_g, kn_g,
           lam_q1, lam_k1, lam_q2, lam_k2, subln_g, w_oa, w_ob, w_out, lam_init):
    bp, tp, d = x_prompt.shape
    bs, ts, _ = x_sample.shape
    past = ck.shape[1]
    slopes = 2.0 ** (-8.0 * jnp.arange(1, H_A + 1, dtype=F32) / H_A)
    log_g = jnp.log(1.0 - 2.0 ** (-5.0 - jnp.arange(H_B, dtype=F32)))
    lam = (jnp.exp(jnp.sum(lam_q1 * lam_k1).astype(F32))
           - jnp.exp(jnp.sum(lam_q2 * lam_k2).astype(F32)) + lam_init)

    w_bf = w_in.astype(BF16)
    woa_bf, wob_bf, wout_bf = w_oa.astype(BF16), w_ob.astype(BF16), w_out.astype(BF16)
    g = norm_g.reshape(1, d)
    qg_t = jnp.tile(qn_g, 2 * H_A).reshape(1, H_A * HEAD_W)
    kg_t = jnp.tile(kn_g, 2 * H_A).reshape(1, H_A * HEAD_W)
    grp = jnp.arange(H_A * HEAD_W, dtype=jnp.int32) // DK_A
    gm = (grp[:, None] == grp[None, :]).astype(BF16)
    sg = jnp.stack([subln_g, jnp.full((DV_A,), 1.0 - lam_init, F32)])

    def scal(n_block):
        return jnp.concatenate([lam.reshape(1), slopes, jnp.exp(n_block * log_g)]).astype(F32)

    (qa, kaf, kab, vaf, vab, za, qb, kb, vb, zb, ga, gb) = _proj_in(
        x_prompt, g, w_bf, b_gate, qg_t, kg_t, gm, tm=ROW_TILE, k_transposed=True)
    tabs = _block_tables(Q_TILE, slopes, log_g)
    oa, ob, ret_p = _mixer_prompt(scal(float(Q_TILE)), qa, kab, vab, za, qb, kb, vb, zb, *tabs, sg)
    flat = lambda a: a.reshape(-1, a.shape[-1])
    y_p = _proj_out(flat(x_prompt), flat(oa), flat(ob), flat(ga), flat(gb),
                    woa_bf, wob_bf, wout_bf).reshape(bp, tp, d)
    k_p = jnp.transpose(kaf.reshape(bp, H_A, 2, DK_A, tp), (0, 4, 1, 2, 3))
    v_p = vaf.reshape(bp, tp, H_A, DV_A)

    (qa, kaf, kab, vaf, vab, za, qb, kb, vb, zb, ga, gb) = _proj_in(
        x_sample.reshape(1, bs * ts, d), g, w_bf, b_gate, qg_t, kg_t, gm,
        tm=min(ROW_TILE, bs * ts), k_transposed=False)
    r3 = lambda a: a.reshape(bs, ts, a.shape[-1])
    tabs = _block_tables(ts, slopes, log_g)
    ck_t = jnp.transpose(ck, (0, 2, 3, 4, 1)).reshape(bs, H_A * HEAD_W, past)
    oa, ob, ret_s = _mixer_sample(scal(float(ts)), r3(qa), ck_t, cv.reshape(bs, past * H_A, HEAD_W),
                                  r3(kab), r3(vab), r3(za), r3(qb), r3(kb), r3(vb), r3(zb), r0, *tabs, sg)
    y_s = _proj_out(flat(x_sample), flat(oa), flat(ob), flat(ga), flat(gb),
                    woa_bf, wob_bf, wout_bf).reshape(bs, ts, d)
    k_s = kaf.reshape(bs, ts, H_A, 2, DK_A)
    v_s = vaf.reshape(bs, ts, H_A, DV_A)
    return y_p, y_s, k_p, v_p, ret_p, k_s, v_s, ret_s


def kernel(x_prompt, x_sample, cache_k_diff, cache_v_diff, state_ret, norm_g, w_in, b_gate, qn_g, kn_g,
           lam_q1, lam_k1, lam_q2, lam_k2, subln_g, w_o_diff, w_o_ret, w_out):
    depth = w_in.shape[0]
    hp, hs = x_prompt, x_sample
    outs = [[] for _ in range(6)]
    for l in range(depth):
        lam_init = 0.8 - 0.6 * math.exp(-0.3 * l)
        hp, hs, k_p, v_p, ret_p, k_s, v_s, ret_s = _layer(
            hp, hs, cache_k_diff[l], cache_v_diff[l], state_ret[l], norm_g[l], w_in[l], b_gate[l],
            qn_g[l], kn_g[l], lam_q1[l], lam_k1[l], lam_q2[l], lam_k2[l], subln_g[l],
            w_o_diff[l], w_o_ret[l], w_out[l], lam_init)
        for lst, a in zip(outs, (k_p, v_p, ret_p, k_s, v_s, ret_s)):
            lst.append(a)
    return (hp, hs) + tuple(jnp.stack(o) for o in outs)
```

```python
import functools
import math

import jax
import jax.numpy as jnp
from jax import lax
from jax.experimental import pallas as pl
from jax.experimental.pallas import tpu as pltpu

F32 = jnp.float32
BF16 = jnp.bfloat16

CHUNK = 64
H_A = 4
DK_A = 64
DV_A = 2 * DK_A
H_B = 4
DK_B = 128
DV_B = 128
HEAD_W = 128
EPS = 1e-6

ROW_TILE = 256
Q_TILE = 256
HEADS_PER_STEP = 4
SAMPLE_KEY_BLOCK = 512
VMEM_LIMIT = 56 * 1024 * 1024


def _nt_dot(a, b):
    return lax.dot_general(a, b, (((1,), (1,)), ((), ())), preferred_element_type=F32)


def _tn_dot(a, b):
    return lax.dot_general(a, b, (((0,), (0,)), ((), ())), preferred_element_type=F32)


def _dot(a, b):
    return jnp.dot(a, b, preferred_element_type=F32)


def _sigmoid(x):
    return 1.0 / (1.0 + jnp.exp(-x))


def _proj_in_kernel(x_ref, g_ref, w_ref, bg_ref, qg_ref, kg_ref, gm_ref,
                    qa_ref, kaf_ref, kab_ref, vaf_ref, vab_ref, za_ref,
                    qb_ref, kb_ref, vb_ref, zb_ref, ga_ref, gb_ref, *, wa, wb, d_model, feature_major):
    x = x_ref[0]
    tm = x.shape[0]
    ms = jnp.mean(x * x, axis=-1, keepdims=True)
    xn = ((x * lax.rsqrt(ms + EPS)) * g_ref[...]).astype(BF16)

    def seg(lo, width):
        return _dot(xn, w_ref[:, lo:lo + width])

    def group_norm(h, gain):
        ss = _dot((h * h).astype(BF16), gm_ref[...]) * (1.0 / DK_A)
        return (h * lax.rsqrt(ss + EPS)) * gain

    def silu(h):
        return h * _sigmoid(h)

    off = 0
    maybe_t = (lambda a: a.T) if feature_major else (lambda a: a)
    qa_ref[0] = maybe_t(group_norm(seg(off, wa), qg_ref[...]) * (DK_A ** -0.5)).astype(BF16)
    off += wa
    ka = group_norm(seg(off, wa), kg_ref[...])
    kaf_ref[0] = maybe_t(ka)
    kab_ref[0] = ka.astype(BF16)
    off += wa
    va = seg(off, wa)
    vab_ref[0] = maybe_t(va).astype(BF16)
    for h in range(H_A):
        vaf_ref[pl.ds(h, tm, stride=H_A), :] = va[:, h * HEAD_W:(h + 1) * HEAD_W]
    off += wa
    za_ref[0] = silu(seg(off, wa)).astype(BF16)
    off += wa
    qb_ref[0] = seg(off, wb).astype(BF16)
    off += wb
    kb_ref[0] = (seg(off, wb) * (DK_B ** -0.5)).astype(BF16)
    off += wb
    vb_ref[0] = seg(off, wb).astype(BF16)
    off += wb
    zb_ref[0] = silu(seg(off, wb)).astype(BF16)
    off += wb
    ga_ref[0] = _sigmoid(seg(off, d_model) + bg_ref[0:1, :]).astype(BF16)
    off += d_model
    gb_ref[0] = _sigmoid(seg(off, d_model) + bg_ref[1:2, :]).astype(BF16)


def _proj_in(x, g, w_bf, bg, qg_t, kg_t, gm, *, tm, feature_major):
    b, t, d = x.shape
    wa = H_A * HEAD_W
    wb = H_B * HEAD_W
    assert t % tm == 0
    nt = t // tm
    row = lambda w: pl.BlockSpec((1, tm, w), lambda bi, i: (bi, i, 0))
    full = lambda a: pl.BlockSpec(a.shape, lambda bi, i: (0,) * a.ndim)
    act = lambda w, dt: jax.ShapeDtypeStruct((b, t, w), dt)
    if feature_major:
        fm_shape = lambda dt: jax.ShapeDtypeStruct((b, wa, t), dt)
        fm_spec = pl.BlockSpec((1, wa, tm), lambda bi, i: (bi, 0, i))
    else:
        fm_shape = lambda dt: act(wa, dt)
        fm_spec = row(wa)
    out_shapes = [
        fm_shape(BF16),
        fm_shape(F32), act(wa, BF16),
        jax.ShapeDtypeStruct((b * t * H_A, HEAD_W), F32),
        fm_shape(BF16),
        act(wa, BF16),
        act(wb, BF16), act(wb, BF16), act(wb, BF16),
        act(wb, BF16),
        act(d, BF16), act(d, BF16),
    ]
    out_specs = [fm_spec, fm_spec, row(wa),
                 pl.BlockSpec((tm * H_A, HEAD_W), lambda bi, i: (bi * nt + i, 0)),
                 fm_spec, row(wa), row(wb), row(wb), row(wb), row(wb), row(d), row(d)]
    return pl.pallas_call(
        functools.partial(_proj_in_kernel, wa=wa, wb=wb, d_model=d, feature_major=feature_major),
        grid=(b, nt),
        in_specs=[row(d), full(g), full(w_bf), full(bg), full(qg_t), full(kg_t), full(gm)],
        out_specs=out_specs,
        out_shape=out_shapes,
        compiler_params=pltpu.CompilerParams(
            dimension_semantics=("arbitrary", "arbitrary"), vmem_limit_bytes=VMEM_LIMIT),
        name="proj_in",
    )(x, g, w_bf, bg, qg_t, kg_t, gm)


def _split_halves(q):
    lane = lax.broadcasted_iota(jnp.int32, q.shape, 1)
    zero = jnp.zeros_like(q)
    return jnp.concatenate([jnp.where(lane < DK_A, q, zero), jnp.where(lane >= DK_A, q, zero)], axis=0)


def _softmax_first(s, v):
    m = jnp.max(s, axis=-1, keepdims=True)
    p = jnp.exp(s - m)
    l = jnp.sum(p, axis=-1, keepdims=True)
    return m, l, _dot(p.astype(BF16), v)


def _softmax_update(carry, s, v):
    m, l, acc = carry
    m_new = jnp.maximum(m, jnp.max(s, axis=-1, keepdims=True))
    alpha = jnp.exp(m - m_new)
    p = jnp.exp(s - m_new)
    l = alpha * l + jnp.sum(p, axis=-1, keepdims=True)
    acc = alpha * acc + _dot(p.astype(BF16), v)
    return m_new, l, acc


def _diff_combine(l, acc, lam, t):
    o = acc / l
    return o[:t] - lam * o[t:]


def _lane_rms(o):
    return o * lax.rsqrt(jnp.mean(o * o, axis=-1, keepdims=True) + EPS)


def _retention_block(q, k, v, r, dmat, xi, zeta, cdecay):
    s = _nt_dot(q, k) * dmat
    o = _dot(s.astype(BF16), v) + _dot(q, r.astype(BF16)) * xi
    kz = (k.astype(F32) * zeta).astype(BF16)
    r_new = cdecay * r + _tn_dot(kz, v)
    return o, r_new


def _mixer_prompt_kernel(scal_ref, qt_ref, k_ref, vt_ref, za_ref, qb_ref, kb_ref, vb_ref, zb_ref,
                         dbias_ref, kbias_ref, dmat_ref, xi_ref, zeta_ref, sg_ref,
                         oa_ref, ob_ref, rout_ref, r_scr, acc_scr, *s_scrs, tq, hps):
    hg = pl.program_id(1)
    qi = pl.program_id(2)
    lam = scal_ref[0]
    lanes = lambda j: slice(j * HEAD_W, (j + 1) * HEAD_W)

    def split_halves_t(qt):
        feat = lax.broadcasted_iota(jnp.int32, qt.shape, 0)
        zero = jnp.zeros_like(qt)
        return jnp.concatenate([jnp.where(feat < DK_A, qt, zero), jnp.where(feat >= DK_A, qt, zero)], axis=1)

    qst = [split_halves_t(qt_ref[0, lanes(j), :]) for j in range(hps)]

    def stage_scores(j, kb, slot):
        k_start = pl.multiple_of(kb * tq, tq)
        s = _dot(k_ref[0, pl.ds(k_start, tq), lanes(j)], qst[j])
        kbias = kbias_ref[j]
        s = s + jnp.concatenate([kbias] * (2 * tq // HEAD_W), axis=1)
        s_scrs[j][slot] = s
        return jnp.max(s, axis=0, keepdims=True)

    def update(j, m, l, s, s_max, shift, kb):
        k_start = pl.multiple_of(kb * tq, tq)
        m_new = jnp.maximum(m, s_max + shift)
        alpha = jnp.exp(m - m_new)
        p = jnp.exp(s - (m_new - shift))
        l = alpha * l + jnp.sum(p, axis=0, keepdims=True)
        acc_scr[j] = alpha * acc_scr[j] + _dot(vt_ref[0, lanes(j), pl.ds(k_start, tq)], p.astype(BF16))
        return m_new, l

    def body(kb, carry):
        slot = lax.rem(kb, 2)
        new = []
        for j in range(hps):
            smax_cur, m, l = carry[j]
            s_cur = s_scrs[j][slot]
            smax_next = stage_scores(j, kb + 1, 1 - slot)
            shift = ((kb - qi) * tq).astype(F32) * scal_ref[1 + hg * hps + j]
            new.append((smax_next,) + update(j, m, l, s_cur, smax_cur, shift, kb))
        return tuple(new)

    acc_scr[...] = jnp.zeros_like(acc_scr)
    init = tuple((stage_scores(j, 0, 0), jnp.full((1, 2 * tq), -jnp.inf, F32), jnp.zeros((1, 2 * tq), F32))
                 for j in range(hps))
    carry = lax.fori_loop(0, qi, body, init)
    for j in range(hps):
        _, m, l = carry[j]
        dbias = dbias_ref[j]
        s_diag = s_scrs[j][lax.rem(qi, 2)] + jnp.concatenate([dbias, dbias], axis=1)
        m, l = update(j, m, l, s_diag, jnp.max(s_diag, axis=0, keepdims=True), 0.0, qi)
        ot = acc_scr[j] * (1.0 / l)
        oa = (ot[:, :tq] - lam * ot[:, tq:]).T
        oa = (_lane_rms(oa) * sg_ref[0:1, :]) * sg_ref[1:2, :]
        oa_ref[0, :, lanes(j)] = (oa * za_ref[0, :, lanes(j)].astype(F32)).astype(BF16)

    @pl.when(qi == 0)
    def _():
        r_scr[...] = jnp.zeros_like(r_scr)

    for j in range(hps):
        cdecay = scal_ref[1 + H_A + hg * hps + j]
        ob, r_new = _retention_block(qb_ref[0, :, lanes(j)], kb_ref[0, :, lanes(j)], vb_ref[0, :, lanes(j)],
                                     r_scr[j], dmat_ref[j], xi_ref[j], zeta_ref[j], cdecay)
        r_scr[j] = r_new
        rout_ref[0, j] = r_new
        ob_ref[0, :, lanes(j)] = (_lane_rms(ob) * zb_ref[0, :, lanes(j)].astype(F32)).astype(BF16)


def _mixer_prompt(scal, qat, kab, vat, za, qb, kb, vb, zb, dbias, kbias, dmat, xi, zeta, sg):
    b, t, _ = kab.shape
    tq = Q_TILE
    hps = HEADS_PER_STEP
    gw = hps * HEAD_W
    assert t % tq == 0 and tq % CHUNK == 0 and H_A % hps == 0 and H_A == H_B
    tile = pl.BlockSpec((1, tq, gw), lambda bi, gi, qi: (bi, qi, gi))
    tile_t = pl.BlockSpec((1, gw, tq), lambda bi, gi, qi: (bi, gi, qi))
    whole = pl.BlockSpec((1, t, gw), lambda bi, gi, qi: (bi, 0, gi))
    whole_t = pl.BlockSpec((1, gw, t), lambda bi, gi, qi: (bi, gi, 0))
    per_group = lambda a: pl.BlockSpec((hps,) + a.shape[1:], lambda bi, gi, qi: (gi,) + (0,) * (a.ndim - 1))
    return pl.pallas_call(
        functools.partial(_mixer_prompt_kernel, tq=tq, hps=hps),
        grid=(b, H_A // hps, t // tq),
        in_specs=[pl.BlockSpec(memory_space=pltpu.SMEM),
                  tile_t, whole, whole_t, tile, tile, tile, tile, tile,
                  per_group(dbias), per_group(kbias), per_group(dmat), per_group(xi), per_group(zeta),
                  pl.BlockSpec(sg.shape, lambda bi, gi, qi: (0, 0))],
        out_specs=[tile, tile,
                   pl.BlockSpec((1, hps, DK_B, DV_B), lambda bi, gi, qi: (bi, gi, 0, 0))],
        out_shape=[jax.ShapeDtypeStruct((b, t, H_A * HEAD_W), BF16),
                   jax.ShapeDtypeStruct((b, t, H_B * HEAD_W), BF16),
                   jax.ShapeDtypeStruct((b, H_B, DK_B, DV_B), F32)],
        scratch_shapes=[pltpu.VMEM((hps, DK_B, DV_B), F32),
                        pltpu.VMEM((hps, HEAD_W, 2 * tq), F32)]
                       + [pltpu.VMEM((2, tq, 2 * tq), F32)] * hps,

        compiler_params=pltpu.CompilerParams(
            dimension_semantics=("arbitrary", "arbitrary", "arbitrary"), vmem_limit_bytes=VMEM_LIMIT),
        name="mixer_prompt",
    )(scal, qat, kab, vat, za, qb, kb, vb, zb, dbias, kbias, dmat, xi, zeta, sg)


def _mixer_sample_kernel(scal_ref, qa_ref, kc_ref, vc_ref, kn_ref, vn_ref, za_ref,
                         qb_ref, kb_ref, vb_ref, zb_ref, r_ref,
                         abias_ref, dmat_ref, xi_ref, zeta_ref, sg_ref,
                         oa_ref, ob_ref, rout_ref, *, n, past, tk):
    h = pl.program_id(1)
    lam = scal_ref[0]
    slope = scal_ref[1 + h]
    cdecay = scal_ref[1 + H_A + h]

    qs = _split_halves(qa_ref[0])
    bias = abias_ref[0]
    s = _nt_dot(qs, kn_ref[0]) + jnp.concatenate([bias, bias], axis=0)
    carry = _softmax_first(s, vn_ref[0])
    col = lax.broadcasted_iota(jnp.int32, (1, tk), 1)

    def body(c, carry):
        k_start = pl.multiple_of(c * tk, tk)
        kk = kc_ref[0, :, pl.ds(k_start, tk)].astype(BF16)
        vv = vc_ref[0, pl.ds(k_start * H_A + h, tk, stride=H_A), :].astype(BF16)
        s = _dot(qs, kk) + (col + (c * tk - past)).astype(F32) * slope
        return _softmax_update(carry, s, vv)

    m, l, acc = lax.fori_loop(0, past // tk, body, carry)
    oa = _diff_combine(l, acc, lam, n)
    oa = (_lane_rms(oa) * sg_ref[0:1, :]) * sg_ref[1:2, :]
    oa_ref[0] = (oa * za_ref[0].astype(F32)).astype(BF16)

    ob, r_new = _retention_block(qb_ref[0], kb_ref[0], vb_ref[0], r_ref[0, 0],
                                 dmat_ref[0], xi_ref[0], zeta_ref[0], cdecay)
    rout_ref[0, 0] = r_new
    ob_ref[0] = (_lane_rms(ob) * zb_ref[0].astype(F32)).astype(BF16)


def _mixer_sample(scal, qa, kc, vc, kab, vab, za, qb, kb, vb, zb, r0, abias, dmat, xi, zeta, sg):
    b, n, _ = qa.shape
    past = kc.shape[2]
    tk = SAMPLE_KEY_BLOCK
    assert past % tk == 0 and past % CHUNK == 0 and n <= CHUNK
    tile = pl.BlockSpec((1, n, HEAD_W), lambda bi, hi: (bi, 0, hi))
    cache_k = pl.BlockSpec((1, HEAD_W, past), lambda bi, hi: (bi, hi, 0))
    cache_v = pl.BlockSpec((1, past * H_A, HEAD_W), lambda bi, hi: (bi, 0, 0))
    state = pl.BlockSpec((1, 1, DK_B, DV_B), lambda bi, hi: (bi, hi, 0, 0))
    per_head = lambda a: pl.BlockSpec((1,) + a.shape[1:], lambda bi, hi: (hi,) + (0,) * (a.ndim - 1))
    return pl.pallas_call(
        functools.partial(_mixer_sample_kernel, n=n, past=past, tk=tk),
        grid=(b, H_A),
        in_specs=[pl.BlockSpec(memory_space=pltpu.SMEM),
                  tile, cache_k, cache_v, tile, tile, tile, tile, tile, tile, tile, state,
                  per_head(abias), per_head(dmat), per_head(xi), per_head(zeta),
                  pl.BlockSpec(sg.shape, lambda bi, hi: (0, 0))],
        out_specs=[tile, tile, state],
        out_shape=[jax.ShapeDtypeStruct((b, n, H_A * HEAD_W), BF16),
                   jax.ShapeDtypeStruct((b, n, H_B * HEAD_W), BF16),
                   jax.ShapeDtypeStruct((b, H_B, DK_B, DV_B), F32)],
        compiler_params=pltpu.CompilerParams(
            dimension_semantics=("arbitrary", "arbitrary"), vmem_limit_bytes=VMEM_LIMIT),
        name="mixer_sample",
    )(scal, qa, kc, vc, kab, vab, za, qb, kb, vb, zb, r0, abias, dmat, xi, zeta, sg)


def _proj_out_kernel(x_ref, oa_ref, ob_ref, ga_ref, gb_ref, woa_ref, wob_ref, wout_ref, y_ref):
    m = (ga_ref[...].astype(F32) * _dot(oa_ref[...], woa_ref[...])
         + gb_ref[...].astype(F32) * _dot(ob_ref[...], wob_ref[...]))
    y_ref[...] = x_ref[...] + _dot(m.astype(BF16), wout_ref[...])


def _proj_out(x2, oa, ob, ga, gb, woa, wob, wout):
    n, d = x2.shape
    tm = ROW_TILE
    row = lambda w: pl.BlockSpec((tm, w), lambda i: (i, 0))
    full = lambda a: pl.BlockSpec(a.shape, lambda i: (0,) * a.ndim)
    return pl.pallas_call(
        _proj_out_kernel,
        grid=(n // tm,),
        in_specs=[row(d), row(oa.shape[1]), row(ob.shape[1]), row(d), row(d),
                  full(woa), full(wob), full(wout)],
        out_specs=row(d),
        out_shape=jax.ShapeDtypeStruct((n, d), F32),
        compiler_params=pltpu.CompilerParams(
            dimension_semantics=("arbitrary",), vmem_limit_bytes=VMEM_LIMIT),
        name="proj_out",
    )(x2, oa, ob, ga, gb, woa, wob, wout)


def _block_tables(t, slopes, log_g):
    i = jnp.arange(t, dtype=jnp.int32)
    allowed = (i[None, :] // CHUNK) <= (i[:, None] // CHUNK)
    dist = jnp.abs(i[:, None] - i[None, :]).astype(F32)
    jf = i.astype(F32)
    abias = -slopes[:, None, None] * dist[None] + slopes[:, None, None] * jf[None, :, None]
    abias = jnp.where(allowed[None], abias, -jnp.inf)
    dmat = jnp.where(allowed[None], jnp.exp(dist[None] * log_g[:, None, None]), 0.0)
    xi = jnp.exp((jf + 1.0)[None] * log_g[:, None])
    zeta = jnp.exp((t - 1.0 - jf)[None] * log_g[:, None])
    bc = lambda a: jnp.broadcast_to(a[:, :, None], a.shape + (HEAD_W,))
    return abias, dmat, bc(xi), bc(zeta)


def _key_bias_table(t, slopes):
    kb = slopes[:, None] * jnp.arange(t, dtype=F32)[None]
    return jnp.broadcast_to(kb[:, :, None], kb.shape + (HEAD_W,))


def _layer(x_prompt, x_sample, ck, cv, r0, norm_g, w_in, b_gate, qn_g, kn_g,
           lam_q1, lam_k1, lam_q2, lam_k2, subln_g, w_oa, w_ob, w_out, lam_init):
    bp, tp, d = x_prompt.shape
    bs, ts, _ = x_sample.shape
    past = ck.shape[1]
    slopes = 2.0 ** (-8.0 * jnp.arange(1, H_A + 1, dtype=F32) / H_A)
    log_g = jnp.log(1.0 - 2.0 ** (-5.0 - jnp.arange(H_B, dtype=F32)))
    lam = (jnp.exp(jnp.sum(lam_q1 * lam_k1).astype(F32))
           - jnp.exp(jnp.sum(lam_q2 * lam_k2).astype(F32)) + lam_init)

    w_bf = w_in.astype(BF16)
    woa_bf, wob_bf, wout_bf = w_oa.astype(BF16), w_ob.astype(BF16), w_out.astype(BF16)
    g = norm_g.reshape(1, d)
    qg_t = jnp.tile(qn_g, 2 * H_A).reshape(1, H_A * HEAD_W)
    kg_t = jnp.tile(kn_g, 2 * H_A).reshape(1, H_A * HEAD_W)
    grp = jnp.arange(H_A * HEAD_W, dtype=jnp.int32) // DK_A
    gm = (grp[:, None] == grp[None, :]).astype(BF16)
    sg = jnp.stack([subln_g, jnp.full((DV_A,), 1.0 - lam_init, F32)])

    def scal(n_block):
        return jnp.concatenate([lam.reshape(1), slopes, jnp.exp(n_block * log_g)]).astype(F32)

    (qat, kaf, kab, vaf, vat, za, qb, kb, vb, zb, ga, gb) = _proj_in(
        x_prompt, g, w_bf, b_gate, qg_t, kg_t, gm, tm=ROW_TILE, feature_major=True)
    abias, dmat, xi, zeta = _block_tables(Q_TILE, slopes, log_g)
    kbias = _key_bias_table(Q_TILE, slopes)
    dbias = jnp.swapaxes(abias, 1, 2) - kbias[:, :, :1]
    oa, ob, ret_p = _mixer_prompt(scal(float(Q_TILE)), qat, kab, vat, za, qb, kb, vb, zb,
                                  dbias, kbias, dmat, xi, zeta, sg)
    flat = lambda a: a.reshape(-1, a.shape[-1])
    y_p = _proj_out(flat(x_prompt), flat(oa), flat(ob), flat(ga), flat(gb),
                    woa_bf, wob_bf, wout_bf).reshape(bp, tp, d)
    k_p = jnp.transpose(kaf.reshape(bp, H_A, 2, DK_A, tp), (0, 4, 1, 2, 3))
    v_p = vaf.reshape(bp, tp, H_A, DV_A)

    (qa, kaf, kab, vaf, vab, za, qb, kb, vb, zb, ga, gb) = _proj_in(
        x_sample.reshape(1, bs * ts, d), g, w_bf, b_gate, qg_t, kg_t, gm,
        tm=min(ROW_TILE, bs * ts), feature_major=False)
    r3 = lambda a: a.reshape(bs, ts, a.shape[-1])
    tabs = _block_tables(ts, slopes, log_g)
    ck_t = jnp.transpose(ck, (0, 2, 3, 4, 1)).reshape(bs, H_A * HEAD_W, past)
    oa, ob, ret_s = _mixer_sample(scal(float(ts)), r3(qa), ck_t, cv.reshape(bs, past * H_A, HEAD_W),
                                  r3(kab), r3(vab), r3(za), r3(qb), r3(kb), r3(vb), r3(zb), r0, *tabs, sg)
    y_s = _proj_out(flat(x_sample), flat(oa), flat(ob), flat(ga), flat(gb),
                    woa_bf, wob_bf, wout_bf).reshape(bs, ts, d)
    k_s = kaf.reshape(bs, ts, H_A, 2, DK_A)
    v_s = vaf.reshape(bs, ts, H_A, DV_A)
    return y_p, y_s, k_p, v_p, ret_p, k_s, v_s, ret_s


def kernel(x_prompt, x_sample, cache_k_diff, cache_v_diff, state_ret, norm_g, w_in, b_gate, qn_g, kn_g,
           lam_q1, lam_k1, lam_q2, lam_k2, subln_g, w_o_diff, w_o_ret, w_out):
    depth = w_in.shape[0]
    hp, hs = x_prompt, x_sample
    outs = [[] for _ in range(6)]
    for l in range(depth):
        lam_init = 0.8 - 0.6 * math.exp(-0.3 * l)
        hp, hs, k_p, v_p, ret_p, k_s, v_s, ret_s = _layer(
            hp, hs, cache_k_diff[l], cache_v_diff[l], state_ret[l], norm_g[l], w_in[l], b_gate[l],
            qn_g[l], kn_g[l], lam_q1[l], lam_k1[l], lam_q2[l], lam_k2[l], subln_g[l],
            w_o_diff[l], w_o_ret[l], w_out[l], lam_init)
        for lst, a in zip(outs, (k_p, v_p, ret_p, k_s, v_s, ret_s)):
            lst.append(a)
    return (hp, hs) + tuple(jnp.stack(o) for o in outs)
```

```python
import functools
import math

import jax
import jax.numpy as jnp
from jax import lax
from jax.experimental import pallas as pl
from jax.experimental.pallas import tpu as pltpu

F32 = jnp.float32
BF16 = jnp.bfloat16

CHUNK = 64
H_A = 4
DK_A = 64
DV_A = 2 * DK_A
H_B = 4
DK_B = 128
DV_B = 128
HEAD_W = 128
EPS = 1e-6

MXU_TILE = 256
ROW_TILE = 512
Q_TILE = 256
HEADS_PER_STEP = 4
SUM_ROWS = 16
SAMPLE_KEY_TILE = 1024
SAMPLE_KEY_BLOCK = 1024
SCORE_LOOKAHEAD = 4
VMEM_LIMIT = 56 * 1024 * 1024


def _nt_dot(a, b):
    return lax.dot_general(a, b, (((1,), (1,)), ((), ())), preferred_element_type=F32)


def _tn_dot(a, b):
    return lax.dot_general(a, b, (((0,), (0,)), ((), ())), preferred_element_type=F32)


def _dot(a, b):
    return jnp.dot(a, b, preferred_element_type=F32)


def _sigmoid(x):
    return 1.0 / (1.0 + jnp.exp(-x))


def _proj_in_kernel(x_ref, g_ref, w_ref, bg_ref, qg_ref, kg_ref, gm_ref,
                    qa_ref, kaf_ref, kab_ref, vaf_ref, vab_ref, za_ref,
                    qb_ref, kb_ref, vb_ref, zb_ref, ga_ref, gb_ref, *, wa, wb, d_model, feature_major):
    x = x_ref[0]
    tm = x.shape[0]
    ms = jnp.mean(x * x, axis=-1, keepdims=True)
    xn = ((x * lax.rsqrt(ms + EPS)) * g_ref[...]).astype(BF16)

    def seg(lo, width):
        return _dot(xn, w_ref[:, lo:lo + width])

    def group_norm(h, gain):
        sq = (h * h).astype(BF16)
        ss = jnp.concatenate([_dot(sq[:, c:c + MXU_TILE], gm_ref[...]) for c in range(0, wa, MXU_TILE)], axis=1)
        return (h * lax.rsqrt(ss * (1.0 / DK_A) + EPS)) * gain

    def silu(h):
        return h * _sigmoid(h)

    off = 0
    maybe_t = (lambda a: a.T) if feature_major else (lambda a: a)
    qa_ref[0] = maybe_t(group_norm(seg(off, wa), qg_ref[...]) * (DK_A ** -0.5)).astype(BF16)
    off += wa
    ka = group_norm(seg(off, wa), kg_ref[...])
    kaf_ref[0] = maybe_t(ka)
    kab_ref[0] = ka.astype(BF16)
    off += wa
    va = seg(off, wa)
    vab_ref[0] = maybe_t(va).astype(BF16)
    for h in range(H_A):
        vaf_ref[pl.ds(h, tm, stride=H_A), :] = va[:, h * HEAD_W:(h + 1) * HEAD_W]
    off += wa
    za_ref[0] = silu(seg(off, wa)).astype(BF16)
    off += wa
    qb_ref[0] = seg(off, wb).astype(BF16)
    off += wb
    kb_ref[0] = (seg(off, wb) * (DK_B ** -0.5)).astype(BF16)
    off += wb
    vb_ref[0] = seg(off, wb).astype(BF16)
    off += wb
    zb_ref[0] = silu(seg(off, wb)).astype(BF16)
    off += wb
    ga_ref[0] = _sigmoid(seg(off, d_model) + bg_ref[0:1, :]).astype(BF16)
    off += d_model
    gb_ref[0] = _sigmoid(seg(off, d_model) + bg_ref[1:2, :]).astype(BF16)


def _proj_in(x, g, w_bf, bg, qg_t, kg_t, gm, *, tm, feature_major):
    b, t, d = x.shape
    wa = H_A * HEAD_W
    wb = H_B * HEAD_W
    assert t % tm == 0
    nt = t // tm
    row = lambda w: pl.BlockSpec((1, tm, w), lambda bi, i: (bi, i, 0))
    full = lambda a: pl.BlockSpec(a.shape, lambda bi, i: (0,) * a.ndim)
    act = lambda w, dt: jax.ShapeDtypeStruct((b, t, w), dt)
    if feature_major:
        fm_shape = lambda dt: jax.ShapeDtypeStruct((b, wa, t), dt)
        fm_spec = pl.BlockSpec((1, wa, tm), lambda bi, i: (bi, 0, i))
    else:
        fm_shape = lambda dt: act(wa, dt)
        fm_spec = row(wa)
    out_shapes = [
        fm_shape(BF16),
        fm_shape(F32), act(wa, BF16),
        jax.ShapeDtypeStruct((b * t * H_A, HEAD_W), F32),
        fm_shape(BF16),
        act(wa, BF16),
        act(wb, BF16), act(wb, BF16), act(wb, BF16),
        act(wb, BF16),
        act(d, BF16), act(d, BF16),
    ]
    out_specs = [fm_spec, fm_spec, row(wa),
                 pl.BlockSpec((tm * H_A, HEAD_W), lambda bi, i: (bi * nt + i, 0)),
                 fm_spec, row(wa), row(wb), row(wb), row(wb), row(wb), row(d), row(d)]
    return pl.pallas_call(
        functools.partial(_proj_in_kernel, wa=wa, wb=wb, d_model=d, feature_major=feature_major),
        grid=(b, nt),
        in_specs=[row(d), full(g), full(w_bf), full(bg), full(qg_t), full(kg_t), full(gm)],
        out_specs=out_specs,
        out_shape=out_shapes,
        compiler_params=pltpu.CompilerParams(
            dimension_semantics=("arbitrary", "arbitrary"), vmem_limit_bytes=VMEM_LIMIT),
        name="proj_in",
    )(x, g, w_bf, bg, qg_t, kg_t, gm)


def _split_halves(q):
    lane = lax.broadcasted_iota(jnp.int32, q.shape, 1)
    zero = jnp.zeros_like(q)
    return jnp.concatenate([jnp.where(lane < DK_A, q, zero), jnp.where(lane >= DK_A, q, zero)], axis=0)


def _softmax_update(carry, s, v):
    m, acc = carry
    m_new = jnp.maximum(m, jnp.max(s, axis=-1, keepdims=True))
    p = jnp.exp(s - m_new).astype(BF16)
    v1 = jnp.concatenate([v, jnp.ones_like(v)], axis=1)
    return m_new, jnp.exp(m - m_new) * acc + _dot(p, v1)


def _diff_combine(acc, lam, t):
    dv = acc.shape[1] // 2
    o = acc[:, :dv] / acc[:, dv:]
    return o[:t] - lam * o[t:]


def _lane_rms(o):
    return o * lax.rsqrt(jnp.mean(o * o, axis=-1, keepdims=True) + EPS)


def _retention_stage1(q, k, v, r, dmat, xi, zeta, cdecay):
    s = (_nt_dot(q, k) * dmat).astype(BF16)
    cross = _dot(q, r.astype(BF16)) * xi
    kz = (k.astype(F32) * zeta).astype(BF16)
    r_new = cdecay * r + _tn_dot(kz, v)
    return s, cross, r_new


def _retention_stage2(s, cross, v):
    return _dot(s, v) + cross


def _mixer_prompt_kernel(scal_ref, qt_ref, k_ref, vt_ref, za_ref, qb_ref, kb_ref, vb_ref, zb_ref,
                         dbias_ref, kbias_ref, dmat_ref, xi_ref, zeta_ref, sg_ref,
                         oa_ref, ob_ref, rout_ref, r_scr, acc_scr, *s_scrs, tq, hps):
    hg = pl.program_id(1)
    qi = pl.program_id(2)
    lam = scal_ref[0]
    lanes = lambda j: slice(j * HEAD_W, (j + 1) * HEAD_W)

    def split_halves_t(qt):
        feat = lax.broadcasted_iota(jnp.int32, qt.shape, 0)
        zero = jnp.zeros_like(qt)
        return jnp.concatenate([jnp.where(feat < DK_A, qt, zero), jnp.where(feat >= DK_A, qt, zero)], axis=1)

    qst = [split_halves_t(qt_ref[0, lanes(j), :]) for j in range(hps)]

    def stage_scores(j, kb, slot):
        k_start = pl.multiple_of(kb * tq, tq)
        s = _dot(k_ref[0, pl.ds(k_start, tq), lanes(j)], qst[j])
        kbias = kbias_ref[j]
        s = s + jnp.concatenate([kbias] * (2 * tq // HEAD_W), axis=1)
        s_scrs[j][slot] = s
        return jnp.max(s, axis=0, keepdims=True)

    ones_rows = jnp.ones((SUM_ROWS, tq), BF16)

    def update(j, m, s, s_max, shift, kb):
        k_start = pl.multiple_of(kb * tq, tq)
        m_new = jnp.maximum(m, s_max + shift)
        alpha = jnp.exp(m - m_new)
        p = jnp.exp(s - (m_new - shift))
        v1 = jnp.concatenate([vt_ref[0, lanes(j), pl.ds(k_start, tq)], ones_rows], axis=0)
        acc_scr[j] = alpha * acc_scr[j] + _dot(v1, p.astype(BF16))
        return m_new

    def body(kb, carry):
        slot = lax.rem(kb, 2)
        new = []
        for j in range(hps):
            smax_cur, m = carry[j]
            s_cur = s_scrs[j][slot]
            smax_next = stage_scores(j, kb + 1, 1 - slot)
            shift = ((kb - qi) * tq).astype(F32) * scal_ref[1 + hg * hps + j]
            new.append((smax_next, update(j, m, s_cur, smax_cur, shift, kb)))
        return tuple(new)

    @pl.when(qi == 0)
    def _():
        r_scr[...] = jnp.zeros_like(r_scr)

    acc_scr[...] = jnp.zeros_like(acc_scr)
    init = tuple((stage_scores(j, 0, 0), jnp.full((1, 2 * tq), -jnp.inf, F32)) for j in range(hps))
    carry = lax.fori_loop(0, qi, body, init)

    ret = []
    for j in range(hps):
        cdecay = scal_ref[1 + H_A + hg * hps + j]
        s_ret, cross, r_new = _retention_stage1(
            qb_ref[0, :, lanes(j)], kb_ref[0, :, lanes(j)], vb_ref[0, :, lanes(j)],
            r_scr[j], dmat_ref[j], xi_ref[j], zeta_ref[j], cdecay)
        r_scr[j] = r_new
        rout_ref[0, j] = r_new
        ret.append((s_ret, cross))
    for j in range(hps):
        dbias = dbias_ref[j]
        s_diag = s_scrs[j][lax.rem(qi, 2)] + jnp.concatenate([dbias, dbias], axis=1)
        update(j, carry[j][1], s_diag, jnp.max(s_diag, axis=0, keepdims=True), 0.0, qi)
    for j in range(hps):
        ob = _retention_stage2(*ret[j], vb_ref[0, :, lanes(j)])
        ob_ref[0, :, lanes(j)] = (_lane_rms(ob) * zb_ref[0, :, lanes(j)].astype(F32)).astype(BF16)
    for j in range(hps):
        ot = acc_scr[j, :HEAD_W] * (1.0 / acc_scr[j, HEAD_W:HEAD_W + 1])
        oa = (ot[:, :tq] - lam * ot[:, tq:]).T
        oa = (_lane_rms(oa) * sg_ref[0:1, :]) * sg_ref[1:2, :]
        oa_ref[0, :, lanes(j)] = (oa * za_ref[0, :, lanes(j)].astype(F32)).astype(BF16)


def _mixer_prompt(scal, qat, kab, vat, za, qb, kb, vb, zb, dbias, kbias, dmat, xi, zeta, sg):
    b, t, _ = kab.shape
    tq = Q_TILE
    hps = HEADS_PER_STEP
    gw = hps * HEAD_W
    assert t % tq == 0 and tq % CHUNK == 0 and H_A % hps == 0 and H_A == H_B
    tile = pl.BlockSpec((1, tq, gw), lambda bi, gi, qi: (bi, qi, gi))
    tile_t = pl.BlockSpec((1, gw, tq), lambda bi, gi, qi: (bi, gi, qi))
    whole = pl.BlockSpec((1, t, gw), lambda bi, gi, qi: (bi, 0, gi))
    whole_t = pl.BlockSpec((1, gw, t), lambda bi, gi, qi: (bi, gi, 0))
    per_group = lambda a: pl.BlockSpec((hps,) + a.shape[1:], lambda bi, gi, qi: (gi,) + (0,) * (a.ndim - 1))
    return pl.pallas_call(
        functools.partial(_mixer_prompt_kernel, tq=tq, hps=hps),
        grid=(b, H_A // hps, t // tq),
        in_specs=[pl.BlockSpec(memory_space=pltpu.SMEM),
                  tile_t, whole, whole_t, tile, tile, tile, tile, tile,
                  per_group(dbias), per_group(kbias), per_group(dmat), per_group(xi), per_group(zeta),
                  pl.BlockSpec(sg.shape, lambda bi, gi, qi: (0, 0))],
        out_specs=[tile, tile,
                   pl.BlockSpec((1, hps, DK_B, DV_B), lambda bi, gi, qi: (bi, gi, 0, 0))],
        out_shape=[jax.ShapeDtypeStruct((b, t, H_A * HEAD_W), BF16),
                   jax.ShapeDtypeStruct((b, t, H_B * HEAD_W), BF16),
                   jax.ShapeDtypeStruct((b, H_B, DK_B, DV_B), F32)],
        scratch_shapes=[pltpu.VMEM((hps, DK_B, DV_B), F32),
                        pltpu.VMEM((hps, HEAD_W + SUM_ROWS, 2 * tq), F32)]
                       + [pltpu.VMEM((2, tq, 2 * tq), F32)] * hps,

        compiler_params=pltpu.CompilerParams(
            dimension_semantics=("arbitrary", "arbitrary", "arbitrary"), vmem_limit_bytes=VMEM_LIMIT),
        name="mixer_prompt",
    )(scal, qat, kab, vat, za, qb, kb, vb, zb, dbias, kbias, dmat, xi, zeta, sg)


def _mixer_sample_kernel(scal_ref, qa_ref, kc_ref, vc_ref, kn_ref, vn_ref, za_ref,
                         qb_ref, kb_ref, vb_ref, zb_ref, r_ref,
                         abias_ref, dmat_ref, xi_ref, zeta_ref, sg_ref,
                         oa_ref, ob_ref, rout_ref, m_scr, acc_scr, *, n, past, tk, sub):
    kbi = pl.program_id(1)
    lam = scal_ref[0]
    lanes = lambda j: slice(j * HEAD_W, (j + 1) * HEAD_W)

    @pl.when(kbi == 0)
    def _():
        m_scr[...] = jnp.full(m_scr.shape, -jnp.inf, F32)
        acc_scr[...] = jnp.zeros_like(acc_scr)

    col = lax.broadcasted_iota(jnp.int32, (1, sub), 1)
    qs = [_split_halves(qa_ref[0, :, lanes(j)]) for j in range(H_A)]

    def scores(j, c):
        kk = kc_ref[0, lanes(j), c * sub:(c + 1) * sub].astype(BF16)
        kpos = (col + (kbi * tk + (c * sub - past))).astype(F32)
        return _dot(qs[j], kk) + kpos * scal_ref[1 + j]

    blocks = [(j, c) for c in range(tk // sub) for j in range(H_A)]
    carry = [(m_scr[j], acc_scr[j]) for j in range(H_A)]
    staged = [scores(*blk) for blk in blocks[:SCORE_LOOKAHEAD]]
    for i, (j, c) in enumerate(blocks):
        if i + SCORE_LOOKAHEAD < len(blocks):
            staged.append(scores(*blocks[i + SCORE_LOOKAHEAD]))
        vv = vc_ref[0, pl.ds(c * sub * H_A + j, sub, stride=H_A), :].astype(BF16)
        carry[j] = _softmax_update(carry[j], staged[i], vv)
    for j in range(H_A):
        m_scr[j], acc_scr[j] = carry[j]

    @pl.when(kbi == pl.num_programs(1) - 1)
    def _():
        s_new, ret = [], []
        for j in range(H_A):
            bias = abias_ref[j]
            s_new.append(_nt_dot(qs[j], kn_ref[0, :, lanes(j)]) + jnp.concatenate([bias, bias], axis=0))
        for j in range(H_A):
            s_ret, cross, r_new = _retention_stage1(
                qb_ref[0, :, lanes(j)], kb_ref[0, :, lanes(j)], vb_ref[0, :, lanes(j)],
                r_ref[0, j], dmat_ref[j], xi_ref[j], zeta_ref[j], scal_ref[1 + H_A + j])
            rout_ref[0, j] = r_new
            ret.append((s_ret, cross))
        for j in range(H_A):
            _, acc = _softmax_update((m_scr[j], acc_scr[j]), s_new[j], vn_ref[0, :, lanes(j)])
            oa = _diff_combine(acc, lam, n)
            oa = (_lane_rms(oa) * sg_ref[0:1, :]) * sg_ref[1:2, :]
            oa_ref[0, :, lanes(j)] = (oa * za_ref[0, :, lanes(j)].astype(F32)).astype(BF16)
        for j in range(H_A):
            ob = _retention_stage2(*ret[j], vb_ref[0, :, lanes(j)])
            ob_ref[0, :, lanes(j)] = (_lane_rms(ob) * zb_ref[0, :, lanes(j)].astype(F32)).astype(BF16)


def _mixer_sample(scal, qa, kc, vc, kab, vab, za, qb, kb, vb, zb, r0, abias, dmat, xi, zeta, sg):
    b, n, w = qa.shape
    past = kc.shape[2]
    tk, sub = SAMPLE_KEY_TILE, SAMPLE_KEY_BLOCK
    assert past % tk == 0 and tk % sub == 0 and past % CHUNK == 0 and n <= CHUNK and H_A == H_B
    tile = pl.BlockSpec((1, n, w), lambda bi, ki: (bi, 0, 0))
    cache_k = pl.BlockSpec((1, w, tk), lambda bi, ki: (bi, 0, ki))
    cache_v = pl.BlockSpec((1, tk * H_A, HEAD_W), lambda bi, ki: (bi, ki, 0))
    state = pl.BlockSpec((1, H_B, DK_B, DV_B), lambda bi, ki: (bi, 0, 0, 0))
    full = lambda a: pl.BlockSpec(a.shape, lambda bi, ki: (0,) * a.ndim)
    return pl.pallas_call(
        functools.partial(_mixer_sample_kernel, n=n, past=past, tk=tk, sub=sub),
        grid=(b, past // tk),
        in_specs=[pl.BlockSpec(memory_space=pltpu.SMEM),
                  tile, cache_k, cache_v, tile, tile, tile, tile, tile, tile, tile, state,
                  full(abias), full(dmat), full(xi), full(zeta), full(sg)],
        out_specs=[tile, tile, state],
        out_shape=[jax.ShapeDtypeStruct((b, n, H_A * HEAD_W), BF16),
                   jax.ShapeDtypeStruct((b, n, H_B * HEAD_W), BF16),
                   jax.ShapeDtypeStruct((b, H_B, DK_B, DV_B), F32)],
        scratch_shapes=[pltpu.VMEM((H_A, 2 * n, 1), F32),
                        pltpu.VMEM((H_A, 2 * n, 2 * DV_A), F32)],
        compiler_params=pltpu.CompilerParams(
            dimension_semantics=("arbitrary", "arbitrary"), vmem_limit_bytes=VMEM_LIMIT),
        name="mixer_sample",
    )(scal, qa, kc, vc, kab, vab, za, qb, kb, vb, zb, r0, abias, dmat, xi, zeta, sg)


def _proj_out_kernel(x_ref, oa_ref, ob_ref, ga_ref, gb_ref, woa_ref, wob_ref, wout_ref, y_ref):
    m = (ga_ref[...].astype(F32) * _dot(oa_ref[...], woa_ref[...])
         + gb_ref[...].astype(F32) * _dot(ob_ref[...], wob_ref[...]))
    y_ref[...] = x_ref[...] + _dot(m.astype(BF16), wout_ref[...])


def _proj_out(x2, oa, ob, ga, gb, woa, wob, wout):
    n, d = x2.shape
    tm = min(ROW_TILE, n)
    assert n % tm == 0
    row = lambda w: pl.BlockSpec((tm, w), lambda i: (i, 0))
    full = lambda a: pl.BlockSpec(a.shape, lambda i: (0,) * a.ndim)
    return pl.pallas_call(
        _proj_out_kernel,
        grid=(n // tm,),
        in_specs=[row(d), row(oa.shape[1]), row(ob.shape[1]), row(d), row(d),
                  full(woa), full(wob), full(wout)],
        out_specs=row(d),
        out_shape=jax.ShapeDtypeStruct((n, d), F32),
        compiler_params=pltpu.CompilerParams(
            dimension_semantics=("arbitrary",), vmem_limit_bytes=VMEM_LIMIT),
        name="proj_out",
    )(x2, oa, ob, ga, gb, woa, wob, wout)


def _block_tables(t, slopes, log_g):
    i = jnp.arange(t, dtype=jnp.int32)
    allowed = (i[None, :] // CHUNK) <= (i[:, None] // CHUNK)
    dist = jnp.abs(i[:, None] - i[None, :]).astype(F32)
    jf = i.astype(F32)
    abias = -slopes[:, None, None] * dist[None] + slopes[:, None, None] * jf[None, :, None]
    abias = jnp.where(allowed[None], abias, -jnp.inf)
    dmat = jnp.where(allowed[None], jnp.exp(dist[None] * log_g[:, None, None]), 0.0)
    xi = jnp.exp((jf + 1.0)[None] * log_g[:, None])
    zeta = jnp.exp((t - 1.0 - jf)[None] * log_g[:, None])
    bc = lambda a: jnp.broadcast_to(a[:, :, None], a.shape + (HEAD_W,))
    return abias, dmat, bc(xi), bc(zeta)


def _key_bias_table(t, slopes):
    kb = slopes[:, None] * jnp.arange(t, dtype=F32)[None]
    return jnp.broadcast_to(kb[:, :, None], kb.shape + (HEAD_W,))


def _layer(x_prompt, x_sample, ck, cv, r0, norm_g, w_in, b_gate, qn_g, kn_g,
           lam_q1, lam_k1, lam_q2, lam_k2, subln_g, w_oa, w_ob, w_out, lam_init):
    bp, tp, d = x_prompt.shape
    bs, ts, _ = x_sample.shape
    past = ck.shape[1]
    slopes = 2.0 ** (-8.0 * jnp.arange(1, H_A + 1, dtype=F32) / H_A)
    log_g = jnp.log(1.0 - 2.0 ** (-5.0 - jnp.arange(H_B, dtype=F32)))
    lam = (jnp.exp(jnp.sum(lam_q1 * lam_k1).astype(F32))
           - jnp.exp(jnp.sum(lam_q2 * lam_k2).astype(F32)) + lam_init)

    w_bf = w_in.astype(BF16)
    woa_bf, wob_bf, wout_bf = w_oa.astype(BF16), w_ob.astype(BF16), w_out.astype(BF16)
    g = norm_g.reshape(1, d)
    qg_t = jnp.tile(qn_g, 2 * H_A).reshape(1, H_A * HEAD_W)
    kg_t = jnp.tile(kn_g, 2 * H_A).reshape(1, H_A * HEAD_W)
    grp = jnp.arange(MXU_TILE, dtype=jnp.int32) // DK_A
    gm = (grp[:, None] == grp[None, :]).astype(BF16)
    sg = jnp.stack([subln_g, jnp.full((DV_A,), 1.0 - lam_init, F32)])

    def scal(n_block):
        return jnp.concatenate([lam.reshape(1), slopes, jnp.exp(n_block * log_g)]).astype(F32)

    (qat, kaf, kab, vaf, vat, za, qb, kb, vb, zb, ga, gb) = _proj_in(
        x_prompt, g, w_bf, b_gate, qg_t, kg_t, gm, tm=ROW_TILE, feature_major=True)
    abias, dmat, xi, zeta = _block_tables(Q_TILE, slopes, log_g)
    kbias = _key_bias_table(Q_TILE, slopes)
    dbias = jnp.swapaxes(abias, 1, 2) - kbias[:, :, :1]
    oa, ob, ret_p = _mixer_prompt(scal(float(Q_TILE)), qat, kab, vat, za, qb, kb, vb, zb,
                                  dbias, kbias, dmat, xi, zeta, sg)
    flat = lambda a: a.reshape(-1, a.shape[-1])
    y_p = _proj_out(flat(x_prompt), flat(oa), flat(ob), flat(ga), flat(gb),
                    woa_bf, wob_bf, wout_bf).reshape(bp, tp, d)
    k_p = jnp.transpose(kaf.reshape(bp, H_A, 2, DK_A, tp), (0, 4, 1, 2, 3))
    v_p = vaf.reshape(bp, tp, H_A, DV_A)

    (qa, kaf, kab, vaf, vab, za, qb, kb, vb, zb, ga, gb) = _proj_in(
        x_sample.reshape(1, bs * ts, d), g, w_bf, b_gate, qg_t, kg_t, gm,
        tm=min(ROW_TILE, bs * ts), feature_major=False)
    r3 = lambda a: a.reshape(bs, ts, a.shape[-1])
    tabs = _block_tables(ts, slopes, log_g)
    ck_t = jnp.transpose(ck, (0, 2, 3, 4, 1)).reshape(bs, H_A * HEAD_W, past)
    oa, ob, ret_s = _mixer_sample(scal(float(ts)), r3(qa), ck_t, cv.reshape(bs, past * H_A, HEAD_W),
                                  r3(kab), r3(vab), r3(za), r3(qb), r3(kb), r3(vb), r3(zb), r0, *tabs, sg)
    y_s = _proj_out(flat(x_sample), flat(oa), flat(ob), flat(ga), flat(gb),
                    woa_bf, wob_bf, wout_bf).reshape(bs, ts, d)
    k_s = kaf.reshape(bs, ts, H_A, 2, DK_A)
    v_s = vaf.reshape(bs, ts, H_A, DV_A)
    return y_p, y_s, k_p, v_p, ret_p, k_s, v_s, ret_s


def kernel(x_prompt, x_sample, cache_k_diff, cache_v_diff, state_ret, norm_g, w_in, b_gate, qn_g, kn_g,
           lam_q1, lam_k1, lam_q2, lam_k2, subln_g, w_o_diff, w_o_ret, w_out):
    depth = w_in.shape[0]
    hp, hs = x_prompt, x_sample
    outs = [[] for _ in range(6)]
    for l in range(depth):
        lam_init = 0.8 - 0.6 * math.exp(-0.3 * l)
        hp, hs, k_p, v_p, ret_p, k_s, v_s, ret_s = _layer(
            hp, hs, cache_k_diff[l], cache_v_diff[l], state_ret[l], norm_g[l], w_in[l], b_gate[l],
            qn_g[l], kn_g[l], lam_q1[l], lam_k1[l], lam_q2[l], lam_k2[l], subln_g[l],
            w_o_diff[l], w_o_ret[l], w_out[l], lam_init)
        for lst, a in zip(outs, (k_p, v_p, ret_p, k_s, v_s, ret_s)):
            lst.append(a)
    return (hp, hs) + tuple(jnp.stack(o) for o in outs)
```

```python
import functools
import math

import jax
import jax.numpy as jnp
from jax import lax
from jax.experimental import pallas as pl
from jax.experimental.pallas import tpu as pltpu

F32 = jnp.float32
BF16 = jnp.bfloat16

CHUNK = 64
H_A = 4
DK_A = 64
DV_A = 2 * DK_A
H_B = 4
DK_B = 128
DV_B = 128
HEAD_W = 128
EPS = 1e-6

MXU_TILE = 256
ROW_TILE = 512
Q_TILE = 256
SUM_ROWS = 16
SAMPLE_KEY_TILE = 1024
SAMPLE_KEY_BLOCK = 1024
SCORE_LOOKAHEAD = 4
VMEM_LIMIT = 56 * 1024 * 1024


def _nt_dot(a, b):
    return lax.dot_general(a, b, (((1,), (1,)), ((), ())), preferred_element_type=F32)


def _tn_dot(a, b):
    return lax.dot_general(a, b, (((0,), (0,)), ((), ())), preferred_element_type=F32)


def _dot(a, b):
    return jnp.dot(a, b, preferred_element_type=F32)


def _sigmoid(x):
    return 1.0 / (1.0 + jnp.exp(-x))


def _proj_in_kernel(x_ref, g_ref, w_ref, bg_ref, qg_ref, kg_ref, gm_ref,
                    qa_ref, kaf_ref, kab_ref, vaf_ref, vab_ref, za_ref,
                    qb_ref, kb_ref, vb_ref, zb_ref, ga_ref, gb_ref, *, wa, wb, d_model, feature_major):
    x = x_ref[0]
    tm = x.shape[0]
    ms = jnp.mean(x * x, axis=-1, keepdims=True)
    xn = ((x * lax.rsqrt(ms + EPS)) * g_ref[...]).astype(BF16)

    def seg(lo, width):
        return _dot(xn, w_ref[:, lo:lo + width])

    def group_norm(h, gain):
        sq = (h * h).astype(BF16)
        ss = jnp.concatenate([_dot(sq[:, c:c + MXU_TILE], gm_ref[...]) for c in range(0, wa, MXU_TILE)], axis=1)
        return (h * lax.rsqrt(ss * (1.0 / DK_A) + EPS)) * gain

    def silu(h):
        return h * _sigmoid(h)

    off = 0
    maybe_t = (lambda a: a.T) if feature_major else (lambda a: a)
    qa_ref[0] = maybe_t(group_norm(seg(off, wa), qg_ref[...]) * (DK_A ** -0.5)).astype(BF16)
    off += wa
    ka = group_norm(seg(off, wa), kg_ref[...])
    kaf_ref[0] = maybe_t(ka)
    kab_ref[0] = ka.astype(BF16)
    off += wa
    va = seg(off, wa)
    vab_ref[0] = maybe_t(va).astype(BF16)
    for h in range(H_A):
        vaf_ref[pl.ds(h, tm, stride=H_A), :] = va[:, h * HEAD_W:(h + 1) * HEAD_W]
    off += wa
    za_ref[0] = silu(seg(off, wa)).astype(BF16)
    off += wa
    qb_ref[0] = seg(off, wb).astype(BF16)
    off += wb
    kb_ref[0] = (seg(off, wb) * (DK_B ** -0.5)).astype(BF16)
    off += wb
    vb_ref[0] = seg(off, wb).astype(BF16)
    off += wb
    zb_ref[0] = silu(seg(off, wb)).astype(BF16)
    off += wb
    ga_ref[0] = _sigmoid(seg(off, d_model) + bg_ref[0:1, :]).astype(BF16)
    off += d_model
    gb_ref[0] = _sigmoid(seg(off, d_model) + bg_ref[1:2, :]).astype(BF16)


def _proj_in(x, g, w_bf, bg, qg_t, kg_t, gm, *, tm, feature_major):
    b, t, d = x.shape
    wa = H_A * HEAD_W
    wb = H_B * HEAD_W
    assert t % tm == 0
    nt = t // tm
    row = lambda w: pl.BlockSpec((1, tm, w), lambda bi, i: (bi, i, 0))
    full = lambda a: pl.BlockSpec(a.shape, lambda bi, i: (0,) * a.ndim)
    act = lambda w, dt: jax.ShapeDtypeStruct((b, t, w), dt)
    if feature_major:
        fm_shape = lambda dt: jax.ShapeDtypeStruct((b, wa, t), dt)
        fm_spec = pl.BlockSpec((1, wa, tm), lambda bi, i: (bi, 0, i))
    else:
        fm_shape = lambda dt: act(wa, dt)
        fm_spec = row(wa)
    out_shapes = [
        fm_shape(BF16),
        fm_shape(F32), act(wa, BF16),
        jax.ShapeDtypeStruct((b * t * H_A, HEAD_W), F32),
        fm_shape(BF16),
        act(wa, BF16),
        act(wb, BF16), act(wb, BF16), act(wb, BF16),
        act(wb, BF16),
        act(d, BF16), act(d, BF16),
    ]
    out_specs = [fm_spec, fm_spec, row(wa),
                 pl.BlockSpec((tm * H_A, HEAD_W), lambda bi, i: (bi * nt + i, 0)),
                 fm_spec, row(wa), row(wb), row(wb), row(wb), row(wb), row(d), row(d)]
    return pl.pallas_call(
        functools.partial(_proj_in_kernel, wa=wa, wb=wb, d_model=d, feature_major=feature_major),
        grid=(b, nt),
        in_specs=[row(d), full(g), full(w_bf), full(bg), full(qg_t), full(kg_t), full(gm)],
        out_specs=out_specs,
        out_shape=out_shapes,
        compiler_params=pltpu.CompilerParams(
            dimension_semantics=("arbitrary", "arbitrary"), vmem_limit_bytes=VMEM_LIMIT),
        name="proj_in",
    )(x, g, w_bf, bg, qg_t, kg_t, gm)


def _split_halves(q):
    lane = lax.broadcasted_iota(jnp.int32, q.shape, 1)
    zero = jnp.zeros_like(q)
    return jnp.concatenate([jnp.where(lane < DK_A, q, zero), jnp.where(lane >= DK_A, q, zero)], axis=0)


def _softmax_update(carry, s, v):
    m, acc = carry
    m_new = jnp.maximum(m, jnp.max(s, axis=-1, keepdims=True))
    p = jnp.exp(s - m_new).astype(BF16)
    v1 = jnp.concatenate([v, jnp.ones_like(v)], axis=1)
    return m_new, jnp.exp(m - m_new) * acc + _dot(p, v1)


def _diff_combine(acc, lam, t):
    dv = acc.shape[1] // 2
    o = acc[:, :dv] / acc[:, dv:]
    return o[:t] - lam * o[t:]


def _lane_rms(o):
    return o * lax.rsqrt(jnp.mean(o * o, axis=-1, keepdims=True) + EPS)


def _retention_stage1(q, k, v, r, dmat, xi, zeta, cdecay):
    s = (_nt_dot(q, k) * dmat).astype(BF16)
    cross = _dot(q, r.astype(BF16)) * xi
    kz = (k.astype(F32) * zeta).astype(BF16)
    r_new = cdecay * r + _tn_dot(kz, v)
    return s, cross, r_new


def _retention_stage2(s, cross, v):
    return _dot(s, v) + cross


def _mixer_prompt_kernel(scal_ref, qt_ref, k_ref, vt_ref, za_ref, qb_ref, kb_ref, vb_ref, zb_ref,
                         dbias_ref, kbias_ref, dmat_ref, xi_ref, zeta_ref, sg_ref,
                         x_ref, ga_ref, gb_ref, woa_ref, wob_ref, wout_ref,
                         y_ref, rout_ref, r_scr, acc_scr, oa_scr, ob_scr, merged_scr, *s_scrs,
                         tq, nq, n_tiles):
    hps = H_A
    t = pl.program_id(0)
    live = t < n_tiles
    qi = lax.rem(jnp.minimum(t, n_tiles - 1), nq)
    lam = scal_ref[0]
    lanes = lambda j: slice(j * HEAD_W, (j + 1) * HEAD_W)

    def split_halves_t(qt):
        feat = lax.broadcasted_iota(jnp.int32, qt.shape, 0)
        zero = jnp.zeros_like(qt)
        return jnp.concatenate([jnp.where(feat < DK_A, qt, zero), jnp.where(feat >= DK_A, qt, zero)], axis=1)

    qst = [split_halves_t(qt_ref[0, lanes(j), :]) for j in range(hps)]

    def stage_scores(j, kb, slot):
        k_start = pl.multiple_of(kb * tq, tq)
        s = _dot(k_ref[0, pl.ds(k_start, tq), lanes(j)], qst[j])
        kbias = kbias_ref[j]
        s = s + jnp.concatenate([kbias] * (2 * tq // HEAD_W), axis=1)
        s_scrs[j][slot] = s
        return jnp.max(s, axis=0, keepdims=True)

    ones_rows = jnp.ones((SUM_ROWS, tq), BF16)

    def update(j, m, s, s_max, shift, kb):
        k_start = pl.multiple_of(kb * tq, tq)
        m_new = jnp.maximum(m, s_max + shift)
        alpha = jnp.exp(m - m_new)
        p = jnp.exp(s - (m_new - shift))
        v1 = jnp.concatenate([vt_ref[0, lanes(j), pl.ds(k_start, tq)], ones_rows], axis=0)
        acc_scr[j] = alpha * acc_scr[j] + _dot(v1, p.astype(BF16))
        return m_new

    def body(kb, carry):
        slot = lax.rem(kb, 2)
        new = []
        for j in range(hps):
            smax_cur, m = carry[j]
            s_cur = s_scrs[j][slot]
            smax_next = stage_scores(j, kb + 1, 1 - slot)
            shift = ((kb - qi) * tq).astype(F32) * scal_ref[1 + j]
            new.append((smax_next, update(j, m, s_cur, smax_cur, shift, kb)))
        return tuple(new)

    @pl.when(t == 0)
    def _():
        oa_scr[...] = jnp.zeros_like(oa_scr)
        ob_scr[...] = jnp.zeros_like(ob_scr)

    @pl.when(qi == 0)
    def _():
        r_scr[...] = jnp.zeros_like(r_scr)

    acc_scr[...] = jnp.zeros_like(acc_scr)
    init = tuple((stage_scores(j, 0, 0), jnp.full((1, 2 * tq), -jnp.inf, F32)) for j in range(hps))
    merged_scr[...] = (ga_ref[...].astype(F32) * _dot(oa_scr[...], woa_ref[...])
                       + gb_ref[...].astype(F32) * _dot(ob_scr[...], wob_ref[...])).astype(BF16)
    carry = lax.fori_loop(0, jnp.where(live, qi, 0), body, init)

    ret = []
    for j in range(hps):
        s_ret, cross, r_new = _retention_stage1(
            qb_ref[0, :, lanes(j)], kb_ref[0, :, lanes(j)], vb_ref[0, :, lanes(j)],
            r_scr[j], dmat_ref[j], xi_ref[j], zeta_ref[j], scal_ref[1 + H_A + j])
        r_new = jnp.where(live, r_new, r_scr[j])
        r_scr[j] = r_new
        rout_ref[0, j] = r_new
        ret.append((s_ret, cross))
    for j in range(hps):
        dbias = dbias_ref[j]
        s_diag = s_scrs[j][lax.rem(qi, 2)] + jnp.concatenate([dbias, dbias], axis=1)
        update(j, carry[j][1], s_diag, jnp.max(s_diag, axis=0, keepdims=True), 0.0, qi)
    for j in range(hps):
        ob = _retention_stage2(*ret[j], vb_ref[0, :, lanes(j)])
        ob_scr[:, lanes(j)] = (_lane_rms(ob) * zb_ref[0, :, lanes(j)].astype(F32)).astype(BF16)
    y_ref[...] = x_ref[...] + _dot(merged_scr[...], wout_ref[...])
    for j in range(hps):
        ot = acc_scr[j, :HEAD_W] * (1.0 / acc_scr[j, HEAD_W:HEAD_W + 1])
        oa = (ot[:, :tq] - lam * ot[:, tq:]).T
        oa = (_lane_rms(oa) * sg_ref[0:1, :]) * sg_ref[1:2, :]
        oa_scr[:, lanes(j)] = (oa * za_ref[0, :, lanes(j)].astype(F32)).astype(BF16)


def _mixer_prompt(scal, qat, kab, vat, za, qb, kb, vb, zb, dbias, kbias, dmat, xi, zeta, sg,
                  x2, ga2, gb2, woa, wob, wout):
    b, t, w = kab.shape
    d = x2.shape[1]
    tq = Q_TILE
    assert t % tq == 0 and tq % CHUNK == 0 and H_A == H_B and w == H_A * HEAD_W
    nq = t // tq
    n_tiles = b * nq
    cur = lambda s: jnp.minimum(s, n_tiles - 1)
    prev = lambda s: jnp.maximum(s - 1, 0)
    tile = pl.BlockSpec((1, tq, w), lambda s: (cur(s) // nq, cur(s) % nq, 0))
    tile_t = pl.BlockSpec((1, w, tq), lambda s: (cur(s) // nq, 0, cur(s) % nq))
    whole = pl.BlockSpec((1, t, w), lambda s: (cur(s) // nq, 0, 0))
    whole_t = pl.BlockSpec((1, w, t), lambda s: (cur(s) // nq, 0, 0))
    rows = pl.BlockSpec((tq, d), lambda s: (prev(s), 0))
    full = lambda a: pl.BlockSpec(a.shape, lambda s: (0,) * a.ndim)
    return pl.pallas_call(
        functools.partial(_mixer_prompt_kernel, tq=tq, nq=nq, n_tiles=n_tiles),
        grid=(n_tiles + 1,),
        in_specs=[pl.BlockSpec(memory_space=pltpu.SMEM),
                  tile_t, whole, whole_t, tile, tile, tile, tile, tile,
                  full(dbias), full(kbias), full(dmat), full(xi), full(zeta), full(sg),
                  rows, rows, rows, full(woa), full(wob), full(wout)],
        out_specs=[rows,
                   pl.BlockSpec((1, H_B, DK_B, DV_B), lambda s: (cur(s) // nq, 0, 0, 0))],
        out_shape=[jax.ShapeDtypeStruct((b * t, d), F32),
                   jax.ShapeDtypeStruct((b, H_B, DK_B, DV_B), F32)],
        scratch_shapes=[pltpu.VMEM((H_B, DK_B, DV_B), F32),
                        pltpu.VMEM((H_A, HEAD_W + SUM_ROWS, 2 * tq), F32),
                        pltpu.VMEM((tq, w), BF16), pltpu.VMEM((tq, w), BF16),
                        pltpu.VMEM((tq, d), BF16)]
                       + [pltpu.VMEM((2, tq, 2 * tq), F32)] * H_A,
        compiler_params=pltpu.CompilerParams(
            dimension_semantics=("arbitrary",), vmem_limit_bytes=VMEM_LIMIT),
        name="mixer_prompt",
    )(scal, qat, kab, vat, za, qb, kb, vb, zb, dbias, kbias, dmat, xi, zeta, sg,
      x2, ga2, gb2, woa, wob, wout)


def _mixer_sample_kernel(scal_ref, qa_ref, kc_ref, vc_ref, kn_ref, vn_ref, za_ref,
                         qb_ref, kb_ref, vb_ref, zb_ref, r_ref,
                         abias_ref, dmat_ref, xi_ref, zeta_ref, sg_ref,
                         oa_ref, ob_ref, rout_ref, m_scr, acc_scr, *, n, past, tk, sub):
    kbi = pl.program_id(1)
    lam = scal_ref[0]
    lanes = lambda j: slice(j * HEAD_W, (j + 1) * HEAD_W)

    @pl.when(kbi == 0)
    def _():
        m_scr[...] = jnp.full(m_scr.shape, -jnp.inf, F32)
        acc_scr[...] = jnp.zeros_like(acc_scr)

    col = lax.broadcasted_iota(jnp.int32, (1, sub), 1)
    qs = [_split_halves(qa_ref[0, :, lanes(j)]) for j in range(H_A)]

    def scores(j, c):
        kk = kc_ref[0, lanes(j), c * sub:(c + 1) * sub].astype(BF16)
        kpos = (col + (kbi * tk + (c * sub - past))).astype(F32)
        return _dot(qs[j], kk) + kpos * scal_ref[1 + j]

    blocks = [(j, c) for c in range(tk // sub) for j in range(H_A)]
    carry = [(m_scr[j], acc_scr[j]) for j in range(H_A)]
    staged = [scores(*blk) for blk in blocks[:SCORE_LOOKAHEAD]]
    for i, (j, c) in enumerate(blocks):
        if i + SCORE_LOOKAHEAD < len(blocks):
            staged.append(scores(*blocks[i + SCORE_LOOKAHEAD]))
        vv = vc_ref[0, pl.ds(c * sub * H_A + j, sub, stride=H_A), :].astype(BF16)
        carry[j] = _softmax_update(carry[j], staged[i], vv)
    for j in range(H_A):
        m_scr[j], acc_scr[j] = carry[j]

    @pl.when(kbi == pl.num_programs(1) - 1)
    def _():
        s_new, ret = [], []
        for j in range(H_A):
            bias = abias_ref[j]
            s_new.append(_nt_dot(qs[j], kn_ref[0, :, lanes(j)]) + jnp.concatenate([bias, bias], axis=0))
        for j in range(H_A):
            s_ret, cross, r_new = _retention_stage1(
                qb_ref[0, :, lanes(j)], kb_ref[0, :, lanes(j)], vb_ref[0, :, lanes(j)],
                r_ref[0, j], dmat_ref[j], xi_ref[j], zeta_ref[j], scal_ref[1 + H_A + j])
            rout_ref[0, j] = r_new
            ret.append((s_ret, cross))
        for j in range(H_A):
            _, acc = _softmax_update((m_scr[j], acc_scr[j]), s_new[j], vn_ref[0, :, lanes(j)])
            oa = _diff_combine(acc, lam, n)
            oa = (_lane_rms(oa) * sg_ref[0:1, :]) * sg_ref[1:2, :]
            oa_ref[0, :, lanes(j)] = (oa * za_ref[0, :, lanes(j)].astype(F32)).astype(BF16)
        for j in range(H_A):
            ob = _retention_stage2(*ret[j], vb_ref[0, :, lanes(j)])
            ob_ref[0, :, lanes(j)] = (_lane_rms(ob) * zb_ref[0, :, lanes(j)].astype(F32)).astype(BF16)


def _mixer_sample(scal, qa, kc, vc, kab, vab, za, qb, kb, vb, zb, r0, abias, dmat, xi, zeta, sg):
    b, n, w = qa.shape
    past = kc.shape[2]
    tk, sub = SAMPLE_KEY_TILE, SAMPLE_KEY_BLOCK
    assert past % tk == 0 and tk % sub == 0 and past % CHUNK == 0 and n <= CHUNK and H_A == H_B
    tile = pl.BlockSpec((1, n, w), lambda bi, ki: (bi, 0, 0))
    cache_k = pl.BlockSpec((1, w, tk), lambda bi, ki: (bi, 0, ki))
    cache_v = pl.BlockSpec((1, tk * H_A, HEAD_W), lambda bi, ki: (bi, ki, 0))
    state = pl.BlockSpec((1, H_B, DK_B, DV_B), lambda bi, ki: (bi, 0, 0, 0))
    full = lambda a: pl.BlockSpec(a.shape, lambda bi, ki: (0,) * a.ndim)
    return pl.pallas_call(
        functools.partial(_mixer_sample_kernel, n=n, past=past, tk=tk, sub=sub),
        grid=(b, past // tk),
        in_specs=[pl.BlockSpec(memory_space=pltpu.SMEM),
                  tile, cache_k, cache_v, tile, tile, tile, tile, tile, tile, tile, state,
                  full(abias), full(dmat), full(xi), full(zeta), full(sg)],
        out_specs=[tile, tile, state],
        out_shape=[jax.ShapeDtypeStruct((b, n, H_A * HEAD_W), BF16),
                   jax.ShapeDtypeStruct((b, n, H_B * HEAD_W), BF16),
                   jax.ShapeDtypeStruct((b, H_B, DK_B, DV_B), F32)],
        scratch_shapes=[pltpu.VMEM((H_A, 2 * n, 1), F32),
                        pltpu.VMEM((H_A, 2 * n, 2 * DV_A), F32)],
        compiler_params=pltpu.CompilerParams(
            dimension_semantics=("arbitrary", "arbitrary"), vmem_limit_bytes=VMEM_LIMIT),
        name="mixer_sample",
    )(scal, qa, kc, vc, kab, vab, za, qb, kb, vb, zb, r0, abias, dmat, xi, zeta, sg)


def _proj_out_kernel(x_ref, oa_ref, ob_ref, ga_ref, gb_ref, woa_ref, wob_ref, wout_ref, y_ref):
    m = (ga_ref[...].astype(F32) * _dot(oa_ref[...], woa_ref[...])
         + gb_ref[...].astype(F32) * _dot(ob_ref[...], wob_ref[...]))
    y_ref[...] = x_ref[...] + _dot(m.astype(BF16), wout_ref[...])


def _proj_out(x2, oa, ob, ga, gb, woa, wob, wout):
    n, d = x2.shape
    tm = min(ROW_TILE, n)
    assert n % tm == 0
    row = lambda w: pl.BlockSpec((tm, w), lambda i: (i, 0))
    full = lambda a: pl.BlockSpec(a.shape, lambda i: (0,) * a.ndim)
    return pl.pallas_call(
        _proj_out_kernel,
        grid=(n // tm,),
        in_specs=[row(d), row(oa.shape[1]), row(ob.shape[1]), row(d), row(d),
                  full(woa), full(wob), full(wout)],
        out_specs=row(d),
        out_shape=jax.ShapeDtypeStruct((n, d), F32),
        compiler_params=pltpu.CompilerParams(
            dimension_semantics=("arbitrary",), vmem_limit_bytes=VMEM_LIMIT),
        name="proj_out",
    )(x2, oa, ob, ga, gb, woa, wob, wout)


def _block_tables(t, slopes, log_g):
    i = jnp.arange(t, dtype=jnp.int32)
    allowed = (i[None, :] // CHUNK) <= (i[:, None] // CHUNK)
    dist = jnp.abs(i[:, None] - i[None, :]).astype(F32)
    jf = i.astype(F32)
    abias = -slopes[:, None, None] * dist[None] + slopes[:, None, None] * jf[None, :, None]
    abias = jnp.where(allowed[None], abias, -jnp.inf)
    dmat = jnp.where(allowed[None], jnp.exp(dist[None] * log_g[:, None, None]), 0.0)
    xi = jnp.exp((jf + 1.0)[None] * log_g[:, None])
    zeta = jnp.exp((t - 1.0 - jf)[None] * log_g[:, None])
    bc = lambda a: jnp.broadcast_to(a[:, :, None], a.shape + (HEAD_W,))
    return abias, dmat, bc(xi), bc(zeta)


def _key_bias_table(t, slopes):
    kb = slopes[:, None] * jnp.arange(t, dtype=F32)[None]
    return jnp.broadcast_to(kb[:, :, None], kb.shape + (HEAD_W,))


def _layer(x_prompt, x_sample, ck, cv, r0, norm_g, w_in, b_gate, qn_g, kn_g,
           lam_q1, lam_k1, lam_q2, lam_k2, subln_g, w_oa, w_ob, w_out, lam_init):
    bp, tp, d = x_prompt.shape
    bs, ts, _ = x_sample.shape
    past = ck.shape[1]
    slopes = 2.0 ** (-8.0 * jnp.arange(1, H_A + 1, dtype=F32) / H_A)
    log_g = jnp.log(1.0 - 2.0 ** (-5.0 - jnp.arange(H_B, dtype=F32)))
    lam = (jnp.exp(jnp.sum(lam_q1 * lam_k1).astype(F32))
           - jnp.exp(jnp.sum(lam_q2 * lam_k2).astype(F32)) + lam_init)

    w_bf = w_in.astype(BF16)
    woa_bf, wob_bf, wout_bf = w_oa.astype(BF16), w_ob.astype(BF16), w_out.astype(BF16)
    g = norm_g.reshape(1, d)
    qg_t = jnp.tile(qn_g, 2 * H_A).reshape(1, H_A * HEAD_W)
    kg_t = jnp.tile(kn_g, 2 * H_A).reshape(1, H_A * HEAD_W)
    grp = jnp.arange(MXU_TILE, dtype=jnp.int32) // DK_A
    gm = (grp[:, None] == grp[None, :]).astype(BF16)
    sg = jnp.stack([subln_g, jnp.full((DV_A,), 1.0 - lam_init, F32)])

    def scal(n_block):
        return jnp.concatenate([lam.reshape(1), slopes, jnp.exp(n_block * log_g)]).astype(F32)

    (qat, kaf, kab, vaf, vat, za, qb, kb, vb, zb, ga, gb) = _proj_in(
        x_prompt, g, w_bf, b_gate, qg_t, kg_t, gm, tm=ROW_TILE, feature_major=True)
    abias, dmat, xi, zeta = _block_tables(Q_TILE, slopes, log_g)
    kbias = _key_bias_table(Q_TILE, slopes)
    dbias = jnp.swapaxes(abias, 1, 2) - kbias[:, :, :1]
    flat = lambda a: a.reshape(-1, a.shape[-1])
    y_p, ret_p = _mixer_prompt(scal(float(Q_TILE)), qat, kab, vat, za, qb, kb, vb, zb,
                               dbias, kbias, dmat, xi, zeta, sg,
                               flat(x_prompt), flat(ga), flat(gb), woa_bf, wob_bf, wout_bf)
    y_p = y_p.reshape(bp, tp, d)
    k_p = jnp.transpose(kaf.reshape(bp, H_A, 2, DK_A, tp), (0, 4, 1, 2, 3))
    v_p = vaf.reshape(bp, tp, H_A, DV_A)

    (qa, kaf, kab, vaf, vab, za, qb, kb, vb, zb, ga, gb) = _proj_in(
        x_sample.reshape(1, bs * ts, d), g, w_bf, b_gate, qg_t, kg_t, gm,
        tm=min(ROW_TILE, bs * ts), feature_major=False)
    r3 = lambda a: a.reshape(bs, ts, a.shape[-1])
    tabs = _block_tables(ts, slopes, log_g)
    ck_t = jnp.transpose(ck, (0, 2, 3, 4, 1)).reshape(bs, H_A * HEAD_W, past)
    oa, ob, ret_s = _mixer_sample(scal(float(ts)), r3(qa), ck_t, cv.reshape(bs, past * H_A, HEAD_W),
                                  r3(kab), r3(vab), r3(za), r3(qb), r3(kb), r3(vb), r3(zb), r0, *tabs, sg)
    y_s = _proj_out(flat(x_sample), flat(oa), flat(ob), flat(ga), flat(gb),
                    woa_bf, wob_bf, wout_bf).reshape(bs, ts, d)
    k_s = kaf.reshape(bs, ts, H_A, 2, DK_A)
    v_s = vaf.reshape(bs, ts, H_A, DV_A)
    return y_p, y_s, k_p, v_p, ret_p, k_s, v_s, ret_s


def kernel(x_prompt, x_sample, cache_k_diff, cache_v_diff, state_ret, norm_g, w_in, b_gate, qn_g, kn_g,
           lam_q1, lam_k1, lam_q2, lam_k2, subln_g, w_o_diff, w_o_ret, w_out):
    depth = w_in.shape[0]
    hp, hs = x_prompt, x_sample
    outs = [[] for _ in range(6)]
    for l in range(depth):
        lam_init = 0.8 - 0.6 * math.exp(-0.3 * l)
        hp, hs, k_p, v_p, ret_p, k_s, v_s, ret_s = _layer(
            hp, hs, cache_k_diff[l], cache_v_diff[l], state_ret[l], norm_g[l], w_in[l], b_gate[l],
            qn_g[l], kn_g[l], lam_q1[l], lam_k1[l], lam_q2[l], lam_k2[l], subln_g[l],
            w_o_diff[l], w_o_ret[l], w_out[l], lam_init)
        for lst, a in zip(outs, (k_p, v_p, ret_p, k_s, v_s, ret_s)):
            lst.append(a)
    return (hp, hs) + tuple(jnp.stack(o) for o in outs)
```

```python
import functools
import math

import jax
import jax.numpy as jnp
from jax import lax
from jax.experimental import pallas as pl
from jax.experimental.pallas import tpu as pltpu

F32 = jnp.float32
BF16 = jnp.bfloat16

CHUNK = 64
H_A = 4
DK_A = 64
DV_A = 2 * DK_A
H_B = 4
DK_B = 128
DV_B = 128
HEAD_W = 128
EPS = 1e-6

MXU_TILE = 256
ROW_TILE = 512
Q_TILE = 256
SUM_ROWS = 16
SAMPLE_KEY_TILE = 2048
SAMPLE_KEY_BLOCK = 1024
SCORE_LOOKAHEAD = 4
VMEM_LIMIT = 56 * 1024 * 1024


def _nt_dot(a, b):
    return lax.dot_general(a, b, (((1,), (1,)), ((), ())), preferred_element_type=F32)


def _tn_dot(a, b):
    return lax.dot_general(a, b, (((0,), (0,)), ((), ())), preferred_element_type=F32)


def _dot(a, b):
    return jnp.dot(a, b, preferred_element_type=F32)


def _sigmoid(x):
    return 1.0 / (1.0 + jnp.exp(-x))


def _proj_in_kernel(x_ref, g_ref, w_ref, bg_ref, qg_ref, kg_ref, gm_ref,
                    qa_ref, kaf_ref, kab_ref, vaf_ref, vab_ref, za_ref,
                    qb_ref, kb_ref, vb_ref, zb_ref, ga_ref, gb_ref, *, wa, wb, d_model, feature_major):
    x = x_ref[0]
    tm = x.shape[0]
    ms = jnp.mean(x * x, axis=-1, keepdims=True)
    xn = (x * lax.rsqrt(ms + EPS)) * g_ref[...]

    names = ("qa", "ka", "va", "za", "qb", "kb", "vb", "zb", "ga", "gb")
    widths = (wa, wa, wa, wa, wb, wb, wb, wb, d_model, d_model)
    offs = dict(zip(names, (sum(widths[:i]) for i in range(len(widths)))))
    width = dict(zip(names, widths))

    def seg(name):
        return _dot(xn, w_ref[:, offs[name]:offs[name] + width[name]])

    def group_norm(h, gain):
        sq = (h * h).astype(BF16)
        ss = jnp.concatenate([_dot(sq[:, c:c + MXU_TILE], gm_ref[...]) for c in range(0, wa, MXU_TILE)], axis=1)
        return (h * lax.rsqrt(ss * (1.0 / DK_A) + EPS)) * gain

    def silu(h):
        return h * _sigmoid(h)

    maybe_t = (lambda a: a.T) if feature_major else (lambda a: a)
    ga_ref[0] = _sigmoid(seg("ga") + bg_ref[0:1, :]).astype(BF16)
    gb_ref[0] = _sigmoid(seg("gb") + bg_ref[1:2, :]).astype(BF16)
    qa_ref[0] = maybe_t(group_norm(seg("qa"), qg_ref[...]) * (DK_A ** -0.5)).astype(BF16)
    ka = group_norm(seg("ka"), kg_ref[...])
    kaf_ref[0] = maybe_t(ka)
    kab_ref[0] = ka.astype(BF16)
    va = seg("va")
    vab_ref[0] = maybe_t(va).astype(BF16)
    for h in range(H_A):
        vaf_ref[pl.ds(h, tm, stride=H_A), :] = va[:, h * HEAD_W:(h + 1) * HEAD_W]
    za_ref[0] = silu(seg("za")).astype(BF16)
    zb_ref[0] = silu(seg("zb")).astype(BF16)
    kb_ref[0] = (seg("kb") * (DK_B ** -0.5)).astype(BF16)
    qb_ref[0] = seg("qb").astype(BF16)
    vb_ref[0] = seg("vb").astype(BF16)


def _proj_in(x, g, w_in, bg, qg_t, kg_t, gm, *, tm, feature_major):
    b, t, d = x.shape
    wa = H_A * HEAD_W
    wb = H_B * HEAD_W
    assert t % tm == 0
    nt = t // tm
    row = lambda w: pl.BlockSpec((1, tm, w), lambda bi, i: (bi, i, 0))
    full = lambda a: pl.BlockSpec(a.shape, lambda bi, i: (0,) * a.ndim)
    act = lambda w, dt: jax.ShapeDtypeStruct((b, t, w), dt)
    if feature_major:
        fm_shape = lambda dt: jax.ShapeDtypeStruct((b, wa, t), dt)
        fm_spec = pl.BlockSpec((1, wa, tm), lambda bi, i: (bi, 0, i))
    else:
        fm_shape = lambda dt: act(wa, dt)
        fm_spec = row(wa)
    out_shapes = [
        fm_shape(BF16),
        fm_shape(F32), act(wa, BF16),
        jax.ShapeDtypeStruct((b * t * H_A, HEAD_W), F32),
        fm_shape(BF16),
        act(wa, BF16),
        act(wb, BF16), act(wb, BF16), act(wb, BF16),
        act(wb, BF16),
        act(d, BF16), act(d, BF16),
    ]
    out_specs = [fm_spec, fm_spec, row(wa),
                 pl.BlockSpec((tm * H_A, HEAD_W), lambda bi, i: (bi * nt + i, 0)),
                 fm_spec, row(wa), row(wb), row(wb), row(wb), row(wb), row(d), row(d)]
    return pl.pallas_call(
        functools.partial(_proj_in_kernel, wa=wa, wb=wb, d_model=d, feature_major=feature_major),
        grid=(b, nt),
        in_specs=[row(d), full(g), full(w_in), full(bg), full(qg_t), full(kg_t), full(gm)],
        out_specs=out_specs,
        out_shape=out_shapes,
        compiler_params=pltpu.CompilerParams(
            dimension_semantics=("arbitrary", "arbitrary"), vmem_limit_bytes=VMEM_LIMIT),
        name="proj_in",
    )(x, g, w_in, bg, qg_t, kg_t, gm)


def _split_halves(q):
    lane = lax.broadcasted_iota(jnp.int32, q.shape, 1)
    zero = jnp.zeros_like(q)
    return jnp.concatenate([jnp.where(lane < DK_A, q, zero), jnp.where(lane >= DK_A, q, zero)], axis=0)


def _softmax_update(carry, s, v):
    m, acc = carry
    m_new = jnp.maximum(m, jnp.max(s, axis=-1, keepdims=True))
    p = jnp.exp(s - m_new).astype(BF16)
    v1 = jnp.concatenate([v, jnp.ones_like(v)], axis=1)
    return m_new, jnp.exp(m - m_new) * acc + _dot(p, v1)


def _diff_combine(acc, lam, t):
    dv = acc.shape[1] // 2
    o = acc[:, :dv] / acc[:, dv:]
    return o[:t] - lam * o[t:]


def _lane_rms(o):
    return o * lax.rsqrt(jnp.mean(o * o, axis=-1, keepdims=True) + EPS)


def _retention_stage1(q, k, v, r, dmat, xi, zeta, cdecay):
    s = (_nt_dot(q, k) * dmat).astype(BF16)
    cross = _dot(q, r.astype(BF16)) * xi
    kz = (k.astype(F32) * zeta).astype(BF16)
    r_new = cdecay * r + _tn_dot(kz, v)
    return s, cross, r_new


def _retention_stage2(s, cross, v):
    return _dot(s, v) + cross


def _mixer_prompt_kernel(scal_ref, qt_ref, k_ref, vt_ref, za_ref, qb_ref, kb_ref, vb_ref, zb_ref,
                         dbias_ref, kbias_ref, dmat_ref, xi_ref, zeta_ref, sg_ref,
                         x_ref, ga_ref, gb_ref, woa_ref, wob_ref, wout_ref,
                         y_ref, rout_ref, r_scr, acc_scr, oa_scr, ob_scr, merged_scr, *s_scrs,
                         tq, nq, n_tiles):
    hps = H_A
    t = pl.program_id(0)
    live = t < n_tiles
    qi = lax.rem(jnp.minimum(t, n_tiles - 1), nq)
    lam = scal_ref[0]
    lanes = lambda j: slice(j * HEAD_W, (j + 1) * HEAD_W)

    def split_halves_t(qt):
        feat = lax.broadcasted_iota(jnp.int32, qt.shape, 0)
        zero = jnp.zeros_like(qt)
        return jnp.concatenate([jnp.where(feat < DK_A, qt, zero), jnp.where(feat >= DK_A, qt, zero)], axis=1)

    qst = [split_halves_t(qt_ref[0, lanes(j), :]) for j in range(hps)]

    def stage_scores(j, kb, slot):
        k_start = pl.multiple_of(kb * tq, tq)
        s = _dot(k_ref[0, pl.ds(k_start, tq), lanes(j)], qst[j])
        kbias = kbias_ref[j]
        s = s + jnp.concatenate([kbias] * (2 * tq // HEAD_W), axis=1)
        s_scrs[j][slot] = s
        return jnp.max(s, axis=0, keepdims=True)

    ones_rows = jnp.ones((SUM_ROWS, tq), BF16)

    def update(j, m, s, s_max, shift, kb):
        k_start = pl.multiple_of(kb * tq, tq)
        m_new = jnp.maximum(m, s_max + shift)
        alpha = jnp.exp(m - m_new)
        p = jnp.exp(s - (m_new - shift))
        v1 = jnp.concatenate([vt_ref[0, lanes(j), pl.ds(k_start, tq)], ones_rows], axis=0)
        acc_scr[j] = alpha * acc_scr[j] + _dot(v1, p.astype(BF16))
        return m_new

    def body(kb, carry):
        slot = lax.rem(kb, 2)
        new = []
        for j in range(hps):
            smax_cur, m = carry[j]
            s_cur = s_scrs[j][slot]
            smax_next = stage_scores(j, kb + 1, 1 - slot)
            shift = ((kb - qi) * tq).astype(F32) * scal_ref[1 + j]
            new.append((smax_next, update(j, m, s_cur, smax_cur, shift, kb)))
        return tuple(new)

    @pl.when(t == 0)
    def _():
        oa_scr[...] = jnp.zeros_like(oa_scr)
        ob_scr[...] = jnp.zeros_like(ob_scr)

    @pl.when(qi == 0)
    def _():
        r_scr[...] = jnp.zeros_like(r_scr)

    acc_scr[...] = jnp.zeros_like(acc_scr)
    init = tuple((stage_scores(j, 0, 0), jnp.full((1, 2 * tq), -jnp.inf, F32)) for j in range(hps))
    merged_scr[...] = (ga_ref[...].astype(F32) * _dot(oa_scr[...], woa_ref[...])
                       + gb_ref[...].astype(F32) * _dot(ob_scr[...], wob_ref[...])).astype(BF16)
    carry = lax.fori_loop(0, jnp.where(live, qi, 0), body, init)

    ret = []
    for j in range(hps):
        s_ret, cross, r_new = _retention_stage1(
            qb_ref[0, :, lanes(j)], kb_ref[0, :, lanes(j)], vb_ref[0, :, lanes(j)],
            r_scr[j], dmat_ref[j], xi_ref[j], zeta_ref[j], scal_ref[1 + H_A + j])
        r_new = jnp.where(live, r_new, r_scr[j])
        r_scr[j] = r_new
        rout_ref[0, j] = r_new
        ret.append((s_ret, cross))
    for j in range(hps):
        dbias = dbias_ref[j]
        s_diag = s_scrs[j][lax.rem(qi, 2)] + jnp.concatenate([dbias, dbias], axis=1)
        update(j, carry[j][1], s_diag, jnp.max(s_diag, axis=0, keepdims=True), 0.0, qi)
    for j in range(hps):
        ob = _retention_stage2(*ret[j], vb_ref[0, :, lanes(j)])
        ob_scr[:, lanes(j)] = (_lane_rms(ob) * zb_ref[0, :, lanes(j)].astype(F32)).astype(BF16)
    y_ref[...] = x_ref[...] + _dot(merged_scr[...], wout_ref[...])
    for j in range(hps):
        ot = acc_scr[j, :HEAD_W] * (1.0 / acc_scr[j, HEAD_W:HEAD_W + 1])
        oa = (ot[:, :tq] - lam * ot[:, tq:]).T
        oa = (_lane_rms(oa) * sg_ref[0:1, :]) * sg_ref[1:2, :]
        oa_scr[:, lanes(j)] = (oa * za_ref[0, :, lanes(j)].astype(F32)).astype(BF16)


def _mixer_prompt(scal, qat, kab, vat, za, qb, kb, vb, zb, dbias, kbias, dmat, xi, zeta, sg,
                  x2, ga2, gb2, woa, wob, wout):
    b, t, w = kab.shape
    d = x2.shape[1]
    tq = Q_TILE
    assert t % tq == 0 and tq % CHUNK == 0 and H_A == H_B and w == H_A * HEAD_W
    nq = t // tq
    n_tiles = b * nq
    cur = lambda s: jnp.minimum(s, n_tiles - 1)
    prev = lambda s: jnp.maximum(s - 1, 0)
    tile = pl.BlockSpec((1, tq, w), lambda s: (cur(s) // nq, cur(s) % nq, 0))
    tile_t = pl.BlockSpec((1, w, tq), lambda s: (cur(s) // nq, 0, cur(s) % nq))
    whole = pl.BlockSpec((1, t, w), lambda s: (cur(s) // nq, 0, 0))
    whole_t = pl.BlockSpec((1, w, t), lambda s: (cur(s) // nq, 0, 0))
    rows = pl.BlockSpec((tq, d), lambda s: (prev(s), 0))
    full = lambda a: pl.BlockSpec(a.shape, lambda s: (0,) * a.ndim)
    return pl.pallas_call(
        functools.partial(_mixer_prompt_kernel, tq=tq, nq=nq, n_tiles=n_tiles),
        grid=(n_tiles + 1,),
        in_specs=[pl.BlockSpec(memory_space=pltpu.SMEM),
                  tile_t, whole, whole_t, tile, tile, tile, tile, tile,
                  full(dbias), full(kbias), full(dmat), full(xi), full(zeta), full(sg),
                  rows, rows, rows, full(woa), full(wob), full(wout)],
        out_specs=[rows,
                   pl.BlockSpec((1, H_B, DK_B, DV_B), lambda s: (cur(s) // nq, 0, 0, 0))],
        out_shape=[jax.ShapeDtypeStruct((b * t, d), F32),
                   jax.ShapeDtypeStruct((b, H_B, DK_B, DV_B), F32)],
        scratch_shapes=[pltpu.VMEM((H_B, DK_B, DV_B), F32),
                        pltpu.VMEM((H_A, HEAD_W + SUM_ROWS, 2 * tq), F32),
                        pltpu.VMEM((tq, w), BF16), pltpu.VMEM((tq, w), BF16),
                        pltpu.VMEM((tq, d), BF16)]
                       + [pltpu.VMEM((2, tq, 2 * tq), F32)] * H_A,
        compiler_params=pltpu.CompilerParams(
            dimension_semantics=("arbitrary",), vmem_limit_bytes=VMEM_LIMIT),
        name="mixer_prompt",
    )(scal, qat, kab, vat, za, qb, kb, vb, zb, dbias, kbias, dmat, xi, zeta, sg,
      x2, ga2, gb2, woa, wob, wout)


def _mixer_sample_kernel(scal_ref, qa_ref, kc_ref, vc_ref, kn_ref, vn_ref, za_ref,
                         qb_ref, kb_ref, vb_ref, zb_ref, r_ref,
                         abias_ref, dmat_ref, xi_ref, zeta_ref, sg_ref,
                         oa_ref, ob_ref, rout_ref, m_scr, acc_scr, *, n, past, tk, sub):
    kbi = pl.program_id(1)
    lam = scal_ref[0]
    lanes = lambda j: slice(j * HEAD_W, (j + 1) * HEAD_W)

    @pl.when(kbi == 0)
    def _():
        m_scr[...] = jnp.full(m_scr.shape, -jnp.inf, F32)
        acc_scr[...] = jnp.zeros_like(acc_scr)

    col = lax.broadcasted_iota(jnp.int32, (1, sub), 1)
    qs = [_split_halves(qa_ref[0, :, lanes(j)]) for j in range(H_A)]

    def scores(j, c):
        kk = kc_ref[0, lanes(j), c * sub:(c + 1) * sub].astype(BF16)
        kpos = (col + (kbi * tk + (c * sub - past))).astype(F32)
        return _dot(qs[j], kk) + kpos * scal_ref[1 + j]

    blocks = [(j, c) for c in range(tk // sub) for j in range(H_A)]
    carry = [(m_scr[j], acc_scr[j]) for j in range(H_A)]
    staged = [scores(*blk) for blk in blocks[:SCORE_LOOKAHEAD]]
    for i, (j, c) in enumerate(blocks):
        if i + SCORE_LOOKAHEAD < len(blocks):
            staged.append(scores(*blocks[i + SCORE_LOOKAHEAD]))
        vv = vc_ref[0, pl.ds(c * sub * H_A + j, sub, stride=H_A), :].astype(BF16)
        carry[j] = _softmax_update(carry[j], staged[i], vv)
    for j in range(H_A):
        m_scr[j], acc_scr[j] = carry[j]

    @pl.when(kbi == pl.num_programs(1) - 1)
    def _():
        s_new, ret = [], []
        for j in range(H_A):
            bias = abias_ref[j]
            s_new.append(_nt_dot(qs[j], kn_ref[0, :, lanes(j)]) + jnp.concatenate([bias, bias], axis=0))
        for j in range(H_A):
            s_ret, cross, r_new = _retention_stage1(
                qb_ref[0, :, lanes(j)], kb_ref[0, :, lanes(j)], vb_ref[0, :, lanes(j)],
                r_ref[0, j], dmat_ref[j], xi_ref[j], zeta_ref[j], scal_ref[1 + H_A + j])
            rout_ref[0, j] = r_new
            ret.append((s_ret, cross))
        for j in range(H_A):
            _, acc = _softmax_update((m_scr[j], acc_scr[j]), s_new[j], vn_ref[0, :, lanes(j)])
            oa = _diff_combine(acc, lam, n)
            oa = (_lane_rms(oa) * sg_ref[0:1, :]) * sg_ref[1:2, :]
            oa_ref[0, :, lanes(j)] = (oa * za_ref[0, :, lanes(j)].astype(F32)).astype(BF16)
        for j in range(H_A):
            ob = _retention_stage2(*ret[j], vb_ref[0, :, lanes(j)])
            ob_ref[0, :, lanes(j)] = (_lane_rms(ob) * zb_ref[0, :, lanes(j)].astype(F32)).astype(BF16)


def _mixer_sample(scal, qa, kc, vc, kab, vab, za, qb, kb, vb, zb, r0, abias, dmat, xi, zeta, sg):
    b, n, w = qa.shape
    past = kc.shape[2]
    tk, sub = SAMPLE_KEY_TILE, SAMPLE_KEY_BLOCK
    assert past % tk == 0 and tk % sub == 0 and past % CHUNK == 0 and n <= CHUNK and H_A == H_B
    tile = pl.BlockSpec((1, n, w), lambda bi, ki: (bi, 0, 0))
    cache_k = pl.BlockSpec((1, w, tk), lambda bi, ki: (bi, 0, ki))
    cache_v = pl.BlockSpec((1, tk * H_A, HEAD_W), lambda bi, ki: (bi, ki, 0))
    state = pl.BlockSpec((1, H_B, DK_B, DV_B), lambda bi, ki: (bi, 0, 0, 0))
    full = lambda a: pl.BlockSpec(a.shape, lambda bi, ki: (0,) * a.ndim)
    return pl.pallas_call(
        functools.partial(_mixer_sample_kernel, n=n, past=past, tk=tk, sub=sub),
        grid=(b, past // tk),
        in_specs=[pl.BlockSpec(memory_space=pltpu.SMEM),
                  tile, cache_k, cache_v, tile, tile, tile, tile, tile, tile, tile, state,
                  full(abias), full(dmat), full(xi), full(zeta), full(sg)],
        out_specs=[tile, tile, state],
        out_shape=[jax.ShapeDtypeStruct((b, n, H_A * HEAD_W), BF16),
                   jax.ShapeDtypeStruct((b, n, H_B * HEAD_W), BF16),
                   jax.ShapeDtypeStruct((b, H_B, DK_B, DV_B), F32)],
        scratch_shapes=[pltpu.VMEM((H_A, 2 * n, 1), F32),
                        pltpu.VMEM((H_A, 2 * n, 2 * DV_A), F32)],
        compiler_params=pltpu.CompilerParams(
            dimension_semantics=("arbitrary", "arbitrary"), vmem_limit_bytes=VMEM_LIMIT),
        name="mixer_sample",
    )(scal, qa, kc, vc, kab, vab, za, qb, kb, vb, zb, r0, abias, dmat, xi, zeta, sg)


def _proj_out_kernel(x_ref, oa_ref, ob_ref, ga_ref, gb_ref, woa_ref, wob_ref, wout_ref, y_ref):
    m = (ga_ref[...].astype(F32) * _dot(oa_ref[...], woa_ref[...])
         + gb_ref[...].astype(F32) * _dot(ob_ref[...], wob_ref[...]))
    y_ref[...] = x_ref[...] + _dot(m.astype(BF16), wout_ref[...])


def _proj_out(x2, oa, ob, ga, gb, woa, wob, wout):
    n, d = x2.shape
    tm = min(ROW_TILE, n)
    assert n % tm == 0
    row = lambda w: pl.BlockSpec((tm, w), lambda i: (i, 0))
    full = lambda a: pl.BlockSpec(a.shape, lambda i: (0,) * a.ndim)
    return pl.pallas_call(
        _proj_out_kernel,
        grid=(n // tm,),
        in_specs=[row(d), row(oa.shape[1]), row(ob.shape[1]), row(d), row(d),
                  full(woa), full(wob), full(wout)],
        out_specs=row(d),
        out_shape=jax.ShapeDtypeStruct((n, d), F32),
        compiler_params=pltpu.CompilerParams(
            dimension_semantics=("arbitrary",), vmem_limit_bytes=VMEM_LIMIT),
        name="proj_out",
    )(x2, oa, ob, ga, gb, woa, wob, wout)


def _block_tables(t, slopes, log_g):
    i = jnp.arange(t, dtype=jnp.int32)
    allowed = (i[None, :] // CHUNK) <= (i[:, None] // CHUNK)
    dist = jnp.abs(i[:, None] - i[None, :]).astype(F32)
    jf = i.astype(F32)
    abias = -slopes[:, None, None] * dist[None] + slopes[:, None, None] * jf[None, :, None]
    abias = jnp.where(allowed[None], abias, -jnp.inf)
    dmat = jnp.where(allowed[None], jnp.exp(dist[None] * log_g[:, None, None]), 0.0)
    xi = jnp.exp((jf + 1.0)[None] * log_g[:, None])
    zeta = jnp.exp((t - 1.0 - jf)[None] * log_g[:, None])
    bc = lambda a: jnp.broadcast_to(a[:, :, None], a.shape + (HEAD_W,))
    return abias, dmat, bc(xi), bc(zeta)


def _key_bias_table(t, slopes):
    kb = slopes[:, None] * jnp.arange(t, dtype=F32)[None]
    return jnp.broadcast_to(kb[:, :, None], kb.shape + (HEAD_W,))


def _layer(x_prompt, x_sample, ck, cv, r0, norm_g, w_in, b_gate, qn_g, kn_g,
           lam_q1, lam_k1, lam_q2, lam_k2, subln_g, w_oa, w_ob, w_out, lam_init):
    bp, tp, d = x_prompt.shape
    bs, ts, _ = x_sample.shape
    past = ck.shape[1]
    slopes = 2.0 ** (-8.0 * jnp.arange(1, H_A + 1, dtype=F32) / H_A)
    log_g = jnp.log(1.0 - 2.0 ** (-5.0 - jnp.arange(H_B, dtype=F32)))
    lam = (jnp.exp(jnp.sum(lam_q1 * lam_k1).astype(F32))
           - jnp.exp(jnp.sum(lam_q2 * lam_k2).astype(F32)) + lam_init)

    woa_bf, wob_bf, wout_bf = w_oa.astype(BF16), w_ob.astype(BF16), w_out.astype(BF16)
    g = norm_g.reshape(1, d)
    qg_t = jnp.tile(qn_g, 2 * H_A).reshape(1, H_A * HEAD_W)
    kg_t = jnp.tile(kn_g, 2 * H_A).reshape(1, H_A * HEAD_W)
    grp = jnp.arange(MXU_TILE, dtype=jnp.int32) // DK_A
    gm = (grp[:, None] == grp[None, :]).astype(BF16)
    sg = jnp.stack([subln_g, jnp.full((DV_A,), 1.0 - lam_init, F32)])

    def scal(n_block):
        return jnp.concatenate([lam.reshape(1), slopes, jnp.exp(n_block * log_g)]).astype(F32)

    (qat, kaf, kab, vaf, vat, za, qb, kb, vb, zb, ga, gb) = _proj_in(
        x_prompt, g, w_in, b_gate, qg_t, kg_t, gm, tm=ROW_TILE, feature_major=True)
    abias, dmat, xi, zeta = _block_tables(Q_TILE, slopes, log_g)
    kbias = _key_bias_table(Q_TILE, slopes)
    dbias = jnp.swapaxes(abias, 1, 2) - kbias[:, :, :1]
    flat = lambda a: a.reshape(-1, a.shape[-1])
    y_p, ret_p = _mixer_prompt(scal(float(Q_TILE)), qat, kab, vat, za, qb, kb, vb, zb,
                               dbias, kbias, dmat, xi, zeta, sg,
                               flat(x_prompt), flat(ga), flat(gb), woa_bf, wob_bf, wout_bf)
    y_p = y_p.reshape(bp, tp, d)
    k_p = jnp.transpose(kaf.reshape(bp, H_A, 2, DK_A, tp), (0, 4, 1, 2, 3))
    v_p = vaf.reshape(bp, tp, H_A, DV_A)

    (qa, kaf, kab, vaf, vab, za, qb, kb, vb, zb, ga, gb) = _proj_in(
        x_sample.reshape(1, bs * ts, d), g, w_in, b_gate, qg_t, kg_t, gm,
        tm=min(ROW_TILE, bs * ts), feature_major=False)
    r3 = lambda a: a.reshape(bs, ts, a.shape[-1])
    tabs = _block_tables(ts, slopes, log_g)
    ck_t = jnp.transpose(ck, (0, 2, 3, 4, 1)).reshape(bs, H_A * HEAD_W, past)
    oa, ob, ret_s = _mixer_sample(scal(float(ts)), r3(qa), ck_t, cv.reshape(bs, past * H_A, HEAD_W),
                                  r3(kab), r3(vab), r3(za), r3(qb), r3(kb), r3(vb), r3(zb), r0, *tabs, sg)
    y_s = _proj_out(flat(x_sample), flat(oa), flat(ob), flat(ga), flat(gb),
                    woa_bf, wob_bf, wout_bf).reshape(bs, ts, d)
    k_s = kaf.reshape(bs, ts, H_A, 2, DK_A)
    v_s = vaf.reshape(bs, ts, H_A, DV_A)
    return y_p, y_s, k_p, v_p, ret_p, k_s, v_s, ret_s


def kernel(x_prompt, x_sample, cache_k_diff, cache_v_diff, state_ret, norm_g, w_in, b_gate, qn_g, kn_g,
           lam_q1, lam_k1, lam_q2, lam_k2, subln_g, w_o_diff, w_o_ret, w_out):
    depth = w_in.shape[0]
    hp, hs = x_prompt, x_sample
    outs = [[] for _ in range(6)]
    for l in range(depth):
        lam_init = 0.8 - 0.6 * math.exp(-0.3 * l)
        hp, hs, k_p, v_p, ret_p, k_s, v_s, ret_s = _layer(
            hp, hs, cache_k_diff[l], cache_v_diff[l], state_ret[l], norm_g[l], w_in[l], b_gate[l],
            qn_g[l], kn_g[l], lam_q1[l], lam_k1[l], lam_q2[l], lam_k2[l], subln_g[l],
            w_o_diff[l], w_o_ret[l], w_out[l], lam_init)
        for lst, a in zip(outs, (k_p, v_p, ret_p, k_s, v_s, ret_s)):
            lst.append(a)
    return (hp, hs) + tuple(jnp.stack(o) for o in outs)
```

```python
import functools
import math

import jax
import jax.numpy as jnp
from jax import lax
from jax.experimental import pallas as pl
from jax.experimental.pallas import tpu as pltpu

F32 = jnp.float32
BF16 = jnp.bfloat16

CHUNK = 64
H_A = 4
DK_A = 64
DV_A = 2 * DK_A
H_B = 4
DK_B = 128
DV_B = 128
HEAD_W = 128
EPS = 1e-6

MXU_TILE = 256
ROW_TILE = 512
Q_TILE = 256
SUM_ROWS = 16
SAMPLE_KEY_TILE = 2048
SAMPLE_KEY_BLOCK = 1024
SCORE_LOOKAHEAD = 4
VMEM_LIMIT = 56 * 1024 * 1024


def _nt_dot(a, b):
    return lax.dot_general(a, b, (((1,), (1,)), ((), ())), preferred_element_type=F32)


def _tn_dot(a, b):
    return lax.dot_general(a, b, (((0,), (0,)), ((), ())), preferred_element_type=F32)


def _dot(a, b):
    return jnp.dot(a, b, preferred_element_type=F32)


def _sigmoid(x):
    return 1.0 / (1.0 + jnp.exp(-x))


MIX_SLABS = ("za", "qb", "kb", "vb", "zb")


def _slab(mix_ref, name, head=None):
    lo = MIX_SLABS.index(name) * H_A * HEAD_W
    if head is None:
        return mix_ref.at[0, :, lo:lo + H_A * HEAD_W]
    return mix_ref.at[0, :, lo + head * HEAD_W:lo + (head + 1) * HEAD_W]


def _proj_in_kernel(x_ref, g_ref, w_ref, bg_ref, qg_ref, kg_ref, gm_ref,
                    qa_ref, kaf_ref, kab_ref, vaf_ref, vab_ref, mix_ref, gate_ref,
                    *, wa, wb, d_model, feature_major):
    x = x_ref[0]
    tm = x.shape[0]
    ms = jnp.mean(x * x, axis=-1, keepdims=True)
    xn = (x * lax.rsqrt(ms + EPS)) * g_ref[...]

    names = ("qa", "ka", "va", "za", "qb", "kb", "vb", "zb", "ga", "gb")
    widths = (wa, wa, wa, wa, wb, wb, wb, wb, d_model, d_model)
    offs = dict(zip(names, (sum(widths[:i]) for i in range(len(widths)))))
    width = dict(zip(names, widths))

    def seg(name):
        return _dot(xn, w_ref[:, offs[name]:offs[name] + width[name]])

    def group_norm(h, gain):
        sq = (h * h).astype(BF16)
        ss = jnp.concatenate([_dot(sq[:, c:c + MXU_TILE], gm_ref[...]) for c in range(0, wa, MXU_TILE)], axis=1)
        return (h * lax.rsqrt(ss * (1.0 / DK_A) + EPS)) * gain

    def silu(h):
        return h * _sigmoid(h)

    maybe_t = (lambda a: a.T) if feature_major else (lambda a: a)
    gate_ref[0, :, :d_model] = _sigmoid(seg("ga") + bg_ref[0:1, :]).astype(BF16)
    gate_ref[0, :, d_model:] = _sigmoid(seg("gb") + bg_ref[1:2, :]).astype(BF16)
    qa_ref[0] = maybe_t(group_norm(seg("qa"), qg_ref[...]) * (DK_A ** -0.5)).astype(BF16)
    ka = group_norm(seg("ka"), kg_ref[...])
    kaf_ref[0] = maybe_t(ka)
    kab_ref[0] = ka.astype(BF16)
    va = seg("va")
    vab_ref[0] = maybe_t(va).astype(BF16)
    for h in range(H_A):
        vaf_ref[pl.ds(h, tm, stride=H_A), :] = va[:, h * HEAD_W:(h + 1) * HEAD_W]
    _slab(mix_ref, "za")[...] = silu(seg("za")).astype(BF16)
    _slab(mix_ref, "zb")[...] = silu(seg("zb")).astype(BF16)
    _slab(mix_ref, "kb")[...] = (seg("kb") * (DK_B ** -0.5)).astype(BF16)
    _slab(mix_ref, "qb")[...] = seg("qb").astype(BF16)
    _slab(mix_ref, "vb")[...] = seg("vb").astype(BF16)


def _proj_in(x, g, w_in, bg, qg_t, kg_t, gm, *, tm, feature_major):
    b, t, d = x.shape
    wa = H_A * HEAD_W
    wb = H_B * HEAD_W
    assert t % tm == 0
    nt = t // tm
    row = lambda w: pl.BlockSpec((1, tm, w), lambda bi, i: (bi, i, 0))
    full = lambda a: pl.BlockSpec(a.shape, lambda bi, i: (0,) * a.ndim)
    act = lambda w, dt: jax.ShapeDtypeStruct((b, t, w), dt)
    if feature_major:
        fm_shape = lambda dt: jax.ShapeDtypeStruct((b, wa, t), dt)
        fm_spec = pl.BlockSpec((1, wa, tm), lambda bi, i: (bi, 0, i))
    else:
        fm_shape = lambda dt: act(wa, dt)
        fm_spec = row(wa)
    assert wa == wb
    out_shapes = [
        fm_shape(BF16),
        fm_shape(F32), act(wa, BF16),
        jax.ShapeDtypeStruct((b * t * H_A, HEAD_W), F32),
        fm_shape(BF16),
        act(len(MIX_SLABS) * wa, BF16),
        act(2 * d, BF16),
    ]
    out_specs = [fm_spec, fm_spec, row(wa),
                 pl.BlockSpec((tm * H_A, HEAD_W), lambda bi, i: (bi * nt + i, 0)),
                 fm_spec, row(len(MIX_SLABS) * wa), row(2 * d)]
    return pl.pallas_call(
        functools.partial(_proj_in_kernel, wa=wa, wb=wb, d_model=d, feature_major=feature_major),
        grid=(b, nt),
        in_specs=[row(d), full(g), full(w_in), full(bg), full(qg_t), full(kg_t), full(gm)],
        out_specs=out_specs,
        out_shape=out_shapes,
        compiler_params=pltpu.CompilerParams(
            dimension_semantics=("arbitrary", "arbitrary"), vmem_limit_bytes=VMEM_LIMIT),
        name="proj_in",
    )(x, g, w_in, bg, qg_t, kg_t, gm)


def _split_halves(q):
    lane = lax.broadcasted_iota(jnp.int32, q.shape, 1)
    zero = jnp.zeros_like(q)
    return jnp.concatenate([jnp.where(lane < DK_A, q, zero), jnp.where(lane >= DK_A, q, zero)], axis=0)


def _softmax_update(carry, s, v):
    m, acc = carry
    m_new = jnp.maximum(m, jnp.max(s, axis=-1, keepdims=True))
    p = jnp.exp(s - m_new).astype(BF16)
    v1 = jnp.concatenate([v, jnp.ones_like(v)], axis=1)
    return m_new, jnp.exp(m - m_new) * acc + _dot(p, v1)


def _diff_combine(acc, lam, t):
    dv = acc.shape[1] // 2
    o = acc[:, :dv] / acc[:, dv:]
    return o[:t] - lam * o[t:]


def _lane_rms(o):
    return o * lax.rsqrt(jnp.mean(o * o, axis=-1, keepdims=True) + EPS)


def _retention_stage1(q, k, v, r, dmat, xi, zeta, cdecay):
    s = (_nt_dot(q, k) * dmat).astype(BF16)
    cross = _dot(q, r.astype(BF16)) * xi
    kz = (k.astype(F32) * zeta).astype(BF16)
    r_new = cdecay * r + _tn_dot(kz, v)
    return s, cross, r_new


def _retention_stage2(s, cross, v):
    return _dot(s, v) + cross


def _mixer_prompt_kernel(scal_ref, qt_ref, k_ref, vt_ref, mix_ref,
                         dbias_ref, kbias_ref, dmat_ref, xi_ref, zeta_ref, sg_ref,
                         x_ref, gate_ref, woa_ref, wob_ref, wout_ref,
                         y_ref, rout_ref, r_scr, acc_scr, oa_scr, ob_scr, merged_scr, *s_scrs,
                         tq, nq, n_tiles):
    hps = H_A
    t = pl.program_id(0)
    live = t < n_tiles
    qi = lax.rem(jnp.minimum(t, n_tiles - 1), nq)
    lam = scal_ref[0]
    lanes = lambda j: slice(j * HEAD_W, (j + 1) * HEAD_W)

    def split_halves_t(qt):
        feat = lax.broadcasted_iota(jnp.int32, qt.shape, 0)
        zero = jnp.zeros_like(qt)
        return jnp.concatenate([jnp.where(feat < DK_A, qt, zero), jnp.where(feat >= DK_A, qt, zero)], axis=1)

    qst = [split_halves_t(qt_ref[0, lanes(j), :]) for j in range(hps)]

    def stage_scores(j, kb, slot):
        k_start = pl.multiple_of(kb * tq, tq)
        s = _dot(k_ref[0, pl.ds(k_start, tq), lanes(j)], qst[j])
        kbias = kbias_ref[j]
        s = s + jnp.concatenate([kbias] * (2 * tq // HEAD_W), axis=1)
        s_scrs[j][slot] = s
        return jnp.max(s, axis=0, keepdims=True)

    ones_rows = jnp.ones((SUM_ROWS, tq), BF16)

    def update(j, m, s, s_max, shift, kb):
        k_start = pl.multiple_of(kb * tq, tq)
        m_new = jnp.maximum(m, s_max + shift)
        alpha = jnp.exp(m - m_new)
        p = jnp.exp(s - (m_new - shift))
        v1 = jnp.concatenate([vt_ref[0, lanes(j), pl.ds(k_start, tq)], ones_rows], axis=0)
        acc_scr[j] = alpha * acc_scr[j] + _dot(v1, p.astype(BF16))
        return m_new

    def body(kb, carry):
        slot = lax.rem(kb, 2)
        new = []
        for j in range(hps):
            smax_cur, m = carry[j]
            s_cur = s_scrs[j][slot]
            smax_next = stage_scores(j, kb + 1, 1 - slot)
            shift = ((kb - qi) * tq).astype(F32) * scal_ref[1 + j]
            new.append((smax_next, update(j, m, s_cur, smax_cur, shift, kb)))
        return tuple(new)

    @pl.when(t == 0)
    def _():
        oa_scr[...] = jnp.zeros_like(oa_scr)
        ob_scr[...] = jnp.zeros_like(ob_scr)

    @pl.when(qi == 0)
    def _():
        r_scr[...] = jnp.zeros_like(r_scr)

    acc_scr[...] = jnp.zeros_like(acc_scr)
    init = tuple((stage_scores(j, 0, 0), jnp.full((1, 2 * tq), -jnp.inf, F32)) for j in range(hps))
    d_model = x_ref.shape[1]
    merged_scr[...] = (gate_ref[:, :d_model].astype(F32) * _dot(oa_scr[...], woa_ref[...])
                       + gate_ref[:, d_model:].astype(F32) * _dot(ob_scr[...], wob_ref[...])).astype(BF16)
    carry = lax.fori_loop(0, jnp.where(live, qi, 0), body, init)

    ret = []
    for j in range(hps):
        s_ret, cross, r_new = _retention_stage1(
            _slab(mix_ref, "qb", j)[...], _slab(mix_ref, "kb", j)[...], _slab(mix_ref, "vb", j)[...],
            r_scr[j], dmat_ref[j], xi_ref[j], zeta_ref[j], scal_ref[1 + H_A + j])
        r_new = jnp.where(live, r_new, r_scr[j])
        r_scr[j] = r_new
        rout_ref[0, j] = r_new
        ret.append((s_ret, cross))
    for j in range(hps):
        dbias = dbias_ref[j]
        s_diag = s_scrs[j][lax.rem(qi, 2)] + jnp.concatenate([dbias, dbias], axis=1)
        update(j, carry[j][1], s_diag, jnp.max(s_diag, axis=0, keepdims=True), 0.0, qi)
    for j in range(hps):
        ob = _retention_stage2(*ret[j], _slab(mix_ref, "vb", j)[...])
        ob_scr[:, lanes(j)] = (_lane_rms(ob) * _slab(mix_ref, "zb", j)[...].astype(F32)).astype(BF16)
    y_ref[...] = x_ref[...] + _dot(merged_scr[...], wout_ref[...])
    for j in range(hps):
        ot = acc_scr[j, :HEAD_W] * (1.0 / acc_scr[j, HEAD_W:HEAD_W + 1])
        oa = (ot[:, :tq] - lam * ot[:, tq:]).T
        oa = (_lane_rms(oa) * sg_ref[0:1, :]) * sg_ref[1:2, :]
        oa_scr[:, lanes(j)] = (oa * _slab(mix_ref, "za", j)[...].astype(F32)).astype(BF16)


def _mixer_prompt(scal, qat, kab, vat, mix, dbias, kbias, dmat, xi, zeta, sg, x2, gates2, woa, wob, wout):
    b, t, w = kab.shape
    d = x2.shape[1]
    tq = Q_TILE
    assert t % tq == 0 and tq % CHUNK == 0 and H_A == H_B and w == H_A * HEAD_W
    nq = t // tq
    n_tiles = b * nq
    cur = lambda s: jnp.minimum(s, n_tiles - 1)
    prev = lambda s: jnp.maximum(s - 1, 0)
    tile = pl.BlockSpec((1, tq, mix.shape[2]), lambda s: (cur(s) // nq, cur(s) % nq, 0))
    tile_t = pl.BlockSpec((1, w, tq), lambda s: (cur(s) // nq, 0, cur(s) % nq))
    whole = pl.BlockSpec((1, t, w), lambda s: (cur(s) // nq, 0, 0))
    whole_t = pl.BlockSpec((1, w, t), lambda s: (cur(s) // nq, 0, 0))
    rows = lambda a: pl.BlockSpec((tq, a.shape[1]), lambda s: (prev(s), 0))
    full = lambda a: pl.BlockSpec(a.shape, lambda s: (0,) * a.ndim)
    return pl.pallas_call(
        functools.partial(_mixer_prompt_kernel, tq=tq, nq=nq, n_tiles=n_tiles),
        grid=(n_tiles + 1,),
        in_specs=[pl.BlockSpec(memory_space=pltpu.SMEM),
                  tile_t, whole, whole_t, tile,
                  full(dbias), full(kbias), full(dmat), full(xi), full(zeta), full(sg),
                  rows(x2), rows(gates2), full(woa), full(wob), full(wout)],
        out_specs=[rows(x2),
                   pl.BlockSpec((1, H_B, DK_B, DV_B), lambda s: (cur(s) // nq, 0, 0, 0))],
        out_shape=[jax.ShapeDtypeStruct((b * t, d), F32),
                   jax.ShapeDtypeStruct((b, H_B, DK_B, DV_B), F32)],
        scratch_shapes=[pltpu.VMEM((H_B, DK_B, DV_B), F32),
                        pltpu.VMEM((H_A, HEAD_W + SUM_ROWS, 2 * tq), F32),
                        pltpu.VMEM((tq, w), BF16), pltpu.VMEM((tq, w), BF16),
                        pltpu.VMEM((tq, d), BF16)]
                       + [pltpu.VMEM((2, tq, 2 * tq), F32)] * H_A,
        compiler_params=pltpu.CompilerParams(
            dimension_semantics=("arbitrary",), vmem_limit_bytes=VMEM_LIMIT),
        name="mixer_prompt",
    )(scal, qat, kab, vat, mix, dbias, kbias, dmat, xi, zeta, sg, x2, gates2, woa, wob, wout)


def _mixer_sample_kernel(scal_ref, qa_ref, kc_ref, vc_ref, kn_ref, vn_ref, mix_ref, r_ref,
                         abias_ref, dmat_ref, xi_ref, zeta_ref, sg_ref,
                         oa_ref, ob_ref, rout_ref, m_scr, acc_scr, *, n, past, tk, sub):
    kbi = pl.program_id(1)
    lam = scal_ref[0]
    lanes = lambda j: slice(j * HEAD_W, (j + 1) * HEAD_W)

    @pl.when(kbi == 0)
    def _():
        m_scr[...] = jnp.full(m_scr.shape, -jnp.inf, F32)
        acc_scr[...] = jnp.zeros_like(acc_scr)

    col = lax.broadcasted_iota(jnp.int32, (1, sub), 1)
    qs = [_split_halves(qa_ref[0, :, lanes(j)]) for j in range(H_A)]

    def scores(j, c):
        kk = kc_ref[0, lanes(j), c * sub:(c + 1) * sub].astype(BF16)
        kpos = (col + (kbi * tk + (c * sub - past))).astype(F32)
        return _dot(qs[j], kk) + kpos * scal_ref[1 + j]

    blocks = [(j, c) for c in range(tk // sub) for j in range(H_A)]
    carry = [(m_scr[j], acc_scr[j]) for j in range(H_A)]
    staged = [scores(*blk) for blk in blocks[:SCORE_LOOKAHEAD]]
    for i, (j, c) in enumerate(blocks):
        if i + SCORE_LOOKAHEAD < len(blocks):
            staged.append(scores(*blocks[i + SCORE_LOOKAHEAD]))
        vv = vc_ref[0, pl.ds(c * sub * H_A + j, sub, stride=H_A), :].astype(BF16)
        carry[j] = _softmax_update(carry[j], staged[i], vv)
    for j in range(H_A):
        m_scr[j], acc_scr[j] = carry[j]

    @pl.when(kbi == pl.num_programs(1) - 1)
    def _():
        s_new, ret = [], []
        for j in range(H_A):
            bias = abias_ref[j]
            s_new.append(_nt_dot(qs[j], kn_ref[0, :, lanes(j)]) + jnp.concatenate([bias, bias], axis=0))
        for j in range(H_A):
            s_ret, cross, r_new = _retention_stage1(
                _slab(mix_ref, "qb", j)[...], _slab(mix_ref, "kb", j)[...], _slab(mix_ref, "vb", j)[...],
                r_ref[0, j], dmat_ref[j], xi_ref[j], zeta_ref[j], scal_ref[1 + H_A + j])
            rout_ref[0, j] = r_new
            ret.append((s_ret, cross))
        for j in range(H_A):
            _, acc = _softmax_update((m_scr[j], acc_scr[j]), s_new[j], vn_ref[0, :, lanes(j)])
            oa = _diff_combine(acc, lam, n)
            oa = (_lane_rms(oa) * sg_ref[0:1, :]) * sg_ref[1:2, :]
            oa_ref[0, :, lanes(j)] = (oa * _slab(mix_ref, "za", j)[...].astype(F32)).astype(BF16)
        for j in range(H_A):
            ob = _retention_stage2(*ret[j], _slab(mix_ref, "vb", j)[...])
            ob_ref[0, :, lanes(j)] = (_lane_rms(ob) * _slab(mix_ref, "zb", j)[...].astype(F32)).astype(BF16)


def _mixer_sample(scal, qa, kc, vc, kab, vab, mix, r0, abias, dmat, xi, zeta, sg):
    b, n, w = qa.shape
    past = kc.shape[2]
    tk, sub = SAMPLE_KEY_TILE, SAMPLE_KEY_BLOCK
    assert past % tk == 0 and tk % sub == 0 and past % CHUNK == 0 and n <= CHUNK and H_A == H_B
    tile = pl.BlockSpec((1, n, w), lambda bi, ki: (bi, 0, 0))
    mix_tile = pl.BlockSpec((1, n, mix.shape[2]), lambda bi, ki: (bi, 0, 0))
    cache_k = pl.BlockSpec((1, w, tk), lambda bi, ki: (bi, 0, ki))
    cache_v = pl.BlockSpec((1, tk * H_A, HEAD_W), lambda bi, ki: (bi, ki, 0))
    state = pl.BlockSpec((1, H_B, DK_B, DV_B), lambda bi, ki: (bi, 0, 0, 0))
    full = lambda a: pl.BlockSpec(a.shape, lambda bi, ki: (0,) * a.ndim)
    return pl.pallas_call(
        functools.partial(_mixer_sample_kernel, n=n, past=past, tk=tk, sub=sub),
        grid=(b, past // tk),
        in_specs=[pl.BlockSpec(memory_space=pltpu.SMEM),
                  tile, cache_k, cache_v, tile, tile, mix_tile, state,
                  full(abias), full(dmat), full(xi), full(zeta), full(sg)],
        out_specs=[tile, tile, state],
        out_shape=[jax.ShapeDtypeStruct((b, n, H_A * HEAD_W), BF16),
                   jax.ShapeDtypeStruct((b, n, H_B * HEAD_W), BF16),
                   jax.ShapeDtypeStruct((b, H_B, DK_B, DV_B), F32)],
        scratch_shapes=[pltpu.VMEM((H_A, 2 * n, 1), F32),
                        pltpu.VMEM((H_A, 2 * n, 2 * DV_A), F32)],
        compiler_params=pltpu.CompilerParams(
            dimension_semantics=("arbitrary", "arbitrary"), vmem_limit_bytes=VMEM_LIMIT),
        name="mixer_sample",
    )(scal, qa, kc, vc, kab, vab, mix, r0, abias, dmat, xi, zeta, sg)


def _proj_out_kernel(x_ref, oa_ref, ob_ref, gate_ref, woa_ref, wob_ref, wout_ref, y_ref):
    d = x_ref.shape[1]
    m = (gate_ref[:, :d].astype(F32) * _dot(oa_ref[...], woa_ref[...])
         + gate_ref[:, d:].astype(F32) * _dot(ob_ref[...], wob_ref[...]))
    y_ref[...] = x_ref[...] + _dot(m.astype(BF16), wout_ref[...])


def _proj_out(x2, oa, ob, gates, woa, wob, wout):
    n, d = x2.shape
    tm = min(ROW_TILE, n)
    assert n % tm == 0
    row = lambda w: pl.BlockSpec((tm, w), lambda i: (i, 0))
    full = lambda a: pl.BlockSpec(a.shape, lambda i: (0,) * a.ndim)
    return pl.pallas_call(
        _proj_out_kernel,
        grid=(n // tm,),
        in_specs=[row(d), row(oa.shape[1]), row(ob.shape[1]), row(gates.shape[1]),
                  full(woa), full(wob), full(wout)],
        out_specs=row(d),
        out_shape=jax.ShapeDtypeStruct((n, d), F32),
        compiler_params=pltpu.CompilerParams(
            dimension_semantics=("arbitrary",), vmem_limit_bytes=VMEM_LIMIT),
        name="proj_out",
    )(x2, oa, ob, gates, woa, wob, wout)


def _block_tables(t, slopes, log_g):
    i = jnp.arange(t, dtype=jnp.int32)
    allowed = (i[None, :] // CHUNK) <= (i[:, None] // CHUNK)
    dist = jnp.abs(i[:, None] - i[None, :]).astype(F32)
    jf = i.astype(F32)
    abias = -slopes[:, None, None] * dist[None] + slopes[:, None, None] * jf[None, :, None]
    abias = jnp.where(allowed[None], abias, -jnp.inf)
    dmat = jnp.where(allowed[None], jnp.exp(dist[None] * log_g[:, None, None]), 0.0)
    xi = jnp.exp((jf + 1.0)[None] * log_g[:, None])
    zeta = jnp.exp((t - 1.0 - jf)[None] * log_g[:, None])
    bc = lambda a: jnp.broadcast_to(a[:, :, None], a.shape + (HEAD_W,))
    return abias, dmat, bc(xi), bc(zeta)


def _key_bias_table(t, slopes):
    kb = slopes[:, None] * jnp.arange(t, dtype=F32)[None]
    return jnp.broadcast_to(kb[:, :, None], kb.shape + (HEAD_W,))


def _layer(x_prompt, x_sample, ck, cv, r0, norm_g, w_in, b_gate, qn_g, kn_g,
           lam_q1, lam_k1, lam_q2, lam_k2, subln_g, w_oa, w_ob, w_out, lam_init):
    bp, tp, d = x_prompt.shape
    bs, ts, _ = x_sample.shape
    past = ck.shape[1]
    slopes = 2.0 ** (-8.0 * jnp.arange(1, H_A + 1, dtype=F32) / H_A)
    log_g = jnp.log(1.0 - 2.0 ** (-5.0 - jnp.arange(H_B, dtype=F32)))
    lam = (jnp.exp(jnp.sum(lam_q1 * lam_k1).astype(F32))
           - jnp.exp(jnp.sum(lam_q2 * lam_k2).astype(F32)) + lam_init)

    woa_bf, wob_bf, wout_bf = w_oa.astype(BF16), w_ob.astype(BF16), w_out.astype(BF16)
    g = norm_g.reshape(1, d)
    qg_t = jnp.tile(qn_g, 2 * H_A).reshape(1, H_A * HEAD_W)
    kg_t = jnp.tile(kn_g, 2 * H_A).reshape(1, H_A * HEAD_W)
    grp = jnp.arange(MXU_TILE, dtype=jnp.int32) // DK_A
    gm = (grp[:, None] == grp[None, :]).astype(BF16)
    sg = jnp.stack([subln_g, jnp.full((DV_A,), 1.0 - lam_init, F32)])

    def scal(n_block):
        return jnp.concatenate([lam.reshape(1), slopes, jnp.exp(n_block * log_g)]).astype(F32)

    qat, kaf, kab, vaf, vat, mix, gates = _proj_in(
        x_prompt, g, w_in, b_gate, qg_t, kg_t, gm, tm=ROW_TILE, feature_major=True)
    abias, dmat, xi, zeta = _block_tables(Q_TILE, slopes, log_g)
    kbias = _key_bias_table(Q_TILE, slopes)
    dbias = jnp.swapaxes(abias, 1, 2) - kbias[:, :, :1]
    flat = lambda a: a.reshape(-1, a.shape[-1])
    y_p, ret_p = _mixer_prompt(scal(float(Q_TILE)), qat, kab, vat, mix, dbias, kbias, dmat, xi, zeta, sg,
                               flat(x_prompt), flat(gates), woa_bf, wob_bf, wout_bf)
    y_p = y_p.reshape(bp, tp, d)
    k_p = jnp.transpose(kaf.reshape(bp, H_A, 2, DK_A, tp), (0, 4, 1, 2, 3))
    v_p = vaf.reshape(bp, tp, H_A, DV_A)

    qa, kaf, kab, vaf, vab, mix, gates = _proj_in(
        x_sample.reshape(1, bs * ts, d), g, w_in, b_gate, qg_t, kg_t, gm,
        tm=min(ROW_TILE, bs * ts), feature_major=False)
    r3 = lambda a: a.reshape(bs, ts, a.shape[-1])
    tabs = _block_tables(ts, slopes, log_g)
    ck_t = jnp.transpose(ck, (0, 2, 3, 4, 1)).reshape(bs, H_A * HEAD_W, past)
    oa, ob, ret_s = _mixer_sample(scal(float(ts)), r3(qa), ck_t, cv.reshape(bs, past * H_A, HEAD_W),
                                  r3(kab), r3(vab), r3(mix), r0, *tabs, sg)
    y_s = _proj_out(flat(x_sample), flat(oa), flat(ob), flat(gates),
                    woa_bf, wob_bf, wout_bf).reshape(bs, ts, d)
    k_s = kaf.reshape(bs, ts, H_A, 2, DK_A)
    v_s = vaf.reshape(bs, ts, H_A, DV_A)
    return y_p, y_s, k_p, v_p, ret_p, k_s, v_s, ret_s


def kernel(x_prompt, x_sample, cache_k_diff, cache_v_diff, state_ret, norm_g, w_in, b_gate, qn_g, kn_g,
           lam_q1, lam_k1, lam_q2, lam_k2, subln_g, w_o_diff, w_o_ret, w_out):
    depth = w_in.shape[0]
    hp, hs = x_prompt, x_sample
    outs = [[] for _ in range(6)]
    for l in range(depth):
        lam_init = 0.8 - 0.6 * math.exp(-0.3 * l)
        hp, hs, k_p, v_p, ret_p, k_s, v_s, ret_s = _layer(
            hp, hs, cache_k_diff[l], cache_v_diff[l], state_ret[l], norm_g[l], w_in[l], b_gate[l],
            qn_g[l], kn_g[l], lam_q1[l], lam_k1[l], lam_q2[l], lam_k2[l], subln_g[l],
            w_o_diff[l], w_o_ret[l], w_out[l], lam_init)
        for lst, a in zip(outs, (k_p, v_p, ret_p, k_s, v_s, ret_s)):
            lst.append(a)
    return (hp, hs) + tuple(jnp.stack(o) for o in outs)
```

```python
import functools
import math

import jax
import jax.numpy as jnp
from jax import lax
from jax.experimental import pallas as pl
from jax.experimental.pallas import tpu as pltpu

F32 = jnp.float32
BF16 = jnp.bfloat16

CHUNK = 64
H_A = 4
DK_A = 64
DV_A = 2 * DK_A
H_B = 4
DK_B = 128
DV_B = 128
HEAD_W = 128
EPS = 1e-6

MXU_TILE = 256
ROW_TILE = 512
Q_TILE = 256
KEY_BLOCK = 256
SUM_ROWS = 16
SAMPLE_KEY_TILE = 4096
SAMPLE_KEY_BLOCK = 1024
SCORE_LOOKAHEAD = 8
VMEM_LIMIT = 56 * 1024 * 1024


def _nt_dot(a, b):
    return lax.dot_general(a, b, (((1,), (1,)), ((), ())), preferred_element_type=F32)


def _tn_dot(a, b):
    return lax.dot_general(a, b, (((0,), (0,)), ((), ())), preferred_element_type=F32)


def _dot(a, b):
    return jnp.dot(a, b, preferred_element_type=F32)


def _sigmoid(x):
    return 1.0 / (1.0 + jnp.exp(-x))


MIX_SLABS = ("za", "qb", "kb", "vb", "zb")


def _slab(mix_ref, name, head=None):
    lo = MIX_SLABS.index(name) * H_A * HEAD_W
    if head is None:
        return mix_ref.at[0, :, lo:lo + H_A * HEAD_W]
    return mix_ref.at[0, :, lo + head * HEAD_W:lo + (head + 1) * HEAD_W]


def _proj_in_kernel(x_ref, g_ref, w_ref, bg_ref, qg_ref, kg_ref, gm_ref,
                    qa_ref, kaf_ref, kab_ref, vaf_ref, vab_ref, mix_ref, gate_ref,
                    *, wa, wb, d_model, feature_major):
    x = x_ref[0]
    tm = x.shape[0]
    ms = jnp.mean(x * x, axis=-1, keepdims=True)
    xn = (x * lax.rsqrt(ms + EPS)) * g_ref[...]

    names = ("qa", "ka", "va", "za", "qb", "kb", "vb", "zb", "ga", "gb")
    widths = (wa, wa, wa, wa, wb, wb, wb, wb, d_model, d_model)
    offs = dict(zip(names, (sum(widths[:i]) for i in range(len(widths)))))
    width = dict(zip(names, widths))

    def seg(name):
        return _dot(xn, w_ref[:, offs[name]:offs[name] + width[name]])

    def group_norm(h, gain):
        sq = (h * h).astype(BF16)
        ss = jnp.concatenate([_dot(sq[:, c:c + MXU_TILE], gm_ref[...]) for c in range(0, wa, MXU_TILE)], axis=1)
        return (h * lax.rsqrt(ss * (1.0 / DK_A) + EPS)) * gain

    def silu(h):
        return h * _sigmoid(h)

    maybe_t = (lambda a: a.T) if feature_major else (lambda a: a)
    gate_ref[0, :, :d_model] = _sigmoid(seg("ga") + bg_ref[0:1, :]).astype(BF16)
    gate_ref[0, :, d_model:] = _sigmoid(seg("gb") + bg_ref[1:2, :]).astype(BF16)
    qa_ref[0] = maybe_t(group_norm(seg("qa"), qg_ref[...]) * (DK_A ** -0.5)).astype(BF16)
    ka = group_norm(seg("ka"), kg_ref[...])
    kaf_ref[0] = maybe_t(ka)
    kab_ref[0] = ka.astype(BF16)
    va = seg("va")
    vab_ref[0] = maybe_t(va).astype(BF16)
    for h in range(H_A):
        vaf_ref[pl.ds(h, tm, stride=H_A), :] = va[:, h * HEAD_W:(h + 1) * HEAD_W]
    _slab(mix_ref, "za")[...] = silu(seg("za")).astype(BF16)
    _slab(mix_ref, "zb")[...] = silu(seg("zb")).astype(BF16)
    _slab(mix_ref, "kb")[...] = (seg("kb") * (DK_B ** -0.5)).astype(BF16)
    _slab(mix_ref, "qb")[...] = seg("qb").astype(BF16)
    _slab(mix_ref, "vb")[...] = seg("vb").astype(BF16)


def _proj_in(x, g, w_in, bg, qg_t, kg_t, gm, *, tm, feature_major):
    b, t, d = x.shape
    wa = H_A * HEAD_W
    wb = H_B * HEAD_W
    assert t % tm == 0
    nt = t // tm
    row = lambda w: pl.BlockSpec((1, tm, w), lambda bi, i: (bi, i, 0))
    full = lambda a: pl.BlockSpec(a.shape, lambda bi, i: (0,) * a.ndim)
    act = lambda w, dt: jax.ShapeDtypeStruct((b, t, w), dt)
    if feature_major:
        fm_shape = lambda dt: jax.ShapeDtypeStruct((b, wa, t), dt)
        fm_spec = pl.BlockSpec((1, wa, tm), lambda bi, i: (bi, 0, i))
    else:
        fm_shape = lambda dt: act(wa, dt)
        fm_spec = row(wa)
    assert wa == wb
    out_shapes = [
        fm_shape(BF16),
        fm_shape(F32), act(wa, BF16),
        jax.ShapeDtypeStruct((b * t * H_A, HEAD_W), F32),
        fm_shape(BF16),
        act(len(MIX_SLABS) * wa, BF16),
        act(2 * d, BF16),
    ]
    out_specs = [fm_spec, fm_spec, row(wa),
                 pl.BlockSpec((tm * H_A, HEAD_W), lambda bi, i: (bi * nt + i, 0)),
                 fm_spec, row(len(MIX_SLABS) * wa), row(2 * d)]
    return pl.pallas_call(
        functools.partial(_proj_in_kernel, wa=wa, wb=wb, d_model=d, feature_major=feature_major),
        grid=(b, nt),
        in_specs=[row(d), full(g), full(w_in), full(bg), full(qg_t), full(kg_t), full(gm)],
        out_specs=out_specs,
        out_shape=out_shapes,
        compiler_params=pltpu.CompilerParams(
            dimension_semantics=("arbitrary", "arbitrary"), vmem_limit_bytes=VMEM_LIMIT),
        name="proj_in",
    )(x, g, w_in, bg, qg_t, kg_t, gm)


def _split_halves(q):
    lane = lax.broadcasted_iota(jnp.int32, q.shape, 1)
    zero = jnp.zeros_like(q)
    return jnp.concatenate([jnp.where(lane < DK_A, q, zero), jnp.where(lane >= DK_A, q, zero)], axis=0)


def _softmax_update(carry, s, v):
    m, acc = carry
    m_new = jnp.maximum(m, jnp.max(s, axis=-1, keepdims=True))
    p = jnp.exp(s - m_new).astype(BF16)
    v1 = jnp.concatenate([v, jnp.ones_like(v)], axis=1)
    return m_new, jnp.exp(m - m_new) * acc + _dot(p, v1)


def _diff_combine(acc, lam, t):
    dv = acc.shape[1] // 2
    o = acc[:, :dv] / acc[:, dv:]
    return o[:t] - lam * o[t:]


def _lane_rms(o):
    return o * lax.rsqrt(jnp.mean(o * o, axis=-1, keepdims=True) + EPS)


def _retention_stage1(q, k, v, r, dmat, xi, zeta, cdecay):
    s = (_nt_dot(q, k) * dmat).astype(BF16)
    cross = _dot(q, r.astype(BF16)) * xi
    kz = (k.astype(F32) * zeta).astype(BF16)
    r_new = cdecay * r + _tn_dot(kz, v)
    return s, cross, r_new


def _retention_stage2(s, cross, v):
    return _dot(s, v) + cross


def _mixer_prompt_kernel(scal_ref, qt_ref, k_ref, vt_ref, mix_ref,
                         dbias_ref, kbias_ref, dmat_ref, xi_ref, zeta_ref, sg_ref,
                         x_ref, gate_ref, woa_ref, wob_ref, wout_ref,
                         y_ref, rout_ref, r_scr, acc_scr, oa_scr, ob_scr, merged_scr, *s_scrs,
                         tq, kblk, nq, n_tiles):
    hps = H_A
    nsub = tq // kblk
    t = pl.program_id(0)
    live = t < n_tiles
    qi = lax.rem(jnp.minimum(t, n_tiles - 1), nq)
    lam = scal_ref[0]
    lanes = lambda j: slice(j * HEAD_W, (j + 1) * HEAD_W)
    n_rep = lambda width: width // HEAD_W

    def split_halves_t(qt):
        feat = lax.broadcasted_iota(jnp.int32, qt.shape, 0)
        zero = jnp.zeros_like(qt)
        return jnp.concatenate([jnp.where(feat < DK_A, qt, zero), jnp.where(feat >= DK_A, qt, zero)], axis=1)

    qst = [split_halves_t(qt_ref[0, lanes(j), :]) for j in range(hps)]

    def keys(j, kb):
        return k_ref[0, pl.ds(pl.multiple_of(kb * kblk, kblk), kblk), lanes(j)]

    def values1(j, kb):
        vt = vt_ref[0, lanes(j), pl.ds(pl.multiple_of(kb * kblk, kblk), kblk)]
        return jnp.concatenate([vt, jnp.ones((SUM_ROWS, kblk), BF16)], axis=0)

    def stage_scores(j, kb, slot):
        s = _dot(keys(j, kb), qst[j])
        s = s + jnp.concatenate([kbias_ref[j]] * n_rep(2 * tq), axis=1)
        s_scrs[j][slot] = s
        return jnp.max(s, axis=0, keepdims=True)

    def softmax_step(m, s, s_max, shift):
        m_new = jnp.maximum(m, s_max + shift)
        return m_new, jnp.exp(m - m_new), jnp.exp(s - (m_new - shift)).astype(BF16)

    def body(kb, carry):
        slot = lax.rem(kb, 2)
        new = []
        for j in range(hps):
            smax_cur, m = carry[j]
            s_cur = s_scrs[j][slot]
            smax_next = stage_scores(j, kb + 1, 1 - slot)
            shift = ((kb - nsub * qi) * kblk).astype(F32) * scal_ref[1 + j]
            m, alpha, p = softmax_step(m, s_cur, smax_cur, shift)
            acc_scr[j] = alpha * acc_scr[j] + _dot(values1(j, kb), p)
            new.append((smax_next, m))
        return tuple(new)

    @pl.when(t == 0)
    def _():
        oa_scr[...] = jnp.zeros_like(oa_scr)
        ob_scr[...] = jnp.zeros_like(ob_scr)

    @pl.when(qi == 0)
    def _():
        r_scr[...] = jnp.zeros_like(r_scr)

    acc_scr[...] = jnp.zeros_like(acc_scr)
    init = tuple((stage_scores(j, 0, 0), jnp.full((1, 2 * tq), -jnp.inf, F32)) for j in range(hps))
    d_model = x_ref.shape[1]
    merged_scr[...] = (gate_ref[:, :d_model].astype(F32) * _dot(oa_scr[...], woa_ref[...])
                       + gate_ref[:, d_model:].astype(F32) * _dot(ob_scr[...], wob_ref[...])).astype(BF16)
    n_before = nsub * qi
    carry = lax.fori_loop(0, jnp.where(live, n_before, 0), body, init)

    def retention_stage1(sub, state):
        rows = slice(sub * kblk, (sub + 1) * kblk)
        return [_retention_stage1(
            _slab(mix_ref, "qb", j)[rows], _slab(mix_ref, "kb", j)[rows], _slab(mix_ref, "vb", j)[rows],
            state[j], dmat_ref[j], xi_ref[j], zeta_ref[j], scal_ref[1 + H_A + j]) for j in range(hps)]

    def retention_stage2(sub, ret):
        rows = slice(sub * kblk, (sub + 1) * kblk)
        for j in range(hps):
            ob = _retention_stage2(ret[j][0], ret[j][1], _slab(mix_ref, "vb", j)[rows])
            ob_scr[rows, lanes(j)] = (_lane_rms(ob) * _slab(mix_ref, "zb", j)[rows].astype(F32)).astype(BF16)

    def cat(pieces):
        pieces = [p for p in pieces if p.shape[1] > 0]
        return pieces[0] if len(pieces) == 1 else jnp.concatenate(pieces, axis=1)

    def diagonal_softmax(dsub, ms):
        kb = n_before + dsub
        w = tq - dsub * kblk
        cols = (slice(dsub * kblk, tq), slice(tq + dsub * kblk, 2 * tq))
        new_ms, staged = [], []
        for j in range(hps):
            if dsub == 0:
                s = s_scrs[j][lax.rem(n_before, 2)]
            else:
                s = (_dot(keys(j, kb), cat([qst[j][:, c] for c in cols]))
                     + jnp.concatenate([kbias_ref[j]] * n_rep(2 * w), axis=1))
            dbias = dbias_ref[j]
            s = cat([s[:, :kblk] + dbias, s[:, kblk:w], s[:, w:w + kblk] + dbias, s[:, w + kblk:]])
            m = ms[j]
            shift = (dsub * kblk) * scal_ref[1 + j]
            m_part, alpha, p = softmax_step(cat([m[:, c] for c in cols]), s,
                                            jnp.max(s, axis=0, keepdims=True), shift)
            new_ms.append(cat([m[:, :dsub * kblk], m_part[:, :w], m[:, tq:tq + dsub * kblk], m_part[:, w:]]))
            staged.append((alpha, p))
        return new_ms, staged

    def diagonal_accumulate(dsub, staged):
        kb = n_before + dsub
        w = tq - dsub * kblk
        cols = (slice(dsub * kblk, tq), slice(tq + dsub * kblk, 2 * tq))
        for j in range(hps):
            alpha, p = staged[j]
            pv = _dot(values1(j, kb), p)
            for h, c in enumerate(cols):
                acc_scr[j, :, c] = alpha[:, h * w:(h + 1) * w] * acc_scr[j, :, c] + pv[:, h * w:(h + 1) * w]

    def final_projection(c):
        y_ref[:, c] = x_ref[:, c] + _dot(merged_scr[...], wout_ref[:, c])

    out_cols = [slice(c * d_model // nsub, (c + 1) * d_model // nsub) for c in range(nsub)]
    state = [r_scr[j] for j in range(hps)]
    ms = [carry[j][1] for j in range(hps)]
    for sub in range(nsub):
        ret = retention_stage1(sub, state)
        state = [r[2] for r in ret]
        ms, staged = diagonal_softmax(sub, ms)
        if sub > 0:
            final_projection(out_cols[sub - 1])
        diagonal_accumulate(sub, staged)
        retention_stage2(sub, ret)
    final_projection(out_cols[nsub - 1])
    for j in range(hps):
        r_new = jnp.where(live, state[j], r_scr[j])
        r_scr[j] = r_new
        rout_ref[0, j] = r_new
    for j in range(hps):
        ot = acc_scr[j, :HEAD_W] * (1.0 / acc_scr[j, HEAD_W:HEAD_W + 1])
        oa = (ot[:, :tq] - lam * ot[:, tq:]).T
        oa = (_lane_rms(oa) * sg_ref[0:1, :]) * sg_ref[1:2, :]
        oa_scr[:, lanes(j)] = (oa * _slab(mix_ref, "za", j)[...].astype(F32)).astype(BF16)


def _mixer_prompt(scal, qat, kab, vat, mix, dbias, kbias, dmat, xi, zeta, sg, x2, gates2, woa, wob, wout):
    b, t, w = kab.shape
    d = x2.shape[1]
    tq, kblk = Q_TILE, KEY_BLOCK
    assert t % tq == 0 and tq % kblk == 0 and kblk % CHUNK == 0 and H_A == H_B and w == H_A * HEAD_W
    assert dbias.shape[1:] == (kblk, kblk) and d % (tq // kblk) == 0
    nq = t // tq
    n_tiles = b * nq
    cur = lambda s: jnp.minimum(s, n_tiles - 1)
    prev = lambda s: jnp.maximum(s - 1, 0)
    tile = pl.BlockSpec((1, tq, mix.shape[2]), lambda s: (cur(s) // nq, cur(s) % nq, 0))
    tile_t = pl.BlockSpec((1, w, tq), lambda s: (cur(s) // nq, 0, cur(s) % nq))
    whole = pl.BlockSpec((1, t, w), lambda s: (cur(s) // nq, 0, 0))
    whole_t = pl.BlockSpec((1, w, t), lambda s: (cur(s) // nq, 0, 0))
    rows = lambda a: pl.BlockSpec((tq, a.shape[1]), lambda s: (prev(s), 0))
    full = lambda a: pl.BlockSpec(a.shape, lambda s: (0,) * a.ndim, pipeline_mode=pl.Buffered(1))
    return pl.pallas_call(
        functools.partial(_mixer_prompt_kernel, tq=tq, kblk=kblk, nq=nq, n_tiles=n_tiles),
        grid=(n_tiles + 1,),
        in_specs=[pl.BlockSpec(memory_space=pltpu.SMEM),
                  tile_t, whole, whole_t, tile,
                  full(dbias), full(kbias), full(dmat), full(xi), full(zeta), full(sg),
                  rows(x2), rows(gates2), full(woa), full(wob), full(wout)],
        out_specs=[rows(x2),
                   pl.BlockSpec((1, H_B, DK_B, DV_B), lambda s: (cur(s) // nq, 0, 0, 0))],
        out_shape=[jax.ShapeDtypeStruct((b * t, d), F32),
                   jax.ShapeDtypeStruct((b, H_B, DK_B, DV_B), F32)],
        scratch_shapes=[pltpu.VMEM((H_B, DK_B, DV_B), F32),
                        pltpu.VMEM((H_A, HEAD_W + SUM_ROWS, 2 * tq), F32),
                        pltpu.VMEM((tq, w), BF16), pltpu.VMEM((tq, w), BF16),
                        pltpu.VMEM((tq, d), BF16)]
                       + [pltpu.VMEM((2, kblk, 2 * tq), F32)] * H_A,
        compiler_params=pltpu.CompilerParams(
            dimension_semantics=("arbitrary",), vmem_limit_bytes=VMEM_LIMIT),
        name="mixer_prompt",
    )(scal, qat, kab, vat, mix, dbias, kbias, dmat, xi, zeta, sg, x2, gates2, woa, wob, wout)


def _mixer_sample_kernel(scal_ref, qa_ref, kc_ref, vc_ref, kn_ref, vn_ref, mix_ref, r_ref,
                         abias_ref, dmat_ref, xi_ref, zeta_ref, sg_ref,
                         oa_ref, ob_ref, rout_ref, m_scr, acc_scr, *, n, past, tk, sub):
    kbi = pl.program_id(1)
    lam = scal_ref[0]
    lanes = lambda j: slice(j * HEAD_W, (j + 1) * HEAD_W)

    @pl.when(kbi == 0)
    def _():
        m_scr[...] = jnp.full(m_scr.shape, -jnp.inf, F32)
        acc_scr[...] = jnp.zeros_like(acc_scr)

    col = lax.broadcasted_iota(jnp.int32, (1, sub), 1)
    qs = [_split_halves(qa_ref[0, :, lanes(j)]) for j in range(H_A)]

    def scores(j, c):
        kk = kc_ref[0, lanes(j), c * sub:(c + 1) * sub].astype(BF16)
        kpos = (col + (kbi * tk + (c * sub - past))).astype(F32)
        return _dot(qs[j], kk) + kpos * scal_ref[1 + j]

    blocks = [(j, c) for c in range(tk // sub) for j in range(H_A)]
    carry = [(m_scr[j], acc_scr[j]) for j in range(H_A)]
    staged = [scores(*blk) for blk in blocks[:SCORE_LOOKAHEAD]]
    for i, (j, c) in enumerate(blocks):
        if i + SCORE_LOOKAHEAD < len(blocks):
            staged.append(scores(*blocks[i + SCORE_LOOKAHEAD]))
        vv = vc_ref[0, pl.ds(c * sub * H_A + j, sub, stride=H_A), :].astype(BF16)
        carry[j] = _softmax_update(carry[j], staged[i], vv)
    for j in range(H_A):
        m_scr[j], acc_scr[j] = carry[j]

    @pl.when(kbi == pl.num_programs(1) - 1)
    def _():
        s_new, ret = [], []
        for j in range(H_A):
            bias = abias_ref[j]
            s_new.append(_nt_dot(qs[j], kn_ref[0, :, lanes(j)]) + jnp.concatenate([bias, bias], axis=0))
        for j in range(H_A):
            s_ret, cross, r_new = _retention_stage1(
                _slab(mix_ref, "qb", j)[...], _slab(mix_ref, "kb", j)[...], _slab(mix_ref, "vb", j)[...],
                r_ref[0, j], dmat_ref[j], xi_ref[j], zeta_ref[j], scal_ref[1 + H_A + j])
            rout_ref[0, j] = r_new
            ret.append((s_ret, cross))
        for j in range(H_A):
            _, acc = _softmax_update((m_scr[j], acc_scr[j]), s_new[j], vn_ref[0, :, lanes(j)])
            oa = _diff_combine(acc, lam, n)
            oa = (_lane_rms(oa) * sg_ref[0:1, :]) * sg_ref[1:2, :]
            oa_ref[0, :, lanes(j)] = (oa * _slab(mix_ref, "za", j)[...].astype(F32)).astype(BF16)
        for j in range(H_A):
            ob = _retention_stage2(*ret[j], _slab(mix_ref, "vb", j)[...])
            ob_ref[0, :, lanes(j)] = (_lane_rms(ob) * _slab(mix_ref, "zb", j)[...].astype(F32)).astype(BF16)


def _mixer_sample(scal, qa, kc, vc, kab, vab, mix, r0, abias, dmat, xi, zeta, sg):
    b, n, w = qa.shape
    past = kc.shape[2]
    tk, sub = SAMPLE_KEY_TILE, SAMPLE_KEY_BLOCK
    assert past % tk == 0 and tk % sub == 0 and past % CHUNK == 0 and n <= CHUNK and H_A == H_B
    tile = pl.BlockSpec((1, n, w), lambda bi, ki: (bi, 0, 0))
    mix_tile = pl.BlockSpec((1, n, mix.shape[2]), lambda bi, ki: (bi, 0, 0))
    cache_k = pl.BlockSpec((1, w, tk), lambda bi, ki: (bi, 0, ki))
    cache_v = pl.BlockSpec((1, tk * H_A, HEAD_W), lambda bi, ki: (bi, ki, 0))
    state = pl.BlockSpec((1, H_B, DK_B, DV_B), lambda bi, ki: (bi, 0, 0, 0))
    full = lambda a: pl.BlockSpec(a.shape, lambda bi, ki: (0,) * a.ndim)
    return pl.pallas_call(
        functools.partial(_mixer_sample_kernel, n=n, past=past, tk=tk, sub=sub),
        grid=(b, past // tk),
        in_specs=[pl.BlockSpec(memory_space=pltpu.SMEM),
                  tile, cache_k, cache_v, tile, tile, mix_tile, state,
                  full(abias), full(dmat), full(xi), full(zeta), full(sg)],
        out_specs=[tile, tile, state],
        out_shape=[jax.ShapeDtypeStruct((b, n, H_A * HEAD_W), BF16),
                   jax.ShapeDtypeStruct((b, n, H_B * HEAD_W), BF16),
                   jax.ShapeDtypeStruct((b, H_B, DK_B, DV_B), F32)],
        scratch_shapes=[pltpu.VMEM((H_A, 2 * n, 1), F32),
                        pltpu.VMEM((H_A, 2 * n, 2 * DV_A), F32)],
        compiler_params=pltpu.CompilerParams(
            dimension_semantics=("arbitrary", "arbitrary"), vmem_limit_bytes=VMEM_LIMIT),
        name="mixer_sample",
    )(scal, qa, kc, vc, kab, vab, mix, r0, abias, dmat, xi, zeta, sg)


def _proj_out_kernel(x_ref, oa_ref, ob_ref, gate_ref, woa_ref, wob_ref, wout_ref, y_ref):
    d = x_ref.shape[1]
    m = (gate_ref[:, :d].astype(F32) * _dot(oa_ref[...], woa_ref[...])
         + gate_ref[:, d:].astype(F32) * _dot(ob_ref[...], wob_ref[...]))
    y_ref[...] = x_ref[...] + _dot(m.astype(BF16), wout_ref[...])


def _proj_out(x2, oa, ob, gates, woa, wob, wout):
    n, d = x2.shape
    tm = min(ROW_TILE, n)
    assert n % tm == 0
    row = lambda w: pl.BlockSpec((tm, w), lambda i: (i, 0))
    full = lambda a: pl.BlockSpec(a.shape, lambda i: (0,) * a.ndim)
    return pl.pallas_call(
        _proj_out_kernel,
        grid=(n // tm,),
        in_specs=[row(d), row(oa.shape[1]), row(ob.shape[1]), row(gates.shape[1]),
                  full(woa), full(wob), full(wout)],
        out_specs=row(d),
        out_shape=jax.ShapeDtypeStruct((n, d), F32),
        compiler_params=pltpu.CompilerParams(
            dimension_semantics=("arbitrary",), vmem_limit_bytes=VMEM_LIMIT),
        name="proj_out",
    )(x2, oa, ob, gates, woa, wob, wout)


def _block_tables(t, slopes, log_g):
    i = jnp.arange(t, dtype=jnp.int32)
    allowed = (i[None, :] // CHUNK) <= (i[:, None] // CHUNK)
    dist = jnp.abs(i[:, None] - i[None, :]).astype(F32)
    jf = i.astype(F32)
    abias = -slopes[:, None, None] * dist[None] + slopes[:, None, None] * jf[None, :, None]
    abias = jnp.where(allowed[None], abias, -jnp.inf)
    dmat = jnp.where(allowed[None], jnp.exp(dist[None] * log_g[:, None, None]), 0.0)
    xi = jnp.exp((jf + 1.0)[None] * log_g[:, None])
    zeta = jnp.exp((t - 1.0 - jf)[None] * log_g[:, None])
    bc = lambda a: jnp.broadcast_to(a[:, :, None], a.shape + (HEAD_W,))
    return abias, dmat, bc(xi), bc(zeta)


def _key_bias_table(t, slopes):
    kb = slopes[:, None] * jnp.arange(t, dtype=F32)[None]
    return jnp.broadcast_to(kb[:, :, None], kb.shape + (HEAD_W,))


def _layer(x_prompt, x_sample, ck, cv, r0, norm_g, w_in, b_gate, qn_g, kn_g,
           lam_q1, lam_k1, lam_q2, lam_k2, subln_g, w_oa, w_ob, w_out, lam_init):
    bp, tp, d = x_prompt.shape
    bs, ts, _ = x_sample.shape
    past = ck.shape[1]
    slopes = 2.0 ** (-8.0 * jnp.arange(1, H_A + 1, dtype=F32) / H_A)
    log_g = jnp.log(1.0 - 2.0 ** (-5.0 - jnp.arange(H_B, dtype=F32)))
    lam = (jnp.exp(jnp.sum(lam_q1 * lam_k1).astype(F32))
           - jnp.exp(jnp.sum(lam_q2 * lam_k2).astype(F32)) + lam_init)

    woa_bf, wob_bf, wout_bf = w_oa.astype(BF16), w_ob.astype(BF16), w_out.astype(BF16)
    g = norm_g.reshape(1, d)
    qg_t = jnp.tile(qn_g, 2 * H_A).reshape(1, H_A * HEAD_W)
    kg_t = jnp.tile(kn_g, 2 * H_A).reshape(1, H_A * HEAD_W)
    grp = jnp.arange(MXU_TILE, dtype=jnp.int32) // DK_A
    gm = (grp[:, None] == grp[None, :]).astype(BF16)
    sg = jnp.stack([subln_g, jnp.full((DV_A,), 1.0 - lam_init, F32)])

    def scal(n_block):
        return jnp.concatenate([lam.reshape(1), slopes, jnp.exp(n_block * log_g)]).astype(F32)

    qat, kaf, kab, vaf, vat, mix, gates = _proj_in(
        x_prompt, g, w_in, b_gate, qg_t, kg_t, gm, tm=ROW_TILE, feature_major=True)
    abias, dmat, xi, zeta = _block_tables(KEY_BLOCK, slopes, log_g)
    kbias = _key_bias_table(KEY_BLOCK, slopes)
    dbias = jnp.swapaxes(abias, 1, 2) - kbias[:, :, :1]
    flat = lambda a: a.reshape(-1, a.shape[-1])
    y_p, ret_p = _mixer_prompt(scal(float(KEY_BLOCK)), qat, kab, vat, mix, dbias, kbias, dmat, xi, zeta, sg,
                               flat(x_prompt), flat(gates), woa_bf, wob_bf, wout_bf)
    y_p = y_p.reshape(bp, tp, d)
    k_p = jnp.transpose(kaf.reshape(bp, H_A, 2, DK_A, tp), (0, 4, 1, 2, 3))
    v_p = vaf.reshape(bp, tp, H_A, DV_A)

    qa, kaf, kab, vaf, vab, mix, gates = _proj_in(
        x_sample.reshape(1, bs * ts, d), g, w_in, b_gate, qg_t, kg_t, gm,
        tm=min(ROW_TILE, bs * ts), feature_major=False)
    r3 = lambda a: a.reshape(bs, ts, a.shape[-1])
    tabs = _block_tables(ts, slopes, log_g)
    ck_t = jnp.transpose(ck, (0, 2, 3, 4, 1)).reshape(bs, H_A * HEAD_W, past)
    oa, ob, ret_s = _mixer_sample(scal(float(ts)), r3(qa), ck_t, cv.reshape(bs, past * H_A, HEAD_W),
                                  r3(kab), r3(vab), r3(mix), r0, *tabs, sg)
    y_s = _proj_out(flat(x_sample), flat(oa), flat(ob), flat(gates),
                    woa_bf, wob_bf, wout_bf).reshape(bs, ts, d)
    k_s = kaf.reshape(bs, ts, H_A, 2, DK_A)
    v_s = vaf.reshape(bs, ts, H_A, DV_A)
    return y_p, y_s, k_p, v_p, ret_p, k_s, v_s, ret_s


def kernel(x_prompt, x_sample, cache_k_diff, cache_v_diff, state_ret, norm_g, w_in, b_gate, qn_g, kn_g,
           lam_q1, lam_k1, lam_q2, lam_k2, subln_g, w_o_diff, w_o_ret, w_out):
    depth = w_in.shape[0]
    hp, hs = x_prompt, x_sample
    outs = [[] for _ in range(6)]
    for l in range(depth):
        lam_init = 0.8 - 0.6 * math.exp(-0.3 * l)
        hp, hs, k_p, v_p, ret_p, k_s, v_s, ret_s = _layer(
            hp, hs, cache_k_diff[l], cache_v_diff[l], state_ret[l], norm_g[l], w_in[l], b_gate[l],
            qn_g[l], kn_g[l], lam_q1[l], lam_k1[l], lam_q2[l], lam_k2[l], subln_g[l],
            w_o_diff[l], w_o_ret[l], w_out[l], lam_init)
        for lst, a in zip(outs, (k_p, v_p, ret_p, k_s, v_s, ret_s)):
            lst.append(a)
    return (hp, hs) + tuple(jnp.stack(o) for o in outs)
```

```python
import functools
import math

import jax
import jax.numpy as jnp
import numpy as np
from jax import lax
from jax.experimental import pallas as pl
from jax.experimental.pallas import tpu as pltpu

F32 = jnp.float32
BF16 = jnp.bfloat16

CHUNK = 64
H_A = 4
DK_A = 64
DV_A = 2 * DK_A
H_B = 4
DK_B = 128
DV_B = 128
HEAD_W = 128
EPS = 1e-6

MXU_TILE = 256
ROW_TILE = 512
Q_TILE = 256
KEY_BLOCK = 256
SUM_ROWS = 16
SAMPLE_KEY_TILE = 4096
SAMPLE_KEY_BLOCK = 1024
SCORE_LOOKAHEAD = 8
VMEM_LIMIT = 56 * 1024 * 1024


def _nt_dot(a, b):
    return lax.dot_general(a, b, (((1,), (1,)), ((), ())), preferred_element_type=F32)


def _tn_dot(a, b):
    return lax.dot_general(a, b, (((0,), (0,)), ((), ())), preferred_element_type=F32)


def _dot(a, b):
    return jnp.dot(a, b, preferred_element_type=F32)


def _sigmoid(x):
    return 1.0 / (1.0 + jnp.exp(-x))


MIX_SLABS = ("za", "qb", "kb", "vb", "zb")


def _slab(mix_ref, name, head=None):
    lo = MIX_SLABS.index(name) * H_A * HEAD_W
    if head is None:
        return mix_ref.at[0, :, lo:lo + H_A * HEAD_W]
    return mix_ref.at[0, :, lo + head * HEAD_W:lo + (head + 1) * HEAD_W]


def _proj_in_kernel(x_ref, g_ref, w_ref, bg_ref, qg_ref, kg_ref, gm_ref,
                    qa_ref, kaf_ref, kab_ref, vaf_ref, vab_ref, mix_ref, gate_ref,
                    *, wa, wb, d_model, feature_major):
    x = x_ref[0]
    tm = x.shape[0]
    ms = jnp.mean(x * x, axis=-1, keepdims=True)
    xn = (x * lax.rsqrt(ms + EPS)) * g_ref[...]

    names = ("qa", "ka", "va", "za", "qb", "kb", "vb", "zb", "ga", "gb")
    widths = (wa, wa, wa, wa, wb, wb, wb, wb, d_model, d_model)
    offs = dict(zip(names, (sum(widths[:i]) for i in range(len(widths)))))
    width = dict(zip(names, widths))

    def seg(name):
        return _dot(xn, w_ref[:, offs[name]:offs[name] + width[name]])

    def group_norm(h, gain):
        sq = (h * h).astype(BF16)
        ss = jnp.concatenate([_dot(sq[:, c:c + MXU_TILE], gm_ref[...]) for c in range(0, wa, MXU_TILE)], axis=1)
        return (h * lax.rsqrt(ss * (1.0 / DK_A) + EPS)) * gain

    def silu(h):
        return h * _sigmoid(h)

    maybe_t = (lambda a: a.T) if feature_major else (lambda a: a)
    gate_ref[0, :, :d_model] = _sigmoid(seg("ga") + bg_ref[0:1, :]).astype(BF16)
    gate_ref[0, :, d_model:] = _sigmoid(seg("gb") + bg_ref[1:2, :]).astype(BF16)
    qa_ref[0] = maybe_t(group_norm(seg("qa"), qg_ref[...]) * (DK_A ** -0.5)).astype(BF16)
    ka = group_norm(seg("ka"), kg_ref[...])
    kaf_ref[0] = maybe_t(ka)
    kab_ref[0] = ka.astype(BF16)
    va = seg("va")
    vab_ref[0] = maybe_t(va).astype(BF16)
    for h in range(H_A):
        vaf_ref[pl.ds(h, tm, stride=H_A), :] = va[:, h * HEAD_W:(h + 1) * HEAD_W]
    _slab(mix_ref, "za")[...] = silu(seg("za")).astype(BF16)
    _slab(mix_ref, "zb")[...] = silu(seg("zb")).astype(BF16)
    _slab(mix_ref, "kb")[...] = (seg("kb") * (DK_B ** -0.5)).astype(BF16)
    _slab(mix_ref, "qb")[...] = seg("qb").astype(BF16)
    _slab(mix_ref, "vb")[...] = seg("vb").astype(BF16)


def _proj_in(x, g, w_in, bg, qg_t, kg_t, gm, *, tm, feature_major):
    b, t, d = x.shape
    wa = H_A * HEAD_W
    wb = H_B * HEAD_W
    assert t % tm == 0
    nt = t // tm
    row = lambda w: pl.BlockSpec((1, tm, w), lambda bi, i: (bi, i, 0))
    full = lambda a: pl.BlockSpec(a.shape, lambda bi, i: (0,) * a.ndim)
    act = lambda w, dt: jax.ShapeDtypeStruct((b, t, w), dt)
    if feature_major:
        fm_shape = lambda dt: jax.ShapeDtypeStruct((b, wa, t), dt)
        fm_spec = pl.BlockSpec((1, wa, tm), lambda bi, i: (bi, 0, i))
    else:
        fm_shape = lambda dt: act(wa, dt)
        fm_spec = row(wa)
    assert wa == wb
    out_shapes = [
        fm_shape(BF16),
        fm_shape(F32), act(wa, BF16),
        jax.ShapeDtypeStruct((b * t * H_A, HEAD_W), F32),
        fm_shape(BF16),
        act(len(MIX_SLABS) * wa, BF16),
        act(2 * d, BF16),
    ]
    out_specs = [fm_spec, fm_spec, row(wa),
                 pl.BlockSpec((tm * H_A, HEAD_W), lambda bi, i: (bi * nt + i, 0)),
                 fm_spec, row(len(MIX_SLABS) * wa), row(2 * d)]
    return pl.pallas_call(
        functools.partial(_proj_in_kernel, wa=wa, wb=wb, d_model=d, feature_major=feature_major),
        grid=(b, nt),
        in_specs=[row(d), full(g), full(w_in), full(bg), full(qg_t), full(kg_t), full(gm)],
        out_specs=out_specs,
        out_shape=out_shapes,
        compiler_params=pltpu.CompilerParams(
            dimension_semantics=("arbitrary", "arbitrary"), vmem_limit_bytes=VMEM_LIMIT),
        name="proj_in",
    )(x, g, w_in, bg, qg_t, kg_t, gm)


def _split_halves(q):
    lane = lax.broadcasted_iota(jnp.int32, q.shape, 1)
    zero = jnp.zeros_like(q)
    return jnp.concatenate([jnp.where(lane < DK_A, q, zero), jnp.where(lane >= DK_A, q, zero)], axis=0)


def _softmax_update(carry, s, v):
    m, acc = carry
    m_new = jnp.maximum(m, jnp.max(s, axis=-1, keepdims=True))
    p = jnp.exp(s - m_new).astype(BF16)
    v1 = jnp.concatenate([v, jnp.ones_like(v)], axis=1)
    return m_new, jnp.exp(m - m_new) * acc + _dot(p, v1)


def _diff_combine(acc, lam, t):
    dv = acc.shape[1] // 2
    o = acc[:, :dv] / acc[:, dv:]
    return o[:t] - lam * o[t:]


def _lane_rms(o):
    return o * lax.rsqrt(jnp.mean(o * o, axis=-1, keepdims=True) + EPS)


def _retention_stage1(q, k, v, r, dmat, xi, zeta, cdecay):
    s = (_nt_dot(q, k) * dmat).astype(BF16)
    cross = _dot(q, r.astype(BF16)) * xi
    kz = (k.astype(F32) * zeta).astype(BF16)
    r_new = cdecay * r + _tn_dot(kz, v)
    return s, cross, r_new


def _retention_stage2(s, cross, v):
    return _dot(s, v) + cross


def _mixer_prompt_kernel(scal_ref, qt_ref, k_ref, vt_ref, mix_ref,
                         dbias_ref, kbias_ref, dmat_ref, xi_ref, zeta_ref, sg_ref,
                         x_ref, gate_ref, woa_ref, wob_ref, wout_ref,
                         y_ref, rout_ref, r_scr, acc_scr, oa_scr, ob_scr, merged_scr, *s_scrs,
                         tq, kblk, nq, n_tiles):
    hps = H_A
    nsub = tq // kblk
    t = pl.program_id(0)
    live = t < n_tiles
    qi = lax.rem(jnp.minimum(t, n_tiles - 1), nq)
    lam = scal_ref[0]
    lanes = lambda j: slice(j * HEAD_W, (j + 1) * HEAD_W)
    n_rep = lambda width: width // HEAD_W

    def split_halves_t(qt):
        feat = lax.broadcasted_iota(jnp.int32, qt.shape, 0)
        zero = jnp.zeros_like(qt)
        return jnp.concatenate([jnp.where(feat < DK_A, qt, zero), jnp.where(feat >= DK_A, qt, zero)], axis=1)

    qst = [split_halves_t(qt_ref[0, lanes(j), :]) for j in range(hps)]

    def keys(j, kb):
        return k_ref[0, pl.ds(pl.multiple_of(kb * kblk, kblk), kblk), lanes(j)]

    def values1(j, kb):
        vt = vt_ref[0, lanes(j), pl.ds(pl.multiple_of(kb * kblk, kblk), kblk)]
        return jnp.concatenate([vt, jnp.ones((SUM_ROWS, kblk), BF16)], axis=0)

    def stage_scores(j, kb, slot):
        s = _dot(keys(j, kb), qst[j])
        s = s + jnp.concatenate([kbias_ref[j]] * n_rep(2 * tq), axis=1)
        s_scrs[j][slot] = s
        return jnp.max(s, axis=0, keepdims=True)

    def softmax_step(m, s, s_max, shift):
        m_new = jnp.maximum(m, s_max + shift)
        return m_new, jnp.exp(m - m_new), jnp.exp(s - (m_new - shift)).astype(BF16)

    def body(kb, carry):
        slot = lax.rem(kb, 2)
        new = []
        for j in range(hps):
            smax_cur, m = carry[j]
            s_cur = s_scrs[j][slot]
            smax_next = stage_scores(j, kb + 1, 1 - slot)
            shift = ((kb - nsub * qi) * kblk).astype(F32) * scal_ref[1 + j]
            m, alpha, p = softmax_step(m, s_cur, smax_cur, shift)
            acc_scr[j] = alpha * acc_scr[j] + _dot(values1(j, kb), p)
            new.append((smax_next, m))
        return tuple(new)

    @pl.when(t == 0)
    def _():
        oa_scr[...] = jnp.zeros_like(oa_scr)
        ob_scr[...] = jnp.zeros_like(ob_scr)

    @pl.when(qi == 0)
    def _():
        r_scr[...] = jnp.zeros_like(r_scr)

    acc_scr[...] = jnp.zeros_like(acc_scr)
    init = tuple((stage_scores(j, 0, 0), jnp.full((1, 2 * tq), -jnp.inf, F32)) for j in range(hps))
    d_model = x_ref.shape[1]
    merged_scr[...] = (gate_ref[:, :d_model].astype(F32) * _dot(oa_scr[...], woa_ref[...])
                       + gate_ref[:, d_model:].astype(F32) * _dot(ob_scr[...], wob_ref[...])).astype(BF16)
    n_before = nsub * qi
    carry = lax.fori_loop(0, jnp.where(live, n_before, 0), body, init)

    def retention_stage1(sub, state):
        rows = slice(sub * kblk, (sub + 1) * kblk)
        return [_retention_stage1(
            _slab(mix_ref, "qb", j)[rows], _slab(mix_ref, "kb", j)[rows], _slab(mix_ref, "vb", j)[rows],
            state[j], dmat_ref[j], xi_ref[j], zeta_ref[j], scal_ref[1 + H_A + j]) for j in range(hps)]

    def retention_stage2(sub, ret):
        rows = slice(sub * kblk, (sub + 1) * kblk)
        for j in range(hps):
            ob = _retention_stage2(ret[j][0], ret[j][1], _slab(mix_ref, "vb", j)[rows])
            ob_scr[rows, lanes(j)] = (_lane_rms(ob) * _slab(mix_ref, "zb", j)[rows].astype(F32)).astype(BF16)

    def cat(pieces):
        pieces = [p for p in pieces if p.shape[1] > 0]
        return pieces[0] if len(pieces) == 1 else jnp.concatenate(pieces, axis=1)

    def diagonal_softmax(dsub, ms):
        kb = n_before + dsub
        w = tq - dsub * kblk
        cols = (slice(dsub * kblk, tq), slice(tq + dsub * kblk, 2 * tq))
        new_ms, staged = [], []
        for j in range(hps):
            if dsub == 0:
                s = s_scrs[j][lax.rem(n_before, 2)]
            else:
                s = (_dot(keys(j, kb), cat([qst[j][:, c] for c in cols]))
                     + jnp.concatenate([kbias_ref[j]] * n_rep(2 * w), axis=1))
            dbias = dbias_ref[j]
            s = cat([s[:, :kblk] + dbias, s[:, kblk:w], s[:, w:w + kblk] + dbias, s[:, w + kblk:]])
            m = ms[j]
            shift = (dsub * kblk) * scal_ref[1 + j]
            m_part, alpha, p = softmax_step(cat([m[:, c] for c in cols]), s,
                                            jnp.max(s, axis=0, keepdims=True), shift)
            new_ms.append(cat([m[:, :dsub * kblk], m_part[:, :w], m[:, tq:tq + dsub * kblk], m_part[:, w:]]))
            staged.append((alpha, p))
        return new_ms, staged

    def diagonal_accumulate(dsub, staged):
        kb = n_before + dsub
        w = tq - dsub * kblk
        cols = (slice(dsub * kblk, tq), slice(tq + dsub * kblk, 2 * tq))
        for j in range(hps):
            alpha, p = staged[j]
            pv = _dot(values1(j, kb), p)
            for h, c in enumerate(cols):
                acc_scr[j, :, c] = alpha[:, h * w:(h + 1) * w] * acc_scr[j, :, c] + pv[:, h * w:(h + 1) * w]

    def final_projection(c):
        y_ref[:, c] = x_ref[:, c] + _dot(merged_scr[...], wout_ref[:, c])

    out_cols = [slice(c * d_model // nsub, (c + 1) * d_model // nsub) for c in range(nsub)]
    state = [r_scr[j] for j in range(hps)]
    ms = [carry[j][1] for j in range(hps)]
    for sub in range(nsub):
        ret = retention_stage1(sub, state)
        state = [r[2] for r in ret]
        ms, staged = diagonal_softmax(sub, ms)
        if sub > 0:
            final_projection(out_cols[sub - 1])
        diagonal_accumulate(sub, staged)
        retention_stage2(sub, ret)
    final_projection(out_cols[nsub - 1])
    for j in range(hps):
        r_new = jnp.where(live, state[j], r_scr[j])
        r_scr[j] = r_new
        rout_ref[0, j] = r_new
    for j in range(hps):
        ot = acc_scr[j, :HEAD_W] * (1.0 / acc_scr[j, HEAD_W:HEAD_W + 1])
        oa = (ot[:, :tq] - lam * ot[:, tq:]).T
        oa = (_lane_rms(oa) * sg_ref[0:1, :]) * sg_ref[1:2, :]
        oa_scr[:, lanes(j)] = (oa * _slab(mix_ref, "za", j)[...].astype(F32)).astype(BF16)


def _mixer_prompt(scal, qat, kab, vat, mix, dbias, kbias, dmat, xi, zeta, sg, x2, gates2, woa, wob, wout):
    b, t, w = kab.shape
    d = x2.shape[1]
    tq, kblk = Q_TILE, KEY_BLOCK
    assert t % tq == 0 and tq % kblk == 0 and kblk % CHUNK == 0 and H_A == H_B and w == H_A * HEAD_W
    assert dbias.shape[1:] == (kblk, kblk) and d % (tq // kblk) == 0
    nq = t // tq
    n_tiles = b * nq
    cur = lambda s: jnp.minimum(s, n_tiles - 1)
    prev = lambda s: jnp.maximum(s - 1, 0)
    tile = pl.BlockSpec((1, tq, mix.shape[2]), lambda s: (cur(s) // nq, cur(s) % nq, 0))
    tile_t = pl.BlockSpec((1, w, tq), lambda s: (cur(s) // nq, 0, cur(s) % nq))
    whole = pl.BlockSpec((1, t, w), lambda s: (cur(s) // nq, 0, 0))
    whole_t = pl.BlockSpec((1, w, t), lambda s: (cur(s) // nq, 0, 0))
    rows = lambda a: pl.BlockSpec((tq, a.shape[1]), lambda s: (prev(s), 0))
    full = lambda a: pl.BlockSpec(a.shape, lambda s: (0,) * a.ndim, pipeline_mode=pl.Buffered(1))
    return pl.pallas_call(
        functools.partial(_mixer_prompt_kernel, tq=tq, kblk=kblk, nq=nq, n_tiles=n_tiles),
        grid=(n_tiles + 1,),
        in_specs=[pl.BlockSpec(memory_space=pltpu.SMEM),
                  tile_t, whole, whole_t, tile,
                  full(dbias), full(kbias), full(dmat), full(xi), full(zeta), full(sg),
                  rows(x2), rows(gates2), full(woa), full(wob), full(wout)],
        out_specs=[rows(x2),
                   pl.BlockSpec((1, H_B, DK_B, DV_B), lambda s: (cur(s) // nq, 0, 0, 0))],
        out_shape=[jax.ShapeDtypeStruct((b * t, d), F32),
                   jax.ShapeDtypeStruct((b, H_B, DK_B, DV_B), F32)],
        scratch_shapes=[pltpu.VMEM((H_B, DK_B, DV_B), F32),
                        pltpu.VMEM((H_A, HEAD_W + SUM_ROWS, 2 * tq), F32),
                        pltpu.VMEM((tq, w), BF16), pltpu.VMEM((tq, w), BF16),
                        pltpu.VMEM((tq, d), BF16)]
                       + [pltpu.VMEM((2, kblk, 2 * tq), F32)] * H_A,
        compiler_params=pltpu.CompilerParams(
            dimension_semantics=("arbitrary",), vmem_limit_bytes=VMEM_LIMIT),
        name="mixer_prompt",
    )(scal, qat, kab, vat, mix, dbias, kbias, dmat, xi, zeta, sg, x2, gates2, woa, wob, wout)


def _mixer_sample_kernel(scal_ref, qa_ref, kc_ref, vc_ref, kn_ref, vn_ref, mix_ref, r_ref,
                         abias_ref, dmat_ref, xi_ref, zeta_ref, sg_ref,
                         oa_ref, ob_ref, rout_ref, m_scr, acc_scr, *, n, past, tk, sub):
    kbi = pl.program_id(1)
    lam = scal_ref[0]
    lanes = lambda j: slice(j * HEAD_W, (j + 1) * HEAD_W)

    @pl.when(kbi == 0)
    def _():
        m_scr[...] = jnp.full(m_scr.shape, -jnp.inf, F32)
        acc_scr[...] = jnp.zeros_like(acc_scr)

    col = lax.broadcasted_iota(jnp.int32, (1, sub), 1)
    qs = [_split_halves(qa_ref[0, :, lanes(j)]) for j in range(H_A)]

    def scores(j, c):
        kk = kc_ref[0, lanes(j), c * sub:(c + 1) * sub].astype(BF16)
        kpos = (col + (kbi * tk + (c * sub - past))).astype(F32)
        return _dot(qs[j], kk) + kpos * scal_ref[1 + j]

    blocks = [(j, c) for c in range(tk // sub) for j in range(H_A)]
    carry = [(m_scr[j], acc_scr[j]) for j in range(H_A)]
    staged = [scores(*blk) for blk in blocks[:SCORE_LOOKAHEAD]]
    for i, (j, c) in enumerate(blocks):
        if i + SCORE_LOOKAHEAD < len(blocks):
            staged.append(scores(*blocks[i + SCORE_LOOKAHEAD]))
        vv = vc_ref[0, pl.ds(c * sub * H_A + j, sub, stride=H_A), :].astype(BF16)
        carry[j] = _softmax_update(carry[j], staged[i], vv)
    for j in range(H_A):
        m_scr[j], acc_scr[j] = carry[j]

    @pl.when(kbi == pl.num_programs(1) - 1)
    def _():
        s_new, ret = [], []
        for j in range(H_A):
            bias = abias_ref[j]
            s_new.append(_nt_dot(qs[j], kn_ref[0, :, lanes(j)]) + jnp.concatenate([bias, bias], axis=0))
        for j in range(H_A):
            s_ret, cross, r_new = _retention_stage1(
                _slab(mix_ref, "qb", j)[...], _slab(mix_ref, "kb", j)[...], _slab(mix_ref, "vb", j)[...],
                r_ref[0, j], dmat_ref[j], xi_ref[j], zeta_ref[j], scal_ref[1 + H_A + j])
            rout_ref[0, j] = r_new
            ret.append((s_ret, cross))
        for j in range(H_A):
            _, acc = _softmax_update((m_scr[j], acc_scr[j]), s_new[j], vn_ref[0, :, lanes(j)])
            oa = _diff_combine(acc, lam, n)
            oa = (_lane_rms(oa) * sg_ref[0:1, :]) * sg_ref[1:2, :]
            oa_ref[0, :, lanes(j)] = (oa * _slab(mix_ref, "za", j)[...].astype(F32)).astype(BF16)
        for j in range(H_A):
            ob = _retention_stage2(*ret[j], _slab(mix_ref, "vb", j)[...])
            ob_ref[0, :, lanes(j)] = (_lane_rms(ob) * _slab(mix_ref, "zb", j)[...].astype(F32)).astype(BF16)


def _mixer_sample(scal, qa, kc, vc, kab, vab, mix, r0, abias, dmat, xi, zeta, sg):
    b, n, w = qa.shape
    past = kc.shape[2]
    tk, sub = SAMPLE_KEY_TILE, SAMPLE_KEY_BLOCK
    assert past % tk == 0 and tk % sub == 0 and past % CHUNK == 0 and n <= CHUNK and H_A == H_B
    tile = pl.BlockSpec((1, n, w), lambda bi, ki: (bi, 0, 0))
    mix_tile = pl.BlockSpec((1, n, mix.shape[2]), lambda bi, ki: (bi, 0, 0))
    cache_k = pl.BlockSpec((1, w, tk), lambda bi, ki: (bi, 0, ki))
    cache_v = pl.BlockSpec((1, tk * H_A, HEAD_W), lambda bi, ki: (bi, ki, 0))
    state = pl.BlockSpec((1, H_B, DK_B, DV_B), lambda bi, ki: (bi, 0, 0, 0))
    full = lambda a: pl.BlockSpec(a.shape, lambda bi, ki: (0,) * a.ndim)
    return pl.pallas_call(
        functools.partial(_mixer_sample_kernel, n=n, past=past, tk=tk, sub=sub),
        grid=(b, past // tk),
        in_specs=[pl.BlockSpec(memory_space=pltpu.SMEM),
                  tile, cache_k, cache_v, tile, tile, mix_tile, state,
                  full(abias), full(dmat), full(xi), full(zeta), full(sg)],
        out_specs=[tile, tile, state],
        out_shape=[jax.ShapeDtypeStruct((b, n, H_A * HEAD_W), BF16),
                   jax.ShapeDtypeStruct((b, n, H_B * HEAD_W), BF16),
                   jax.ShapeDtypeStruct((b, H_B, DK_B, DV_B), F32)],
        scratch_shapes=[pltpu.VMEM((H_A, 2 * n, 1), F32),
                        pltpu.VMEM((H_A, 2 * n, 2 * DV_A), F32)],
        compiler_params=pltpu.CompilerParams(
            dimension_semantics=("arbitrary", "arbitrary"), vmem_limit_bytes=VMEM_LIMIT),
        name="mixer_sample",
    )(scal, qa, kc, vc, kab, vab, mix, r0, abias, dmat, xi, zeta, sg)


def _proj_out_kernel(x_ref, oa_ref, ob_ref, gate_ref, woa_ref, wob_ref, wout_ref, y_ref):
    d = x_ref.shape[1]
    m = (gate_ref[:, :d].astype(F32) * _dot(oa_ref[...], woa_ref[...])
         + gate_ref[:, d:].astype(F32) * _dot(ob_ref[...], wob_ref[...]))
    y_ref[...] = x_ref[...] + _dot(m.astype(BF16), wout_ref[...])


def _proj_out(x2, oa, ob, gates, woa, wob, wout):
    n, d = x2.shape
    tm = min(ROW_TILE, n)
    assert n % tm == 0
    row = lambda w: pl.BlockSpec((tm, w), lambda i: (i, 0))
    full = lambda a: pl.BlockSpec(a.shape, lambda i: (0,) * a.ndim)
    return pl.pallas_call(
        _proj_out_kernel,
        grid=(n // tm,),
        in_specs=[row(d), row(oa.shape[1]), row(ob.shape[1]), row(gates.shape[1]),
                  full(woa), full(wob), full(wout)],
        out_specs=row(d),
        out_shape=jax.ShapeDtypeStruct((n, d), F32),
        compiler_params=pltpu.CompilerParams(
            dimension_semantics=("arbitrary",), vmem_limit_bytes=VMEM_LIMIT),
        name="proj_out",
    )(x2, oa, ob, gates, woa, wob, wout)


def _block_tables(t, slopes, log_g):
    f32 = np.float32
    i = np.arange(t, dtype=np.int32)
    allowed = (i[None, :] // CHUNK) <= (i[:, None] // CHUNK)
    dist = np.abs(i[:, None] - i[None, :]).astype(f32)
    jf = i.astype(f32)
    abias = -slopes[:, None, None] * dist[None] + slopes[:, None, None] * jf[None, :, None]
    abias = np.where(allowed[None], abias, f32(-np.inf)).astype(f32)
    dmat = np.where(allowed[None], np.exp(dist[None] * log_g[:, None, None]), f32(0.0)).astype(f32)
    xi = np.exp((jf + f32(1.0))[None] * log_g[:, None]).astype(f32)
    zeta = np.exp((f32(t) - f32(1.0) - jf)[None] * log_g[:, None]).astype(f32)
    bc = lambda a: np.ascontiguousarray(np.broadcast_to(a[:, :, None], a.shape + (HEAD_W,)))
    return abias, dmat, bc(xi), bc(zeta)


def _key_bias_table(t, slopes):
    kb = (slopes[:, None] * np.arange(t, dtype=np.float32)[None]).astype(np.float32)
    return np.ascontiguousarray(np.broadcast_to(kb[:, :, None], kb.shape + (HEAD_W,)))


def _layer(x_prompt, x_sample, ck, cv, r0, norm_g, w_in, b_gate, qn_g, kn_g,
           lam_q1, lam_k1, lam_q2, lam_k2, subln_g, w_oa, w_ob, w_out, lam_init):
    bp, tp, d = x_prompt.shape
    bs, ts, _ = x_sample.shape
    past = ck.shape[1]
    slopes = (2.0 ** (-8.0 * np.arange(1, H_A + 1, dtype=np.float32) / H_A)).astype(np.float32)
    log_g = np.log(1.0 - 2.0 ** (-5.0 - np.arange(H_B, dtype=np.float32))).astype(np.float32)
    lam = (jnp.exp(jnp.sum(lam_q1 * lam_k1).astype(F32))
           - jnp.exp(jnp.sum(lam_q2 * lam_k2).astype(F32)) + lam_init)

    woa_bf, wob_bf, wout_bf = w_oa.astype(BF16), w_ob.astype(BF16), w_out.astype(BF16)
    g = norm_g.reshape(1, d)
    qg_t = jnp.tile(qn_g, 2 * H_A).reshape(1, H_A * HEAD_W)
    kg_t = jnp.tile(kn_g, 2 * H_A).reshape(1, H_A * HEAD_W)
    grp = np.arange(MXU_TILE, dtype=np.int32) // DK_A
    gm = jnp.asarray(grp[:, None] == grp[None, :], dtype=BF16)
    sg = jnp.stack([subln_g, jnp.full((DV_A,), 1.0 - lam_init, F32)])

    def scal(n_block):
        consts = np.concatenate([slopes, np.exp(np.float32(n_block) * log_g)]).astype(np.float32)
        return jnp.concatenate([lam.reshape(1).astype(F32), jnp.asarray(consts)])

    qat, kaf, kab, vaf, vat, mix, gates = _proj_in(
        x_prompt, g, w_in, b_gate, qg_t, kg_t, gm, tm=ROW_TILE, feature_major=True)
    abias, dmat, xi, zeta = _block_tables(KEY_BLOCK, slopes, log_g)
    kbias = _key_bias_table(KEY_BLOCK, slopes)
    dbias = np.ascontiguousarray(np.swapaxes(abias, 1, 2) - kbias[:, :, :1])
    flat = lambda a: a.reshape(-1, a.shape[-1])
    y_p, ret_p = _mixer_prompt(scal(float(KEY_BLOCK)), qat, kab, vat, mix, dbias, kbias, dmat, xi, zeta, sg,
                               flat(x_prompt), flat(gates), woa_bf, wob_bf, wout_bf)
    y_p = y_p.reshape(bp, tp, d)
    k_p = jnp.transpose(kaf.reshape(bp, H_A, 2, DK_A, tp), (0, 4, 1, 2, 3))
    v_p = vaf.reshape(bp, tp, H_A, DV_A)

    qa, kaf, kab, vaf, vab, mix, gates = _proj_in(
        x_sample.reshape(1, bs * ts, d), g, w_in, b_gate, qg_t, kg_t, gm,
        tm=min(ROW_TILE, bs * ts), feature_major=False)
    r3 = lambda a: a.reshape(bs, ts, a.shape[-1])
    tabs = _block_tables(ts, slopes, log_g)
    ck_t = jnp.transpose(ck, (0, 2, 3, 4, 1)).reshape(bs, H_A * HEAD_W, past)
    oa, ob, ret_s = _mixer_sample(scal(float(ts)), r3(qa), ck_t, cv.reshape(bs, past * H_A, HEAD_W),
                                  r3(kab), r3(vab), r3(mix), r0, *tabs, sg)
    y_s = _proj_out(flat(x_sample), flat(oa), flat(ob), flat(gates),
                    woa_bf, wob_bf, wout_bf).reshape(bs, ts, d)
    k_s = kaf.reshape(bs, ts, H_A, 2, DK_A)
    v_s = vaf.reshape(bs, ts, H_A, DV_A)
    return y_p, y_s, k_p, v_p, ret_p, k_s, v_s, ret_s


def kernel(x_prompt, x_sample, cache_k_diff, cache_v_diff, state_ret, norm_g, w_in, b_gate, qn_g, kn_g,
           lam_q1, lam_k1, lam_q2, lam_k2, subln_g, w_o_diff, w_o_ret, w_out):
    depth = w_in.shape[0]
    hp, hs = x_prompt, x_sample
    outs = [[] for _ in range(6)]
    for l in range(depth):
        lam_init = 0.8 - 0.6 * math.exp(-0.3 * l)
        hp, hs, k_p, v_p, ret_p, k_s, v_s, ret_s = _layer(
            hp, hs, cache_k_diff[l], cache_v_diff[l], state_ret[l], norm_g[l], w_in[l], b_gate[l],
            qn_g[l], kn_g[l], lam_q1[l], lam_k1[l], lam_q2[l], lam_k2[l], subln_g[l],
            w_o_diff[l], w_o_ret[l], w_out[l], lam_init)
        for lst, a in zip(outs, (k_p, v_p, ret_p, k_s, v_s, ret_s)):
            lst.append(a)
    return (hp, hs) + tuple(jnp.stack(o) for o in outs)
```

```python
import functools
import math

import jax
import jax.numpy as jnp
import numpy as np
from jax import lax
from jax.experimental import pallas as pl
from jax.experimental.pallas import tpu as pltpu

F32 = jnp.float32
BF16 = jnp.bfloat16

CHUNK = 64
H_A = 4
DK_A = 64
DV_A = 2 * DK_A
H_B = 4
DK_B = 128
DV_B = 128
HEAD_W = 128
EPS = 1e-6

MXU_TILE = 256
ROW_TILE = 512
Q_TILE = 256
KEY_BLOCK = 256
SEQS_PER_STEP = 2
SUM_ROWS = 16
SAMPLE_KEY_TILE = 4096
SAMPLE_KEY_BLOCK = 1024
SCORE_LOOKAHEAD = 8
VMEM_LIMIT = 61 * 1024 * 1024


def _nt_dot(a, b):
    return lax.dot_general(a, b, (((1,), (1,)), ((), ())), preferred_element_type=F32)


def _tn_dot(a, b):
    return lax.dot_general(a, b, (((0,), (0,)), ((), ())), preferred_element_type=F32)


def _dot(a, b):
    return jnp.dot(a, b, preferred_element_type=F32)


def _sigmoid(x):
    return 1.0 / (1.0 + jnp.exp(-x))


MIX_SLABS = ("za", "qb", "kb", "vb", "zb")


def _slab(mix_ref, name, head=None, seq=0):
    lo = MIX_SLABS.index(name) * H_A * HEAD_W
    if head is None:
        return mix_ref.at[seq, :, lo:lo + H_A * HEAD_W]
    return mix_ref.at[seq, :, lo + head * HEAD_W:lo + (head + 1) * HEAD_W]


def _proj_in_kernel(x_ref, g_ref, w_ref, bg_ref, qg_ref, kg_ref, gm_ref,
                    qa_ref, kaf_ref, kab_ref, vaf_ref, vab_ref, mix_ref, gate_ref,
                    *, wa, wb, d_model, feature_major):
    x = x_ref[0]
    tm = x.shape[0]
    ms = jnp.mean(x * x, axis=-1, keepdims=True)
    xn = (x * lax.rsqrt(ms + EPS)) * g_ref[...]

    names = ("qa", "ka", "va", "za", "qb", "kb", "vb", "zb", "ga", "gb")
    widths = (wa, wa, wa, wa, wb, wb, wb, wb, d_model, d_model)
    offs = dict(zip(names, (sum(widths[:i]) for i in range(len(widths)))))
    width = dict(zip(names, widths))

    def seg(name):
        return _dot(xn, w_ref[:, offs[name]:offs[name] + width[name]])

    def group_norm(h, gain):
        sq = (h * h).astype(BF16)
        ss = jnp.concatenate([_dot(sq[:, c:c + MXU_TILE], gm_ref[...]) for c in range(0, wa, MXU_TILE)], axis=1)
        return (h * lax.rsqrt(ss * (1.0 / DK_A) + EPS)) * gain

    def silu(h):
        return h * _sigmoid(h)

    maybe_t = (lambda a: a.T) if feature_major else (lambda a: a)
    gate_ref[0, :, :d_model] = _sigmoid(seg("ga") + bg_ref[0:1, :]).astype(BF16)
    gate_ref[0, :, d_model:] = _sigmoid(seg("gb") + bg_ref[1:2, :]).astype(BF16)
    qa_ref[0] = maybe_t(group_norm(seg("qa"), qg_ref[...]) * (DK_A ** -0.5)).astype(BF16)
    ka = group_norm(seg("ka"), kg_ref[...])
    kaf_ref[0] = maybe_t(ka)
    kab_ref[0] = ka.astype(BF16)
    va = seg("va")
    vab_ref[0] = maybe_t(va).astype(BF16)
    for h in range(H_A):
        vaf_ref[pl.ds(h, tm, stride=H_A), :] = va[:, h * HEAD_W:(h + 1) * HEAD_W]
    _slab(mix_ref, "za")[...] = silu(seg("za")).astype(BF16)
    _slab(mix_ref, "zb")[...] = silu(seg("zb")).astype(BF16)
    _slab(mix_ref, "kb")[...] = (seg("kb") * (DK_B ** -0.5)).astype(BF16)
    _slab(mix_ref, "qb")[...] = seg("qb").astype(BF16)
    _slab(mix_ref, "vb")[...] = seg("vb").astype(BF16)


def _proj_in(x, g, w_in, bg, qg_t, kg_t, gm, *, tm, feature_major):
    b, t, d = x.shape
    wa = H_A * HEAD_W
    wb = H_B * HEAD_W
    assert t % tm == 0
    nt = t // tm
    row = lambda w: pl.BlockSpec((1, tm, w), lambda bi, i: (bi, i, 0))
    full = lambda a: pl.BlockSpec(a.shape, lambda bi, i: (0,) * a.ndim)
    act = lambda w, dt: jax.ShapeDtypeStruct((b, t, w), dt)
    if feature_major:
        fm_shape = lambda dt: jax.ShapeDtypeStruct((b, wa, t), dt)
        fm_spec = pl.BlockSpec((1, wa, tm), lambda bi, i: (bi, 0, i))
    else:
        fm_shape = lambda dt: act(wa, dt)
        fm_spec = row(wa)
    assert wa == wb
    out_shapes = [
        fm_shape(BF16),
        fm_shape(F32), act(wa, BF16),
        jax.ShapeDtypeStruct((b * t * H_A, HEAD_W), F32),
        fm_shape(BF16),
        act(len(MIX_SLABS) * wa, BF16),
        act(2 * d, BF16),
    ]
    out_specs = [fm_spec, fm_spec, row(wa),
                 pl.BlockSpec((tm * H_A, HEAD_W), lambda bi, i: (bi * nt + i, 0)),
                 fm_spec, row(len(MIX_SLABS) * wa), row(2 * d)]
    return pl.pallas_call(
        functools.partial(_proj_in_kernel, wa=wa, wb=wb, d_model=d, feature_major=feature_major),
        grid=(b, nt),
        in_specs=[row(d), full(g), full(w_in), full(bg), full(qg_t), full(kg_t), full(gm)],
        out_specs=out_specs,
        out_shape=out_shapes,
        compiler_params=pltpu.CompilerParams(
            dimension_semantics=("arbitrary", "arbitrary"), vmem_limit_bytes=VMEM_LIMIT),
        name="proj_in",
    )(x, g, w_in, bg, qg_t, kg_t, gm)


def _split_halves(q):
    lane = lax.broadcasted_iota(jnp.int32, q.shape, 1)
    zero = jnp.zeros_like(q)
    return jnp.concatenate([jnp.where(lane < DK_A, q, zero), jnp.where(lane >= DK_A, q, zero)], axis=0)


def _softmax_update(carry, s, v):
    m, acc = carry
    m_new = jnp.maximum(m, jnp.max(s, axis=-1, keepdims=True))
    p = jnp.exp(s - m_new).astype(BF16)
    v1 = jnp.concatenate([v, jnp.ones_like(v)], axis=1)
    return m_new, jnp.exp(m - m_new) * acc + _dot(p, v1)


def _diff_combine(acc, lam, t):
    dv = acc.shape[1] // 2
    o = acc[:, :dv] / acc[:, dv:]
    return o[:t] - lam * o[t:]


def _lane_rms(o):
    return o * lax.rsqrt(jnp.mean(o * o, axis=-1, keepdims=True) + EPS)


def _retention_stage1(q, k, v, r, dmat, xi, zeta, cdecay):
    s = (_nt_dot(q, k) * dmat).astype(BF16)
    cross = _dot(q, r.astype(BF16)) * xi
    kz = (k.astype(F32) * zeta).astype(BF16)
    r_new = cdecay * r + _tn_dot(kz, v)
    return s, cross, r_new


def _retention_stage2(s, cross, v):
    return _dot(s, v) + cross


def _mixer_prompt_kernel(scal_ref, qt_ref, k_ref, vt_ref, mix_ref,
                         dbias_ref, kbias_ref, dmat_ref, xi_ref, zeta_ref, sg_ref,
                         x_ref, gate_ref, woa_ref, wob_ref, wout_ref,
                         y_ref, rout_ref, r_scr, acc_scr, oa_scr, ob_scr, merged_scr, *s_scrs,
                         tq, kblk, nb, nq, n_tiles):
    chains = [(bi, j) for bi in range(nb) for j in range(H_A)]
    hps = len(chains)
    nsub = tq // kblk
    t = pl.program_id(0)
    live = t < n_tiles
    qi = lax.rem(jnp.minimum(t, n_tiles - 1), nq)
    lam = scal_ref[0]
    lanes = lambda j: slice(j * HEAD_W, (j + 1) * HEAD_W)
    n_rep = lambda width: width // HEAD_W

    def split_halves_t(qt):
        feat = lax.broadcasted_iota(jnp.int32, qt.shape, 0)
        zero = jnp.zeros_like(qt)
        return jnp.concatenate([jnp.where(feat < DK_A, qt, zero), jnp.where(feat >= DK_A, qt, zero)], axis=1)

    qst = [split_halves_t(qt_ref[bi, lanes(j), :]) for bi, j in chains]

    head = lambda c: chains[c][1]

    def keys(c, kb):
        bi, j = chains[c]
        return k_ref[bi, pl.ds(pl.multiple_of(kb * kblk, kblk), kblk), lanes(j)]

    def values1(c, kb):
        bi, j = chains[c]
        vt = vt_ref[bi, lanes(j), pl.ds(pl.multiple_of(kb * kblk, kblk), kblk)]
        return jnp.concatenate([vt, jnp.ones((SUM_ROWS, kblk), BF16)], axis=0)

    def stage_scores(c, kb, slot):
        s = _dot(keys(c, kb), qst[c])
        s = s + jnp.concatenate([kbias_ref[head(c)]] * n_rep(2 * tq), axis=1)
        s_scrs[c][slot] = s
        return jnp.max(s, axis=0, keepdims=True)

    def softmax_step(m, s, s_max, shift):
        m_new = jnp.maximum(m, s_max + shift)
        return m_new, jnp.exp(m - m_new), jnp.exp(s - (m_new - shift)).astype(BF16)

    def body(kb, carry):
        slot = lax.rem(kb, 2)
        new = []
        for c in range(hps):
            smax_cur, m = carry[c]
            s_cur = s_scrs[c][slot]
            smax_next = stage_scores(c, kb + 1, 1 - slot)
            shift = ((kb - nsub * qi) * kblk).astype(F32) * scal_ref[1 + head(c)]
            m, alpha, p = softmax_step(m, s_cur, smax_cur, shift)
            acc_scr[c] = alpha * acc_scr[c] + _dot(values1(c, kb), p)
            new.append((smax_next, m))
        return tuple(new)

    @pl.when(t == 0)
    def _():
        oa_scr[...] = jnp.zeros_like(oa_scr)
        ob_scr[...] = jnp.zeros_like(ob_scr)

    @pl.when(qi == 0)
    def _():
        r_scr[...] = jnp.zeros_like(r_scr)

    acc_scr[...] = jnp.zeros_like(acc_scr)
    init = tuple((stage_scores(c, 0, 0), jnp.full((1, 2 * tq), -jnp.inf, F32)) for c in range(hps))
    d_model = x_ref.shape[2]
    gates = gate_ref[...].reshape(nb * tq, 2 * d_model)
    merged_scr[...] = (gates[:, :d_model].astype(F32) * _dot(oa_scr[...], woa_ref[...])
                       + gates[:, d_model:].astype(F32) * _dot(ob_scr[...], wob_ref[...])).astype(BF16)
    n_before = nsub * qi
    carry = lax.fori_loop(0, jnp.where(live, n_before, 0), body, init)

    def retention_stage1(sub, state):
        rows = slice(sub * kblk, (sub + 1) * kblk)
        return [_retention_stage1(
            _slab(mix_ref, "qb", j, bi)[rows], _slab(mix_ref, "kb", j, bi)[rows], _slab(mix_ref, "vb", j, bi)[rows],
            state[c], dmat_ref[j], xi_ref[j], zeta_ref[j], scal_ref[1 + H_A + j])
            for c, (bi, j) in enumerate(chains)]

    def retention_stage2(sub, ret):
        rows = slice(sub * kblk, (sub + 1) * kblk)
        for c, (bi, j) in enumerate(chains):
            ob = _retention_stage2(ret[c][0], ret[c][1], _slab(mix_ref, "vb", j, bi)[rows])
            ob = (_lane_rms(ob) * _slab(mix_ref, "zb", j, bi)[rows].astype(F32)).astype(BF16)
            ob_scr[bi * tq + sub * kblk:bi * tq + (sub + 1) * kblk, lanes(j)] = ob

    def cat(pieces):
        pieces = [p for p in pieces if p.shape[1] > 0]
        return pieces[0] if len(pieces) == 1 else jnp.concatenate(pieces, axis=1)

    def diagonal_softmax(dsub, ms):
        kb = n_before + dsub
        w = tq - dsub * kblk
        cols = (slice(dsub * kblk, tq), slice(tq + dsub * kblk, 2 * tq))
        new_ms, staged = [], []
        for c in range(hps):
            if dsub == 0:
                s = s_scrs[c][lax.rem(n_before, 2)]
            else:
                s = (_dot(keys(c, kb), cat([qst[c][:, q] for q in cols]))
                     + jnp.concatenate([kbias_ref[head(c)]] * n_rep(2 * w), axis=1))
            dbias = dbias_ref[head(c)]
            s = cat([s[:, :kblk] + dbias, s[:, kblk:w], s[:, w:w + kblk] + dbias, s[:, w + kblk:]])
            m = ms[c]
            shift = (dsub * kblk) * scal_ref[1 + head(c)]
            m_part, alpha, p = softmax_step(cat([m[:, q] for q in cols]), s,
                                            jnp.max(s, axis=0, keepdims=True), shift)
            new_ms.append(cat([m[:, :dsub * kblk], m_part[:, :w], m[:, tq:tq + dsub * kblk], m_part[:, w:]]))
            staged.append((alpha, p))
        return new_ms, staged

    def diagonal_accumulate(dsub, staged):
        kb = n_before + dsub
        w = tq - dsub * kblk
        cols = (slice(dsub * kblk, tq), slice(tq + dsub * kblk, 2 * tq))
        for c in range(hps):
            alpha, p = staged[c]
            pv = _dot(values1(c, kb), p)
            for h, q in enumerate(cols):
                acc_scr[c, :, q] = alpha[:, h * w:(h + 1) * w] * acc_scr[c, :, q] + pv[:, h * w:(h + 1) * w]

    def final_projection(cols):
        y = x_ref[:, :, cols].reshape(nb * tq, -1) + _dot(merged_scr[...], wout_ref[:, cols])
        y_ref[:, :, cols] = y.reshape(nb, tq, -1)

    out_cols = [slice(c * d_model // nsub, (c + 1) * d_model // nsub) for c in range(nsub)]
    state = [r_scr[c] for c in range(hps)]
    ms = [carry[c][1] for c in range(hps)]
    for sub in range(nsub):
        ret = retention_stage1(sub, state)
        state = [r[2] for r in ret]
        ms, staged = diagonal_softmax(sub, ms)
        if sub > 0:
            final_projection(out_cols[sub - 1])
        diagonal_accumulate(sub, staged)
        retention_stage2(sub, ret)
    final_projection(out_cols[nsub - 1])
    for c, (bi, j) in enumerate(chains):
        r_new = jnp.where(live, state[c], r_scr[c])
        r_scr[c] = r_new
        rout_ref[bi, j] = r_new
    for c, (bi, j) in enumerate(chains):
        ot = acc_scr[c, :HEAD_W] * (1.0 / acc_scr[c, HEAD_W:HEAD_W + 1])
        oa = (ot[:, :tq] - lam * ot[:, tq:]).T
        oa = (_lane_rms(oa) * sg_ref[0:1, :]) * sg_ref[1:2, :]
        oa = (oa * _slab(mix_ref, "za", j, bi)[...].astype(F32)).astype(BF16)
        oa_scr[bi * tq:(bi + 1) * tq, lanes(j)] = oa


def _mixer_prompt(scal, qat, kab, vat, mix, dbias, kbias, dmat, xi, zeta, sg, x, gates, woa, wob, wout):
    b, t, w = kab.shape
    d = x.shape[2]
    tq, kblk, nb = Q_TILE, KEY_BLOCK, SEQS_PER_STEP
    assert t % tq == 0 and tq % kblk == 0 and kblk % CHUNK == 0 and H_A == H_B and w == H_A * HEAD_W
    assert dbias.shape[1:] == (kblk, kblk) and d % (tq // kblk) == 0 and b % nb == 0
    nq = t // tq
    n_tiles = (b // nb) * nq
    cur = lambda s: jnp.minimum(s, n_tiles - 1)
    prev = lambda s: jnp.maximum(s - 1, 0)
    tile = pl.BlockSpec((nb, tq, mix.shape[2]), lambda s: (cur(s) // nq, cur(s) % nq, 0))
    tile_t = pl.BlockSpec((nb, w, tq), lambda s: (cur(s) // nq, 0, cur(s) % nq))
    whole = pl.BlockSpec((nb, t, w), lambda s: (cur(s) // nq, 0, 0))
    whole_t = pl.BlockSpec((nb, w, t), lambda s: (cur(s) // nq, 0, 0))
    rows = lambda a: pl.BlockSpec((nb, tq, a.shape[2]), lambda s: (prev(s) // nq, prev(s) % nq, 0))
    full = lambda a: pl.BlockSpec(a.shape, lambda s: (0,) * a.ndim, pipeline_mode=pl.Buffered(1))
    return pl.pallas_call(
        functools.partial(_mixer_prompt_kernel, tq=tq, kblk=kblk, nb=nb, nq=nq, n_tiles=n_tiles),
        grid=(n_tiles + 1,),
        in_specs=[pl.BlockSpec(memory_space=pltpu.SMEM),
                  tile_t, whole, whole_t, tile,
                  full(dbias), full(kbias), full(dmat), full(xi), full(zeta), full(sg),
                  rows(x), rows(gates), full(woa), full(wob), full(wout)],
        out_specs=[rows(x),
                   pl.BlockSpec((nb, H_B, DK_B, DV_B), lambda s: (cur(s) // nq, 0, 0, 0))],
        out_shape=[jax.ShapeDtypeStruct((b, t, d), F32),
                   jax.ShapeDtypeStruct((b, H_B, DK_B, DV_B), F32)],
        scratch_shapes=[pltpu.VMEM((nb * H_B, DK_B, DV_B), F32),
                        pltpu.VMEM((nb * H_A, HEAD_W + SUM_ROWS, 2 * tq), F32),
                        pltpu.VMEM((nb * tq, w), BF16), pltpu.VMEM((nb * tq, w), BF16),
                        pltpu.VMEM((nb * tq, d), BF16)]
                       + [pltpu.VMEM((2, kblk, 2 * tq), F32)] * (nb * H_A),
        compiler_params=pltpu.CompilerParams(
            dimension_semantics=("arbitrary",), vmem_limit_bytes=VMEM_LIMIT),
        name="mixer_prompt",
    )(scal, qat, kab, vat, mix, dbias, kbias, dmat, xi, zeta, sg, x, gates, woa, wob, wout)


def _mixer_sample_kernel(scal_ref, qa_ref, kc_ref, vc_ref, kn_ref, vn_ref, mix_ref, r_ref,
                         abias_ref, dmat_ref, xi_ref, zeta_ref, sg_ref,
                         oa_ref, ob_ref, rout_ref, m_scr, acc_scr, *, n, past, tk, sub):
    kbi = pl.program_id(1)
    lam = scal_ref[0]
    lanes = lambda j: slice(j * HEAD_W, (j + 1) * HEAD_W)

    @pl.when(kbi == 0)
    def _():
        m_scr[...] = jnp.full(m_scr.shape, -jnp.inf, F32)
        acc_scr[...] = jnp.zeros_like(acc_scr)

    col = lax.broadcasted_iota(jnp.int32, (1, sub), 1)
    qs = [_split_halves(qa_ref[0, :, lanes(j)]) for j in range(H_A)]

    def scores(j, c):
        kk = kc_ref[0, lanes(j), c * sub:(c + 1) * sub].astype(BF16)
        kpos = (col + (kbi * tk + (c * sub - past))).astype(F32)
        return _dot(qs[j], kk) + kpos * scal_ref[1 + j]

    blocks = [(j, c) for c in range(tk // sub) for j in range(H_A)]
    carry = [(m_scr[j], acc_scr[j]) for j in range(H_A)]
    staged = [scores(*blk) for blk in blocks[:SCORE_LOOKAHEAD]]
    for i, (j, c) in enumerate(blocks):
        if i + SCORE_LOOKAHEAD < len(blocks):
            staged.append(scores(*blocks[i + SCORE_LOOKAHEAD]))
        vv = vc_ref[0, pl.ds(c * sub * H_A + j, sub, stride=H_A), :].astype(BF16)
        carry[j] = _softmax_update(carry[j], staged[i], vv)
    for j in range(H_A):
        m_scr[j], acc_scr[j] = carry[j]

    @pl.when(kbi == pl.num_programs(1) - 1)
    def _():
        s_new, ret = [], []
        for j in range(H_A):
            bias = abias_ref[j]
            s_new.append(_nt_dot(qs[j], kn_ref[0, :, lanes(j)]) + jnp.concatenate([bias, bias], axis=0))
        for j in range(H_A):
            s_ret, cross, r_new = _retention_stage1(
                _slab(mix_ref, "qb", j)[...], _slab(mix_ref, "kb", j)[...], _slab(mix_ref, "vb", j)[...],
                r_ref[0, j], dmat_ref[j], xi_ref[j], zeta_ref[j], scal_ref[1 + H_A + j])
            rout_ref[0, j] = r_new
            ret.append((s_ret, cross))
        for j in range(H_A):
            _, acc = _softmax_update((m_scr[j], acc_scr[j]), s_new[j], vn_ref[0, :, lanes(j)])
            oa = _diff_combine(acc, lam, n)
            oa = (_lane_rms(oa) * sg_ref[0:1, :]) * sg_ref[1:2, :]
            oa_ref[0, :, lanes(j)] = (oa * _slab(mix_ref, "za", j)[...].astype(F32)).astype(BF16)
        for j in range(H_A):
            ob = _retention_stage2(*ret[j], _slab(mix_ref, "vb", j)[...])
            ob_ref[0, :, lanes(j)] = (_lane_rms(ob) * _slab(mix_ref, "zb", j)[...].astype(F32)).astype(BF16)


def _mixer_sample(scal, qa, kc, vc, kab, vab, mix, r0, abias, dmat, xi, zeta, sg):
    b, n, w = qa.shape
    past = kc.shape[2]
    tk, sub = SAMPLE_KEY_TILE, SAMPLE_KEY_BLOCK
    assert past % tk == 0 and tk % sub == 0 and past % CHUNK == 0 and n <= CHUNK and H_A == H_B
    tile = pl.BlockSpec((1, n, w), lambda bi, ki: (bi, 0, 0))
    mix_tile = pl.BlockSpec((1, n, mix.shape[2]), lambda bi, ki: (bi, 0, 0))
    cache_k = pl.BlockSpec((1, w, tk), lambda bi, ki: (bi, 0, ki))
    cache_v = pl.BlockSpec((1, tk * H_A, HEAD_W), lambda bi, ki: (bi, ki, 0))
    state = pl.BlockSpec((1, H_B, DK_B, DV_B), lambda bi, ki: (bi, 0, 0, 0))
    full = lambda a: pl.BlockSpec(a.shape, lambda bi, ki: (0,) * a.ndim)
    return pl.pallas_call(
        functools.partial(_mixer_sample_kernel, n=n, past=past, tk=tk, sub=sub),
        grid=(b, past // tk),
        in_specs=[pl.BlockSpec(memory_space=pltpu.SMEM),
                  tile, cache_k, cache_v, tile, tile, mix_tile, state,
                  full(abias), full(dmat), full(xi), full(zeta), full(sg)],
        out_specs=[tile, tile, state],
        out_shape=[jax.ShapeDtypeStruct((b, n, H_A * HEAD_W), BF16),
                   jax.ShapeDtypeStruct((b, n, H_B * HEAD_W), BF16),
                   jax.ShapeDtypeStruct((b, H_B, DK_B, DV_B), F32)],
        scratch_shapes=[pltpu.VMEM((H_A, 2 * n, 1), F32),
                        pltpu.VMEM((H_A, 2 * n, 2 * DV_A), F32)],
        compiler_params=pltpu.CompilerParams(
            dimension_semantics=("arbitrary", "arbitrary"), vmem_limit_bytes=VMEM_LIMIT),
        name="mixer_sample",
    )(scal, qa, kc, vc, kab, vab, mix, r0, abias, dmat, xi, zeta, sg)


def _proj_out_kernel(x_ref, oa_ref, ob_ref, gate_ref, woa_ref, wob_ref, wout_ref, y_ref):
    d = x_ref.shape[1]
    m = (gate_ref[:, :d].astype(F32) * _dot(oa_ref[...], woa_ref[...])
         + gate_ref[:, d:].astype(F32) * _dot(ob_ref[...], wob_ref[...]))
    y_ref[...] = x_ref[...] + _dot(m.astype(BF16), wout_ref[...])


def _proj_out(x2, oa, ob, gates, woa, wob, wout):
    n, d = x2.shape
    tm = min(ROW_TILE, n)
    assert n % tm == 0
    row = lambda w: pl.BlockSpec((tm, w), lambda i: (i, 0))
    full = lambda a: pl.BlockSpec(a.shape, lambda i: (0,) * a.ndim)
    return pl.pallas_call(
        _proj_out_kernel,
        grid=(n // tm,),
        in_specs=[row(d), row(oa.shape[1]), row(ob.shape[1]), row(gates.shape[1]),
                  full(woa), full(wob), full(wout)],
        out_specs=row(d),
        out_shape=jax.ShapeDtypeStruct((n, d), F32),
        compiler_params=pltpu.CompilerParams(
            dimension_semantics=("arbitrary",), vmem_limit_bytes=VMEM_LIMIT),
        name="proj_out",
    )(x2, oa, ob, gates, woa, wob, wout)


def _block_tables(t, slopes, log_g):
    f32 = np.float32
    i = np.arange(t, dtype=np.int32)
    allowed = (i[None, :] // CHUNK) <= (i[:, None] // CHUNK)
    dist = np.abs(i[:, None] - i[None, :]).astype(f32)
    jf = i.astype(f32)
    abias = -slopes[:, None, None] * dist[None] + slopes[:, None, None] * jf[None, :, None]
    abias = np.where(allowed[None], abias, f32(-np.inf)).astype(f32)
    dmat = np.where(allowed[None], np.exp(dist[None] * log_g[:, None, None]), f32(0.0)).astype(f32)
    xi = np.exp((jf + f32(1.0))[None] * log_g[:, None]).astype(f32)
    zeta = np.exp((f32(t) - f32(1.0) - jf)[None] * log_g[:, None]).astype(f32)
    bc = lambda a: np.ascontiguousarray(np.broadcast_to(a[:, :, None], a.shape + (HEAD_W,)))
    return abias, dmat, bc(xi), bc(zeta)


def _key_bias_table(t, slopes):
    kb = (slopes[:, None] * np.arange(t, dtype=np.float32)[None]).astype(np.float32)
    return np.ascontiguousarray(np.broadcast_to(kb[:, :, None], kb.shape + (HEAD_W,)))


def _layer(x_prompt, x_sample, ck, cv, r0, norm_g, w_in, b_gate, qn_g, kn_g,
           lam_q1, lam_k1, lam_q2, lam_k2, subln_g, w_oa, w_ob, w_out, lam_init):
    bp, tp, d = x_prompt.shape
    bs, ts, _ = x_sample.shape
    past = ck.shape[1]
    slopes = (2.0 ** (-8.0 * np.arange(1, H_A + 1, dtype=np.float32) / H_A)).astype(np.float32)
    log_g = np.log(1.0 - 2.0 ** (-5.0 - np.arange(H_B, dtype=np.float32))).astype(np.float32)
    lam = (jnp.exp(jnp.sum(lam_q1 * lam_k1).astype(F32))
           - jnp.exp(jnp.sum(lam_q2 * lam_k2).astype(F32)) + lam_init)

    woa_bf, wob_bf, wout_bf = w_oa.astype(BF16), w_ob.astype(BF16), w_out.astype(BF16)
    g = norm_g.reshape(1, d)
    qg_t = jnp.tile(qn_g, 2 * H_A).reshape(1, H_A * HEAD_W)
    kg_t = jnp.tile(kn_g, 2 * H_A).reshape(1, H_A * HEAD_W)
    grp = np.arange(MXU_TILE, dtype=np.int32) // DK_A
    gm = jnp.asarray(grp[:, None] == grp[None, :], dtype=BF16)
    sg = jnp.stack([subln_g, jnp.full((DV_A,), 1.0 - lam_init, F32)])

    def scal(n_block):
        consts = np.concatenate([slopes, np.exp(np.float32(n_block) * log_g)]).astype(np.float32)
        return jnp.concatenate([lam.reshape(1).astype(F32), jnp.asarray(consts)])

    qat, kaf, kab, vaf, vat, mix, gates = _proj_in(
        x_prompt, g, w_in, b_gate, qg_t, kg_t, gm, tm=ROW_TILE, feature_major=True)
    abias, dmat, xi, zeta = _block_tables(KEY_BLOCK, slopes, log_g)
    kbias = _key_bias_table(KEY_BLOCK, slopes)
    dbias = np.ascontiguousarray(np.swapaxes(abias, 1, 2) - kbias[:, :, :1])
    flat = lambda a: a.reshape(-1, a.shape[-1])
    y_p, ret_p = _mixer_prompt(scal(float(KEY_BLOCK)), qat, kab, vat, mix, dbias, kbias, dmat, xi, zeta, sg,
                               x_prompt, gates, woa_bf, wob_bf, wout_bf)
    k_p = jnp.transpose(kaf.reshape(bp, H_A, 2, DK_A, tp), (0, 4, 1, 2, 3))
    v_p = vaf.reshape(bp, tp, H_A, DV_A)

    qa, kaf, kab, vaf, vab, mix, gates = _proj_in(
        x_sample.reshape(1, bs * ts, d), g, w_in, b_gate, qg_t, kg_t, gm,
        tm=min(ROW_TILE, bs * ts), feature_major=False)
    r3 = lambda a: a.reshape(bs, ts, a.shape[-1])
    tabs = _block_tables(ts, slopes, log_g)
    ck_t = jnp.transpose(ck, (0, 2, 3, 4, 1)).reshape(bs, H_A * HEAD_W, past)
    oa, ob, ret_s = _mixer_sample(scal(float(ts)), r3(qa), ck_t, cv.reshape(bs, past * H_A, HEAD_W),
                                  r3(kab), r3(vab), r3(mix), r0, *tabs, sg)
    y_s = _proj_out(flat(x_sample), flat(oa), flat(ob), flat(gates),
                    woa_bf, wob_bf, wout_bf).reshape(bs, ts, d)
    k_s = kaf.reshape(bs, ts, H_A, 2, DK_A)
    v_s = vaf.reshape(bs, ts, H_A, DV_A)
    return y_p, y_s, k_p, v_p, ret_p, k_s, v_s, ret_s


def kernel(x_prompt, x_sample, cache_k_diff, cache_v_diff, state_ret, norm_g, w_in, b_gate, qn_g, kn_g,
           lam_q1, lam_k1, lam_q2, lam_k2, subln_g, w_o_diff, w_o_ret, w_out):
    depth = w_in.shape[0]
    hp, hs = x_prompt, x_sample
    outs = [[] for _ in range(6)]
    for l in range(depth):
        lam_init = 0.8 - 0.6 * math.exp(-0.3 * l)
        hp, hs, k_p, v_p, ret_p, k_s, v_s, ret_s = _layer(
            hp, hs, cache_k_diff[l], cache_v_diff[l], state_ret[l], norm_g[l], w_in[l], b_gate[l],
            qn_g[l], kn_g[l], lam_q1[l], lam_k1[l], lam_q2[l], lam_k2[l], subln_g[l],
            w_o_diff[l], w_o_ret[l], w_out[l], lam_init)
        for lst, a in zip(outs, (k_p, v_p, ret_p, k_s, v_s, ret_s)):
            lst.append(a)
    return (hp, hs) + tuple(jnp.stack(o) for o in outs)
```

```python
import functools
import math

import jax
import jax.numpy as jnp
import numpy as np
from jax import lax
from jax.experimental import pallas as pl
from jax.experimental.pallas import tpu as pltpu

F32 = jnp.float32
BF16 = jnp.bfloat16

CHUNK = 64
H_A = 4
DK_A = 64
DV_A = 2 * DK_A
H_B = 4
DK_B = 128
DV_B = 128
HEAD_W = 128
EPS = 1e-6

MXU_TILE = 256
ROW_TILE = 512
Q_TILE = 256
KEY_BLOCK = 256
SEQS_PER_STEP = 2
SUM_ROWS = 16
SAMPLE_KEY_TILE = 4096
SAMPLE_KEY_BLOCK = 1024
SCORE_LOOKAHEAD = 8
V7X_VMEM_BYTES = 64 * 1024 * 1024
VMEM_LIMIT = V7X_VMEM_BYTES - 3 * 1024 * 1024


def _nt_dot(a, b):
    return lax.dot_general(a, b, (((1,), (1,)), ((), ())), preferred_element_type=F32)


def _tn_dot(a, b):
    return lax.dot_general(a, b, (((0,), (0,)), ((), ())), preferred_element_type=F32)


def _dot(a, b):
    return jnp.dot(a, b, preferred_element_type=F32)


def _sigmoid(x):
    return 1.0 / (1.0 + jnp.exp(-x))


MIX_SLABS = ("za", "qb", "kb", "vb", "zb")


def _slab(mix_ref, name, head=None, seq=0):
    lo = MIX_SLABS.index(name) * H_A * HEAD_W
    if head is None:
        return mix_ref.at[seq, :, lo:lo + H_A * HEAD_W]
    return mix_ref.at[seq, :, lo + head * HEAD_W:lo + (head + 1) * HEAD_W]


def _proj_in_kernel(x_ref, g_ref, w_ref, bg_ref, qg_ref, kg_ref, gm_ref,
                    qa_ref, kaf_ref, kab_ref, vaf_ref, vab_ref, mix_ref, gate_ref,
                    *, wa, wb, d_model, feature_major):
    x = x_ref[0]
    tm = x.shape[0]
    ms = jnp.mean(x * x, axis=-1, keepdims=True)
    xn = (x * lax.rsqrt(ms + EPS)) * g_ref[...]

    names = ("qa", "ka", "va", "za", "qb", "kb", "vb", "zb", "ga", "gb")
    widths = (wa, wa, wa, wa, wb, wb, wb, wb, d_model, d_model)
    offs = dict(zip(names, (sum(widths[:i]) for i in range(len(widths)))))
    width = dict(zip(names, widths))

    def seg(name):
        return _dot(xn, w_ref[:, offs[name]:offs[name] + width[name]])

    def group_norm(h, gain):
        sq = (h * h).astype(BF16)
        ss = jnp.concatenate([_dot(sq[:, c:c + MXU_TILE], gm_ref[...]) for c in range(0, wa, MXU_TILE)], axis=1)
        return (h * lax.rsqrt(ss * (1.0 / DK_A) + EPS)) * gain

    def silu(h):
        return h * _sigmoid(h)

    maybe_t = (lambda a: a.T) if feature_major else (lambda a: a)
    gate_ref[0, :, :d_model] = _sigmoid(seg("ga") + bg_ref[0:1, :]).astype(BF16)
    gate_ref[0, :, d_model:] = _sigmoid(seg("gb") + bg_ref[1:2, :]).astype(BF16)
    qa_ref[0] = maybe_t(group_norm(seg("qa"), qg_ref[...]) * (DK_A ** -0.5)).astype(BF16)
    ka = group_norm(seg("ka"), kg_ref[...])
    kaf_ref[0] = maybe_t(ka)
    kab_ref[0] = ka.astype(BF16)
    va = seg("va")
    vab_ref[0] = maybe_t(va).astype(BF16)
    for h in range(H_A):
        vaf_ref[pl.ds(h, tm, stride=H_A), :] = va[:, h * HEAD_W:(h + 1) * HEAD_W]
    _slab(mix_ref, "za")[...] = silu(seg("za")).astype(BF16)
    _slab(mix_ref, "zb")[...] = silu(seg("zb")).astype(BF16)
    _slab(mix_ref, "kb")[...] = (seg("kb") * (DK_B ** -0.5)).astype(BF16)
    _slab(mix_ref, "qb")[...] = seg("qb").astype(BF16)
    _slab(mix_ref, "vb")[...] = seg("vb").astype(BF16)


def _proj_in(x, g, w_in, bg, qg_t, kg_t, gm, *, tm, feature_major):
    b, t, d = x.shape
    wa = H_A * HEAD_W
    wb = H_B * HEAD_W
    assert t % tm == 0
    nt = t // tm
    row = lambda w: pl.BlockSpec((1, tm, w), lambda bi, i: (bi, i, 0))
    full = lambda a: pl.BlockSpec(a.shape, lambda bi, i: (0,) * a.ndim)
    act = lambda w, dt: jax.ShapeDtypeStruct((b, t, w), dt)
    if feature_major:
        fm_shape = lambda dt: jax.ShapeDtypeStruct((b, wa, t), dt)
        fm_spec = pl.BlockSpec((1, wa, tm), lambda bi, i: (bi, 0, i))
    else:
        fm_shape = lambda dt: act(wa, dt)
        fm_spec = row(wa)
    assert wa == wb
    out_shapes = [
        fm_shape(BF16),
        fm_shape(F32), act(wa, BF16),
        jax.ShapeDtypeStruct((b * t * H_A, HEAD_W), F32),
        fm_shape(BF16),
        act(len(MIX_SLABS) * wa, BF16),
        act(2 * d, BF16),
    ]
    out_specs = [fm_spec, fm_spec, row(wa),
                 pl.BlockSpec((tm * H_A, HEAD_W), lambda bi, i: (bi * nt + i, 0)),
                 fm_spec, row(len(MIX_SLABS) * wa), row(2 * d)]
    return pl.pallas_call(
        functools.partial(_proj_in_kernel, wa=wa, wb=wb, d_model=d, feature_major=feature_major),
        grid=(b, nt),
        in_specs=[row(d), full(g), full(w_in), full(bg), full(qg_t), full(kg_t), full(gm)],
        out_specs=out_specs,
        out_shape=out_shapes,
        compiler_params=pltpu.CompilerParams(
            dimension_semantics=("arbitrary", "arbitrary"), vmem_limit_bytes=VMEM_LIMIT),
        name="proj_in",
    )(x, g, w_in, bg, qg_t, kg_t, gm)


def _split_halves(q):
    lane = lax.broadcasted_iota(jnp.int32, q.shape, 1)
    zero = jnp.zeros_like(q)
    return jnp.concatenate([jnp.where(lane < DK_A, q, zero), jnp.where(lane >= DK_A, q, zero)], axis=0)


def _softmax_update(carry, s, v):
    m, acc = carry
    m_new = jnp.maximum(m, jnp.max(s, axis=-1, keepdims=True))
    p = jnp.exp(s - m_new).astype(BF16)
    v1 = jnp.concatenate([v, jnp.ones_like(v)], axis=1)
    return m_new, jnp.exp(m - m_new) * acc + _dot(p, v1)


def _diff_combine(acc, lam, t):
    dv = acc.shape[1] // 2
    o = acc[:, :dv] / acc[:, dv:]
    return o[:t] - lam * o[t:]


def _lane_rms(o):
    return o * lax.rsqrt(jnp.mean(o * o, axis=-1, keepdims=True) + EPS)


def _retention_stage1(q, k, v, r, dmat, xi, zeta, cdecay):
    s = (_nt_dot(q, k) * dmat).astype(BF16)
    cross = _dot(q, r.astype(BF16)) * xi
    kz = (k.astype(F32) * zeta).astype(BF16)
    r_new = cdecay * r + _tn_dot(kz, v)
    return s, cross, r_new


def _retention_stage2(s, cross, v):
    return _dot(s, v) + cross


def _mixer_prompt_kernel(scal_ref, qt_ref, k_ref, vt_ref, mix_ref,
                         dbias_ref, kbias_ref, dmat_ref, xi_ref, zeta_ref, sg_ref,
                         x_ref, gate_ref, woa_ref, wob_ref, wout_ref,
                         y_ref, rout_ref, r_scr, acc_scr, oa_scr, ob_scr, merged_scr, *s_scrs,
                         tq, kblk, nb, nq, n_tiles):
    chains = [(bi, j) for bi in range(nb) for j in range(H_A)]
    hps = len(chains)
    nsub = tq // kblk
    t = pl.program_id(0)
    live = t < n_tiles
    qi = lax.rem(jnp.minimum(t, n_tiles - 1), nq)
    lam = scal_ref[0]
    lanes = lambda j: slice(j * HEAD_W, (j + 1) * HEAD_W)
    n_rep = lambda width: width // HEAD_W

    def split_halves_t(qt):
        feat = lax.broadcasted_iota(jnp.int32, qt.shape, 0)
        zero = jnp.zeros_like(qt)
        return jnp.concatenate([jnp.where(feat < DK_A, qt, zero), jnp.where(feat >= DK_A, qt, zero)], axis=1)

    qst = [split_halves_t(qt_ref[bi, lanes(j), :]) for bi, j in chains]

    head = lambda c: chains[c][1]

    def keys(c, kb):
        bi, j = chains[c]
        return k_ref[bi, pl.ds(pl.multiple_of(kb * kblk, kblk), kblk), lanes(j)]

    def values1(c, kb):
        bi, j = chains[c]
        vt = vt_ref[bi, lanes(j), pl.ds(pl.multiple_of(kb * kblk, kblk), kblk)]
        return jnp.concatenate([vt, jnp.ones((SUM_ROWS, kblk), BF16)], axis=0)

    def stage_scores(c, kb, slot):
        s = _dot(keys(c, kb), qst[c])
        s = s + jnp.concatenate([kbias_ref[head(c)]] * n_rep(2 * tq), axis=1)
        s_scrs[c][slot] = s
        return jnp.max(s, axis=0, keepdims=True)

    def softmax_step(m, s, s_max, shift):
        m_new = jnp.maximum(m, s_max + shift)
        return m_new, jnp.exp(m - m_new), jnp.exp(s - (m_new - shift)).astype(BF16)

    def body(kb, carry):
        slot = lax.rem(kb, 2)
        new = []
        for c in range(hps):
            smax_cur, m = carry[c]
            s_cur = s_scrs[c][slot]
            smax_next = stage_scores(c, kb + 1, 1 - slot)
            shift = ((kb - nsub * qi) * kblk).astype(F32) * scal_ref[1 + head(c)]
            m, alpha, p = softmax_step(m, s_cur, smax_cur, shift)
            acc_scr[c] = alpha * acc_scr[c] + _dot(values1(c, kb), p)
            new.append((smax_next, m))
        return tuple(new)

    @pl.when(t == 0)
    def _():
        oa_scr[...] = jnp.zeros_like(oa_scr)
        ob_scr[...] = jnp.zeros_like(ob_scr)

    @pl.when(qi == 0)
    def _():
        r_scr[...] = jnp.zeros_like(r_scr)

    acc_scr[...] = jnp.zeros_like(acc_scr)
    init = tuple((stage_scores(c, 0, 0), jnp.full((1, 2 * tq), -jnp.inf, F32)) for c in range(hps))
    d_model = x_ref.shape[2]
    gates = gate_ref[...].reshape(nb * tq, 2 * d_model)
    merged_scr[...] = (gates[:, :d_model].astype(F32) * _dot(oa_scr[...], woa_ref[...])
                       + gates[:, d_model:].astype(F32) * _dot(ob_scr[...], wob_ref[...])).astype(BF16)
    n_before = nsub * qi
    carry = lax.fori_loop(0, jnp.where(live, n_before, 0), body, init)

    def retention_stage1(sub, state):
        rows = slice(sub * kblk, (sub + 1) * kblk)
        return [_retention_stage1(
            _slab(mix_ref, "qb", j, bi)[rows], _slab(mix_ref, "kb", j, bi)[rows], _slab(mix_ref, "vb", j, bi)[rows],
            state[c], dmat_ref[j], xi_ref[j], zeta_ref[j], scal_ref[1 + H_A + j])
            for c, (bi, j) in enumerate(chains)]

    def retention_stage2(sub, ret):
        rows = slice(sub * kblk, (sub + 1) * kblk)
        for c, (bi, j) in enumerate(chains):
            ob = _retention_stage2(ret[c][0], ret[c][1], _slab(mix_ref, "vb", j, bi)[rows])
            ob = (_lane_rms(ob) * _slab(mix_ref, "zb", j, bi)[rows].astype(F32)).astype(BF16)
            ob_scr[bi * tq + sub * kblk:bi * tq + (sub + 1) * kblk, lanes(j)] = ob

    def cat(pieces):
        pieces = [p for p in pieces if p.shape[1] > 0]
        return pieces[0] if len(pieces) == 1 else jnp.concatenate(pieces, axis=1)

    def diagonal_block(dsub, c, m):
        kb = n_before + dsub
        w = tq - dsub * kblk
        cols = (slice(dsub * kblk, tq), slice(tq + dsub * kblk, 2 * tq))
        if dsub == 0:
            s = s_scrs[c][lax.rem(n_before, 2)]
        else:
            s = (_dot(keys(c, kb), cat([qst[c][:, q] for q in cols]))
                 + jnp.concatenate([kbias_ref[head(c)]] * n_rep(2 * w), axis=1))
        dbias = dbias_ref[head(c)]
        s = cat([s[:, :kblk] + dbias, s[:, kblk:w], s[:, w:w + kblk] + dbias, s[:, w + kblk:]])
        shift = (dsub * kblk) * scal_ref[1 + head(c)]
        m_part, alpha, p = softmax_step(cat([m[:, q] for q in cols]), s,
                                        jnp.max(s, axis=0, keepdims=True), shift)
        pv = _dot(values1(c, kb), p)
        for h, q in enumerate(cols):
            acc_scr[c, :, q] = alpha[:, h * w:(h + 1) * w] * acc_scr[c, :, q] + pv[:, h * w:(h + 1) * w]
        return cat([m[:, :dsub * kblk], m_part[:, :w], m[:, tq:tq + dsub * kblk], m_part[:, w:]])

    def final_projection(cols):
        y = x_ref[:, :, cols].reshape(nb * tq, -1) + _dot(merged_scr[...], wout_ref[:, cols])
        y_ref[:, :, cols] = y.reshape(nb, tq, -1)

    out_cols = [slice(c, c + MXU_TILE) for c in range(0, d_model, MXU_TILE)]
    every = max(1, (hps * nsub) // len(out_cols))
    state = [r_scr[c] for c in range(hps)]
    ms = [carry[c][1] for c in range(hps)]
    for sub in range(nsub):
        ret = retention_stage1(sub, state)
        state = [r[2] for r in ret]
        for c in range(hps):
            if (sub * hps + c) % every == 0 and out_cols:
                final_projection(out_cols.pop(0))
            ms[c] = diagonal_block(sub, c, ms[c])
        retention_stage2(sub, ret)
    while out_cols:
        final_projection(out_cols.pop(0))
    for c, (bi, j) in enumerate(chains):
        r_new = jnp.where(live, state[c], r_scr[c])
        r_scr[c] = r_new
        rout_ref[bi, j] = r_new
    for c, (bi, j) in enumerate(chains):
        ot = acc_scr[c, :HEAD_W] * (1.0 / acc_scr[c, HEAD_W:HEAD_W + 1])
        oa = (ot[:, :tq] - lam * ot[:, tq:]).T
        oa = (_lane_rms(oa) * sg_ref[0:1, :]) * sg_ref[1:2, :]
        oa = (oa * _slab(mix_ref, "za", j, bi)[...].astype(F32)).astype(BF16)
        oa_scr[bi * tq:(bi + 1) * tq, lanes(j)] = oa


def _mixer_prompt(scal, qat, kab, vat, mix, dbias, kbias, dmat, xi, zeta, sg, x, gates, woa, wob, wout):
    b, t, w = kab.shape
    d = x.shape[2]
    tq, kblk, nb = Q_TILE, KEY_BLOCK, SEQS_PER_STEP
    assert t % tq == 0 and tq % kblk == 0 and kblk % CHUNK == 0 and H_A == H_B and w == H_A * HEAD_W
    assert dbias.shape[1:] == (kblk, kblk) and d % (tq // kblk) == 0 and b % nb == 0
    nq = t // tq
    n_tiles = (b // nb) * nq
    cur = lambda s: jnp.minimum(s, n_tiles - 1)
    prev = lambda s: jnp.maximum(s - 1, 0)
    tile = pl.BlockSpec((nb, tq, mix.shape[2]), lambda s: (cur(s) // nq, cur(s) % nq, 0))
    tile_t = pl.BlockSpec((nb, w, tq), lambda s: (cur(s) // nq, 0, cur(s) % nq))
    whole = pl.BlockSpec((nb, t, w), lambda s: (cur(s) // nq, 0, 0))
    whole_t = pl.BlockSpec((nb, w, t), lambda s: (cur(s) // nq, 0, 0))
    rows = lambda a: pl.BlockSpec((nb, tq, a.shape[2]), lambda s: (prev(s) // nq, prev(s) % nq, 0))
    full = lambda a: pl.BlockSpec(a.shape, lambda s: (0,) * a.ndim, pipeline_mode=pl.Buffered(1))
    return pl.pallas_call(
        functools.partial(_mixer_prompt_kernel, tq=tq, kblk=kblk, nb=nb, nq=nq, n_tiles=n_tiles),
        grid=(n_tiles + 1,),
        in_specs=[pl.BlockSpec(memory_space=pltpu.SMEM),
                  tile_t, whole, whole_t, tile,
                  full(dbias), full(kbias), full(dmat), full(xi), full(zeta), full(sg),
                  rows(x), rows(gates), full(woa), full(wob), full(wout)],
        out_specs=[rows(x),
                   pl.BlockSpec((nb, H_B, DK_B, DV_B), lambda s: (cur(s) // nq, 0, 0, 0))],
        out_shape=[jax.ShapeDtypeStruct((b, t, d), F32),
                   jax.ShapeDtypeStruct((b, H_B, DK_B, DV_B), F32)],
        scratch_shapes=[pltpu.VMEM((nb * H_B, DK_B, DV_B), F32),
                        pltpu.VMEM((nb * H_A, HEAD_W + SUM_ROWS, 2 * tq), F32),
                        pltpu.VMEM((nb * tq, w), BF16), pltpu.VMEM((nb * tq, w), BF16),
                        pltpu.VMEM((nb * tq, d), BF16)]
                       + [pltpu.VMEM((2, kblk, 2 * tq), F32)] * (nb * H_A),
        compiler_params=pltpu.CompilerParams(
            dimension_semantics=("arbitrary",), vmem_limit_bytes=VMEM_LIMIT),
        name="mixer_prompt",
    )(scal, qat, kab, vat, mix, dbias, kbias, dmat, xi, zeta, sg, x, gates, woa, wob, wout)


def _mixer_sample_kernel(scal_ref, qa_ref, kc_ref, vc_ref, kn_ref, vn_ref, mix_ref, r_ref,
                         abias_ref, dmat_ref, xi_ref, zeta_ref, sg_ref,
                         oa_ref, ob_ref, rout_ref, m_scr, acc_scr, *, n, past, tk, sub):
    kbi = pl.program_id(1)
    lam = scal_ref[0]
    lanes = lambda j: slice(j * HEAD_W, (j + 1) * HEAD_W)

    @pl.when(kbi == 0)
    def _():
        m_scr[...] = jnp.full(m_scr.shape, -jnp.inf, F32)
        acc_scr[...] = jnp.zeros_like(acc_scr)

    col = lax.broadcasted_iota(jnp.int32, (1, sub), 1)
    qs = [_split_halves(qa_ref[0, :, lanes(j)]) for j in range(H_A)]

    def scores(j, c):
        kk = kc_ref[0, lanes(j), c * sub:(c + 1) * sub].astype(BF16)
        kpos = (col + (kbi * tk + (c * sub - past))).astype(F32)
        return _dot(qs[j], kk) + kpos * scal_ref[1 + j]

    blocks = [(j, c) for c in range(tk // sub) for j in range(H_A)]
    carry = [(m_scr[j], acc_scr[j]) for j in range(H_A)]
    staged = [scores(*blk) for blk in blocks[:SCORE_LOOKAHEAD]]
    for i, (j, c) in enumerate(blocks):
        if i + SCORE_LOOKAHEAD < len(blocks):
            staged.append(scores(*blocks[i + SCORE_LOOKAHEAD]))
        vv = vc_ref[0, pl.ds(c * sub * H_A + j, sub, stride=H_A), :].astype(BF16)
        carry[j] = _softmax_update(carry[j], staged[i], vv)
    for j in range(H_A):
        m_scr[j], acc_scr[j] = carry[j]

    @pl.when(kbi == pl.num_programs(1) - 1)
    def _():
        s_new, ret = [], []
        for j in range(H_A):
            bias = abias_ref[j]
            s_new.append(_nt_dot(qs[j], kn_ref[0, :, lanes(j)]) + jnp.concatenate([bias, bias], axis=0))
        for j in range(H_A):
            s_ret, cross, r_new = _retention_stage1(
                _slab(mix_ref, "qb", j)[...], _slab(mix_ref, "kb", j)[...], _slab(mix_ref, "vb", j)[...],
                r_ref[0, j], dmat_ref[j], xi_ref[j], zeta_ref[j], scal_ref[1 + H_A + j])
            rout_ref[0, j] = r_new
            ret.append((s_ret, cross))
        for j in range(H_A):
            _, acc = _softmax_update((m_scr[j], acc_scr[j]), s_new[j], vn_ref[0, :, lanes(j)])
            oa = _diff_combine(acc, lam, n)
            oa = (_lane_rms(oa) * sg_ref[0:1, :]) * sg_ref[1:2, :]
            oa_ref[0, :, lanes(j)] = (oa * _slab(mix_ref, "za", j)[...].astype(F32)).astype(BF16)
        for j in range(H_A):
            ob = _retention_stage2(*ret[j], _slab(mix_ref, "vb", j)[...])
            ob_ref[0, :, lanes(j)] = (_lane_rms(ob) * _slab(mix_ref, "zb", j)[...].astype(F32)).astype(BF16)


def _mixer_sample(scal, qa, kc, vc, kab, vab, mix, r0, abias, dmat, xi, zeta, sg):
    b, n, w = qa.shape
    past = kc.shape[2]
    tk, sub = SAMPLE_KEY_TILE, SAMPLE_KEY_BLOCK
    assert past % tk == 0 and tk % sub == 0 and past % CHUNK == 0 and n <= CHUNK and H_A == H_B
    tile = pl.BlockSpec((1, n, w), lambda bi, ki: (bi, 0, 0))
    mix_tile = pl.BlockSpec((1, n, mix.shape[2]), lambda bi, ki: (bi, 0, 0))
    cache_k = pl.BlockSpec((1, w, tk), lambda bi, ki: (bi, 0, ki))
    cache_v = pl.BlockSpec((1, tk * H_A, HEAD_W), lambda bi, ki: (bi, ki, 0))
    state = pl.BlockSpec((1, H_B, DK_B, DV_B), lambda bi, ki: (bi, 0, 0, 0))
    full = lambda a: pl.BlockSpec(a.shape, lambda bi, ki: (0,) * a.ndim)
    return pl.pallas_call(
        functools.partial(_mixer_sample_kernel, n=n, past=past, tk=tk, sub=sub),
        grid=(b, past // tk),
        in_specs=[pl.BlockSpec(memory_space=pltpu.SMEM),
                  tile, cache_k, cache_v, tile, tile, mix_tile, state,
                  full(abias), full(dmat), full(xi), full(zeta), full(sg)],
        out_specs=[tile, tile, state],
        out_shape=[jax.ShapeDtypeStruct((b, n, H_A * HEAD_W), BF16),
                   jax.ShapeDtypeStruct((b, n, H_B * HEAD_W), BF16),
                   jax.ShapeDtypeStruct((b, H_B, DK_B, DV_B), F32)],
        scratch_shapes=[pltpu.VMEM((H_A, 2 * n, 1), F32),
                        pltpu.VMEM((H_A, 2 * n, 2 * DV_A), F32)],
        compiler_params=pltpu.CompilerParams(
            dimension_semantics=("arbitrary", "arbitrary"), vmem_limit_bytes=VMEM_LIMIT),
        name="mixer_sample",
    )(scal, qa, kc, vc, kab, vab, mix, r0, abias, dmat, xi, zeta, sg)


def _proj_out_kernel(x_ref, oa_ref, ob_ref, gate_ref, woa_ref, wob_ref, wout_ref, y_ref):
    d = x_ref.shape[1]
    m = (gate_ref[:, :d].astype(F32) * _dot(oa_ref[...], woa_ref[...])
         + gate_ref[:, d:].astype(F32) * _dot(ob_ref[...], wob_ref[...]))
    y_ref[...] = x_ref[...] + _dot(m.astype(BF16), wout_ref[...])


def _proj_out(x2, oa, ob, gates, woa, wob, wout):
    n, d = x2.shape
    tm = min(ROW_TILE, n)
    assert n % tm == 0
    row = lambda w: pl.BlockSpec((tm, w), lambda i: (i, 0))
    full = lambda a: pl.BlockSpec(a.shape, lambda i: (0,) * a.ndim)
    return pl.pallas_call(
        _proj_out_kernel,
        grid=(n // tm,),
        in_specs=[row(d), row(oa.shape[1]), row(ob.shape[1]), row(gates.shape[1]),
                  full(woa), full(wob), full(wout)],
        out_specs=row(d),
        out_shape=jax.ShapeDtypeStruct((n, d), F32),
        compiler_params=pltpu.CompilerParams(
            dimension_semantics=("arbitrary",), vmem_limit_bytes=VMEM_LIMIT),
        name="proj_out",
    )(x2, oa, ob, gates, woa, wob, wout)


def _block_tables(t, slopes, log_g):
    f32 = np.float32
    i = np.arange(t, dtype=np.int32)
    allowed = (i[None, :] // CHUNK) <= (i[:, None] // CHUNK)
    dist = np.abs(i[:, None] - i[None, :]).astype(f32)
    jf = i.astype(f32)
    abias = -slopes[:, None, None] * dist[None] + slopes[:, None, None] * jf[None, :, None]
    abias = np.where(allowed[None], abias, f32(-np.inf)).astype(f32)
    dmat = np.where(allowed[None], np.exp(dist[None] * log_g[:, None, None]), f32(0.0)).astype(f32)
    xi = np.exp((jf + f32(1.0))[None] * log_g[:, None]).astype(f32)
    zeta = np.exp((f32(t) - f32(1.0) - jf)[None] * log_g[:, None]).astype(f32)
    bc = lambda a: np.ascontiguousarray(np.broadcast_to(a[:, :, None], a.shape + (HEAD_W,)))
    return abias, dmat, bc(xi), bc(zeta)


def _key_bias_table(t, slopes):
    kb = (slopes[:, None] * np.arange(t, dtype=np.float32)[None]).astype(np.float32)
    return np.ascontiguousarray(np.broadcast_to(kb[:, :, None], kb.shape + (HEAD_W,)))


def _layer(x_prompt, x_sample, ck, cv, r0, norm_g, w_in, b_gate, qn_g, kn_g,
           lam_q1, lam_k1, lam_q2, lam_k2, subln_g, w_oa, w_ob, w_out, lam_init):
    bp, tp, d = x_prompt.shape
    bs, ts, _ = x_sample.shape
    past = ck.shape[1]
    slopes = (2.0 ** (-8.0 * np.arange(1, H_A + 1, dtype=np.float32) / H_A)).astype(np.float32)
    log_g = np.log(1.0 - 2.0 ** (-5.0 - np.arange(H_B, dtype=np.float32))).astype(np.float32)
    lam = (jnp.exp(jnp.sum(lam_q1 * lam_k1).astype(F32))
           - jnp.exp(jnp.sum(lam_q2 * lam_k2).astype(F32)) + lam_init)

    woa_bf, wob_bf, wout_bf = w_oa.astype(BF16), w_ob.astype(BF16), w_out.astype(BF16)
    g = norm_g.reshape(1, d)
    qg_t = jnp.tile(qn_g, 2 * H_A).reshape(1, H_A * HEAD_W)
    kg_t = jnp.tile(kn_g, 2 * H_A).reshape(1, H_A * HEAD_W)
    grp = np.arange(MXU_TILE, dtype=np.int32) // DK_A
    gm = jnp.asarray(grp[:, None] == grp[None, :], dtype=BF16)
    sg = jnp.stack([subln_g, jnp.full((DV_A,), 1.0 - lam_init, F32)])

    def scal(n_block):
        consts = np.concatenate([slopes, np.exp(np.float32(n_block) * log_g)]).astype(np.float32)
        return jnp.concatenate([lam.reshape(1).astype(F32), jnp.asarray(consts)])

    qat, kaf, kab, vaf, vat, mix, gates = _proj_in(
        x_prompt, g, w_in, b_gate, qg_t, kg_t, gm, tm=ROW_TILE, feature_major=True)
    abias, dmat, xi, zeta = _block_tables(KEY_BLOCK, slopes, log_g)
    kbias = _key_bias_table(KEY_BLOCK, slopes)
    dbias = np.ascontiguousarray(np.swapaxes(abias, 1, 2) - kbias[:, :, :1])
    flat = lambda a: a.reshape(-1, a.shape[-1])
    y_p, ret_p = _mixer_prompt(scal(float(KEY_BLOCK)), qat, kab, vat, mix, dbias, kbias, dmat, xi, zeta, sg,
                               x_prompt, gates, woa_bf, wob_bf, wout_bf)
    k_p = jnp.transpose(kaf.reshape(bp, H_A, 2, DK_A, tp), (0, 4, 1, 2, 3))
    v_p = vaf.reshape(bp, tp, H_A, DV_A)

    qa, kaf, kab, vaf, vab, mix, gates = _proj_in(
        x_sample.reshape(1, bs * ts, d), g, w_in, b_gate, qg_t, kg_t, gm,
        tm=min(ROW_TILE, bs * ts), feature_major=False)
    r3 = lambda a: a.reshape(bs, ts, a.shape[-1])
    tabs = _block_tables(ts, slopes, log_g)
    ck_t = jnp.transpose(ck, (0, 2, 3, 4, 1)).reshape(bs, H_A * HEAD_W, past)
    oa, ob, ret_s = _mixer_sample(scal(float(ts)), r3(qa), ck_t, cv.reshape(bs, past * H_A, HEAD_W),
                                  r3(kab), r3(vab), r3(mix), r0, *tabs, sg)
    y_s = _proj_out(flat(x_sample), flat(oa), flat(ob), flat(gates),
                    woa_bf, wob_bf, wout_bf).reshape(bs, ts, d)
    k_s = kaf.reshape(bs, ts, H_A, 2, DK_A)
    v_s = vaf.reshape(bs, ts, H_A, DV_A)
    return y_p, y_s, k_p, v_p, ret_p, k_s, v_s, ret_s


def kernel(x_prompt, x_sample, cache_k_diff, cache_v_diff, state_ret, norm_g, w_in, b_gate, qn_g, kn_g,
           lam_q1, lam_k1, lam_q2, lam_k2, subln_g, w_o_diff, w_o_ret, w_out):
    depth = w_in.shape[0]
    hp, hs = x_prompt, x_sample
    outs = [[] for _ in range(6)]
    for l in range(depth):
        lam_init = 0.8 - 0.6 * math.exp(-0.3 * l)
        hp, hs, k_p, v_p, ret_p, k_s, v_s, ret_s = _layer(
            hp, hs, cache_k_diff[l], cache_v_diff[l], state_ret[l], norm_g[l], w_in[l], b_gate[l],
            qn_g[l], kn_g[l], lam_q1[l], lam_k1[l], lam_q2[l], lam_k2[l], subln_g[l],
            w_o_diff[l], w_o_ret[l], w_out[l], lam_init)
        for lst, a in zip(outs, (k_p, v_p, ret_p, k_s, v_s, ret_s)):
            lst.append(a)
    return (hp, hs) + tuple(jnp.stack(o) for o in outs)
```

```python
import functools
import math

import jax
import jax.numpy as jnp
import numpy as np
from jax import lax
from jax.experimental import pallas as pl
from jax.experimental.pallas import tpu as pltpu

F32 = jnp.float32
BF16 = jnp.bfloat16

CHUNK = 64
H_A = 4
DK_A = 64
DV_A = 2 * DK_A
H_B = 4
DK_B = 128
DV_B = 128
HEAD_W = 128
EPS = 1e-6

MXU_TILE = 256
ROW_TILE = 512
SAMPLE_ROW_TILE = 256
Q_TILE = 256
KEY_BLOCK = 256
SEQS_PER_STEP = 2
SUM_ROWS = 16
SAMPLE_KEY_TILE = 4096
SAMPLE_KEY_BLOCK = 1024
SCORE_LOOKAHEAD = 8
V7X_VMEM_BYTES = 64 * 1024 * 1024
VMEM_LIMIT = V7X_VMEM_BYTES - 3 * 1024 * 1024


def _nt_dot(a, b):
    return lax.dot_general(a, b, (((1,), (1,)), ((), ())), preferred_element_type=F32)


def _tn_dot(a, b):
    return lax.dot_general(a, b, (((0,), (0,)), ((), ())), preferred_element_type=F32)


def _dot(a, b):
    return jnp.dot(a, b, preferred_element_type=F32)


def _sigmoid(x):
    return 1.0 / (1.0 + jnp.exp(-x))


MIX_SLABS = ("za", "qb", "kb", "vb", "zb")


def _slab(mix_ref, name, head=None, seq=0):
    lo = MIX_SLABS.index(name) * H_A * HEAD_W
    if head is None:
        return mix_ref.at[seq, :, lo:lo + H_A * HEAD_W]
    return mix_ref.at[seq, :, lo + head * HEAD_W:lo + (head + 1) * HEAD_W]


def _proj_in_kernel(x_ref, g_ref, w_ref, bg_ref, qg_ref, kg_ref, gm_ref,
                    qa_ref, kaf_ref, kab_ref, vaf_ref, vab_ref, mix_ref, gate_ref,
                    *, wa, wb, d_model, feature_major):
    x = x_ref[0]
    tm = x.shape[0]
    ms = jnp.mean(x * x, axis=-1, keepdims=True)
    xn = (x * lax.rsqrt(ms + EPS)) * g_ref[...]

    names = ("qa", "ka", "va", "za", "qb", "kb", "vb", "zb", "ga", "gb")
    widths = (wa, wa, wa, wa, wb, wb, wb, wb, d_model, d_model)
    offs = dict(zip(names, (sum(widths[:i]) for i in range(len(widths)))))
    width = dict(zip(names, widths))

    def seg(name):
        return _dot(xn, w_ref[:, offs[name]:offs[name] + width[name]])

    def group_norm(h, gain):
        sq = (h * h).astype(BF16)
        ss = jnp.concatenate([_dot(sq[:, c:c + MXU_TILE], gm_ref[...]) for c in range(0, wa, MXU_TILE)], axis=1)
        return (h * lax.rsqrt(ss * (1.0 / DK_A) + EPS)) * gain

    def silu(h):
        return h * _sigmoid(h)

    maybe_t = (lambda a: a.T) if feature_major else (lambda a: a)
    gate_ref[0, :, :d_model] = _sigmoid(seg("ga") + bg_ref[0:1, :]).astype(BF16)
    gate_ref[0, :, d_model:] = _sigmoid(seg("gb") + bg_ref[1:2, :]).astype(BF16)
    qa_ref[0] = maybe_t(group_norm(seg("qa"), qg_ref[...]) * (DK_A ** -0.5)).astype(BF16)
    ka = group_norm(seg("ka"), kg_ref[...])
    kaf_ref[0] = maybe_t(ka)
    kab_ref[0] = ka.astype(BF16)
    va = seg("va")
    vab_ref[0] = maybe_t(va).astype(BF16)
    for h in range(H_A):
        vaf_ref[pl.ds(h, tm, stride=H_A), :] = va[:, h * HEAD_W:(h + 1) * HEAD_W]
    _slab(mix_ref, "za")[...] = silu(seg("za")).astype(BF16)
    _slab(mix_ref, "zb")[...] = silu(seg("zb")).astype(BF16)
    _slab(mix_ref, "kb")[...] = (seg("kb") * (DK_B ** -0.5)).astype(BF16)
    _slab(mix_ref, "qb")[...] = seg("qb").astype(BF16)
    _slab(mix_ref, "vb")[...] = seg("vb").astype(BF16)


N_PROJ_OUTS = 7


def _proj_in_both_kernel(xp_ref, xs_ref, g_ref, w_ref, bg_ref, qg_ref, kg_ref, gm_ref, *out_refs,
                         n_prompt_steps, wa, wb, d_model):
    step = pl.program_id(0)
    consts = (g_ref, w_ref, bg_ref, qg_ref, kg_ref, gm_ref)

    @pl.when(step < n_prompt_steps)
    def _():
        _proj_in_kernel(xp_ref, *consts, *out_refs[:N_PROJ_OUTS],
                        wa=wa, wb=wb, d_model=d_model, feature_major=True)

    @pl.when(step >= n_prompt_steps)
    def _():
        _proj_in_kernel(xs_ref, *consts, *out_refs[N_PROJ_OUTS:],
                        wa=wa, wb=wb, d_model=d_model, feature_major=False)


def _proj_in(x_prompt, x_sample, g, w_in, bg, qg_t, kg_t, gm, *, tm, tm_sample):
    b, t, d = x_prompt.shape
    n = x_sample.shape[1]
    wa = H_A * HEAD_W
    wb = H_B * HEAD_W
    assert t % tm == 0 and n % tm_sample == 0 and wa == wb and x_sample.shape[0] == 1
    nt = t // tm
    n_p, n_s = b * nt, n // tm_sample
    p_tile = lambda s: jnp.minimum(s, n_p - 1)
    s_tile = lambda s: jnp.clip(s - n_p, 0, n_s - 1)
    full = lambda a: pl.BlockSpec(a.shape, lambda s: (0,) * a.ndim)

    def group_specs(bsz, length, rows, tile, feature_major):
        nt_g = length // rows
        row = lambda w: pl.BlockSpec((1, rows, w), lambda s: (tile(s) // nt_g, tile(s) % nt_g, 0))
        act = lambda w, dt: jax.ShapeDtypeStruct((bsz, length, w), dt)
        if feature_major:
            fm_shape = lambda dt: jax.ShapeDtypeStruct((bsz, wa, length), dt)
            fm_spec = pl.BlockSpec((1, wa, rows), lambda s: (tile(s) // nt_g, 0, tile(s) % nt_g))
        else:
            fm_shape, fm_spec = (lambda dt: act(wa, dt)), row(wa)
        shapes = [fm_shape(BF16),
                  fm_shape(F32), act(wa, BF16),
                  jax.ShapeDtypeStruct((bsz * length * H_A, HEAD_W), F32),
                  fm_shape(BF16),
                  act(len(MIX_SLABS) * wa, BF16),
                  act(2 * d, BF16)]
        specs = [fm_spec, fm_spec, row(wa),
                 pl.BlockSpec((rows * H_A, HEAD_W), lambda s: (tile(s), 0)),
                 fm_spec, row(len(MIX_SLABS) * wa), row(2 * d)]
        return row(d), shapes, specs

    xp_spec, p_shapes, p_specs = group_specs(b, t, tm, p_tile, True)
    xs_spec, s_shapes, s_specs = group_specs(1, n, tm_sample, s_tile, False)
    assert len(p_shapes) == N_PROJ_OUTS
    outs = pl.pallas_call(
        functools.partial(_proj_in_both_kernel, n_prompt_steps=n_p, wa=wa, wb=wb, d_model=d),
        grid=(n_p + n_s,),
        in_specs=[xp_spec, xs_spec, full(g), full(w_in), full(bg), full(qg_t), full(kg_t), full(gm)],
        out_specs=p_specs + s_specs,
        out_shape=p_shapes + s_shapes,
        compiler_params=pltpu.CompilerParams(
            dimension_semantics=("arbitrary",), vmem_limit_bytes=VMEM_LIMIT),
        name="proj_in",
    )(x_prompt, x_sample, g, w_in, bg, qg_t, kg_t, gm)
    return outs[:N_PROJ_OUTS], outs[N_PROJ_OUTS:]


def _split_halves(q):
    lane = lax.broadcasted_iota(jnp.int32, q.shape, 1)
    zero = jnp.zeros_like(q)
    return jnp.concatenate([jnp.where(lane < DK_A, q, zero), jnp.where(lane >= DK_A, q, zero)], axis=0)


def _softmax_update(carry, s, v):
    m, acc = carry
    m_new = jnp.maximum(m, jnp.max(s, axis=-1, keepdims=True))
    p = jnp.exp(s - m_new).astype(BF16)
    v1 = jnp.concatenate([v, jnp.ones_like(v)], axis=1)
    return m_new, jnp.exp(m - m_new) * acc + _dot(p, v1)


def _diff_combine(acc, lam, t):
    dv = acc.shape[1] // 2
    o = acc[:, :dv] / acc[:, dv:]
    return o[:t] - lam * o[t:]


def _lane_rms(o):
    return o * lax.rsqrt(jnp.mean(o * o, axis=-1, keepdims=True) + EPS)


def _retention_stage1(q, k, v, r, dmat, xi, zeta, cdecay):
    s = (_nt_dot(q, k) * dmat).astype(BF16)
    cross = _dot(q, r.astype(BF16)) * xi
    kz = (k.astype(F32) * zeta).astype(BF16)
    r_new = cdecay * r + _tn_dot(kz, v)
    return s, cross, r_new


def _retention_stage2(s, cross, v):
    return _dot(s, v) + cross


def _mixer_prompt_kernel(scal_ref, qt_ref, k_ref, vt_ref, mix_ref,
                         dbias_ref, kbias_ref, dmat_ref, xi_ref, zeta_ref, sg_ref,
                         x_ref, gate_ref, woa_ref, wob_ref, wout_ref,
                         y_ref, rout_ref, r_scr, acc_scr, oa_scr, ob_scr, merged_scr, *s_scrs,
                         tq, kblk, nb, nq, n_tiles):
    chains = [(bi, j) for bi in range(nb) for j in range(H_A)]
    hps = len(chains)
    nsub = tq // kblk
    t = pl.program_id(0)
    live = t < n_tiles
    qi = lax.rem(jnp.minimum(t, n_tiles - 1), nq)
    lam = scal_ref[0]
    lanes = lambda j: slice(j * HEAD_W, (j + 1) * HEAD_W)
    n_rep = lambda width: width // HEAD_W

    def split_halves_t(qt):
        feat = lax.broadcasted_iota(jnp.int32, qt.shape, 0)
        zero = jnp.zeros_like(qt)
        return jnp.concatenate([jnp.where(feat < DK_A, qt, zero), jnp.where(feat >= DK_A, qt, zero)], axis=1)

    qst = [split_halves_t(qt_ref[bi, lanes(j), :]) for bi, j in chains]

    head = lambda c: chains[c][1]

    def keys(c, kb):
        bi, j = chains[c]
        return k_ref[bi, pl.ds(pl.multiple_of(kb * kblk, kblk), kblk), lanes(j)]

    def values1(c, kb):
        bi, j = chains[c]
        vt = vt_ref[bi, lanes(j), pl.ds(pl.multiple_of(kb * kblk, kblk), kblk)]
        return jnp.concatenate([vt, jnp.ones((SUM_ROWS, kblk), BF16)], axis=0)

    def stage_scores(c, kb, slot):
        s = _dot(keys(c, kb), qst[c])
        s = s + jnp.concatenate([kbias_ref[head(c)]] * n_rep(2 * tq), axis=1)
        s_scrs[c][slot] = s
        return jnp.max(s, axis=0, keepdims=True)

    def softmax_step(m, s, s_max, shift):
        m_new = jnp.maximum(m, s_max + shift)
        return m_new, jnp.exp(m - m_new), jnp.exp(s - (m_new - shift)).astype(BF16)

    def body(kb, carry):
        slot = lax.rem(kb, 2)
        new = []
        for c in range(hps):
            smax_cur, m = carry[c]
            s_cur = s_scrs[c][slot]
            smax_next = stage_scores(c, kb + 1, 1 - slot)
            shift = ((kb - nsub * qi) * kblk).astype(F32) * scal_ref[1 + head(c)]
            m, alpha, p = softmax_step(m, s_cur, smax_cur, shift)
            acc_scr[c] = alpha * acc_scr[c] + _dot(values1(c, kb), p)
            new.append((smax_next, m))
        return tuple(new)

    @pl.when(t == 0)
    def _():
        oa_scr[...] = jnp.zeros_like(oa_scr)
        ob_scr[...] = jnp.zeros_like(ob_scr)

    @pl.when(qi == 0)
    def _():
        r_scr[...] = jnp.zeros_like(r_scr)

    acc_scr[...] = jnp.zeros_like(acc_scr)
    init = tuple((stage_scores(c, 0, 0), jnp.full((1, 2 * tq), -jnp.inf, F32)) for c in range(hps))
    d_model = x_ref.shape[2]
    gates = gate_ref[...].reshape(nb * tq, 2 * d_model)
    merged_scr[...] = (gates[:, :d_model].astype(F32) * _dot(oa_scr[...], woa_ref[...])
                       + gates[:, d_model:].astype(F32) * _dot(ob_scr[...], wob_ref[...])).astype(BF16)
    n_before = nsub * qi
    carry = lax.fori_loop(0, jnp.where(live, n_before, 0), body, init)

    def retention_stage1(sub, state):
        rows = slice(sub * kblk, (sub + 1) * kblk)
        return [_retention_stage1(
            _slab(mix_ref, "qb", j, bi)[rows], _slab(mix_ref, "kb", j, bi)[rows], _slab(mix_ref, "vb", j, bi)[rows],
            state[c], dmat_ref[j], xi_ref[j], zeta_ref[j], scal_ref[1 + H_A + j])
            for c, (bi, j) in enumerate(chains)]

    def retention_stage2(sub, ret):
        rows = slice(sub * kblk, (sub + 1) * kblk)
        for c, (bi, j) in enumerate(chains):
            ob = _retention_stage2(ret[c][0], ret[c][1], _slab(mix_ref, "vb", j, bi)[rows])
            ob = (_lane_rms(ob) * _slab(mix_ref, "zb", j, bi)[rows].astype(F32)).astype(BF16)
            ob_scr[bi * tq + sub * kblk:bi * tq + (sub + 1) * kblk, lanes(j)] = ob

    def cat(pieces):
        pieces = [p for p in pieces if p.shape[1] > 0]
        return pieces[0] if len(pieces) == 1 else jnp.concatenate(pieces, axis=1)

    def diagonal_block(dsub, c, m):
        kb = n_before + dsub
        w = tq - dsub * kblk
        cols = (slice(dsub * kblk, tq), slice(tq + dsub * kblk, 2 * tq))
        if dsub == 0:
            s = s_scrs[c][lax.rem(n_before, 2)]
        else:
            s = (_dot(keys(c, kb), cat([qst[c][:, q] for q in cols]))
                 + jnp.concatenate([kbias_ref[head(c)]] * n_rep(2 * w), axis=1))
        dbias = dbias_ref[head(c)]
        s = cat([s[:, :kblk] + dbias, s[:, kblk:w], s[:, w:w + kblk] + dbias, s[:, w + kblk:]])
        shift = (dsub * kblk) * scal_ref[1 + head(c)]
        m_part, alpha, p = softmax_step(cat([m[:, q] for q in cols]), s,
                                        jnp.max(s, axis=0, keepdims=True), shift)
        pv = _dot(values1(c, kb), p)
        for h, q in enumerate(cols):
            acc_scr[c, :, q] = alpha[:, h * w:(h + 1) * w] * acc_scr[c, :, q] + pv[:, h * w:(h + 1) * w]
        return cat([m[:, :dsub * kblk], m_part[:, :w], m[:, tq:tq + dsub * kblk], m_part[:, w:]])

    def final_projection(cols):
        y = x_ref[:, :, cols].reshape(nb * tq, -1) + _dot(merged_scr[...], wout_ref[:, cols])
        y_ref[:, :, cols] = y.reshape(nb, tq, -1)

    out_cols = [slice(c, c + MXU_TILE) for c in range(0, d_model, MXU_TILE)]
    every = max(1, (hps * nsub) // len(out_cols))
    state = [r_scr[c] for c in range(hps)]
    ms = [carry[c][1] for c in range(hps)]
    for sub in range(nsub):
        ret = retention_stage1(sub, state)
        state = [r[2] for r in ret]
        for c in range(hps):
            if (sub * hps + c) % every == 0 and out_cols:
                final_projection(out_cols.pop(0))
            ms[c] = diagonal_block(sub, c, ms[c])
        retention_stage2(sub, ret)
    while out_cols:
        final_projection(out_cols.pop(0))
    for c, (bi, j) in enumerate(chains):
        r_new = jnp.where(live, state[c], r_scr[c])
        r_scr[c] = r_new
        rout_ref[bi, j] = r_new
    for c, (bi, j) in enumerate(chains):
        ot = acc_scr[c, :HEAD_W] * (1.0 / acc_scr[c, HEAD_W:HEAD_W + 1])
        oa = (ot[:, :tq] - lam * ot[:, tq:]).T
        oa = (_lane_rms(oa) * sg_ref[0:1, :]) * sg_ref[1:2, :]
        oa = (oa * _slab(mix_ref, "za", j, bi)[...].astype(F32)).astype(BF16)
        oa_scr[bi * tq:(bi + 1) * tq, lanes(j)] = oa


def _mixer_prompt(scal, qat, kab, vat, mix, dbias, kbias, dmat, xi, zeta, sg, x, gates, woa, wob, wout):
    b, t, w = kab.shape
    d = x.shape[2]
    tq, kblk, nb = Q_TILE, KEY_BLOCK, SEQS_PER_STEP
    assert t % tq == 0 and tq % kblk == 0 and kblk % CHUNK == 0 and H_A == H_B and w == H_A * HEAD_W
    assert dbias.shape[1:] == (kblk, kblk) and d % (tq // kblk) == 0 and b % nb == 0
    nq = t // tq
    n_tiles = (b // nb) * nq
    cur = lambda s: jnp.minimum(s, n_tiles - 1)
    prev = lambda s: jnp.maximum(s - 1, 0)
    tile = pl.BlockSpec((nb, tq, mix.shape[2]), lambda s: (cur(s) // nq, cur(s) % nq, 0))
    tile_t = pl.BlockSpec((nb, w, tq), lambda s: (cur(s) // nq, 0, cur(s) % nq))
    whole = pl.BlockSpec((nb, t, w), lambda s: (cur(s) // nq, 0, 0))
    whole_t = pl.BlockSpec((nb, w, t), lambda s: (cur(s) // nq, 0, 0))
    rows = lambda a: pl.BlockSpec((nb, tq, a.shape[2]), lambda s: (prev(s) // nq, prev(s) % nq, 0))
    full = lambda a: pl.BlockSpec(a.shape, lambda s: (0,) * a.ndim, pipeline_mode=pl.Buffered(1))
    return pl.pallas_call(
        functools.partial(_mixer_prompt_kernel, tq=tq, kblk=kblk, nb=nb, nq=nq, n_tiles=n_tiles),
        grid=(n_tiles + 1,),
        in_specs=[pl.BlockSpec(memory_space=pltpu.SMEM),
                  tile_t, whole, whole_t, tile,
                  full(dbias), full(kbias), full(dmat), full(xi), full(zeta), full(sg),
                  rows(x), rows(gates), full(woa), full(wob), full(wout)],
        out_specs=[rows(x),
                   pl.BlockSpec((nb, H_B, DK_B, DV_B), lambda s: (cur(s) // nq, 0, 0, 0))],
        out_shape=[jax.ShapeDtypeStruct((b, t, d), F32),
                   jax.ShapeDtypeStruct((b, H_B, DK_B, DV_B), F32)],
        scratch_shapes=[pltpu.VMEM((nb * H_B, DK_B, DV_B), F32),
                        pltpu.VMEM((nb * H_A, HEAD_W + SUM_ROWS, 2 * tq), F32),
                        pltpu.VMEM((nb * tq, w), BF16), pltpu.VMEM((nb * tq, w), BF16),
                        pltpu.VMEM((nb * tq, d), BF16)]
                       + [pltpu.VMEM((2, kblk, 2 * tq), F32)] * (nb * H_A),
        compiler_params=pltpu.CompilerParams(
            dimension_semantics=("arbitrary",), vmem_limit_bytes=VMEM_LIMIT),
        name="mixer_prompt",
    )(scal, qat, kab, vat, mix, dbias, kbias, dmat, xi, zeta, sg, x, gates, woa, wob, wout)


def _mixer_sample_kernel(scal_ref, qa_ref, kc_ref, vc_ref, kn_ref, vn_ref, mix_ref, r_ref,
                         abias_ref, dmat_ref, xi_ref, zeta_ref, sg_ref,
                         oa_ref, ob_ref, rout_ref, m_scr, acc_scr, *, n, past, tk, sub):
    kbi = pl.program_id(1)
    lam = scal_ref[0]
    lanes = lambda j: slice(j * HEAD_W, (j + 1) * HEAD_W)

    @pl.when(kbi == 0)
    def _():
        m_scr[...] = jnp.full(m_scr.shape, -jnp.inf, F32)
        acc_scr[...] = jnp.zeros_like(acc_scr)

    col = lax.broadcasted_iota(jnp.int32, (1, sub), 1)
    qs = [_split_halves(qa_ref[0, :, lanes(j)]) for j in range(H_A)]

    def scores(j, c):
        kk = kc_ref[0, lanes(j), c * sub:(c + 1) * sub].astype(BF16)
        kpos = (col + (kbi * tk + (c * sub - past))).astype(F32)
        return _dot(qs[j], kk) + kpos * scal_ref[1 + j]

    blocks = [(j, c) for c in range(tk // sub) for j in range(H_A)]
    carry = [(m_scr[j], acc_scr[j]) for j in range(H_A)]
    staged = [scores(*blk) for blk in blocks[:SCORE_LOOKAHEAD]]
    for i, (j, c) in enumerate(blocks):
        if i + SCORE_LOOKAHEAD < len(blocks):
            staged.append(scores(*blocks[i + SCORE_LOOKAHEAD]))
        vv = vc_ref[0, pl.ds(c * sub * H_A + j, sub, stride=H_A), :].astype(BF16)
        carry[j] = _softmax_update(carry[j], staged[i], vv)
    for j in range(H_A):
        m_scr[j], acc_scr[j] = carry[j]

    @pl.when(kbi == pl.num_programs(1) - 1)
    def _():
        s_new, ret = [], []
        for j in range(H_A):
            bias = abias_ref[j]
            s_new.append(_nt_dot(qs[j], kn_ref[0, :, lanes(j)]) + jnp.concatenate([bias, bias], axis=0))
        for j in range(H_A):
            s_ret, cross, r_new = _retention_stage1(
                _slab(mix_ref, "qb", j)[...], _slab(mix_ref, "kb", j)[...], _slab(mix_ref, "vb", j)[...],
                r_ref[0, j], dmat_ref[j], xi_ref[j], zeta_ref[j], scal_ref[1 + H_A + j])
            rout_ref[0, j] = r_new
            ret.append((s_ret, cross))
        for j in range(H_A):
            _, acc = _softmax_update((m_scr[j], acc_scr[j]), s_new[j], vn_ref[0, :, lanes(j)])
            oa = _diff_combine(acc, lam, n)
            oa = (_lane_rms(oa) * sg_ref[0:1, :]) * sg_ref[1:2, :]
            oa_ref[0, :, lanes(j)] = (oa * _slab(mix_ref, "za", j)[...].astype(F32)).astype(BF16)
        for j in range(H_A):
            ob = _retention_stage2(*ret[j], _slab(mix_ref, "vb", j)[...])
            ob_ref[0, :, lanes(j)] = (_lane_rms(ob) * _slab(mix_ref, "zb", j)[...].astype(F32)).astype(BF16)


def _mixer_sample(scal, qa, kc, vc, kab, vab, mix, r0, abias, dmat, xi, zeta, sg):
    b, n, w = qa.shape
    past = kc.shape[2]
    tk, sub = SAMPLE_KEY_TILE, SAMPLE_KEY_BLOCK
    assert past % tk == 0 and tk % sub == 0 and past % CHUNK == 0 and n <= CHUNK and H_A == H_B
    tile = pl.BlockSpec((1, n, w), lambda bi, ki: (bi, 0, 0))
    mix_tile = pl.BlockSpec((1, n, mix.shape[2]), lambda bi, ki: (bi, 0, 0))
    cache_k = pl.BlockSpec((1, w, tk), lambda bi, ki: (bi, 0, ki))
    cache_v = pl.BlockSpec((1, tk * H_A, HEAD_W), lambda bi, ki: (bi, ki, 0))
    state = pl.BlockSpec((1, H_B, DK_B, DV_B), lambda bi, ki: (bi, 0, 0, 0))
    full = lambda a: pl.BlockSpec(a.shape, lambda bi, ki: (0,) * a.ndim)
    return pl.pallas_call(
        functools.partial(_mixer_sample_kernel, n=n, past=past, tk=tk, sub=sub),
        grid=(b, past // tk),
        in_specs=[pl.BlockSpec(memory_space=pltpu.SMEM),
                  tile, cache_k, cache_v, tile, tile, mix_tile, state,
                  full(abias), full(dmat), full(xi), full(zeta), full(sg)],
        out_specs=[tile, tile, state],
        out_shape=[jax.ShapeDtypeStruct((b, n, H_A * HEAD_W), BF16),
                   jax.ShapeDtypeStruct((b, n, H_B * HEAD_W), BF16),
                   jax.ShapeDtypeStruct((b, H_B, DK_B, DV_B), F32)],
        scratch_shapes=[pltpu.VMEM((H_A, 2 * n, 1), F32),
                        pltpu.VMEM((H_A, 2 * n, 2 * DV_A), F32)],
        compiler_params=pltpu.CompilerParams(
            dimension_semantics=("arbitrary", "arbitrary"), vmem_limit_bytes=VMEM_LIMIT),
        name="mixer_sample",
    )(scal, qa, kc, vc, kab, vab, mix, r0, abias, dmat, xi, zeta, sg)


def _proj_out_kernel(x_ref, oa_ref, ob_ref, gate_ref, woa_ref, wob_ref, wout_ref, y_ref):
    d = x_ref.shape[1]
    m = (gate_ref[:, :d].astype(F32) * _dot(oa_ref[...], woa_ref[...])
         + gate_ref[:, d:].astype(F32) * _dot(ob_ref[...], wob_ref[...]))
    y_ref[...] = x_ref[...] + _dot(m.astype(BF16), wout_ref[...])


def _proj_out(x2, oa, ob, gates, woa, wob, wout):
    n, d = x2.shape
    tm = min(ROW_TILE, n)
    assert n % tm == 0
    row = lambda w: pl.BlockSpec((tm, w), lambda i: (i, 0))
    full = lambda a: pl.BlockSpec(a.shape, lambda i: (0,) * a.ndim)
    return pl.pallas_call(
        _proj_out_kernel,
        grid=(n // tm,),
        in_specs=[row(d), row(oa.shape[1]), row(ob.shape[1]), row(gates.shape[1]),
                  full(woa), full(wob), full(wout)],
        out_specs=row(d),
        out_shape=jax.ShapeDtypeStruct((n, d), F32),
        compiler_params=pltpu.CompilerParams(
            dimension_semantics=("arbitrary",), vmem_limit_bytes=VMEM_LIMIT),
        name="proj_out",
    )(x2, oa, ob, gates, woa, wob, wout)


def _block_tables(t, slopes, log_g):
    f32 = np.float32
    i = np.arange(t, dtype=np.int32)
    allowed = (i[None, :] // CHUNK) <= (i[:, None] // CHUNK)
    dist = np.abs(i[:, None] - i[None, :]).astype(f32)
    jf = i.astype(f32)
    abias = -slopes[:, None, None] * dist[None] + slopes[:, None, None] * jf[None, :, None]
    abias = np.where(allowed[None], abias, f32(-np.inf)).astype(f32)
    dmat = np.where(allowed[None], np.exp(dist[None] * log_g[:, None, None]), f32(0.0)).astype(f32)
    xi = np.exp((jf + f32(1.0))[None] * log_g[:, None]).astype(f32)
    zeta = np.exp((f32(t) - f32(1.0) - jf)[None] * log_g[:, None]).astype(f32)
    bc = lambda a: np.ascontiguousarray(np.broadcast_to(a[:, :, None], a.shape + (HEAD_W,)))
    return abias, dmat, bc(xi), bc(zeta)


def _key_bias_table(t, slopes):
    kb = (slopes[:, None] * np.arange(t, dtype=np.float32)[None]).astype(np.float32)
    return np.ascontiguousarray(np.broadcast_to(kb[:, :, None], kb.shape + (HEAD_W,)))


def _layer(x_prompt, x_sample, ck, cv, r0, norm_g, w_in, b_gate, qn_g, kn_g,
           lam_q1, lam_k1, lam_q2, lam_k2, subln_g, w_oa, w_ob, w_out, lam_init):
    bp, tp, d = x_prompt.shape
    bs, ts, _ = x_sample.shape
    past = ck.shape[1]
    slopes = (2.0 ** (-8.0 * np.arange(1, H_A + 1, dtype=np.float32) / H_A)).astype(np.float32)
    log_g = np.log(1.0 - 2.0 ** (-5.0 - np.arange(H_B, dtype=np.float32))).astype(np.float32)
    lam = (jnp.exp(jnp.sum(lam_q1 * lam_k1).astype(F32))
           - jnp.exp(jnp.sum(lam_q2 * lam_k2).astype(F32)) + lam_init)

    woa_bf, wob_bf, wout_bf = w_oa.astype(BF16), w_ob.astype(BF16), w_out.astype(BF16)
    g = norm_g.reshape(1, d)
    qg_t = jnp.tile(qn_g, 2 * H_A).reshape(1, H_A * HEAD_W)
    kg_t = jnp.tile(kn_g, 2 * H_A).reshape(1, H_A * HEAD_W)
    grp = np.arange(MXU_TILE, dtype=np.int32) // DK_A
    gm = jnp.asarray(grp[:, None] == grp[None, :], dtype=BF16)
    sg = jnp.stack([subln_g, jnp.full((DV_A,), 1.0 - lam_init, F32)])

    def scal(n_block):
        consts = np.concatenate([slopes, np.exp(np.float32(n_block) * log_g)]).astype(np.float32)
        return jnp.concatenate([lam.reshape(1).astype(F32), jnp.asarray(consts)])

    prompt_acts, sample_acts = _proj_in(
        x_prompt, x_sample.reshape(1, bs * ts, d), g, w_in, b_gate, qg_t, kg_t, gm,
        tm=ROW_TILE, tm_sample=min(SAMPLE_ROW_TILE, bs * ts))

    qat, kaf, kab, vaf, vat, mix, gates = prompt_acts
    abias, dmat, xi, zeta = _block_tables(KEY_BLOCK, slopes, log_g)
    kbias = _key_bias_table(KEY_BLOCK, slopes)
    dbias = np.ascontiguousarray(np.swapaxes(abias, 1, 2) - kbias[:, :, :1])
    flat = lambda a: a.reshape(-1, a.shape[-1])
    y_p, ret_p = _mixer_prompt(scal(float(KEY_BLOCK)), qat, kab, vat, mix, dbias, kbias, dmat, xi, zeta, sg,
                               x_prompt, gates, woa_bf, wob_bf, wout_bf)
    k_p = jnp.transpose(kaf.reshape(bp, H_A, 2, DK_A, tp), (0, 4, 1, 2, 3))
    v_p = vaf.reshape(bp, tp, H_A, DV_A)

    qa, kaf, kab, vaf, vab, mix, gates = sample_acts
    r3 = lambda a: a.reshape(bs, ts, a.shape[-1])
    tabs = _block_tables(ts, slopes, log_g)
    ck_t = jnp.transpose(ck, (0, 2, 3, 4, 1)).reshape(bs, H_A * HEAD_W, past)
    oa, ob, ret_s = _mixer_sample(scal(float(ts)), r3(qa), ck_t, cv.reshape(bs, past * H_A, HEAD_W),
                                  r3(kab), r3(vab), r3(mix), r0, *tabs, sg)
    y_s = _proj_out(flat(x_sample), flat(oa), flat(ob), flat(gates),
                    woa_bf, wob_bf, wout_bf).reshape(bs, ts, d)
    k_s = kaf.reshape(bs, ts, H_A, 2, DK_A)
    v_s = vaf.reshape(bs, ts, H_A, DV_A)
    return y_p, y_s, k_p, v_p, ret_p, k_s, v_s, ret_s


def kernel(x_prompt, x_sample, cache_k_diff, cache_v_diff, state_ret, norm_g, w_in, b_gate, qn_g, kn_g,
           lam_q1, lam_k1, lam_q2, lam_k2, subln_g, w_o_diff, w_o_ret, w_out):
    depth = w_in.shape[0]
    hp, hs = x_prompt, x_sample
    outs = [[] for _ in range(6)]
    for l in range(depth):
        lam_init = 0.8 - 0.6 * math.exp(-0.3 * l)
        hp, hs, k_p, v_p, ret_p, k_s, v_s, ret_s = _layer(
            hp, hs, cache_k_diff[l], cache_v_diff[l], state_ret[l], norm_g[l], w_in[l], b_gate[l],
            qn_g[l], kn_g[l], lam_q1[l], lam_k1[l], lam_q2[l], lam_k2[l], subln_g[l],
            w_o_diff[l], w_o_ret[l], w_out[l], lam_init)
        for lst, a in zip(outs, (k_p, v_p, ret_p, k_s, v_s, ret_s)):
            lst.append(a)
    return (hp, hs) + tuple(jnp.stack(o) for o in outs)
```

```python
import functools
import math

import jax
import jax.numpy as jnp
import numpy as np
from jax import lax
from jax.experimental import pallas as pl
from jax.experimental.pallas import tpu as pltpu

F32 = jnp.float32
BF16 = jnp.bfloat16

CHUNK = 64
H_A = 4
DK_A = 64
DV_A = 2 * DK_A
H_B = 4
DK_B = 128
DV_B = 128
HEAD_W = 128
EPS = 1e-6
LOG2E = math.log2(math.e)

MXU_TILE = 256
ROW_TILE = 512
SAMPLE_ROW_TILE = 256
Q_TILE = 256
KEY_BLOCK = 256
SEQS_PER_STEP = 2
SUM_ROWS = 16
SAMPLE_KEY_TILE = 4096
SAMPLE_KEY_BLOCK = 1024
SCORE_LOOKAHEAD = 8
V7X_VMEM_BYTES = 64 * 1024 * 1024
VMEM_LIMIT = V7X_VMEM_BYTES - 3 * 1024 * 1024


def _nt_dot(a, b):
    return lax.dot_general(a, b, (((1,), (1,)), ((), ())), preferred_element_type=F32)


def _tn_dot(a, b):
    return lax.dot_general(a, b, (((0,), (0,)), ((), ())), preferred_element_type=F32)


def _dot(a, b):
    return jnp.dot(a, b, preferred_element_type=F32)


def _sigmoid(x):
    return 1.0 / (1.0 + jnp.exp(-x))


MIX_SLABS = ("za", "qb", "kb", "vb", "zb")


def _slab(mix_ref, name, head=None, seq=0):
    lo = MIX_SLABS.index(name) * H_A * HEAD_W
    if head is None:
        return mix_ref.at[seq, :, lo:lo + H_A * HEAD_W]
    return mix_ref.at[seq, :, lo + head * HEAD_W:lo + (head + 1) * HEAD_W]


def _proj_in_kernel(x_ref, g_ref, w_ref, bg_ref, qg_ref, kg_ref, gm_ref,
                    qa_ref, kaf_ref, kab_ref, vaf_ref, vab_ref, mix_ref, gate_ref,
                    *, wa, wb, d_model, feature_major, q_scale):
    x = x_ref[0]
    tm = x.shape[0]
    ms = jnp.mean(x * x, axis=-1, keepdims=True)
    xn = (x * lax.rsqrt(ms + EPS)) * g_ref[...]

    names = ("qa", "ka", "va", "za", "qb", "kb", "vb", "zb", "ga", "gb")
    widths = (wa, wa, wa, wa, wb, wb, wb, wb, d_model, d_model)
    offs = dict(zip(names, (sum(widths[:i]) for i in range(len(widths)))))
    width = dict(zip(names, widths))

    def seg(name):
        return _dot(xn, w_ref[:, offs[name]:offs[name] + width[name]])

    def group_norm(h, gain):
        sq = (h * h).astype(BF16)
        ss = jnp.concatenate([_dot(sq[:, c:c + MXU_TILE], gm_ref[...]) for c in range(0, wa, MXU_TILE)], axis=1)
        return (h * lax.rsqrt(ss * (1.0 / DK_A) + EPS)) * gain

    def silu(h):
        return h * _sigmoid(h)

    maybe_t = (lambda a: a.T) if feature_major else (lambda a: a)
    gate_ref[0, :, :d_model] = _sigmoid(seg("ga") + bg_ref[0:1, :]).astype(BF16)
    gate_ref[0, :, d_model:] = _sigmoid(seg("gb") + bg_ref[1:2, :]).astype(BF16)
    qa_ref[0] = maybe_t(group_norm(seg("qa"), qg_ref[...]) * q_scale).astype(BF16)
    ka = group_norm(seg("ka"), kg_ref[...])
    kaf_ref[0] = maybe_t(ka)
    kab_ref[0] = ka.astype(BF16)
    va = seg("va")
    vab_ref[0] = maybe_t(va).astype(BF16)
    for h in range(H_A):
        vaf_ref[pl.ds(h, tm, stride=H_A), :] = va[:, h * HEAD_W:(h + 1) * HEAD_W]
    _slab(mix_ref, "za")[...] = silu(seg("za")).astype(BF16)
    _slab(mix_ref, "zb")[...] = silu(seg("zb")).astype(BF16)
    _slab(mix_ref, "kb")[...] = (seg("kb") * (DK_B ** -0.5)).astype(BF16)
    _slab(mix_ref, "qb")[...] = seg("qb").astype(BF16)
    _slab(mix_ref, "vb")[...] = seg("vb").astype(BF16)


N_PROJ_OUTS = 7


def _proj_in_both_kernel(xp_ref, xs_ref, g_ref, w_ref, bg_ref, qg_ref, kg_ref, gm_ref, *out_refs,
                         n_prompt_steps, wa, wb, d_model):
    step = pl.program_id(0)
    consts = (g_ref, w_ref, bg_ref, qg_ref, kg_ref, gm_ref)

    @pl.when(step < n_prompt_steps)
    def _():
        _proj_in_kernel(xp_ref, *consts, *out_refs[:N_PROJ_OUTS],
                        wa=wa, wb=wb, d_model=d_model, feature_major=True,
                        q_scale=DK_A ** -0.5 * LOG2E)

    @pl.when(step >= n_prompt_steps)
    def _():
        _proj_in_kernel(xs_ref, *consts, *out_refs[N_PROJ_OUTS:],
                        wa=wa, wb=wb, d_model=d_model, feature_major=False, q_scale=DK_A ** -0.5)


def _proj_in(x_prompt, x_sample, g, w_in, bg, qg_t, kg_t, gm, *, tm, tm_sample):
    b, t, d = x_prompt.shape
    n = x_sample.shape[1]
    wa = H_A * HEAD_W
    wb = H_B * HEAD_W
    assert t % tm == 0 and n % tm_sample == 0 and wa == wb and x_sample.shape[0] == 1
    nt = t // tm
    n_p, n_s = b * nt, n // tm_sample
    p_tile = lambda s: jnp.minimum(s, n_p - 1)
    s_tile = lambda s: jnp.clip(s - n_p, 0, n_s - 1)
    full = lambda a: pl.BlockSpec(a.shape, lambda s: (0,) * a.ndim)

    def group_specs(bsz, length, rows, tile, feature_major):
        nt_g = length // rows
        row = lambda w: pl.BlockSpec((1, rows, w), lambda s: (tile(s) // nt_g, tile(s) % nt_g, 0))
        act = lambda w, dt: jax.ShapeDtypeStruct((bsz, length, w), dt)
        if feature_major:
            fm_shape = lambda dt: jax.ShapeDtypeStruct((bsz, wa, length), dt)
            fm_spec = pl.BlockSpec((1, wa, rows), lambda s: (tile(s) // nt_g, 0, tile(s) % nt_g))
        else:
            fm_shape, fm_spec = (lambda dt: act(wa, dt)), row(wa)
        shapes = [fm_shape(BF16),
                  fm_shape(F32), act(wa, BF16),
                  jax.ShapeDtypeStruct((bsz * length * H_A, HEAD_W), F32),
                  fm_shape(BF16),
                  act(len(MIX_SLABS) * wa, BF16),
                  act(2 * d, BF16)]
        specs = [fm_spec, fm_spec, row(wa),
                 pl.BlockSpec((rows * H_A, HEAD_W), lambda s: (tile(s), 0)),
                 fm_spec, row(len(MIX_SLABS) * wa), row(2 * d)]
        return row(d), shapes, specs

    xp_spec, p_shapes, p_specs = group_specs(b, t, tm, p_tile, True)
    xs_spec, s_shapes, s_specs = group_specs(1, n, tm_sample, s_tile, False)
    assert len(p_shapes) == N_PROJ_OUTS
    outs = pl.pallas_call(
        functools.partial(_proj_in_both_kernel, n_prompt_steps=n_p, wa=wa, wb=wb, d_model=d),
        grid=(n_p + n_s,),
        in_specs=[xp_spec, xs_spec, full(g), full(w_in), full(bg), full(qg_t), full(kg_t), full(gm)],
        out_specs=p_specs + s_specs,
        out_shape=p_shapes + s_shapes,
        compiler_params=pltpu.CompilerParams(
            dimension_semantics=("arbitrary",), vmem_limit_bytes=VMEM_LIMIT),
        name="proj_in",
    )(x_prompt, x_sample, g, w_in, bg, qg_t, kg_t, gm)
    return outs[:N_PROJ_OUTS], outs[N_PROJ_OUTS:]


def _split_halves(q):
    lane = lax.broadcasted_iota(jnp.int32, q.shape, 1)
    zero = jnp.zeros_like(q)
    return jnp.concatenate([jnp.where(lane < DK_A, q, zero), jnp.where(lane >= DK_A, q, zero)], axis=0)


def _softmax_update(carry, s, v):
    m, acc = carry
    m_new = jnp.maximum(m, jnp.max(s, axis=-1, keepdims=True))
    p = jnp.exp(s - m_new).astype(BF16)
    v1 = jnp.concatenate([v, jnp.ones_like(v)], axis=1)
    return m_new, jnp.exp(m - m_new) * acc + _dot(p, v1)


def _diff_combine(acc, lam, t):
    dv = acc.shape[1] // 2
    o = acc[:, :dv] / acc[:, dv:]
    return o[:t] - lam * o[t:]


def _lane_rms(o):
    return o * lax.rsqrt(jnp.mean(o * o, axis=-1, keepdims=True) + EPS)


def _retention_stage1(q, k, v, r, dmat, xi, zeta, cdecay):
    s = (_nt_dot(q, k) * dmat).astype(BF16)
    cross = _dot(q, r.astype(BF16)) * xi
    kz = (k.astype(F32) * zeta).astype(BF16)
    r_new = cdecay * r + _tn_dot(kz, v)
    return s, cross, r_new


def _retention_stage2(s, cross, v):
    return _dot(s, v) + cross


def _mixer_prompt_kernel(scal_ref, qt_ref, k_ref, vt_ref, mix_ref,
                         dbias_ref, kbias_ref, dmat_ref, xi_ref, zeta_ref, sg_ref,
                         x_ref, gate_ref, woa_ref, wob_ref, wout_ref,
                         y_ref, rout_ref, r_scr, acc_scr, oa_scr, ob_scr, merged_scr, *s_scrs,
                         tq, kblk, nb, nq, n_tiles):
    chains = [(bi, j) for bi in range(nb) for j in range(H_A)]
    hps = len(chains)
    nsub = tq // kblk
    t = pl.program_id(0)
    live = t < n_tiles
    qi = lax.rem(jnp.minimum(t, n_tiles - 1), nq)
    lam = scal_ref[0]
    lanes = lambda j: slice(j * HEAD_W, (j + 1) * HEAD_W)
    n_rep = lambda width: width // HEAD_W

    def split_halves_t(qt):
        feat = lax.broadcasted_iota(jnp.int32, qt.shape, 0)
        zero = jnp.zeros_like(qt)
        return jnp.concatenate([jnp.where(feat < DK_A, qt, zero), jnp.where(feat >= DK_A, qt, zero)], axis=1)

    qst = [split_halves_t(qt_ref[bi, lanes(j), :]) for bi, j in chains]

    head = lambda c: chains[c][1]

    def keys(c, kb):
        bi, j = chains[c]
        return k_ref[bi, pl.ds(pl.multiple_of(kb * kblk, kblk), kblk), lanes(j)]

    def values1(c, kb):
        bi, j = chains[c]
        vt = vt_ref[bi, lanes(j), pl.ds(pl.multiple_of(kb * kblk, kblk), kblk)]
        return jnp.concatenate([vt, jnp.ones((SUM_ROWS, kblk), BF16)], axis=0)

    def stage_scores(c, kb, slot):
        s = _dot(keys(c, kb), qst[c])
        s = s + jnp.concatenate([kbias_ref[head(c)]] * n_rep(2 * tq), axis=1)
        s_scrs[c][slot] = s
        return jnp.max(s, axis=0, keepdims=True)

    def softmax_step(m, s, s_max, shift):
        m_new = jnp.maximum(m, s_max + shift)
        return m_new, jnp.exp2(m - m_new), jnp.exp2(s - (m_new - shift)).astype(BF16)

    def body(kb, carry):
        slot = lax.rem(kb, 2)
        new = []
        for c in range(hps):
            smax_cur, m = carry[c]
            s_cur = s_scrs[c][slot]
            smax_next = stage_scores(c, kb + 1, 1 - slot)
            shift = ((kb - nsub * qi) * kblk).astype(F32) * scal_ref[1 + head(c)]
            m, alpha, p = softmax_step(m, s_cur, smax_cur, shift)
            acc_scr[c] = alpha * acc_scr[c] + _dot(values1(c, kb), p)
            new.append((smax_next, m))
        return tuple(new)

    @pl.when(t == 0)
    def _():
        oa_scr[...] = jnp.zeros_like(oa_scr)
        ob_scr[...] = jnp.zeros_like(ob_scr)

    @pl.when(qi == 0)
    def _():
        r_scr[...] = jnp.zeros_like(r_scr)

    acc_scr[...] = jnp.zeros_like(acc_scr)
    init = tuple((stage_scores(c, 0, 0), jnp.full((1, 2 * tq), -jnp.inf, F32)) for c in range(hps))
    d_model = x_ref.shape[2]
    gates = gate_ref[...].reshape(nb * tq, 2 * d_model)
    merged_scr[...] = (gates[:, :d_model].astype(F32) * _dot(oa_scr[...], woa_ref[...])
                       + gates[:, d_model:].astype(F32) * _dot(ob_scr[...], wob_ref[...])).astype(BF16)
    n_before = nsub * qi
    carry = lax.fori_loop(0, jnp.where(live, n_before, 0), body, init)

    def retention_stage1(sub, state):
        rows = slice(sub * kblk, (sub + 1) * kblk)
        return [_retention_stage1(
            _slab(mix_ref, "qb", j, bi)[rows], _slab(mix_ref, "kb", j, bi)[rows], _slab(mix_ref, "vb", j, bi)[rows],
            state[c], dmat_ref[j], xi_ref[j], zeta_ref[j], scal_ref[1 + H_A + j])
            for c, (bi, j) in enumerate(chains)]

    def retention_stage2(sub, ret):
        rows = slice(sub * kblk, (sub + 1) * kblk)
        for c, (bi, j) in enumerate(chains):
            ob = _retention_stage2(ret[c][0], ret[c][1], _slab(mix_ref, "vb", j, bi)[rows])
            ob = (_lane_rms(ob) * _slab(mix_ref, "zb", j, bi)[rows].astype(F32)).astype(BF16)
            ob_scr[bi * tq + sub * kblk:bi * tq + (sub + 1) * kblk, lanes(j)] = ob

    def cat(pieces):
        pieces = [p for p in pieces if p.shape[1] > 0]
        return pieces[0] if len(pieces) == 1 else jnp.concatenate(pieces, axis=1)

    def diagonal_block(dsub, c, m):
        kb = n_before + dsub
        w = tq - dsub * kblk
        cols = (slice(dsub * kblk, tq), slice(tq + dsub * kblk, 2 * tq))
        if dsub == 0:
            s = s_scrs[c][lax.rem(n_before, 2)]
        else:
            s = (_dot(keys(c, kb), cat([qst[c][:, q] for q in cols]))
                 + jnp.concatenate([kbias_ref[head(c)]] * n_rep(2 * w), axis=1))
        dbias = dbias_ref[head(c)]
        s = cat([s[:, :kblk] + dbias, s[:, kblk:w], s[:, w:w + kblk] + dbias, s[:, w + kblk:]])
        shift = (dsub * kblk) * scal_ref[1 + head(c)]
        m_part, alpha, p = softmax_step(cat([m[:, q] for q in cols]), s,
                                        jnp.max(s, axis=0, keepdims=True), shift)
        pv = _dot(values1(c, kb), p)
        for h, q in enumerate(cols):
            acc_scr[c, :, q] = alpha[:, h * w:(h + 1) * w] * acc_scr[c, :, q] + pv[:, h * w:(h + 1) * w]
        return cat([m[:, :dsub * kblk], m_part[:, :w], m[:, tq:tq + dsub * kblk], m_part[:, w:]])

    def final_projection(cols):
        y = x_ref[:, :, cols].reshape(nb * tq, -1) + _dot(merged_scr[...], wout_ref[:, cols])
        y_ref[:, :, cols] = y.reshape(nb, tq, -1)

    out_cols = [slice(c, c + MXU_TILE) for c in range(0, d_model, MXU_TILE)]
    every = max(1, (hps * nsub) // len(out_cols))
    state = [r_scr[c] for c in range(hps)]
    ms = [carry[c][1] for c in range(hps)]
    for sub in range(nsub):
        ret = retention_stage1(sub, state)
        state = [r[2] for r in ret]
        for c in range(hps):
            if (sub * hps + c) % every == 0 and out_cols:
                final_projection(out_cols.pop(0))
            ms[c] = diagonal_block(sub, c, ms[c])
        retention_stage2(sub, ret)
    while out_cols:
        final_projection(out_cols.pop(0))
    for c, (bi, j) in enumerate(chains):
        r_new = jnp.where(live, state[c], r_scr[c])
        r_scr[c] = r_new
        rout_ref[bi, j] = r_new
    for c, (bi, j) in enumerate(chains):
        ot = acc_scr[c, :HEAD_W] * (1.0 / acc_scr[c, HEAD_W:HEAD_W + 1])
        oa = (ot[:, :tq] - lam * ot[:, tq:]).T
        oa = (_lane_rms(oa) * sg_ref[0:1, :]) * sg_ref[1:2, :]
        oa = (oa * _slab(mix_ref, "za", j, bi)[...].astype(F32)).astype(BF16)
        oa_scr[bi * tq:(bi + 1) * tq, lanes(j)] = oa


def _mixer_prompt(scal, qat, kab, vat, mix, dbias, kbias, dmat, xi, zeta, sg, x, gates, woa, wob, wout):
    b, t, w = kab.shape
    d = x.shape[2]
    tq, kblk, nb = Q_TILE, KEY_BLOCK, SEQS_PER_STEP
    assert t % tq == 0 and tq % kblk == 0 and kblk % CHUNK == 0 and H_A == H_B and w == H_A * HEAD_W
    assert dbias.shape[1:] == (kblk, kblk) and d % (tq // kblk) == 0 and b % nb == 0
    nq = t // tq
    n_tiles = (b // nb) * nq
    cur = lambda s: jnp.minimum(s, n_tiles - 1)
    prev = lambda s: jnp.maximum(s - 1, 0)
    tile = pl.BlockSpec((nb, tq, mix.shape[2]), lambda s: (cur(s) // nq, cur(s) % nq, 0))
    tile_t = pl.BlockSpec((nb, w, tq), lambda s: (cur(s) // nq, 0, cur(s) % nq))
    whole = pl.BlockSpec((nb, t, w), lambda s: (cur(s) // nq, 0, 0))
    whole_t = pl.BlockSpec((nb, w, t), lambda s: (cur(s) // nq, 0, 0))
    rows = lambda a: pl.BlockSpec((nb, tq, a.shape[2]), lambda s: (prev(s) // nq, prev(s) % nq, 0))
    full = lambda a: pl.BlockSpec(a.shape, lambda s: (0,) * a.ndim, pipeline_mode=pl.Buffered(1))
    return pl.pallas_call(
        functools.partial(_mixer_prompt_kernel, tq=tq, kblk=kblk, nb=nb, nq=nq, n_tiles=n_tiles),
        grid=(n_tiles + 1,),
        in_specs=[pl.BlockSpec(memory_space=pltpu.SMEM),
                  tile_t, whole, whole_t, tile,
                  full(dbias), full(kbias), full(dmat), full(xi), full(zeta), full(sg),
                  rows(x), rows(gates), full(woa), full(wob), full(wout)],
        out_specs=[rows(x),
                   pl.BlockSpec((nb, H_B, DK_B, DV_B), lambda s: (cur(s) // nq, 0, 0, 0))],
        out_shape=[jax.ShapeDtypeStruct((b, t, d), F32),
                   jax.ShapeDtypeStruct((b, H_B, DK_B, DV_B), F32)],
        scratch_shapes=[pltpu.VMEM((nb * H_B, DK_B, DV_B), F32),
                        pltpu.VMEM((nb * H_A, HEAD_W + SUM_ROWS, 2 * tq), F32),
                        pltpu.VMEM((nb * tq, w), BF16), pltpu.VMEM((nb * tq, w), BF16),
                        pltpu.VMEM((nb * tq, d), BF16)]
                       + [pltpu.VMEM((2, kblk, 2 * tq), F32)] * (nb * H_A),
        compiler_params=pltpu.CompilerParams(
            dimension_semantics=("arbitrary",), vmem_limit_bytes=VMEM_LIMIT),
        name="mixer_prompt",
    )(scal, qat, kab, vat, mix, dbias, kbias, dmat, xi, zeta, sg, x, gates, woa, wob, wout)


def _mixer_sample_kernel(scal_ref, qa_ref, kc_ref, vc_ref, kn_ref, vn_ref, mix_ref, r_ref,
                         abias_ref, dmat_ref, xi_ref, zeta_ref, sg_ref,
                         oa_ref, ob_ref, rout_ref, m_scr, acc_scr, *, n, past, tk, sub):
    kbi = pl.program_id(1)
    lam = scal_ref[0]
    lanes = lambda j: slice(j * HEAD_W, (j + 1) * HEAD_W)

    @pl.when(kbi == 0)
    def _():
        m_scr[...] = jnp.full(m_scr.shape, -jnp.inf, F32)
        acc_scr[...] = jnp.zeros_like(acc_scr)

    col = lax.broadcasted_iota(jnp.int32, (1, sub), 1)
    qs = [_split_halves(qa_ref[0, :, lanes(j)]) for j in range(H_A)]

    def scores(j, c):
        kk = kc_ref[0, lanes(j), c * sub:(c + 1) * sub].astype(BF16)
        kpos = (col + (kbi * tk + (c * sub - past))).astype(F32)
        return _dot(qs[j], kk) + kpos * scal_ref[1 + j]

    blocks = [(j, c) for c in range(tk // sub) for j in range(H_A)]
    carry = [(m_scr[j], acc_scr[j]) for j in range(H_A)]
    staged = [scores(*blk) for blk in blocks[:SCORE_LOOKAHEAD]]
    for i, (j, c) in enumerate(blocks):
        if i + SCORE_LOOKAHEAD < len(blocks):
            staged.append(scores(*blocks[i + SCORE_LOOKAHEAD]))
        vv = vc_ref[0, pl.ds(c * sub * H_A + j, sub, stride=H_A), :].astype(BF16)
        carry[j] = _softmax_update(carry[j], staged[i], vv)
    for j in range(H_A):
        m_scr[j], acc_scr[j] = carry[j]

    @pl.when(kbi == pl.num_programs(1) - 1)
    def _():
        s_new, ret = [], []
        for j in range(H_A):
            bias = abias_ref[j]
            s_new.append(_nt_dot(qs[j], kn_ref[0, :, lanes(j)]) + jnp.concatenate([bias, bias], axis=0))
        for j in range(H_A):
            s_ret, cross, r_new = _retention_stage1(
                _slab(mix_ref, "qb", j)[...], _slab(mix_ref, "kb", j)[...], _slab(mix_ref, "vb", j)[...],
                r_ref[0, j], dmat_ref[j], xi_ref[j], zeta_ref[j], scal_ref[1 + H_A + j])
            rout_ref[0, j] = r_new
            ret.append((s_ret, cross))
        for j in range(H_A):
            _, acc = _softmax_update((m_scr[j], acc_scr[j]), s_new[j], vn_ref[0, :, lanes(j)])
            oa = _diff_combine(acc, lam, n)
            oa = (_lane_rms(oa) * sg_ref[0:1, :]) * sg_ref[1:2, :]
            oa_ref[0, :, lanes(j)] = (oa * _slab(mix_ref, "za", j)[...].astype(F32)).astype(BF16)
        for j in range(H_A):
            ob = _retention_stage2(*ret[j], _slab(mix_ref, "vb", j)[...])
            ob_ref[0, :, lanes(j)] = (_lane_rms(ob) * _slab(mix_ref, "zb", j)[...].astype(F32)).astype(BF16)


def _mixer_sample(scal, qa, kc, vc, kab, vab, mix, r0, abias, dmat, xi, zeta, sg):
    b, n, w = qa.shape
    past = kc.shape[2]
    tk, sub = SAMPLE_KEY_TILE, SAMPLE_KEY_BLOCK
    assert past % tk == 0 and tk % sub == 0 and past % CHUNK == 0 and n <= CHUNK and H_A == H_B
    tile = pl.BlockSpec((1, n, w), lambda bi, ki: (bi, 0, 0))
    mix_tile = pl.BlockSpec((1, n, mix.shape[2]), lambda bi, ki: (bi, 0, 0))
    cache_k = pl.BlockSpec((1, w, tk), lambda bi, ki: (bi, 0, ki))
    cache_v = pl.BlockSpec((1, tk * H_A, HEAD_W), lambda bi, ki: (bi, ki, 0))
    state = pl.BlockSpec((1, H_B, DK_B, DV_B), lambda bi, ki: (bi, 0, 0, 0))
    full = lambda a: pl.BlockSpec(a.shape, lambda bi, ki: (0,) * a.ndim)
    return pl.pallas_call(
        functools.partial(_mixer_sample_kernel, n=n, past=past, tk=tk, sub=sub),
        grid=(b, past // tk),
        in_specs=[pl.BlockSpec(memory_space=pltpu.SMEM),
                  tile, cache_k, cache_v, tile, tile, mix_tile, state,
                  full(abias), full(dmat), full(xi), full(zeta), full(sg)],
        out_specs=[tile, tile, state],
        out_shape=[jax.ShapeDtypeStruct((b, n, H_A * HEAD_W), BF16),
                   jax.ShapeDtypeStruct((b, n, H_B * HEAD_W), BF16),
                   jax.ShapeDtypeStruct((b, H_B, DK_B, DV_B), F32)],
        scratch_shapes=[pltpu.VMEM((H_A, 2 * n, 1), F32),
                        pltpu.VMEM((H_A, 2 * n, 2 * DV_A), F32)],
        compiler_params=pltpu.CompilerParams(
            dimension_semantics=("arbitrary", "arbitrary"), vmem_limit_bytes=VMEM_LIMIT),
        name="mixer_sample",
    )(scal, qa, kc, vc, kab, vab, mix, r0, abias, dmat, xi, zeta, sg)


def _proj_out_kernel(x_ref, oa_ref, ob_ref, gate_ref, woa_ref, wob_ref, wout_ref, y_ref):
    d = x_ref.shape[1]
    m = (gate_ref[:, :d].astype(F32) * _dot(oa_ref[...], woa_ref[...])
         + gate_ref[:, d:].astype(F32) * _dot(ob_ref[...], wob_ref[...]))
    y_ref[...] = x_ref[...] + _dot(m.astype(BF16), wout_ref[...])


def _proj_out(x2, oa, ob, gates, woa, wob, wout):
    n, d = x2.shape
    tm = min(ROW_TILE, n)
    assert n % tm == 0
    row = lambda w: pl.BlockSpec((tm, w), lambda i: (i, 0))
    full = lambda a: pl.BlockSpec(a.shape, lambda i: (0,) * a.ndim)
    return pl.pallas_call(
        _proj_out_kernel,
        grid=(n // tm,),
        in_specs=[row(d), row(oa.shape[1]), row(ob.shape[1]), row(gates.shape[1]),
                  full(woa), full(wob), full(wout)],
        out_specs=row(d),
        out_shape=jax.ShapeDtypeStruct((n, d), F32),
        compiler_params=pltpu.CompilerParams(
            dimension_semantics=("arbitrary",), vmem_limit_bytes=VMEM_LIMIT),
        name="proj_out",
    )(x2, oa, ob, gates, woa, wob, wout)


def _block_tables(t, slopes, log_g):
    f32 = np.float32
    i = np.arange(t, dtype=np.int32)
    allowed = (i[None, :] // CHUNK) <= (i[:, None] // CHUNK)
    dist = np.abs(i[:, None] - i[None, :]).astype(f32)
    jf = i.astype(f32)
    abias = -slopes[:, None, None] * dist[None] + slopes[:, None, None] * jf[None, :, None]
    abias = np.where(allowed[None], abias, f32(-np.inf)).astype(f32)
    dmat = np.where(allowed[None], np.exp(dist[None] * log_g[:, None, None]), f32(0.0)).astype(f32)
    xi = np.exp((jf + f32(1.0))[None] * log_g[:, None]).astype(f32)
    zeta = np.exp((f32(t) - f32(1.0) - jf)[None] * log_g[:, None]).astype(f32)
    bc = lambda a: np.ascontiguousarray(np.broadcast_to(a[:, :, None], a.shape + (HEAD_W,)))
    return abias, dmat, bc(xi), bc(zeta)


def _key_bias_table(t, slopes):
    kb = (slopes[:, None] * np.arange(t, dtype=np.float32)[None]).astype(np.float32)
    return np.ascontiguousarray(np.broadcast_to(kb[:, :, None], kb.shape + (HEAD_W,)))


def _layer(x_prompt, x_sample, ck, cv, r0, norm_g, w_in, b_gate, qn_g, kn_g,
           lam_q1, lam_k1, lam_q2, lam_k2, subln_g, w_oa, w_ob, w_out, lam_init):
    bp, tp, d = x_prompt.shape
    bs, ts, _ = x_sample.shape
    past = ck.shape[1]
    slopes = (2.0 ** (-8.0 * np.arange(1, H_A + 1, dtype=np.float32) / H_A)).astype(np.float32)
    log_g = np.log(1.0 - 2.0 ** (-5.0 - np.arange(H_B, dtype=np.float32))).astype(np.float32)
    lam = (jnp.exp(jnp.sum(lam_q1 * lam_k1).astype(F32))
           - jnp.exp(jnp.sum(lam_q2 * lam_k2).astype(F32)) + lam_init)

    woa_bf, wob_bf, wout_bf = w_oa.astype(BF16), w_ob.astype(BF16), w_out.astype(BF16)
    g = norm_g.reshape(1, d)
    qg_t = jnp.tile(qn_g, 2 * H_A).reshape(1, H_A * HEAD_W)
    kg_t = jnp.tile(kn_g, 2 * H_A).reshape(1, H_A * HEAD_W)
    grp = np.arange(MXU_TILE, dtype=np.int32) // DK_A
    gm = jnp.asarray(grp[:, None] == grp[None, :], dtype=BF16)
    sg = jnp.stack([subln_g, jnp.full((DV_A,), 1.0 - lam_init, F32)])

    def scal(n_block, logit_scale):
        consts = np.concatenate([slopes * logit_scale, np.exp(np.float32(n_block) * log_g)]).astype(np.float32)
        return jnp.concatenate([lam.reshape(1).astype(F32), jnp.asarray(consts)])

    prompt_acts, sample_acts = _proj_in(
        x_prompt, x_sample.reshape(1, bs * ts, d), g, w_in, b_gate, qg_t, kg_t, gm,
        tm=ROW_TILE, tm_sample=min(SAMPLE_ROW_TILE, bs * ts))

    qat, kaf, kab, vaf, vat, mix, gates = prompt_acts
    abias, dmat, xi, zeta = _block_tables(KEY_BLOCK, slopes, log_g)
    kbias = _key_bias_table(KEY_BLOCK, slopes)
    dbias = np.ascontiguousarray(np.swapaxes(abias, 1, 2) - kbias[:, :, :1])
    flat = lambda a: a.reshape(-1, a.shape[-1])
    log2e = np.float32(LOG2E)
    y_p, ret_p = _mixer_prompt(scal(float(KEY_BLOCK), log2e), qat, kab, vat, mix,
                               dbias * log2e, kbias * log2e, dmat, xi, zeta, sg,
                               x_prompt, gates, woa_bf, wob_bf, wout_bf)
    k_p = jnp.transpose(kaf.reshape(bp, H_A, 2, DK_A, tp), (0, 4, 1, 2, 3))
    v_p = vaf.reshape(bp, tp, H_A, DV_A)

    qa, kaf, kab, vaf, vab, mix, gates = sample_acts
    r3 = lambda a: a.reshape(bs, ts, a.shape[-1])
    tabs = _block_tables(ts, slopes, log_g)
    ck_t = jnp.transpose(ck, (0, 2, 3, 4, 1)).reshape(bs, H_A * HEAD_W, past)
    oa, ob, ret_s = _mixer_sample(scal(float(ts), np.float32(1.0)), r3(qa), ck_t, cv.reshape(bs, past * H_A, HEAD_W),
                                  r3(kab), r3(vab), r3(mix), r0, *tabs, sg)
    y_s = _proj_out(flat(x_sample), flat(oa), flat(ob), flat(gates),
                    woa_bf, wob_bf, wout_bf).reshape(bs, ts, d)
    k_s = kaf.reshape(bs, ts, H_A, 2, DK_A)
    v_s = vaf.reshape(bs, ts, H_A, DV_A)
    return y_p, y_s, k_p, v_p, ret_p, k_s, v_s, ret_s


def kernel(x_prompt, x_sample, cache_k_diff, cache_v_diff, state_ret, norm_g, w_in, b_gate, qn_g, kn_g,
           lam_q1, lam_k1, lam_q2, lam_k2, subln_g, w_o_diff, w_o_ret, w_out):
    depth = w_in.shape[0]
    hp, hs = x_prompt, x_sample
    outs = [[] for _ in range(6)]
    for l in range(depth):
        lam_init = 0.8 - 0.6 * math.exp(-0.3 * l)
        hp, hs, k_p, v_p, ret_p, k_s, v_s, ret_s = _layer(
            hp, hs, cache_k_diff[l], cache_v_diff[l], state_ret[l], norm_g[l], w_in[l], b_gate[l],
            qn_g[l], kn_g[l], lam_q1[l], lam_k1[l], lam_q2[l], lam_k2[l], subln_g[l],
            w_o_diff[l], w_o_ret[l], w_out[l], lam_init)
        for lst, a in zip(outs, (k_p, v_p, ret_p, k_s, v_s, ret_s)):
            lst.append(a)
    return (hp, hs) + tuple(jnp.stack(o) for o in outs)
```

```python
import functools
import math

import jax
import jax.numpy as jnp
import numpy as np
from jax import lax
from jax.experimental import pallas as pl
from jax.experimental.pallas import tpu as pltpu

F32 = jnp.float32
BF16 = jnp.bfloat16

CHUNK = 64
H_A = 4
DK_A = 64
DV_A = 2 * DK_A
H_B = 4
DK_B = 128
DV_B = 128
HEAD_W = 128
EPS = 1e-6
LOG2E = math.log2(math.e)

MXU_TILE = 256
ROW_TILE = 512
SAMPLE_ROW_TILE = 256
Q_TILE = 256
KEY_BLOCK = 256
SEQS_PER_STEP = 2
SUM_ROWS = 16
SAMPLE_KEY_TILE = 4096
SAMPLE_KEY_BLOCK = 1024
SCORE_LOOKAHEAD = 8
V7X_VMEM_BYTES = 64 * 1024 * 1024
VMEM_LIMIT = V7X_VMEM_BYTES - 3 * 1024 * 1024


def _nt_dot(a, b):
    return lax.dot_general(a, b, (((1,), (1,)), ((), ())), preferred_element_type=F32)


def _tn_dot(a, b):
    return lax.dot_general(a, b, (((0,), (0,)), ((), ())), preferred_element_type=F32)


def _dot(a, b):
    return jnp.dot(a, b, preferred_element_type=F32)


def _sigmoid(x):
    return 1.0 / (1.0 + jnp.exp(-x))


MIX_SLABS = ("za", "qb", "kb", "vb", "zb")


def _slab(mix_ref, name, head=None, seq=0):
    lo = MIX_SLABS.index(name) * H_A * HEAD_W
    if head is None:
        return mix_ref.at[seq, :, lo:lo + H_A * HEAD_W]
    return mix_ref.at[seq, :, lo + head * HEAD_W:lo + (head + 1) * HEAD_W]


def _proj_in_kernel(x_ref, g_ref, w_ref, bg_ref, qg_ref, kg_ref, gm_ref,
                    qa_ref, kaf_ref, kab_ref, vaf_ref, vab_ref, mix_ref, gate_ref,
                    *, wa, wb, d_model, feature_major, q_scale):
    x = x_ref[0]
    tm = x.shape[0]
    ms = jnp.mean(x * x, axis=-1, keepdims=True)
    xn = (x * lax.rsqrt(ms + EPS)) * g_ref[...]

    names = ("qa", "ka", "va", "za", "qb", "kb", "vb", "zb", "ga", "gb")
    widths = (wa, wa, wa, wa, wb, wb, wb, wb, d_model, d_model)
    offs = dict(zip(names, (sum(widths[:i]) for i in range(len(widths)))))
    width = dict(zip(names, widths))

    def seg(name):
        return _dot(xn, w_ref[:, offs[name]:offs[name] + width[name]])

    def group_norm(h, gain):
        sq = (h * h).astype(BF16)
        ss = jnp.concatenate([_dot(sq[:, c:c + MXU_TILE], gm_ref[...]) for c in range(0, wa, MXU_TILE)], axis=1)
        return (h * lax.rsqrt(ss * (1.0 / DK_A) + EPS)) * gain

    def silu(h):
        return h * _sigmoid(h)

    maybe_t = (lambda a: a.T) if feature_major else (lambda a: a)
    gate_ref[0, :, :d_model] = _sigmoid(seg("ga") + bg_ref[0:1, :]).astype(BF16)
    gate_ref[0, :, d_model:] = _sigmoid(seg("gb") + bg_ref[1:2, :]).astype(BF16)
    qa_ref[0] = maybe_t(group_norm(seg("qa"), qg_ref[...]) * q_scale).astype(BF16)
    ka = group_norm(seg("ka"), kg_ref[...])
    kaf_ref[0] = maybe_t(ka)
    kab_ref[0] = ka.astype(BF16)
    va = seg("va")
    vab_ref[0] = maybe_t(va).astype(BF16)
    for h in range(H_A):
        vaf_ref[pl.ds(h, tm, stride=H_A), :] = va[:, h * HEAD_W:(h + 1) * HEAD_W]
    _slab(mix_ref, "za")[...] = silu(seg("za")).astype(BF16)
    _slab(mix_ref, "zb")[...] = silu(seg("zb")).astype(BF16)
    _slab(mix_ref, "kb")[...] = (seg("kb") * (DK_B ** -0.5)).astype(BF16)
    _slab(mix_ref, "qb")[...] = seg("qb").astype(BF16)
    _slab(mix_ref, "vb")[...] = seg("vb").astype(BF16)


N_PROJ_OUTS = 7


def _proj_in_both_kernel(xp_ref, xs_ref, g_ref, w_ref, bg_ref, qg_ref, kg_ref, gm_ref, *out_refs,
                         n_prompt_steps, wa, wb, d_model):
    step = pl.program_id(0)
    consts = (g_ref, w_ref, bg_ref, qg_ref, kg_ref, gm_ref)

    @pl.when(step < n_prompt_steps)
    def _():
        _proj_in_kernel(xp_ref, *consts, *out_refs[:N_PROJ_OUTS],
                        wa=wa, wb=wb, d_model=d_model, feature_major=True,
                        q_scale=DK_A ** -0.5 * LOG2E)

    @pl.when(step >= n_prompt_steps)
    def _():
        _proj_in_kernel(xs_ref, *consts, *out_refs[N_PROJ_OUTS:],
                        wa=wa, wb=wb, d_model=d_model, feature_major=False, q_scale=DK_A ** -0.5)


def _proj_in(x_prompt, x_sample, g, w_in, bg, qg_t, kg_t, gm, *, tm, tm_sample):
    b, t, d = x_prompt.shape
    n = x_sample.shape[1]
    wa = H_A * HEAD_W
    wb = H_B * HEAD_W
    assert t % tm == 0 and n % tm_sample == 0 and wa == wb and x_sample.shape[0] == 1
    nt = t // tm
    n_p, n_s = b * nt, n // tm_sample
    p_tile = lambda s: jnp.minimum(s, n_p - 1)
    s_tile = lambda s: jnp.clip(s - n_p, 0, n_s - 1)
    full = lambda a: pl.BlockSpec(a.shape, lambda s: (0,) * a.ndim)

    def group_specs(bsz, length, rows, tile, feature_major):
        nt_g = length // rows
        row = lambda w: pl.BlockSpec((1, rows, w), lambda s: (tile(s) // nt_g, tile(s) % nt_g, 0))
        act = lambda w, dt: jax.ShapeDtypeStruct((bsz, length, w), dt)
        if feature_major:
            fm_shape = lambda dt: jax.ShapeDtypeStruct((bsz, wa, length), dt)
            fm_spec = pl.BlockSpec((1, wa, rows), lambda s: (tile(s) // nt_g, 0, tile(s) % nt_g))
        else:
            fm_shape, fm_spec = (lambda dt: act(wa, dt)), row(wa)
        shapes = [fm_shape(BF16),
                  fm_shape(F32), act(wa, BF16),
                  jax.ShapeDtypeStruct((bsz * length * H_A, HEAD_W), F32),
                  fm_shape(BF16),
                  act(len(MIX_SLABS) * wa, BF16),
                  act(2 * d, BF16)]
        specs = [fm_spec, fm_spec, row(wa),
                 pl.BlockSpec((rows * H_A, HEAD_W), lambda s: (tile(s), 0)),
                 fm_spec, row(len(MIX_SLABS) * wa), row(2 * d)]
        return row(d), shapes, specs

    xp_spec, p_shapes, p_specs = group_specs(b, t, tm, p_tile, True)
    xs_spec, s_shapes, s_specs = group_specs(1, n, tm_sample, s_tile, False)
    assert len(p_shapes) == N_PROJ_OUTS
    outs = pl.pallas_call(
        functools.partial(_proj_in_both_kernel, n_prompt_steps=n_p, wa=wa, wb=wb, d_model=d),
        grid=(n_p + n_s,),
        in_specs=[xp_spec, xs_spec, full(g), full(w_in), full(bg), full(qg_t), full(kg_t), full(gm)],
        out_specs=p_specs + s_specs,
        out_shape=p_shapes + s_shapes,
        compiler_params=pltpu.CompilerParams(
            dimension_semantics=("arbitrary",), vmem_limit_bytes=VMEM_LIMIT),
        name="proj_in",
    )(x_prompt, x_sample, g, w_in, bg, qg_t, kg_t, gm)
    return outs[:N_PROJ_OUTS], outs[N_PROJ_OUTS:]


def _split_halves(q):
    lane = lax.broadcasted_iota(jnp.int32, q.shape, 1)
    zero = jnp.zeros_like(q)
    return jnp.concatenate([jnp.where(lane < DK_A, q, zero), jnp.where(lane >= DK_A, q, zero)], axis=0)


def _softmax_update(carry, s, v):
    m, acc = carry
    m_new = jnp.maximum(m, jnp.max(s, axis=-1, keepdims=True))
    p = jnp.exp(s - m_new).astype(BF16)
    v1 = jnp.concatenate([v, jnp.ones_like(v)], axis=1)
    return m_new, jnp.exp(m - m_new) * acc + _dot(p, v1)


def _diff_combine(acc, lam, t):
    dv = acc.shape[1] // 2
    o = acc[:, :dv] / acc[:, dv:]
    return o[:t] - lam * o[t:]


def _lane_rms(o):
    return o * lax.rsqrt(jnp.mean(o * o, axis=-1, keepdims=True) + EPS)


def _retention_stage1(q, k, v, r, dmat, xi, zeta, cdecay):
    s = (_nt_dot(q, k) * dmat).astype(BF16)
    cross = _dot(q, r.astype(BF16)) * xi
    kz = (k.astype(F32) * zeta).astype(BF16)
    r_new = cdecay * r + _tn_dot(kz, v)
    return s, cross, r_new


def _retention_stage2(s, cross, v):
    return _dot(s, v) + cross


def _mixer_prompt_kernel(scal_ref, qt_ref, k_ref, vt_ref, mix_ref,
                         dbias_ref, kpos_ref, kslope_ref, dmat_ref, xi_ref, zeta_ref, sg_ref,
                         x_ref, gate_ref, woa_ref, wob_ref, wout_ref,
                         y_ref, rout_ref, r_scr, acc_scr, oa_scr, ob_scr, merged_scr, *s_scrs,
                         tq, kblk, nb, nq, n_tiles):
    chains = [(bi, j) for bi in range(nb) for j in range(H_A)]
    hps = len(chains)
    nsub = tq // kblk
    t = pl.program_id(0)
    live = t < n_tiles
    qi = lax.rem(jnp.minimum(t, n_tiles - 1), nq)
    lam = scal_ref[0]
    lanes = lambda j: slice(j * HEAD_W, (j + 1) * HEAD_W)

    def split_halves_t(qt):
        feat = lax.broadcasted_iota(jnp.int32, qt.shape, 0)
        zero = jnp.zeros_like(qt)
        return jnp.concatenate([jnp.where(feat < DK_A, qt, zero), jnp.where(feat >= DK_A, qt, zero)], axis=1)

    qst = [jnp.concatenate([split_halves_t(qt_ref[bi, lanes(j), :]), kslope_ref[j]], axis=0)
           for bi, j in chains]

    head = lambda c: chains[c][1]

    def keys(c, kb):
        bi, j = chains[c]
        k = k_ref[bi, pl.ds(pl.multiple_of(kb * kblk, kblk), kblk), lanes(j)]
        return jnp.concatenate([k, kpos_ref[...]], axis=1)

    def values1(c, kb):
        bi, j = chains[c]
        vt = vt_ref[bi, lanes(j), pl.ds(pl.multiple_of(kb * kblk, kblk), kblk)]
        return jnp.concatenate([vt, jnp.ones((SUM_ROWS, kblk), BF16)], axis=0)

    def stage_scores(c, kb, slot):
        s = _dot(keys(c, kb), qst[c])
        s_scrs[c][slot] = s
        return jnp.max(s, axis=0, keepdims=True)

    def softmax_step(m, s, s_max, shift):
        m_new = jnp.maximum(m, s_max + shift)
        return m_new, jnp.exp2(m - m_new), jnp.exp2(s - (m_new - shift)).astype(BF16)

    def body(kb, carry):
        slot = lax.rem(kb, 2)
        new = []
        for c in range(hps):
            smax_cur, m = carry[c]
            s_cur = s_scrs[c][slot]
            smax_next = stage_scores(c, kb + 1, 1 - slot)
            shift = ((kb - nsub * qi) * kblk).astype(F32) * scal_ref[1 + head(c)]
            m, alpha, p = softmax_step(m, s_cur, smax_cur, shift)
            acc_scr[c] = alpha * acc_scr[c] + _dot(values1(c, kb), p)
            new.append((smax_next, m))
        return tuple(new)

    @pl.when(t == 0)
    def _():
        oa_scr[...] = jnp.zeros_like(oa_scr)
        ob_scr[...] = jnp.zeros_like(ob_scr)

    @pl.when(qi == 0)
    def _():
        r_scr[...] = jnp.zeros_like(r_scr)

    acc_scr[...] = jnp.zeros_like(acc_scr)
    init = tuple((stage_scores(c, 0, 0), jnp.full((1, 2 * tq), -jnp.inf, F32)) for c in range(hps))
    d_model = x_ref.shape[2]
    gates = gate_ref[...].reshape(nb * tq, 2 * d_model)
    merged_scr[...] = (gates[:, :d_model] * _dot(oa_scr[...], woa_ref[...]).astype(BF16)
                       + gates[:, d_model:] * _dot(ob_scr[...], wob_ref[...]).astype(BF16))
    n_before = nsub * qi
    carry = lax.fori_loop(0, jnp.where(live, n_before, 0), body, init)

    def retention_stage1(sub, state):
        rows = slice(sub * kblk, (sub + 1) * kblk)
        return [_retention_stage1(
            _slab(mix_ref, "qb", j, bi)[rows], _slab(mix_ref, "kb", j, bi)[rows], _slab(mix_ref, "vb", j, bi)[rows],
            state[c], dmat_ref[j], xi_ref[j], zeta_ref[j], scal_ref[1 + H_A + j])
            for c, (bi, j) in enumerate(chains)]

    def retention_stage2(sub, ret):
        rows = slice(sub * kblk, (sub + 1) * kblk)
        for c, (bi, j) in enumerate(chains):
            ob = _retention_stage2(ret[c][0], ret[c][1], _slab(mix_ref, "vb", j, bi)[rows])
            ob = (_lane_rms(ob) * _slab(mix_ref, "zb", j, bi)[rows].astype(F32)).astype(BF16)
            ob_scr[bi * tq + sub * kblk:bi * tq + (sub + 1) * kblk, lanes(j)] = ob

    def cat(pieces):
        pieces = [p for p in pieces if p.shape[1] > 0]
        return pieces[0] if len(pieces) == 1 else jnp.concatenate(pieces, axis=1)

    def diagonal_block(dsub, c, m):
        kb = n_before + dsub
        w = tq - dsub * kblk
        cols = (slice(dsub * kblk, tq), slice(tq + dsub * kblk, 2 * tq))
        if dsub == 0:
            s = s_scrs[c][lax.rem(n_before, 2)]
        else:
            s = _dot(keys(c, kb), cat([qst[c][:, q] for q in cols]))
        dbias = dbias_ref[head(c)]
        s = cat([s[:, :kblk] + dbias, s[:, kblk:w], s[:, w:w + kblk] + dbias, s[:, w + kblk:]])
        shift = (dsub * kblk) * scal_ref[1 + head(c)]
        m_part, alpha, p = softmax_step(cat([m[:, q] for q in cols]), s,
                                        jnp.max(s, axis=0, keepdims=True), shift)
        pv = _dot(values1(c, kb), p)
        for h, q in enumerate(cols):
            acc_scr[c, :, q] = alpha[:, h * w:(h + 1) * w] * acc_scr[c, :, q] + pv[:, h * w:(h + 1) * w]
        return cat([m[:, :dsub * kblk], m_part[:, :w], m[:, tq:tq + dsub * kblk], m_part[:, w:]])

    def final_projection(cols):
        y = x_ref[:, :, cols].reshape(nb * tq, -1) + _dot(merged_scr[...], wout_ref[:, cols])
        y_ref[:, :, cols] = y.reshape(nb, tq, -1)

    out_cols = [slice(c, c + MXU_TILE) for c in range(0, d_model, MXU_TILE)]
    every = max(1, (hps * nsub) // len(out_cols))
    state = [r_scr[c] for c in range(hps)]
    ms = [carry[c][1] for c in range(hps)]
    for sub in range(nsub):
        ret = retention_stage1(sub, state)
        state = [r[2] for r in ret]
        for c in range(hps):
            if (sub * hps + c) % every == 0 and out_cols:
                final_projection(out_cols.pop(0))
            ms[c] = diagonal_block(sub, c, ms[c])
        retention_stage2(sub, ret)
    while out_cols:
        final_projection(out_cols.pop(0))
    for c, (bi, j) in enumerate(chains):
        r_new = jnp.where(live, state[c], r_scr[c])
        r_scr[c] = r_new
        rout_ref[bi, j] = r_new
    for c, (bi, j) in enumerate(chains):
        ot = acc_scr[c, :HEAD_W] * (1.0 / acc_scr[c, HEAD_W:HEAD_W + 1])
        oa = (ot[:, :tq] - lam * ot[:, tq:]).T
        oa = (_lane_rms(oa) * sg_ref[0:1, :]) * sg_ref[1:2, :]
        oa = (oa * _slab(mix_ref, "za", j, bi)[...].astype(F32)).astype(BF16)
        oa_scr[bi * tq:(bi + 1) * tq, lanes(j)] = oa


def _mixer_prompt(scal, qat, kab, vat, mix, dbias, kpos, kslope, dmat, xi, zeta, sg, x, gates, woa, wob, wout):
    b, t, w = kab.shape
    d = x.shape[2]
    tq, kblk, nb = Q_TILE, KEY_BLOCK, SEQS_PER_STEP
    assert t % tq == 0 and tq % kblk == 0 and kblk % CHUNK == 0 and H_A == H_B and w == H_A * HEAD_W
    assert dbias.shape[1:] == (kblk, kblk) and d % (tq // kblk) == 0 and b % nb == 0
    nq = t // tq
    n_tiles = (b // nb) * nq
    cur = lambda s: jnp.minimum(s, n_tiles - 1)
    prev = lambda s: jnp.maximum(s - 1, 0)
    tile = pl.BlockSpec((nb, tq, mix.shape[2]), lambda s: (cur(s) // nq, cur(s) % nq, 0))
    tile_t = pl.BlockSpec((nb, w, tq), lambda s: (cur(s) // nq, 0, cur(s) % nq))
    whole = pl.BlockSpec((nb, t, w), lambda s: (cur(s) // nq, 0, 0))
    whole_t = pl.BlockSpec((nb, w, t), lambda s: (cur(s) // nq, 0, 0))
    rows = lambda a: pl.BlockSpec((nb, tq, a.shape[2]), lambda s: (prev(s) // nq, prev(s) % nq, 0))
    full = lambda a: pl.BlockSpec(a.shape, lambda s: (0,) * a.ndim, pipeline_mode=pl.Buffered(1))
    return pl.pallas_call(
        functools.partial(_mixer_prompt_kernel, tq=tq, kblk=kblk, nb=nb, nq=nq, n_tiles=n_tiles),
        grid=(n_tiles + 1,),
        in_specs=[pl.BlockSpec(memory_space=pltpu.SMEM),
                  tile_t, whole, whole_t, tile,
                  full(dbias), full(kpos), full(kslope), full(dmat), full(xi), full(zeta), full(sg),
                  rows(x), rows(gates), full(woa), full(wob), full(wout)],
        out_specs=[rows(x),
                   pl.BlockSpec((nb, H_B, DK_B, DV_B), lambda s: (cur(s) // nq, 0, 0, 0))],
        out_shape=[jax.ShapeDtypeStruct((b, t, d), F32),
                   jax.ShapeDtypeStruct((b, H_B, DK_B, DV_B), F32)],
        scratch_shapes=[pltpu.VMEM((nb * H_B, DK_B, DV_B), F32),
                        pltpu.VMEM((nb * H_A, HEAD_W + SUM_ROWS, 2 * tq), F32),
                        pltpu.VMEM((nb * tq, w), BF16), pltpu.VMEM((nb * tq, w), BF16),
                        pltpu.VMEM((nb * tq, d), BF16)]
                       + [pltpu.VMEM((2, kblk, 2 * tq), F32)] * (nb * H_A),
        compiler_params=pltpu.CompilerParams(
            dimension_semantics=("arbitrary",), vmem_limit_bytes=VMEM_LIMIT),
        name="mixer_prompt",
    )(scal, qat, kab, vat, mix, dbias, kpos, kslope, dmat, xi, zeta, sg, x, gates, woa, wob, wout)


def _mixer_sample_kernel(scal_ref, qa_ref, kc_ref, vc_ref, kn_ref, vn_ref, mix_ref, r_ref,
                         abias_ref, dmat_ref, xi_ref, zeta_ref, sg_ref,
                         oa_ref, ob_ref, rout_ref, m_scr, acc_scr, *, n, past, tk, sub):
    kbi = pl.program_id(1)
    lam = scal_ref[0]
    lanes = lambda j: slice(j * HEAD_W, (j + 1) * HEAD_W)

    @pl.when(kbi == 0)
    def _():
        m_scr[...] = jnp.full(m_scr.shape, -jnp.inf, F32)
        acc_scr[...] = jnp.zeros_like(acc_scr)

    col = lax.broadcasted_iota(jnp.int32, (1, sub), 1)
    qs = [_split_halves(qa_ref[0, :, lanes(j)]) for j in range(H_A)]

    def scores(j, c):
        kk = kc_ref[0, lanes(j), c * sub:(c + 1) * sub].astype(BF16)
        kpos = (col + (kbi * tk + (c * sub - past))).astype(F32)
        return _dot(qs[j], kk) + kpos * scal_ref[1 + j]

    blocks = [(j, c) for c in range(tk // sub) for j in range(H_A)]
    carry = [(m_scr[j], acc_scr[j]) for j in range(H_A)]
    staged = [scores(*blk) for blk in blocks[:SCORE_LOOKAHEAD]]
    for i, (j, c) in enumerate(blocks):
        if i + SCORE_LOOKAHEAD < len(blocks):
            staged.append(scores(*blocks[i + SCORE_LOOKAHEAD]))
        vv = vc_ref[0, pl.ds(c * sub * H_A + j, sub, stride=H_A), :].astype(BF16)
        carry[j] = _softmax_update(carry[j], staged[i], vv)
    for j in range(H_A):
        m_scr[j], acc_scr[j] = carry[j]

    @pl.when(kbi == pl.num_programs(1) - 1)
    def _():
        s_new, ret = [], []
        for j in range(H_A):
            bias = abias_ref[j]
            s_new.append(_nt_dot(qs[j], kn_ref[0, :, lanes(j)]) + jnp.concatenate([bias, bias], axis=0))
        for j in range(H_A):
            s_ret, cross, r_new = _retention_stage1(
                _slab(mix_ref, "qb", j)[...], _slab(mix_ref, "kb", j)[...], _slab(mix_ref, "vb", j)[...],
                r_ref[0, j], dmat_ref[j], xi_ref[j], zeta_ref[j], scal_ref[1 + H_A + j])
            rout_ref[0, j] = r_new
            ret.append((s_ret, cross))
        for j in range(H_A):
            _, acc = _softmax_update((m_scr[j], acc_scr[j]), s_new[j], vn_ref[0, :, lanes(j)])
            oa = _diff_combine(acc, lam, n)
            oa = (_lane_rms(oa) * sg_ref[0:1, :]) * sg_ref[1:2, :]
            oa_ref[0, :, lanes(j)] = (oa * _slab(mix_ref, "za", j)[...].astype(F32)).astype(BF16)
        for j in range(H_A):
            ob = _retention_stage2(*ret[j], _slab(mix_ref, "vb", j)[...])
            ob_ref[0, :, lanes(j)] = (_lane_rms(ob) * _slab(mix_ref, "zb", j)[...].astype(F32)).astype(BF16)


def _mixer_sample(scal, qa, kc, vc, kab, vab, mix, r0, abias, dmat, xi, zeta, sg):
    b, n, w = qa.shape
    past = kc.shape[2]
    tk, sub = SAMPLE_KEY_TILE, SAMPLE_KEY_BLOCK
    assert past % tk == 0 and tk % sub == 0 and past % CHUNK == 0 and n <= CHUNK and H_A == H_B
    tile = pl.BlockSpec((1, n, w), lambda bi, ki: (bi, 0, 0))
    mix_tile = pl.BlockSpec((1, n, mix.shape[2]), lambda bi, ki: (bi, 0, 0))
    cache_k = pl.BlockSpec((1, w, tk), lambda bi, ki: (bi, 0, ki))
    cache_v = pl.BlockSpec((1, tk * H_A, HEAD_W), lambda bi, ki: (bi, ki, 0))
    state = pl.BlockSpec((1, H_B, DK_B, DV_B), lambda bi, ki: (bi, 0, 0, 0))
    full = lambda a: pl.BlockSpec(a.shape, lambda bi, ki: (0,) * a.ndim)
    return pl.pallas_call(
        functools.partial(_mixer_sample_kernel, n=n, past=past, tk=tk, sub=sub),
        grid=(b, past // tk),
        in_specs=[pl.BlockSpec(memory_space=pltpu.SMEM),
                  tile, cache_k, cache_v, tile, tile, mix_tile, state,
                  full(abias), full(dmat), full(xi), full(zeta), full(sg)],
        out_specs=[tile, tile, state],
        out_shape=[jax.ShapeDtypeStruct((b, n, H_A * HEAD_W), BF16),
                   jax.ShapeDtypeStruct((b, n, H_B * HEAD_W), BF16),
                   jax.ShapeDtypeStruct((b, H_B, DK_B, DV_B), F32)],
        scratch_shapes=[pltpu.VMEM((H_A, 2 * n, 1), F32),
                        pltpu.VMEM((H_A, 2 * n, 2 * DV_A), F32)],
        compiler_params=pltpu.CompilerParams(
            dimension_semantics=("arbitrary", "arbitrary"), vmem_limit_bytes=VMEM_LIMIT),
        name="mixer_sample",
    )(scal, qa, kc, vc, kab, vab, mix, r0, abias, dmat, xi, zeta, sg)


def _proj_out_kernel(x_ref, oa_ref, ob_ref, gate_ref, woa_ref, wob_ref, wout_ref, y_ref):
    d = x_ref.shape[1]
    m = (gate_ref[:, :d].astype(F32) * _dot(oa_ref[...], woa_ref[...])
         + gate_ref[:, d:].astype(F32) * _dot(ob_ref[...], wob_ref[...]))
    y_ref[...] = x_ref[...] + _dot(m.astype(BF16), wout_ref[...])


def _proj_out(x2, oa, ob, gates, woa, wob, wout):
    n, d = x2.shape
    tm = min(ROW_TILE, n)
    assert n % tm == 0
    row = lambda w: pl.BlockSpec((tm, w), lambda i: (i, 0))
    full = lambda a: pl.BlockSpec(a.shape, lambda i: (0,) * a.ndim)
    return pl.pallas_call(
        _proj_out_kernel,
        grid=(n // tm,),
        in_specs=[row(d), row(oa.shape[1]), row(ob.shape[1]), row(gates.shape[1]),
                  full(woa), full(wob), full(wout)],
        out_specs=row(d),
        out_shape=jax.ShapeDtypeStruct((n, d), F32),
        compiler_params=pltpu.CompilerParams(
            dimension_semantics=("arbitrary",), vmem_limit_bytes=VMEM_LIMIT),
        name="proj_out",
    )(x2, oa, ob, gates, woa, wob, wout)


def _block_tables(t, slopes, log_g):
    f32 = np.float32
    i = np.arange(t, dtype=np.int32)
    allowed = (i[None, :] // CHUNK) <= (i[:, None] // CHUNK)
    dist = np.abs(i[:, None] - i[None, :]).astype(f32)
    jf = i.astype(f32)
    abias = -slopes[:, None, None] * dist[None] + slopes[:, None, None] * jf[None, :, None]
    abias = np.where(allowed[None], abias, f32(-np.inf)).astype(f32)
    dmat = np.where(allowed[None], np.exp(dist[None] * log_g[:, None, None]), f32(0.0)).astype(f32)
    xi = np.exp((jf + f32(1.0))[None] * log_g[:, None]).astype(f32)
    zeta = np.exp((f32(t) - f32(1.0) - jf)[None] * log_g[:, None]).astype(f32)
    bc = lambda a: np.ascontiguousarray(np.broadcast_to(a[:, :, None], a.shape + (HEAD_W,)))
    return abias, dmat, bc(xi), bc(zeta)


N_SLOPE_PARTS = 3


def _alibi_features(t, width, slopes):
    assert t <= 256
    bf16 = jnp.bfloat16
    kpos = np.zeros((t, HEAD_W), np.float32)
    kpos[:, :N_SLOPE_PARTS] = np.arange(t, dtype=np.float32)[:, None]
    kslope = np.zeros((len(slopes), HEAD_W, width), np.float32)
    rest = slopes.astype(np.float32)
    for part in range(N_SLOPE_PARTS):
        piece = rest.astype(bf16).astype(np.float32)
        kslope[:, part, :] = piece[:, None]
        rest = rest - piece
    return kpos.astype(bf16), kslope.astype(bf16)


def _key_bias_table(t, slopes):
    kb = (slopes[:, None] * np.arange(t, dtype=np.float32)[None]).astype(np.float32)
    return np.ascontiguousarray(np.broadcast_to(kb[:, :, None], kb.shape + (HEAD_W,)))


def _layer(x_prompt, x_sample, ck, cv, r0, norm_g, w_in, b_gate, qn_g, kn_g,
           lam_q1, lam_k1, lam_q2, lam_k2, subln_g, w_oa, w_ob, w_out, lam_init):
    bp, tp, d = x_prompt.shape
    bs, ts, _ = x_sample.shape
    past = ck.shape[1]
    slopes = (2.0 ** (-8.0 * np.arange(1, H_A + 1, dtype=np.float32) / H_A)).astype(np.float32)
    log_g = np.log(1.0 - 2.0 ** (-5.0 - np.arange(H_B, dtype=np.float32))).astype(np.float32)
    lam = (jnp.exp(jnp.sum(lam_q1 * lam_k1).astype(F32))
           - jnp.exp(jnp.sum(lam_q2 * lam_k2).astype(F32)) + lam_init)

    woa_bf, wob_bf, wout_bf = w_oa.astype(BF16), w_ob.astype(BF16), w_out.astype(BF16)
    g = norm_g.reshape(1, d)
    qg_t = jnp.tile(qn_g, 2 * H_A).reshape(1, H_A * HEAD_W)
    kg_t = jnp.tile(kn_g, 2 * H_A).reshape(1, H_A * HEAD_W)
    grp = np.arange(MXU_TILE, dtype=np.int32) // DK_A
    gm = jnp.asarray(grp[:, None] == grp[None, :], dtype=BF16)
    sg = jnp.stack([subln_g, jnp.full((DV_A,), 1.0 - lam_init, F32)])

    def scal(n_block, logit_scale):
        consts = np.concatenate([slopes * logit_scale, np.exp(np.float32(n_block) * log_g)]).astype(np.float32)
        return jnp.concatenate([lam.reshape(1).astype(F32), jnp.asarray(consts)])

    prompt_acts, sample_acts = _proj_in(
        x_prompt, x_sample.reshape(1, bs * ts, d), g, w_in, b_gate, qg_t, kg_t, gm,
        tm=ROW_TILE, tm_sample=min(SAMPLE_ROW_TILE, bs * ts))

    qat, kaf, kab, vaf, vat, mix, gates = prompt_acts
    abias, dmat, xi, zeta = _block_tables(KEY_BLOCK, slopes, log_g)
    kbias = _key_bias_table(KEY_BLOCK, slopes)
    dbias = np.ascontiguousarray(np.swapaxes(abias, 1, 2) - kbias[:, :, :1])
    flat = lambda a: a.reshape(-1, a.shape[-1])
    log2e = np.float32(LOG2E)
    kpos, kslope = _alibi_features(KEY_BLOCK, 2 * Q_TILE, slopes * log2e)
    y_p, ret_p = _mixer_prompt(scal(float(KEY_BLOCK), log2e), qat, kab, vat, mix,
                               dbias * log2e, kpos, kslope, dmat, xi, zeta, sg,
                               x_prompt, gates, woa_bf, wob_bf, wout_bf)
    k_p = jnp.transpose(kaf.reshape(bp, H_A, 2, DK_A, tp), (0, 4, 1, 2, 3))
    v_p = vaf.reshape(bp, tp, H_A, DV_A)

    qa, kaf, kab, vaf, vab, mix, gates = sample_acts
    r3 = lambda a: a.reshape(bs, ts, a.shape[-1])
    tabs = _block_tables(ts, slopes, log_g)
    ck_t = jnp.transpose(ck, (0, 2, 3, 4, 1)).reshape(bs, H_A * HEAD_W, past)
    oa, ob, ret_s = _mixer_sample(scal(float(ts), np.float32(1.0)), r3(qa), ck_t, cv.reshape(bs, past * H_A, HEAD_W),
                                  r3(kab), r3(vab), r3(mix), r0, *tabs, sg)
    y_s = _proj_out(flat(x_sample), flat(oa), flat(ob), flat(gates),
                    woa_bf, wob_bf, wout_bf).reshape(bs, ts, d)
    k_s = kaf.reshape(bs, ts, H_A, 2, DK_A)
    v_s = vaf.reshape(bs, ts, H_A, DV_A)
    return y_p, y_s, k_p, v_p, ret_p, k_s, v_s, ret_s


def kernel(x_prompt, x_sample, cache_k_diff, cache_v_diff, state_ret, norm_g, w_in, b_gate, qn_g, kn_g,
           lam_q1, lam_k1, lam_q2, lam_k2, subln_g, w_o_diff, w_o_ret, w_out):
    depth = w_in.shape[0]
    hp, hs = x_prompt, x_sample
    outs = [[] for _ in range(6)]
    for l in range(depth):
        lam_init = 0.8 - 0.6 * math.exp(-0.3 * l)
        hp, hs, k_p, v_p, ret_p, k_s, v_s, ret_s = _layer(
            hp, hs, cache_k_diff[l], cache_v_diff[l], state_ret[l], norm_g[l], w_in[l], b_gate[l],
            qn_g[l], kn_g[l], lam_q1[l], lam_k1[l], lam_q2[l], lam_k2[l], subln_g[l],
            w_o_diff[l], w_o_ret[l], w_out[l], lam_init)
        for lst, a in zip(outs, (k_p, v_p, ret_p, k_s, v_s, ret_s)):
            lst.append(a)
    return (hp, hs) + tuple(jnp.stack(o) for o in outs)
```

```python
import functools
import math

import jax
import jax.numpy as jnp
import numpy as np
from jax import lax
from jax.experimental import pallas as pl
from jax.experimental.pallas import tpu as pltpu

F32 = jnp.float32
BF16 = jnp.bfloat16

CHUNK = 64
H_A = 4
DK_A = 64
DV_A = 2 * DK_A
H_B = 4
DK_B = 128
DV_B = 128
HEAD_W = 128
EPS = 1e-6
LOG2E = math.log2(math.e)

MXU_TILE = 256
ROW_TILE = 512
SAMPLE_ROW_TILE = 256
Q_TILE = 256
KEY_BLOCK = 256
SEQS_PER_STEP = 2
SUM_ROWS = 16
SAMPLE_KEY_TILE = 4096
SAMPLE_KEY_BLOCK = 1024
SCORE_LOOKAHEAD = 8
V7X_VMEM_BYTES = 64 * 1024 * 1024
VMEM_LIMIT = V7X_VMEM_BYTES - 3 * 1024 * 1024


def _nt_dot(a, b):
    return lax.dot_general(a, b, (((1,), (1,)), ((), ())), preferred_element_type=F32)


def _tn_dot(a, b):
    return lax.dot_general(a, b, (((0,), (0,)), ((), ())), preferred_element_type=F32)


def _dot(a, b):
    return jnp.dot(a, b, preferred_element_type=F32)


def _sigmoid(x):
    return 1.0 / (1.0 + jnp.exp(-x))


MIX_SLABS = ("za", "qb", "kb", "vb", "zb")


def _slab(mix_ref, name, head=None, seq=0):
    lo = MIX_SLABS.index(name) * H_A * HEAD_W
    if head is None:
        return mix_ref.at[seq, :, lo:lo + H_A * HEAD_W]
    return mix_ref.at[seq, :, lo + head * HEAD_W:lo + (head + 1) * HEAD_W]


def _proj_in_kernel(x_ref, g_ref, w_ref, bg_ref, qg_ref, kg_ref, gm_ref,
                    qa_ref, kaf_ref, kab_ref, vaf_ref, vab_ref, mix_ref, gate_ref,
                    *, wa, wb, d_model, feature_major, q_scale):
    x = x_ref[0]
    tm = x.shape[0]
    ms = jnp.mean(x * x, axis=-1, keepdims=True)
    xn = (x * lax.rsqrt(ms + EPS)) * g_ref[...]

    names = ("qa", "ka", "va", "za", "qb", "kb", "vb", "zb", "ga", "gb")
    widths = (wa, wa, wa, wa, wb, wb, wb, wb, d_model, d_model)
    offs = dict(zip(names, (sum(widths[:i]) for i in range(len(widths)))))
    width = dict(zip(names, widths))

    def seg(name):
        return _dot(xn, w_ref[:, offs[name]:offs[name] + width[name]])

    def group_norm(h, gain):
        sq = (h * h).astype(BF16)
        ss = jnp.concatenate([_dot(sq[:, c:c + MXU_TILE], gm_ref[...]) for c in range(0, wa, MXU_TILE)], axis=1)
        return (h * lax.rsqrt(ss * (1.0 / DK_A) + EPS)) * gain

    def silu(h):
        return h * _sigmoid(h)

    maybe_t = (lambda a: a.T) if feature_major else (lambda a: a)
    gate_ref[0, :, :d_model] = _sigmoid(seg("ga") + bg_ref[0:1, :]).astype(BF16)
    gate_ref[0, :, d_model:] = _sigmoid(seg("gb") + bg_ref[1:2, :]).astype(BF16)
    qa_ref[0] = maybe_t(group_norm(seg("qa"), qg_ref[...]) * q_scale).astype(BF16)
    ka = group_norm(seg("ka"), kg_ref[...])
    kaf_ref[0] = maybe_t(ka)
    kab_ref[0] = ka.astype(BF16)
    va = seg("va")
    vab_ref[0] = maybe_t(va).astype(BF16)
    for h in range(H_A):
        vaf_ref[pl.ds(h, tm, stride=H_A), :] = va[:, h * HEAD_W:(h + 1) * HEAD_W]
    _slab(mix_ref, "za")[...] = silu(seg("za")).astype(BF16)
    _slab(mix_ref, "zb")[...] = silu(seg("zb")).astype(BF16)
    _slab(mix_ref, "kb")[...] = (seg("kb") * (DK_B ** -0.5)).astype(BF16)
    _slab(mix_ref, "qb")[...] = seg("qb").astype(BF16)
    _slab(mix_ref, "vb")[...] = seg("vb").astype(BF16)


N_PROJ_OUTS = 7


def _proj_in_both_kernel(xp_ref, xs_ref, g_ref, w_ref, bg_ref, qg_ref, kg_ref, gm_ref, *out_refs,
                         n_prompt_steps, wa, wb, d_model):
    step = pl.program_id(0)
    consts = (g_ref, w_ref, bg_ref, qg_ref, kg_ref, gm_ref)

    @pl.when(step < n_prompt_steps)
    def _():
        _proj_in_kernel(xp_ref, *consts, *out_refs[:N_PROJ_OUTS],
                        wa=wa, wb=wb, d_model=d_model, feature_major=True,
                        q_scale=DK_A ** -0.5 * LOG2E)

    @pl.when(step >= n_prompt_steps)
    def _():
        _proj_in_kernel(xs_ref, *consts, *out_refs[N_PROJ_OUTS:],
                        wa=wa, wb=wb, d_model=d_model, feature_major=False, q_scale=DK_A ** -0.5)


def _proj_in(x_prompt, x_sample, g, w_in, bg, qg_t, kg_t, gm, *, tm, tm_sample):
    b, t, d = x_prompt.shape
    n = x_sample.shape[1]
    wa = H_A * HEAD_W
    wb = H_B * HEAD_W
    assert t % tm == 0 and n % tm_sample == 0 and wa == wb and x_sample.shape[0] == 1
    nt = t // tm
    n_p, n_s = b * nt, n // tm_sample
    p_tile = lambda s: jnp.minimum(s, n_p - 1)
    s_tile = lambda s: jnp.clip(s - n_p, 0, n_s - 1)
    full = lambda a: pl.BlockSpec(a.shape, lambda s: (0,) * a.ndim)

    def group_specs(bsz, length, rows, tile, feature_major):
        nt_g = length // rows
        row = lambda w: pl.BlockSpec((1, rows, w), lambda s: (tile(s) // nt_g, tile(s) % nt_g, 0))
        act = lambda w, dt: jax.ShapeDtypeStruct((bsz, length, w), dt)
        if feature_major:
            fm_shape = lambda dt: jax.ShapeDtypeStruct((bsz, wa, length), dt)
            fm_spec = pl.BlockSpec((1, wa, rows), lambda s: (tile(s) // nt_g, 0, tile(s) % nt_g))
        else:
            fm_shape, fm_spec = (lambda dt: act(wa, dt)), row(wa)
        shapes = [fm_shape(BF16),
                  fm_shape(F32), act(wa, BF16),
                  jax.ShapeDtypeStruct((bsz * length * H_A, HEAD_W), F32),
                  fm_shape(BF16),
                  act(len(MIX_SLABS) * wa, BF16),
                  act(2 * d, BF16)]
        specs = [fm_spec, fm_spec, row(wa),
                 pl.BlockSpec((rows * H_A, HEAD_W), lambda s: (tile(s), 0)),
                 fm_spec, row(len(MIX_SLABS) * wa), row(2 * d)]
        return row(d), shapes, specs

    xp_spec, p_shapes, p_specs = group_specs(b, t, tm, p_tile, True)
    xs_spec, s_shapes, s_specs = group_specs(1, n, tm_sample, s_tile, False)
    assert len(p_shapes) == N_PROJ_OUTS
    outs = pl.pallas_call(
        functools.partial(_proj_in_both_kernel, n_prompt_steps=n_p, wa=wa, wb=wb, d_model=d),
        grid=(n_p + n_s,),
        in_specs=[xp_spec, xs_spec, full(g), full(w_in), full(bg), full(qg_t), full(kg_t), full(gm)],
        out_specs=p_specs + s_specs,
        out_shape=p_shapes + s_shapes,
        compiler_params=pltpu.CompilerParams(
            dimension_semantics=("arbitrary",), vmem_limit_bytes=VMEM_LIMIT),
        name="proj_in",
    )(x_prompt, x_sample, g, w_in, bg, qg_t, kg_t, gm)
    return outs[:N_PROJ_OUTS], outs[N_PROJ_OUTS:]


def _split_halves(q):
    lane = lax.broadcasted_iota(jnp.int32, q.shape, 1)
    zero = jnp.zeros_like(q)
    return jnp.concatenate([jnp.where(lane < DK_A, q, zero), jnp.where(lane >= DK_A, q, zero)], axis=0)


def _softmax_update(carry, s, v):
    m, acc = carry
    m_new = jnp.maximum(m, jnp.max(s, axis=-1, keepdims=True))
    p = jnp.exp(s - m_new).astype(BF16)
    v1 = jnp.concatenate([v, jnp.ones_like(v)], axis=1)
    return m_new, jnp.exp(m - m_new) * acc + _dot(p, v1)


def _diff_combine(acc, lam, t):
    dv = acc.shape[1] // 2
    o = acc[:, :dv] / acc[:, dv:]
    return o[:t] - lam * o[t:]


def _lane_rms(o):
    return o * lax.rsqrt(jnp.mean(o * o, axis=-1, keepdims=True) + EPS)


def _retention_stage1(q, k, v, r, dmat, xi, zeta, cdecay):
    s = (_nt_dot(q, k) * dmat).astype(BF16)
    cross = _dot(q, r.astype(BF16)) * xi
    kz = (k.astype(F32) * zeta).astype(BF16)
    r_new = cdecay * r + _tn_dot(kz, v)
    return s, cross, r_new


def _retention_stage2(s, cross, v):
    return _dot(s, v) + cross


def _mixer_prompt_kernel(scal_ref, qt_ref, k_ref, vt_ref, mix_ref,
                         dbias_ref, kpos_ref, kslope_ref, dmat_ref, xi_ref, zeta_ref, sg_ref,
                         x_ref, gate_ref, woa_ref, wob_ref, wout_ref,
                         y_ref, rout_ref, r_scr, acc_scr, oa_scr, ob_scr, merged_scr, *s_scrs,
                         tq, kblk, nb, nq, n_tiles):
    chains = [(bi, j) for bi in range(nb) for j in range(H_A)]
    hps = len(chains)
    nsub = tq // kblk
    t = pl.program_id(0)
    live = t < n_tiles
    qi = lax.rem(jnp.minimum(t, n_tiles - 1), nq)
    lam = scal_ref[0]
    lanes = lambda j: slice(j * HEAD_W, (j + 1) * HEAD_W)

    def split_halves_t(qt):
        feat = lax.broadcasted_iota(jnp.int32, qt.shape, 0)
        zero = jnp.zeros_like(qt)
        return jnp.concatenate([jnp.where(feat < DK_A, qt, zero), jnp.where(feat >= DK_A, qt, zero)], axis=1)

    qst = [jnp.concatenate([split_halves_t(qt_ref[bi, lanes(j), :]), kslope_ref[j]], axis=0)
           for bi, j in chains]

    head = lambda c: chains[c][1]

    def keys(c, kb):
        bi, j = chains[c]
        k = k_ref[bi, pl.ds(pl.multiple_of(kb * kblk, kblk), kblk), lanes(j)]
        return jnp.concatenate([k, kpos_ref[...]], axis=1)

    def values1(c, kb):
        bi, j = chains[c]
        vt = vt_ref[bi, lanes(j), pl.ds(pl.multiple_of(kb * kblk, kblk), kblk)]
        return jnp.concatenate([vt, jnp.ones((SUM_ROWS, kblk), BF16)], axis=0)

    def stage_scores(c, kb, slot):
        s = _dot(keys(c, kb), qst[c])
        s_scrs[c][slot] = s
        return jnp.max(s, axis=0, keepdims=True)

    def softmax_step(m, s, s_max, shift):
        m_new = jnp.maximum(m, s_max + shift)
        return m_new, jnp.exp2(m - m_new), jnp.exp2(s - (m_new - shift)).astype(BF16)

    def body(kb, carry):
        slot = lax.rem(kb, 2)
        new = []
        for c in range(hps):
            smax_cur, m = carry[c]
            s_cur = s_scrs[c][slot]
            smax_next = stage_scores(c, kb + 1, 1 - slot)
            shift = ((kb - nsub * qi) * kblk).astype(F32) * scal_ref[1 + head(c)]
            m, alpha, p = softmax_step(m, s_cur, smax_cur, shift)
            acc_scr[c] = alpha * acc_scr[c] + _dot(values1(c, kb), p)
            new.append((smax_next, m))
        return tuple(new)

    @pl.when(t == 0)
    def _():
        oa_scr[...] = jnp.zeros_like(oa_scr)
        ob_scr[...] = jnp.zeros_like(ob_scr)

    @pl.when(qi == 0)
    def _():
        r_scr[...] = jnp.zeros_like(r_scr)

    acc_scr[...] = jnp.zeros_like(acc_scr)
    init = tuple((stage_scores(c, 0, 0), jnp.full((1, 2 * tq), -jnp.inf, F32)) for c in range(hps))
    d_model = x_ref.shape[2]
    pa = _dot(oa_scr[...], woa_ref[...]).astype(BF16)
    pb = _dot(ob_scr[...], wob_ref[...]).astype(BF16)
    for bi in range(nb):
        rows = slice(bi * tq, (bi + 1) * tq)
        merged_scr[rows] = gate_ref[bi, :, :d_model] * pa[rows] + gate_ref[bi, :, d_model:] * pb[rows]
    n_before = nsub * qi
    carry = lax.fori_loop(0, jnp.where(live, n_before, 0), body, init)

    def retention_stage1(sub, state):
        rows = slice(sub * kblk, (sub + 1) * kblk)
        return [_retention_stage1(
            _slab(mix_ref, "qb", j, bi)[rows], _slab(mix_ref, "kb", j, bi)[rows], _slab(mix_ref, "vb", j, bi)[rows],
            state[c], dmat_ref[j], xi_ref[j], zeta_ref[j], scal_ref[1 + H_A + j])
            for c, (bi, j) in enumerate(chains)]

    def retention_stage2(sub, ret):
        rows = slice(sub * kblk, (sub + 1) * kblk)
        for c, (bi, j) in enumerate(chains):
            ob = _retention_stage2(ret[c][0], ret[c][1], _slab(mix_ref, "vb", j, bi)[rows])
            ob = (_lane_rms(ob) * _slab(mix_ref, "zb", j, bi)[rows].astype(F32)).astype(BF16)
            ob_scr[bi * tq + sub * kblk:bi * tq + (sub + 1) * kblk, lanes(j)] = ob

    def cat(pieces):
        pieces = [p for p in pieces if p.shape[1] > 0]
        return pieces[0] if len(pieces) == 1 else jnp.concatenate(pieces, axis=1)

    def diagonal_block(dsub, c, m):
        kb = n_before + dsub
        w = tq - dsub * kblk
        cols = (slice(dsub * kblk, tq), slice(tq + dsub * kblk, 2 * tq))
        if dsub == 0:
            s = s_scrs[c][lax.rem(n_before, 2)]
        else:
            s = _dot(keys(c, kb), cat([qst[c][:, q] for q in cols]))
        dbias = dbias_ref[head(c)]
        s = cat([s[:, :kblk] + dbias, s[:, kblk:w], s[:, w:w + kblk] + dbias, s[:, w + kblk:]])
        shift = (dsub * kblk) * scal_ref[1 + head(c)]
        m_part, alpha, p = softmax_step(cat([m[:, q] for q in cols]), s,
                                        jnp.max(s, axis=0, keepdims=True), shift)
        pv = _dot(values1(c, kb), p)
        for h, q in enumerate(cols):
            acc_scr[c, :, q] = alpha[:, h * w:(h + 1) * w] * acc_scr[c, :, q] + pv[:, h * w:(h + 1) * w]
        return cat([m[:, :dsub * kblk], m_part[:, :w], m[:, tq:tq + dsub * kblk], m_part[:, w:]])

    def final_projection(cols):
        proj = _dot(merged_scr[...], wout_ref[:, cols])
        for bi in range(nb):
            y_ref[bi, :, cols] = x_ref[bi, :, cols] + proj[bi * tq:(bi + 1) * tq]

    out_cols = [slice(c, c + MXU_TILE) for c in range(0, d_model, MXU_TILE)]
    every = max(1, (hps * nsub) // len(out_cols))
    state = [r_scr[c] for c in range(hps)]
    ms = [carry[c][1] for c in range(hps)]
    for sub in range(nsub):
        ret = retention_stage1(sub, state)
        state = [r[2] for r in ret]
        for c in range(hps):
            if (sub * hps + c) % every == 0 and out_cols:
                final_projection(out_cols.pop(0))
            ms[c] = diagonal_block(sub, c, ms[c])
        retention_stage2(sub, ret)
    while out_cols:
        final_projection(out_cols.pop(0))
    for c, (bi, j) in enumerate(chains):
        r_new = jnp.where(live, state[c], r_scr[c])
        r_scr[c] = r_new
        rout_ref[bi, j] = r_new
    for c, (bi, j) in enumerate(chains):
        inv_l = 1.0 / acc_scr[c, HEAD_W:HEAD_W + 1]
        inv_l = jnp.concatenate([inv_l[:, :tq], lam * inv_l[:, tq:]], axis=1)
        ot = acc_scr[c, :HEAD_W] * inv_l
        oat = ot[:, :tq] - ot[:, tq:]
        oat = oat * lax.rsqrt(jnp.mean(oat * oat, axis=0, keepdims=True) + EPS)
        oa = (oat.T * sg_ref[0:1, :]) * sg_ref[1:2, :]
        oa = (oa * _slab(mix_ref, "za", j, bi)[...].astype(F32)).astype(BF16)
        oa_scr[bi * tq:(bi + 1) * tq, lanes(j)] = oa


def _mixer_prompt(scal, qat, kab, vat, mix, dbias, kpos, kslope, dmat, xi, zeta, sg, x, gates, woa, wob, wout):
    b, t, w = kab.shape
    d = x.shape[2]
    tq, kblk, nb = Q_TILE, KEY_BLOCK, SEQS_PER_STEP
    assert t % tq == 0 and tq % kblk == 0 and kblk % CHUNK == 0 and H_A == H_B and w == H_A * HEAD_W
    assert dbias.shape[1:] == (kblk, kblk) and d % (tq // kblk) == 0 and b % nb == 0
    nq = t // tq
    n_tiles = (b // nb) * nq
    cur = lambda s: jnp.minimum(s, n_tiles - 1)
    prev = lambda s: jnp.maximum(s - 1, 0)
    tile = pl.BlockSpec((nb, tq, mix.shape[2]), lambda s: (cur(s) // nq, cur(s) % nq, 0))
    tile_t = pl.BlockSpec((nb, w, tq), lambda s: (cur(s) // nq, 0, cur(s) % nq))
    whole = pl.BlockSpec((nb, t, w), lambda s: (cur(s) // nq, 0, 0))
    whole_t = pl.BlockSpec((nb, w, t), lambda s: (cur(s) // nq, 0, 0))
    rows = lambda a: pl.BlockSpec((nb, tq, a.shape[2]), lambda s: (prev(s) // nq, prev(s) % nq, 0))
    full = lambda a: pl.BlockSpec(a.shape, lambda s: (0,) * a.ndim, pipeline_mode=pl.Buffered(1))
    return pl.pallas_call(
        functools.partial(_mixer_prompt_kernel, tq=tq, kblk=kblk, nb=nb, nq=nq, n_tiles=n_tiles),
        grid=(n_tiles + 1,),
        in_specs=[pl.BlockSpec(memory_space=pltpu.SMEM),
                  tile_t, whole, whole_t, tile,
                  full(dbias), full(kpos), full(kslope), full(dmat), full(xi), full(zeta), full(sg),
                  rows(x), rows(gates), full(woa), full(wob), full(wout)],
        out_specs=[rows(x),
                   pl.BlockSpec((nb, H_B, DK_B, DV_B), lambda s: (cur(s) // nq, 0, 0, 0))],
        out_shape=[jax.ShapeDtypeStruct((b, t, d), F32),
                   jax.ShapeDtypeStruct((b, H_B, DK_B, DV_B), F32)],
        scratch_shapes=[pltpu.VMEM((nb * H_B, DK_B, DV_B), F32),
                        pltpu.VMEM((nb * H_A, HEAD_W + SUM_ROWS, 2 * tq), F32),
                        pltpu.VMEM((nb * tq, w), BF16), pltpu.VMEM((nb * tq, w), BF16),
                        pltpu.VMEM((nb * tq, d), BF16)]
                       + [pltpu.VMEM((2, kblk, 2 * tq), F32)] * (nb * H_A),
        compiler_params=pltpu.CompilerParams(
            dimension_semantics=("arbitrary",), vmem_limit_bytes=VMEM_LIMIT),
        name="mixer_prompt",
    )(scal, qat, kab, vat, mix, dbias, kpos, kslope, dmat, xi, zeta, sg, x, gates, woa, wob, wout)


def _mixer_sample_kernel(scal_ref, qa_ref, kc_ref, vc_ref, kn_ref, vn_ref, mix_ref, r_ref,
                         abias_ref, dmat_ref, xi_ref, zeta_ref, sg_ref,
                         oa_ref, ob_ref, rout_ref, m_scr, acc_scr, *, n, past, tk, sub):
    kbi = pl.program_id(1)
    lam = scal_ref[0]
    lanes = lambda j: slice(j * HEAD_W, (j + 1) * HEAD_W)

    @pl.when(kbi == 0)
    def _():
        m_scr[...] = jnp.full(m_scr.shape, -jnp.inf, F32)
        acc_scr[...] = jnp.zeros_like(acc_scr)

    col = lax.broadcasted_iota(jnp.int32, (1, sub), 1)
    qs = [_split_halves(qa_ref[0, :, lanes(j)]) for j in range(H_A)]

    def scores(j, c):
        kk = kc_ref[0, lanes(j), c * sub:(c + 1) * sub].astype(BF16)
        kpos = (col + (kbi * tk + (c * sub - past))).astype(F32)
        return _dot(qs[j], kk) + kpos * scal_ref[1 + j]

    blocks = [(j, c) for c in range(tk // sub) for j in range(H_A)]
    carry = [(m_scr[j], acc_scr[j]) for j in range(H_A)]
    staged = [scores(*blk) for blk in blocks[:SCORE_LOOKAHEAD]]
    for i, (j, c) in enumerate(blocks):
        if i + SCORE_LOOKAHEAD < len(blocks):
            staged.append(scores(*blocks[i + SCORE_LOOKAHEAD]))
        vv = vc_ref[0, pl.ds(c * sub * H_A + j, sub, stride=H_A), :].astype(BF16)
        carry[j] = _softmax_update(carry[j], staged[i], vv)
    for j in range(H_A):
        m_scr[j], acc_scr[j] = carry[j]

    @pl.when(kbi == pl.num_programs(1) - 1)
    def _():
        s_new, ret = [], []
        for j in range(H_A):
            bias = abias_ref[j]
            s_new.append(_nt_dot(qs[j], kn_ref[0, :, lanes(j)]) + jnp.concatenate([bias, bias], axis=0))
        for j in range(H_A):
            s_ret, cross, r_new = _retention_stage1(
                _slab(mix_ref, "qb", j)[...], _slab(mix_ref, "kb", j)[...], _slab(mix_ref, "vb", j)[...],
                r_ref[0, j], dmat_ref[j], xi_ref[j], zeta_ref[j], scal_ref[1 + H_A + j])
            rout_ref[0, j] = r_new
            ret.append((s_ret, cross))
        for j in range(H_A):
            _, acc = _softmax_update((m_scr[j], acc_scr[j]), s_new[j], vn_ref[0, :, lanes(j)])
            oa = _diff_combine(acc, lam, n)
            oa = (_lane_rms(oa) * sg_ref[0:1, :]) * sg_ref[1:2, :]
            oa_ref[0, :, lanes(j)] = (oa * _slab(mix_ref, "za", j)[...].astype(F32)).astype(BF16)
        for j in range(H_A):
            ob = _retention_stage2(*ret[j], _slab(mix_ref, "vb", j)[...])
            ob_ref[0, :, lanes(j)] = (_lane_rms(ob) * _slab(mix_ref, "zb", j)[...].astype(F32)).astype(BF16)


def _mixer_sample(scal, qa, kc, vc, kab, vab, mix, r0, abias, dmat, xi, zeta, sg):
    b, n, w = qa.shape
    past = kc.shape[2]
    tk, sub = SAMPLE_KEY_TILE, SAMPLE_KEY_BLOCK
    assert past % tk == 0 and tk % sub == 0 and past % CHUNK == 0 and n <= CHUNK and H_A == H_B
    tile = pl.BlockSpec((1, n, w), lambda bi, ki: (bi, 0, 0))
    mix_tile = pl.BlockSpec((1, n, mix.shape[2]), lambda bi, ki: (bi, 0, 0))
    cache_k = pl.BlockSpec((1, w, tk), lambda bi, ki: (bi, 0, ki))
    cache_v = pl.BlockSpec((1, tk * H_A, HEAD_W), lambda bi, ki: (bi, ki, 0))
    state = pl.BlockSpec((1, H_B, DK_B, DV_B), lambda bi, ki: (bi, 0, 0, 0))
    full = lambda a: pl.BlockSpec(a.shape, lambda bi, ki: (0,) * a.ndim)
    return pl.pallas_call(
        functools.partial(_mixer_sample_kernel, n=n, past=past, tk=tk, sub=sub),
        grid=(b, past // tk),
        in_specs=[pl.BlockSpec(memory_space=pltpu.SMEM),
                  tile, cache_k, cache_v, tile, tile, mix_tile, state,
                  full(abias), full(dmat), full(xi), full(zeta), full(sg)],
        out_specs=[tile, tile, state],
        out_shape=[jax.ShapeDtypeStruct((b, n, H_A * HEAD_W), BF16),
                   jax.ShapeDtypeStruct((b, n, H_B * HEAD_W), BF16),
                   jax.ShapeDtypeStruct((b, H_B, DK_B, DV_B), F32)],
        scratch_shapes=[pltpu.VMEM((H_A, 2 * n, 1), F32),
                        pltpu.VMEM((H_A, 2 * n, 2 * DV_A), F32)],
        compiler_params=pltpu.CompilerParams(
            dimension_semantics=("arbitrary", "arbitrary"), vmem_limit_bytes=VMEM_LIMIT),
        name="mixer_sample",
    )(scal, qa, kc, vc, kab, vab, mix, r0, abias, dmat, xi, zeta, sg)


def _proj_out_kernel(x_ref, oa_ref, ob_ref, gate_ref, woa_ref, wob_ref, wout_ref, y_ref):
    d = x_ref.shape[1]
    m = (gate_ref[:, :d].astype(F32) * _dot(oa_ref[...], woa_ref[...])
         + gate_ref[:, d:].astype(F32) * _dot(ob_ref[...], wob_ref[...]))
    y_ref[...] = x_ref[...] + _dot(m.astype(BF16), wout_ref[...])


def _proj_out(x2, oa, ob, gates, woa, wob, wout):
    n, d = x2.shape
    tm = min(ROW_TILE, n)
    assert n % tm == 0
    row = lambda w: pl.BlockSpec((tm, w), lambda i: (i, 0))
    full = lambda a: pl.BlockSpec(a.shape, lambda i: (0,) * a.ndim)
    return pl.pallas_call(
        _proj_out_kernel,
        grid=(n // tm,),
        in_specs=[row(d), row(oa.shape[1]), row(ob.shape[1]), row(gates.shape[1]),
                  full(woa), full(wob), full(wout)],
        out_specs=row(d),
        out_shape=jax.ShapeDtypeStruct((n, d), F32),
        compiler_params=pltpu.CompilerParams(
            dimension_semantics=("arbitrary",), vmem_limit_bytes=VMEM_LIMIT),
        name="proj_out",
    )(x2, oa, ob, gates, woa, wob, wout)


def _block_tables(t, slopes, log_g):
    f32 = np.float32
    i = np.arange(t, dtype=np.int32)
    allowed = (i[None, :] // CHUNK) <= (i[:, None] // CHUNK)
    dist = np.abs(i[:, None] - i[None, :]).astype(f32)
    jf = i.astype(f32)
    abias = -slopes[:, None, None] * dist[None] + slopes[:, None, None] * jf[None, :, None]
    abias = np.where(allowed[None], abias, f32(-np.inf)).astype(f32)
    dmat = np.where(allowed[None], np.exp(dist[None] * log_g[:, None, None]), f32(0.0)).astype(f32)
    xi = np.exp((jf + f32(1.0))[None] * log_g[:, None]).astype(f32)
    zeta = np.exp((f32(t) - f32(1.0) - jf)[None] * log_g[:, None]).astype(f32)
    bc = lambda a: np.ascontiguousarray(np.broadcast_to(a[:, :, None], a.shape + (HEAD_W,)))
    return abias, dmat, bc(xi), bc(zeta)


N_SLOPE_PARTS = 3


def _alibi_features(t, width, slopes):
    assert t <= 256
    bf16 = jnp.bfloat16
    kpos = np.zeros((t, HEAD_W), np.float32)
    kpos[:, :N_SLOPE_PARTS] = np.arange(t, dtype=np.float32)[:, None]
    kslope = np.zeros((len(slopes), HEAD_W, width), np.float32)
    rest = slopes.astype(np.float32)
    for part in range(N_SLOPE_PARTS):
        piece = rest.astype(bf16).astype(np.float32)
        kslope[:, part, :] = piece[:, None]
        rest = rest - piece
    return kpos.astype(bf16), kslope.astype(bf16)


def _key_bias_table(t, slopes):
    kb = (slopes[:, None] * np.arange(t, dtype=np.float32)[None]).astype(np.float32)
    return np.ascontiguousarray(np.broadcast_to(kb[:, :, None], kb.shape + (HEAD_W,)))


def _layer(x_prompt, x_sample, ck, cv, r0, norm_g, w_in, b_gate, qn_g, kn_g,
           lam_q1, lam_k1, lam_q2, lam_k2, subln_g, w_oa, w_ob, w_out, lam_init):
    bp, tp, d = x_prompt.shape
    bs, ts, _ = x_sample.shape
    past = ck.shape[1]
    slopes = (2.0 ** (-8.0 * np.arange(1, H_A + 1, dtype=np.float32) / H_A)).astype(np.float32)
    log_g = np.log(1.0 - 2.0 ** (-5.0 - np.arange(H_B, dtype=np.float32))).astype(np.float32)
    lam = (jnp.exp(jnp.sum(lam_q1 * lam_k1).astype(F32))
           - jnp.exp(jnp.sum(lam_q2 * lam_k2).astype(F32)) + lam_init)

    woa_bf, wob_bf, wout_bf = w_oa.astype(BF16), w_ob.astype(BF16), w_out.astype(BF16)
    g = norm_g.reshape(1, d)
    qg_t = jnp.tile(qn_g, 2 * H_A).reshape(1, H_A * HEAD_W)
    kg_t = jnp.tile(kn_g, 2 * H_A).reshape(1, H_A * HEAD_W)
    grp = np.arange(MXU_TILE, dtype=np.int32) // DK_A
    gm = jnp.asarray(grp[:, None] == grp[None, :], dtype=BF16)
    sg = jnp.stack([subln_g, jnp.full((DV_A,), 1.0 - lam_init, F32)])

    def scal(n_block, logit_scale):
        consts = np.concatenate([slopes * logit_scale, np.exp(np.float32(n_block) * log_g)]).astype(np.float32)
        return jnp.concatenate([lam.reshape(1).astype(F32), jnp.asarray(consts)])

    prompt_acts, sample_acts = _proj_in(
        x_prompt, x_sample.reshape(1, bs * ts, d), g, w_in, b_gate, qg_t, kg_t, gm,
        tm=ROW_TILE, tm_sample=min(SAMPLE_ROW_TILE, bs * ts))

    qat, kaf, kab, vaf, vat, mix, gates = prompt_acts
    abias, dmat, xi, zeta = _block_tables(KEY_BLOCK, slopes, log_g)
    kbias = _key_bias_table(KEY_BLOCK, slopes)
    dbias = np.ascontiguousarray(np.swapaxes(abias, 1, 2) - kbias[:, :, :1])
    flat = lambda a: a.reshape(-1, a.shape[-1])
    log2e = np.float32(LOG2E)
    kpos, kslope = _alibi_features(KEY_BLOCK, 2 * Q_TILE, slopes * log2e)
    y_p, ret_p = _mixer_prompt(scal(float(KEY_BLOCK), log2e), qat, kab, vat, mix,
                               dbias * log2e, kpos, kslope, dmat, xi, zeta, sg,
                               x_prompt, gates, woa_bf, wob_bf, wout_bf)
    k_p = jnp.transpose(kaf.reshape(bp, H_A, 2, DK_A, tp), (0, 4, 1, 2, 3))
    v_p = vaf.reshape(bp, tp, H_A, DV_A)

    qa, kaf, kab, vaf, vab, mix, gates = sample_acts
    r3 = lambda a: a.reshape(bs, ts, a.shape[-1])
    tabs = _block_tables(ts, slopes, log_g)
    ck_t = jnp.transpose(ck, (0, 2, 3, 4, 1)).reshape(bs, H_A * HEAD_W, past)
    oa, ob, ret_s = _mixer_sample(scal(float(ts), np.float32(1.0)), r3(qa), ck_t, cv.reshape(bs, past * H_A, HEAD_W),
                                  r3(kab), r3(vab), r3(mix), r0, *tabs, sg)
    y_s = _proj_out(flat(x_sample), flat(oa), flat(ob), flat(gates),
                    woa_bf, wob_bf, wout_bf).reshape(bs, ts, d)
    k_s = kaf.reshape(bs, ts, H_A, 2, DK_A)
    v_s = vaf.reshape(bs, ts, H_A, DV_A)
    return y_p, y_s, k_p, v_p, ret_p, k_s, v_s, ret_s


def kernel(x_prompt, x_sample, cache_k_diff, cache_v_diff, state_ret, norm_g, w_in, b_gate, qn_g, kn_g,
           lam_q1, lam_k1, lam_q2, lam_k2, subln_g, w_o_diff, w_o_ret, w_out):
    depth = w_in.shape[0]
    hp, hs = x_prompt, x_sample
    outs = [[] for _ in range(6)]
    for l in range(depth):
        lam_init = 0.8 - 0.6 * math.exp(-0.3 * l)
        hp, hs, k_p, v_p, ret_p, k_s, v_s, ret_s = _layer(
            hp, hs, cache_k_diff[l], cache_v_diff[l], state_ret[l], norm_g[l], w_in[l], b_gate[l],
            qn_g[l], kn_g[l], lam_q1[l], lam_k1[l], lam_q2[l], lam_k2[l], subln_g[l],
            w_o_diff[l], w_o_ret[l], w_out[l], lam_init)
        for lst, a in zip(outs, (k_p, v_p, ret_p, k_s, v_s, ret_s)):
            lst.append(a)
    return (hp, hs) + tuple(jnp.stack(o) for o in outs)
```

```python
import functools
import math

import jax
import jax.numpy as jnp
import numpy as np
from jax import lax
from jax.experimental import pallas as pl
from jax.experimental.pallas import tpu as pltpu

F32 = jnp.float32
BF16 = jnp.bfloat16

CHUNK = 64
H_A = 4
DK_A = 64
DV_A = 2 * DK_A
H_B = 4
DK_B = 128
DV_B = 128
HEAD_W = 128
EPS = 1e-6
LOG2E = math.log2(math.e)

MXU_TILE = 256
ROW_TILE = 512
SAMPLE_ROW_TILE = 256
Q_TILE = 256
KEY_BLOCK = 256
SEQS_PER_STEP = 2
SUM_ROWS = 16
SAMPLE_KEY_TILE = 4096
SAMPLE_KEY_BLOCK = 1024
SCORE_LOOKAHEAD = 8
V7X_VMEM_BYTES = 64 * 1024 * 1024
VMEM_LIMIT = V7X_VMEM_BYTES - 3 * 1024 * 1024


def _nt_dot(a, b):
    return lax.dot_general(a, b, (((1,), (1,)), ((), ())), preferred_element_type=F32)


def _tn_dot(a, b):
    return lax.dot_general(a, b, (((0,), (0,)), ((), ())), preferred_element_type=F32)


def _dot(a, b):
    return jnp.dot(a, b, preferred_element_type=F32)


def _sigmoid(x):
    return 1.0 / (1.0 + jnp.exp2(x * (-LOG2E)))


MIX_SLABS = ("za", "qb", "kb", "vb", "zb")


def _slab(mix_ref, name, head=None, seq=0):
    lo = MIX_SLABS.index(name) * H_A * HEAD_W
    if head is None:
        return mix_ref.at[seq, :, lo:lo + H_A * HEAD_W]
    return mix_ref.at[seq, :, lo + head * HEAD_W:lo + (head + 1) * HEAD_W]


def _proj_in_kernel(x_ref, g_ref, w_ref, bg_ref, qg_ref, kg_ref, gm_ref,
                    qa_ref, kaf_ref, kab_ref, vaf_ref, vab_ref, mix_ref, gate_ref,
                    *, wa, wb, d_model, feature_major, q_scale):
    x = x_ref[0]
    tm = x.shape[0]
    ms = jnp.mean(x * x, axis=-1, keepdims=True)
    xn = (x * lax.rsqrt(ms + EPS)) * g_ref[...]

    names = ("qa", "ka", "va", "za", "qb", "kb", "vb", "zb", "ga", "gb")
    widths = (wa, wa, wa, wa, wb, wb, wb, wb, d_model, d_model)
    offs = dict(zip(names, (sum(widths[:i]) for i in range(len(widths)))))
    width = dict(zip(names, widths))

    def seg(name):
        return _dot(xn, w_ref[:, offs[name]:offs[name] + width[name]])

    def group_norm(h, gain):
        sq = (h * h).astype(BF16)
        ms = jnp.concatenate([_dot(sq[:, c:c + MXU_TILE], gm_ref[...]) for c in range(0, wa, MXU_TILE)], axis=1)
        return (h * lax.rsqrt(ms + EPS)) * gain

    def silu(h):
        return h * _sigmoid(h)

    maybe_t = (lambda a: a.T) if feature_major else (lambda a: a)
    gate_ref[0, :, :d_model] = _sigmoid(seg("ga") + bg_ref[0:1, :]).astype(BF16)
    gate_ref[0, :, d_model:] = _sigmoid(seg("gb") + bg_ref[1:2, :]).astype(BF16)
    qa_ref[0] = maybe_t(group_norm(seg("qa"), qg_ref[...]) * q_scale).astype(BF16)
    ka = group_norm(seg("ka"), kg_ref[...])
    kaf_ref[0] = maybe_t(ka)
    kab_ref[0] = ka.astype(BF16)
    va = seg("va")
    vab_ref[0] = maybe_t(va).astype(BF16)
    for h in range(H_A):
        vaf_ref[pl.ds(h, tm, stride=H_A), :] = va[:, h * HEAD_W:(h + 1) * HEAD_W]
    _slab(mix_ref, "za")[...] = silu(seg("za")).astype(BF16)
    _slab(mix_ref, "zb")[...] = silu(seg("zb")).astype(BF16)
    _slab(mix_ref, "kb")[...] = (seg("kb") * (DK_B ** -0.5)).astype(BF16)
    _slab(mix_ref, "qb")[...] = seg("qb").astype(BF16)
    _slab(mix_ref, "vb")[...] = seg("vb").astype(BF16)


N_PROJ_OUTS = 7


def _proj_in_both_kernel(xp_ref, xs_ref, g_ref, w_ref, bg_ref, qg_ref, kg_ref, gm_ref, *out_refs,
                         n_prompt_steps, wa, wb, d_model):
    step = pl.program_id(0)
    consts = (g_ref, w_ref, bg_ref, qg_ref, kg_ref, gm_ref)

    @pl.when(step < n_prompt_steps)
    def _():
        _proj_in_kernel(xp_ref, *consts, *out_refs[:N_PROJ_OUTS],
                        wa=wa, wb=wb, d_model=d_model, feature_major=True,
                        q_scale=DK_A ** -0.5 * LOG2E)

    @pl.when(step >= n_prompt_steps)
    def _():
        _proj_in_kernel(xs_ref, *consts, *out_refs[N_PROJ_OUTS:],
                        wa=wa, wb=wb, d_model=d_model, feature_major=False, q_scale=DK_A ** -0.5)


def _proj_in(x_prompt, x_sample, g, w_in, bg, qg_t, kg_t, gm, *, tm, tm_sample):
    b, t, d = x_prompt.shape
    n = x_sample.shape[1]
    wa = H_A * HEAD_W
    wb = H_B * HEAD_W
    assert t % tm == 0 and n % tm_sample == 0 and wa == wb and x_sample.shape[0] == 1
    nt = t // tm
    n_p, n_s = b * nt, n // tm_sample
    p_tile = lambda s: jnp.minimum(s, n_p - 1)
    s_tile = lambda s: jnp.clip(s - n_p, 0, n_s - 1)
    full = lambda a: pl.BlockSpec(a.shape, lambda s: (0,) * a.ndim)

    def group_specs(bsz, length, rows, tile, feature_major):
        nt_g = length // rows
        row = lambda w: pl.BlockSpec((1, rows, w), lambda s: (tile(s) // nt_g, tile(s) % nt_g, 0))
        act = lambda w, dt: jax.ShapeDtypeStruct((bsz, length, w), dt)
        if feature_major:
            fm_shape = lambda dt: jax.ShapeDtypeStruct((bsz, wa, length), dt)
            fm_spec = pl.BlockSpec((1, wa, rows), lambda s: (tile(s) // nt_g, 0, tile(s) % nt_g))
        else:
            fm_shape, fm_spec = (lambda dt: act(wa, dt)), row(wa)
        shapes = [fm_shape(BF16),
                  fm_shape(F32), act(wa, BF16),
                  jax.ShapeDtypeStruct((bsz * length * H_A, HEAD_W), F32),
                  fm_shape(BF16),
                  act(len(MIX_SLABS) * wa, BF16),
                  act(2 * d, BF16)]
        specs = [fm_spec, fm_spec, row(wa),
                 pl.BlockSpec((rows * H_A, HEAD_W), lambda s: (tile(s), 0)),
                 fm_spec, row(len(MIX_SLABS) * wa), row(2 * d)]
        return row(d), shapes, specs

    xp_spec, p_shapes, p_specs = group_specs(b, t, tm, p_tile, True)
    xs_spec, s_shapes, s_specs = group_specs(1, n, tm_sample, s_tile, False)
    assert len(p_shapes) == N_PROJ_OUTS
    outs = pl.pallas_call(
        functools.partial(_proj_in_both_kernel, n_prompt_steps=n_p, wa=wa, wb=wb, d_model=d),
        grid=(n_p + n_s,),
        in_specs=[xp_spec, xs_spec, full(g), full(w_in), full(bg), full(qg_t), full(kg_t), full(gm)],
        out_specs=p_specs + s_specs,
        out_shape=p_shapes + s_shapes,
        compiler_params=pltpu.CompilerParams(
            dimension_semantics=("arbitrary",), vmem_limit_bytes=VMEM_LIMIT),
        name="proj_in",
    )(x_prompt, x_sample, g, w_in, bg, qg_t, kg_t, gm)
    return outs[:N_PROJ_OUTS], outs[N_PROJ_OUTS:]


def _split_halves(q):
    lane = lax.broadcasted_iota(jnp.int32, q.shape, 1)
    zero = jnp.zeros_like(q)
    return jnp.concatenate([jnp.where(lane < DK_A, q, zero), jnp.where(lane >= DK_A, q, zero)], axis=0)


def _softmax_update(carry, s, v):
    m, acc = carry
    m_new = jnp.maximum(m, jnp.max(s, axis=-1, keepdims=True))
    p = jnp.exp(s - m_new).astype(BF16)
    v1 = jnp.concatenate([v, jnp.ones_like(v)], axis=1)
    return m_new, jnp.exp(m - m_new) * acc + _dot(p, v1)


def _diff_combine(acc, lam, t):
    dv = acc.shape[1] // 2
    o = acc[:, :dv] / acc[:, dv:]
    return o[:t] - lam * o[t:]


def _lane_rms(o):
    return o * lax.rsqrt(jnp.mean(o * o, axis=-1, keepdims=True) + EPS)


def _retention_stage1(q, k, v, r, dmat, xi, zeta, cdecay):
    s = (_nt_dot(q, k) * dmat).astype(BF16)
    cross = _dot(q, r.astype(BF16)) * xi
    kz = (k.astype(F32) * zeta).astype(BF16)
    r_new = cdecay * r + _tn_dot(kz, v)
    return s, cross, r_new


def _retention_stage2(s, cross, v):
    return _dot(s, v) + cross


def _mixer_prompt_kernel(scal_ref, qt_ref, k_ref, vt_ref, mix_ref,
                         dbias_ref, kpos_ref, kslope_ref, dmat_ref, xi_ref, zeta_ref, sg_ref,
                         x_ref, gate_ref, woa_ref, wob_ref, wout_ref,
                         y_ref, rout_ref, r_scr, acc_scr, oa_scr, ob_scr, merged_scr, *s_scrs,
                         tq, kblk, nb, nq, n_tiles):
    chains = [(bi, j) for bi in range(nb) for j in range(H_A)]
    hps = len(chains)
    nsub = tq // kblk
    t = pl.program_id(0)
    live = t < n_tiles
    qi = lax.rem(jnp.minimum(t, n_tiles - 1), nq)
    lam = scal_ref[0]
    lanes = lambda j: slice(j * HEAD_W, (j + 1) * HEAD_W)

    def split_halves_t(qt):
        feat = lax.broadcasted_iota(jnp.int32, qt.shape, 0)
        zero = jnp.zeros_like(qt)
        return jnp.concatenate([jnp.where(feat < DK_A, qt, zero), jnp.where(feat >= DK_A, qt, zero)], axis=1)

    qst = [jnp.concatenate([split_halves_t(qt_ref[bi, lanes(j), :]), kslope_ref[j]], axis=0)
           for bi, j in chains]

    head = lambda c: chains[c][1]

    def keys(c, kb):
        bi, j = chains[c]
        k = k_ref[bi, pl.ds(pl.multiple_of(kb * kblk, kblk), kblk), lanes(j)]
        return jnp.concatenate([k, kpos_ref[...]], axis=1)

    def values1(c, kb):
        bi, j = chains[c]
        vt = vt_ref[bi, lanes(j), pl.ds(pl.multiple_of(kb * kblk, kblk), kblk)]
        return jnp.concatenate([vt, jnp.ones((SUM_ROWS, kblk), BF16)], axis=0)

    def stage_scores(c, kb, slot):
        s = _dot(keys(c, kb), qst[c])
        s_scrs[c][slot] = s
        return jnp.max(s, axis=0, keepdims=True)

    def softmax_step(m, s, s_max, shift):
        m_new = jnp.maximum(m, s_max + shift)
        return m_new, jnp.exp2(m - m_new), jnp.exp2(s - (m_new - shift)).astype(BF16)

    def body(kb, carry):
        slot = lax.rem(kb, 2)
        new = []
        for c in range(hps):
            smax_cur, m = carry[c]
            s_cur = s_scrs[c][slot]
            smax_next = stage_scores(c, kb + 1, 1 - slot)
            shift = ((kb - nsub * qi) * kblk).astype(F32) * scal_ref[1 + head(c)]
            m, alpha, p = softmax_step(m, s_cur, smax_cur, shift)
            acc_scr[c] = alpha * acc_scr[c] + _dot(values1(c, kb), p)
            new.append((smax_next, m))
        return tuple(new)

    @pl.when(t == 0)
    def _():
        oa_scr[...] = jnp.zeros_like(oa_scr)
        ob_scr[...] = jnp.zeros_like(ob_scr)

    @pl.when(qi == 0)
    def _():
        r_scr[...] = jnp.zeros_like(r_scr)

    acc_scr[...] = jnp.zeros_like(acc_scr)
    init = tuple((stage_scores(c, 0, 0), jnp.full((1, 2 * tq), -jnp.inf, F32)) for c in range(hps))
    d_model = x_ref.shape[2]
    pa = _dot(oa_scr[...], woa_ref[...]).astype(BF16)
    pb = _dot(ob_scr[...], wob_ref[...]).astype(BF16)
    for bi in range(nb):
        rows = slice(bi * tq, (bi + 1) * tq)
        merged_scr[rows] = gate_ref[bi, :, :d_model] * pa[rows] + gate_ref[bi, :, d_model:] * pb[rows]
    n_before = nsub * qi
    carry = lax.fori_loop(0, jnp.where(live, n_before, 0), body, init)

    def retention_stage1(sub, state):
        rows = slice(sub * kblk, (sub + 1) * kblk)
        return [_retention_stage1(
            _slab(mix_ref, "qb", j, bi)[rows], _slab(mix_ref, "kb", j, bi)[rows], _slab(mix_ref, "vb", j, bi)[rows],
            state[c], dmat_ref[j], xi_ref[j], zeta_ref[j], scal_ref[1 + H_A + j])
            for c, (bi, j) in enumerate(chains)]

    def retention_stage2(sub, ret):
        rows = slice(sub * kblk, (sub + 1) * kblk)
        for c, (bi, j) in enumerate(chains):
            ob = _retention_stage2(ret[c][0], ret[c][1], _slab(mix_ref, "vb", j, bi)[rows])
            ob = (_lane_rms(ob) * _slab(mix_ref, "zb", j, bi)[rows].astype(F32)).astype(BF16)
            ob_scr[bi * tq + sub * kblk:bi * tq + (sub + 1) * kblk, lanes(j)] = ob

    def cat(pieces):
        pieces = [p for p in pieces if p.shape[1] > 0]
        return pieces[0] if len(pieces) == 1 else jnp.concatenate(pieces, axis=1)

    def diagonal_block(dsub, c, m):
        kb = n_before + dsub
        w = tq - dsub * kblk
        cols = (slice(dsub * kblk, tq), slice(tq + dsub * kblk, 2 * tq))
        if dsub == 0:
            s = s_scrs[c][lax.rem(n_before, 2)]
        else:
            s = _dot(keys(c, kb), cat([qst[c][:, q] for q in cols]))
        dbias = dbias_ref[head(c)]
        s = cat([s[:, :kblk] + dbias, s[:, kblk:w], s[:, w:w + kblk] + dbias, s[:, w + kblk:]])
        shift = (dsub * kblk) * scal_ref[1 + head(c)]
        m_part, alpha, p = softmax_step(cat([m[:, q] for q in cols]), s,
                                        jnp.max(s, axis=0, keepdims=True), shift)
        pv = _dot(values1(c, kb), p)
        for h, q in enumerate(cols):
            acc_scr[c, :, q] = alpha[:, h * w:(h + 1) * w] * acc_scr[c, :, q] + pv[:, h * w:(h + 1) * w]
        return cat([m[:, :dsub * kblk], m_part[:, :w], m[:, tq:tq + dsub * kblk], m_part[:, w:]])

    def final_projection(cols):
        proj = _dot(merged_scr[...], wout_ref[:, cols])
        for bi in range(nb):
            y_ref[bi, :, cols] = x_ref[bi, :, cols] + proj[bi * tq:(bi + 1) * tq]

    out_cols = [slice(c, c + MXU_TILE) for c in range(0, d_model, MXU_TILE)]
    every = max(1, (hps * nsub) // len(out_cols))
    state = [r_scr[c] for c in range(hps)]
    ms = [carry[c][1] for c in range(hps)]
    for sub in range(nsub):
        ret = retention_stage1(sub, state)
        state = [r[2] for r in ret]
        for c in range(hps):
            if (sub * hps + c) % every == 0 and out_cols:
                final_projection(out_cols.pop(0))
            ms[c] = diagonal_block(sub, c, ms[c])
        retention_stage2(sub, ret)
    while out_cols:
        final_projection(out_cols.pop(0))
    for c, (bi, j) in enumerate(chains):
        r_new = jnp.where(live, state[c], r_scr[c])
        r_scr[c] = r_new
        rout_ref[bi, j] = r_new
    for c, (bi, j) in enumerate(chains):
        inv_l = 1.0 / acc_scr[c, HEAD_W:HEAD_W + 1]
        inv_l = jnp.concatenate([inv_l[:, :tq], lam * inv_l[:, tq:]], axis=1)
        ot = acc_scr[c, :HEAD_W] * inv_l
        oat = ot[:, :tq] - ot[:, tq:]
        oat = oat * lax.rsqrt(jnp.mean(oat * oat, axis=0, keepdims=True) + EPS)
        oa = (oat.T * sg_ref[0:1, :]) * sg_ref[1:2, :]
        oa = (oa * _slab(mix_ref, "za", j, bi)[...].astype(F32)).astype(BF16)
        oa_scr[bi * tq:(bi + 1) * tq, lanes(j)] = oa


def _mixer_prompt(scal, qat, kab, vat, mix, dbias, kpos, kslope, dmat, xi, zeta, sg, x, gates, woa, wob, wout):
    b, t, w = kab.shape
    d = x.shape[2]
    tq, kblk, nb = Q_TILE, KEY_BLOCK, SEQS_PER_STEP
    assert t % tq == 0 and tq % kblk == 0 and kblk % CHUNK == 0 and H_A == H_B and w == H_A * HEAD_W
    assert dbias.shape[1:] == (kblk, kblk) and d % (tq // kblk) == 0 and b % nb == 0
    nq = t // tq
    n_tiles = (b // nb) * nq
    cur = lambda s: jnp.minimum(s, n_tiles - 1)
    prev = lambda s: jnp.maximum(s - 1, 0)
    tile = pl.BlockSpec((nb, tq, mix.shape[2]), lambda s: (cur(s) // nq, cur(s) % nq, 0))
    tile_t = pl.BlockSpec((nb, w, tq), lambda s: (cur(s) // nq, 0, cur(s) % nq))
    whole = pl.BlockSpec((nb, t, w), lambda s: (cur(s) // nq, 0, 0))
    whole_t = pl.BlockSpec((nb, w, t), lambda s: (cur(s) // nq, 0, 0))
    rows = lambda a: pl.BlockSpec((nb, tq, a.shape[2]), lambda s: (prev(s) // nq, prev(s) % nq, 0))
    full = lambda a: pl.BlockSpec(a.shape, lambda s: (0,) * a.ndim, pipeline_mode=pl.Buffered(1))
    return pl.pallas_call(
        functools.partial(_mixer_prompt_kernel, tq=tq, kblk=kblk, nb=nb, nq=nq, n_tiles=n_tiles),
        grid=(n_tiles + 1,),
        in_specs=[pl.BlockSpec(memory_space=pltpu.SMEM),
                  tile_t, whole, whole_t, tile,
                  full(dbias), full(kpos), full(kslope), full(dmat), full(xi), full(zeta), full(sg),
                  rows(x), rows(gates), full(woa), full(wob), full(wout)],
        out_specs=[rows(x),
                   pl.BlockSpec((nb, H_B, DK_B, DV_B), lambda s: (cur(s) // nq, 0, 0, 0))],
        out_shape=[jax.ShapeDtypeStruct((b, t, d), F32),
                   jax.ShapeDtypeStruct((b, H_B, DK_B, DV_B), F32)],
        scratch_shapes=[pltpu.VMEM((nb * H_B, DK_B, DV_B), F32),
                        pltpu.VMEM((nb * H_A, HEAD_W + SUM_ROWS, 2 * tq), F32),
                        pltpu.VMEM((nb * tq, w), BF16), pltpu.VMEM((nb * tq, w), BF16),
                        pltpu.VMEM((nb * tq, d), BF16)]
                       + [pltpu.VMEM((2, kblk, 2 * tq), F32)] * (nb * H_A),
        compiler_params=pltpu.CompilerParams(
            dimension_semantics=("arbitrary",), vmem_limit_bytes=VMEM_LIMIT),
        name="mixer_prompt",
    )(scal, qat, kab, vat, mix, dbias, kpos, kslope, dmat, xi, zeta, sg, x, gates, woa, wob, wout)


def _mixer_sample_kernel(scal_ref, qa_ref, kc_ref, vc_ref, kn_ref, vn_ref, mix_ref, r_ref,
                         abias_ref, dmat_ref, xi_ref, zeta_ref, sg_ref,
                         oa_ref, ob_ref, rout_ref, m_scr, acc_scr, *, n, past, tk, sub):
    kbi = pl.program_id(1)
    lam = scal_ref[0]
    lanes = lambda j: slice(j * HEAD_W, (j + 1) * HEAD_W)

    @pl.when(kbi == 0)
    def _():
        m_scr[...] = jnp.full(m_scr.shape, -jnp.inf, F32)
        acc_scr[...] = jnp.zeros_like(acc_scr)

    col = lax.broadcasted_iota(jnp.int32, (1, sub), 1)
    qs = [_split_halves(qa_ref[0, :, lanes(j)]) for j in range(H_A)]

    def scores(j, c):
        kk = kc_ref[0, lanes(j), c * sub:(c + 1) * sub].astype(BF16)
        kpos = (col + (kbi * tk + (c * sub - past))).astype(F32)
        return _dot(qs[j], kk) + kpos * scal_ref[1 + j]

    blocks = [(j, c) for c in range(tk // sub) for j in range(H_A)]
    carry = [(m_scr[j], acc_scr[j]) for j in range(H_A)]
    staged = [scores(*blk) for blk in blocks[:SCORE_LOOKAHEAD]]
    for i, (j, c) in enumerate(blocks):
        if i + SCORE_LOOKAHEAD < len(blocks):
            staged.append(scores(*blocks[i + SCORE_LOOKAHEAD]))
        vv = vc_ref[0, pl.ds(c * sub * H_A + j, sub, stride=H_A), :].astype(BF16)
        carry[j] = _softmax_update(carry[j], staged[i], vv)
    for j in range(H_A):
        m_scr[j], acc_scr[j] = carry[j]

    @pl.when(kbi == pl.num_programs(1) - 1)
    def _():
        s_new, ret = [], []
        for j in range(H_A):
            bias = abias_ref[j]
            s_new.append(_nt_dot(qs[j], kn_ref[0, :, lanes(j)]) + jnp.concatenate([bias, bias], axis=0))
        for j in range(H_A):
            s_ret, cross, r_new = _retention_stage1(
                _slab(mix_ref, "qb", j)[...], _slab(mix_ref, "kb", j)[...], _slab(mix_ref, "vb", j)[...],
                r_ref[0, j], dmat_ref[j], xi_ref[j], zeta_ref[j], scal_ref[1 + H_A + j])
            rout_ref[0, j] = r_new
            ret.append((s_ret, cross))
        for j in range(H_A):
            _, acc = _softmax_update((m_scr[j], acc_scr[j]), s_new[j], vn_ref[0, :, lanes(j)])
            oa = _diff_combine(acc, lam, n)
            oa = (_lane_rms(oa) * sg_ref[0:1, :]) * sg_ref[1:2, :]
            oa_ref[0, :, lanes(j)] = (oa * _slab(mix_ref, "za", j)[...].astype(F32)).astype(BF16)
        for j in range(H_A):
            ob = _retention_stage2(*ret[j], _slab(mix_ref, "vb", j)[...])
            ob_ref[0, :, lanes(j)] = (_lane_rms(ob) * _slab(mix_ref, "zb", j)[...].astype(F32)).astype(BF16)


def _mixer_sample(scal, qa, kc, vc, kab, vab, mix, r0, abias, dmat, xi, zeta, sg):
    b, n, w = qa.shape
    past = kc.shape[2]
    tk, sub = SAMPLE_KEY_TILE, SAMPLE_KEY_BLOCK
    assert past % tk == 0 and tk % sub == 0 and past % CHUNK == 0 and n <= CHUNK and H_A == H_B
    tile = pl.BlockSpec((1, n, w), lambda bi, ki: (bi, 0, 0))
    mix_tile = pl.BlockSpec((1, n, mix.shape[2]), lambda bi, ki: (bi, 0, 0))
    cache_k = pl.BlockSpec((1, w, tk), lambda bi, ki: (bi, 0, ki))
    cache_v = pl.BlockSpec((1, tk * H_A, HEAD_W), lambda bi, ki: (bi, ki, 0))
    state = pl.BlockSpec((1, H_B, DK_B, DV_B), lambda bi, ki: (bi, 0, 0, 0))
    full = lambda a: pl.BlockSpec(a.shape, lambda bi, ki: (0,) * a.ndim)
    return pl.pallas_call(
        functools.partial(_mixer_sample_kernel, n=n, past=past, tk=tk, sub=sub),
        grid=(b, past // tk),
        in_specs=[pl.BlockSpec(memory_space=pltpu.SMEM),
                  tile, cache_k, cache_v, tile, tile, mix_tile, state,
                  full(abias), full(dmat), full(xi), full(zeta), full(sg)],
        out_specs=[tile, tile, state],
        out_shape=[jax.ShapeDtypeStruct((b, n, H_A * HEAD_W), BF16),
                   jax.ShapeDtypeStruct((b, n, H_B * HEAD_W), BF16),
                   jax.ShapeDtypeStruct((b, H_B, DK_B, DV_B), F32)],
        scratch_shapes=[pltpu.VMEM((H_A, 2 * n, 1), F32),
                        pltpu.VMEM((H_A, 2 * n, 2 * DV_A), F32)],
        compiler_params=pltpu.CompilerParams(
            dimension_semantics=("arbitrary", "arbitrary"), vmem_limit_bytes=VMEM_LIMIT),
        name="mixer_sample",
    )(scal, qa, kc, vc, kab, vab, mix, r0, abias, dmat, xi, zeta, sg)


def _proj_out_kernel(x_ref, oa_ref, ob_ref, gate_ref, woa_ref, wob_ref, wout_ref, y_ref):
    d = x_ref.shape[1]
    m = (gate_ref[:, :d].astype(F32) * _dot(oa_ref[...], woa_ref[...])
         + gate_ref[:, d:].astype(F32) * _dot(ob_ref[...], wob_ref[...]))
    y_ref[...] = x_ref[...] + _dot(m.astype(BF16), wout_ref[...])


def _proj_out(x2, oa, ob, gates, woa, wob, wout):
    n, d = x2.shape
    tm = min(ROW_TILE, n)
    assert n % tm == 0
    row = lambda w: pl.BlockSpec((tm, w), lambda i: (i, 0))
    full = lambda a: pl.BlockSpec(a.shape, lambda i: (0,) * a.ndim)
    return pl.pallas_call(
        _proj_out_kernel,
        grid=(n // tm,),
        in_specs=[row(d), row(oa.shape[1]), row(ob.shape[1]), row(gates.shape[1]),
                  full(woa), full(wob), full(wout)],
        out_specs=row(d),
        out_shape=jax.ShapeDtypeStruct((n, d), F32),
        compiler_params=pltpu.CompilerParams(
            dimension_semantics=("arbitrary",), vmem_limit_bytes=VMEM_LIMIT),
        name="proj_out",
    )(x2, oa, ob, gates, woa, wob, wout)


def _block_tables(t, slopes, log_g):
    f32 = np.float32
    i = np.arange(t, dtype=np.int32)
    allowed = (i[None, :] // CHUNK) <= (i[:, None] // CHUNK)
    dist = np.abs(i[:, None] - i[None, :]).astype(f32)
    jf = i.astype(f32)
    abias = -slopes[:, None, None] * dist[None] + slopes[:, None, None] * jf[None, :, None]
    abias = np.where(allowed[None], abias, f32(-np.inf)).astype(f32)
    dmat = np.where(allowed[None], np.exp(dist[None] * log_g[:, None, None]), f32(0.0)).astype(f32)
    xi = np.exp((jf + f32(1.0))[None] * log_g[:, None]).astype(f32)
    zeta = np.exp((f32(t) - f32(1.0) - jf)[None] * log_g[:, None]).astype(f32)
    bc = lambda a: np.ascontiguousarray(np.broadcast_to(a[:, :, None], a.shape + (HEAD_W,)))
    return abias, dmat, bc(xi), bc(zeta)


N_SLOPE_PARTS = 3


def _alibi_features(t, width, slopes):
    assert t <= 256
    bf16 = jnp.bfloat16
    kpos = np.zeros((t, HEAD_W), np.float32)
    kpos[:, :N_SLOPE_PARTS] = np.arange(t, dtype=np.float32)[:, None]
    kslope = np.zeros((len(slopes), HEAD_W, width), np.float32)
    rest = slopes.astype(np.float32)
    for part in range(N_SLOPE_PARTS):
        piece = rest.astype(bf16).astype(np.float32)
        kslope[:, part, :] = piece[:, None]
        rest = rest - piece
    return kpos.astype(bf16), kslope.astype(bf16)


def _key_bias_table(t, slopes):
    kb = (slopes[:, None] * np.arange(t, dtype=np.float32)[None]).astype(np.float32)
    return np.ascontiguousarray(np.broadcast_to(kb[:, :, None], kb.shape + (HEAD_W,)))


def _layer(x_prompt, x_sample, ck, cv, r0, norm_g, w_in, b_gate, qn_g, kn_g,
           lam_q1, lam_k1, lam_q2, lam_k2, subln_g, w_oa, w_ob, w_out, lam_init):
    bp, tp, d = x_prompt.shape
    bs, ts, _ = x_sample.shape
    past = ck.shape[1]
    slopes = (2.0 ** (-8.0 * np.arange(1, H_A + 1, dtype=np.float32) / H_A)).astype(np.float32)
    log_g = np.log(1.0 - 2.0 ** (-5.0 - np.arange(H_B, dtype=np.float32))).astype(np.float32)
    lam = (jnp.exp(jnp.sum(lam_q1 * lam_k1).astype(F32))
           - jnp.exp(jnp.sum(lam_q2 * lam_k2).astype(F32)) + lam_init)

    woa_bf, wob_bf, wout_bf = w_oa.astype(BF16), w_ob.astype(BF16), w_out.astype(BF16)
    g = norm_g.reshape(1, d)
    qg_t = jnp.tile(qn_g, 2 * H_A).reshape(1, H_A * HEAD_W)
    kg_t = jnp.tile(kn_g, 2 * H_A).reshape(1, H_A * HEAD_W)
    grp = np.arange(MXU_TILE, dtype=np.int32) // DK_A
    gm = jnp.asarray((grp[:, None] == grp[None, :]) * (1.0 / DK_A), dtype=BF16)
    sg = jnp.stack([subln_g, jnp.full((DV_A,), 1.0 - lam_init, F32)])

    def scal(n_block, logit_scale):
        consts = np.concatenate([slopes * logit_scale, np.exp(np.float32(n_block) * log_g)]).astype(np.float32)
        return jnp.concatenate([lam.reshape(1).astype(F32), jnp.asarray(consts)])

    prompt_acts, sample_acts = _proj_in(
        x_prompt, x_sample.reshape(1, bs * ts, d), g, w_in, b_gate, qg_t, kg_t, gm,
        tm=ROW_TILE, tm_sample=min(SAMPLE_ROW_TILE, bs * ts))

    qat, kaf, kab, vaf, vat, mix, gates = prompt_acts
    abias, dmat, xi, zeta = _block_tables(KEY_BLOCK, slopes, log_g)
    kbias = _key_bias_table(KEY_BLOCK, slopes)
    dbias = np.ascontiguousarray(np.swapaxes(abias, 1, 2) - kbias[:, :, :1])
    flat = lambda a: a.reshape(-1, a.shape[-1])
    log2e = np.float32(LOG2E)
    kpos, kslope = _alibi_features(KEY_BLOCK, 2 * Q_TILE, slopes * log2e)
    y_p, ret_p = _mixer_prompt(scal(float(KEY_BLOCK), log2e), qat, kab, vat, mix,
                               dbias * log2e, kpos, kslope, dmat, xi, zeta, sg,
                               x_prompt, gates, woa_bf, wob_bf, wout_bf)
    k_p = jnp.transpose(kaf.reshape(bp, H_A, 2, DK_A, tp), (0, 4, 1, 2, 3))
    v_p = vaf.reshape(bp, tp, H_A, DV_A)

    qa, kaf, kab, vaf, vab, mix, gates = sample_acts
    r3 = lambda a: a.reshape(bs, ts, a.shape[-1])
    tabs = _block_tables(ts, slopes, log_g)
    ck_t = jnp.transpose(ck, (0, 2, 3, 4, 1)).reshape(bs, H_A * HEAD_W, past)
    oa, ob, ret_s = _mixer_sample(scal(float(ts), np.float32(1.0)), r3(qa), ck_t, cv.reshape(bs, past * H_A, HEAD_W),
                                  r3(kab), r3(vab), r3(mix), r0, *tabs, sg)
    y_s = _proj_out(flat(x_sample), flat(oa), flat(ob), flat(gates),
                    woa_bf, wob_bf, wout_bf).reshape(bs, ts, d)
    k_s = kaf.reshape(bs, ts, H_A, 2, DK_A)
    v_s = vaf.reshape(bs, ts, H_A, DV_A)
    return y_p, y_s, k_p, v_p, ret_p, k_s, v_s, ret_s


def kernel(x_prompt, x_sample, cache_k_diff, cache_v_diff, state_ret, norm_g, w_in, b_gate, qn_g, kn_g,
           lam_q1, lam_k1, lam_q2, lam_k2, subln_g, w_o_diff, w_o_ret, w_out):
    depth = w_in.shape[0]
    hp, hs = x_prompt, x_sample
    outs = [[] for _ in range(6)]
    for l in range(depth):
        lam_init = 0.8 - 0.6 * math.exp(-0.3 * l)
        hp, hs, k_p, v_p, ret_p, k_s, v_s, ret_s = _layer(
            hp, hs, cache_k_diff[l], cache_v_diff[l], state_ret[l], norm_g[l], w_in[l], b_gate[l],
            qn_g[l], kn_g[l], lam_q1[l], lam_k1[l], lam_q2[l], lam_k2[l], subln_g[l],
            w_o_diff[l], w_o_ret[l], w_out[l], lam_init)
        for lst, a in zip(outs, (k_p, v_p, ret_p, k_s, v_s, ret_s)):
            lst.append(a)
    return (hp, hs) + tuple(jnp.stack(o) for o in outs)
```

```python
import functools
import math

import jax
import jax.numpy as jnp
import numpy as np
from jax import lax
from jax.experimental import pallas as pl
from jax.experimental.pallas import tpu as pltpu

F32 = jnp.float32
BF16 = jnp.bfloat16

CHUNK = 64
H_A = 4
DK_A = 64
DV_A = 2 * DK_A
H_B = 4
DK_B = 128
DV_B = 128
HEAD_W = 128
EPS = 1e-6
LOG2E = math.log2(math.e)

MXU_TILE = 256
ROW_TILE = 512
SAMPLE_ROW_TILE = 256
Q_TILE = 256
KEY_BLOCK = 256
SEQS_PER_STEP = 2
SUM_ROWS = 16
SAMPLE_KEY_TILE = 4096
SAMPLE_KEY_BLOCK = 1024
SCORE_LOOKAHEAD = 8
V7X_VMEM_BYTES = 64 * 1024 * 1024
VMEM_LIMIT = V7X_VMEM_BYTES - 3 * 1024 * 1024


def _nt_dot(a, b):
    return lax.dot_general(a, b, (((1,), (1,)), ((), ())), preferred_element_type=F32)


def _tn_dot(a, b):
    return lax.dot_general(a, b, (((0,), (0,)), ((), ())), preferred_element_type=F32)


def _dot(a, b):
    return jnp.dot(a, b, preferred_element_type=F32)


def _sigmoid(x):
    return 1.0 / (1.0 + jnp.exp2(x * (-LOG2E)))


MIX_SLABS = ("za", "qb", "kb", "vb", "zb")


def _slab(mix_ref, name, head=None, seq=0):
    lo = MIX_SLABS.index(name) * H_A * HEAD_W
    if head is None:
        return mix_ref.at[seq, :, lo:lo + H_A * HEAD_W]
    return mix_ref.at[seq, :, lo + head * HEAD_W:lo + (head + 1) * HEAD_W]


W_IN_SEGMENTS = ("qa", "ka", "va", "za", "qb", "kb", "vb", "zb", "ga", "gb")
W_IN_USE_ORDER = ("ga", "gb", "qa", "ka", "va", "za", "zb", "kb", "qb", "vb")


def _w_in_columns(name, wa, wb, d_model):
    widths = (wa, wa, wa, wa, wb, wb, wb, wb, d_model, d_model)
    i = W_IN_SEGMENTS.index(name)
    return slice(sum(widths[:i]), sum(widths[:i + 1]))


def _proj_in_kernel(x_ref, g_ref, w_ref, bg_ref, qg_ref, kg_ref, gm_ref,
                    qa_ref, kaf_ref, kab_ref, vaf_ref, vab_ref, mix_ref, gate_ref,
                    *, wa, wb, d_model, feature_major, q_scale, before_segment=None):
    x = x_ref[0]
    tm = x.shape[0]
    ms = jnp.mean(x * x, axis=-1, keepdims=True)
    xn = (x * lax.rsqrt(ms + EPS)) * g_ref[...]
    used = []

    def seg(name):
        used.append(name)
        if before_segment is not None:
            before_segment(name)
        return _dot(xn, w_ref[:, _w_in_columns(name, wa, wb, d_model)])

    def group_norm(h, gain):
        sq = (h * h).astype(BF16)
        ms = jnp.concatenate([_dot(sq[:, c:c + MXU_TILE], gm_ref[...]) for c in range(0, wa, MXU_TILE)], axis=1)
        return (h * lax.rsqrt(ms + EPS)) * gain

    def silu(h):
        return h * _sigmoid(h)

    maybe_t = (lambda a: a.T) if feature_major else (lambda a: a)
    gate_ref[0, :, :d_model] = _sigmoid(seg("ga") + bg_ref[0:1, :]).astype(BF16)
    gate_ref[0, :, d_model:] = _sigmoid(seg("gb") + bg_ref[1:2, :]).astype(BF16)
    qa_ref[0] = maybe_t(group_norm(seg("qa"), qg_ref[...]) * q_scale).astype(BF16)
    ka = group_norm(seg("ka"), kg_ref[...])
    kaf_ref[0] = maybe_t(ka)
    kab_ref[0] = ka.astype(BF16)
    va = seg("va")
    vab_ref[0] = maybe_t(va).astype(BF16)
    for h in range(H_A):
        vaf_ref[pl.ds(h, tm, stride=H_A), :] = va[:, h * HEAD_W:(h + 1) * HEAD_W]
    _slab(mix_ref, "za")[...] = silu(seg("za")).astype(BF16)
    _slab(mix_ref, "zb")[...] = silu(seg("zb")).astype(BF16)
    _slab(mix_ref, "kb")[...] = (seg("kb") * (DK_B ** -0.5)).astype(BF16)
    _slab(mix_ref, "qb")[...] = seg("qb").astype(BF16)
    _slab(mix_ref, "vb")[...] = seg("vb").astype(BF16)
    assert tuple(used) == W_IN_USE_ORDER


N_PROJ_OUTS = 7


def _proj_in_both_kernel(xp_ref, xs_ref, g_ref, w_hbm_ref, bg_ref, qg_ref, kg_ref, gm_ref, *refs,
                         n_prompt_steps, wa, wb, d_model):
    out_refs, (w_ref, w_sem) = refs[:-2], refs[-2:]
    step = pl.program_id(0)
    consts = (g_ref, w_ref, bg_ref, qg_ref, kg_ref, gm_ref)
    prompt_body = functools.partial(
        _proj_in_kernel, xp_ref, *consts, *out_refs[:N_PROJ_OUTS], wa=wa, wb=wb, d_model=d_model,
        feature_major=True, q_scale=DK_A ** -0.5 * LOG2E)

    def segment_copy(name):
        cols = _w_in_columns(name, wa, wb, d_model)
        return pltpu.make_async_copy(w_hbm_ref.at[:, cols], w_ref.at[:, cols],
                                     w_sem.at[W_IN_USE_ORDER.index(name)])

    @pl.when(step == 0)
    def _():
        for name in W_IN_USE_ORDER:
            segment_copy(name).start()
        prompt_body(before_segment=lambda name: segment_copy(name).wait())

    @pl.when((step > 0) & (step < n_prompt_steps))
    def _():
        prompt_body()

    @pl.when(step >= n_prompt_steps)
    def _():
        _proj_in_kernel(xs_ref, *consts, *out_refs[N_PROJ_OUTS:],
                        wa=wa, wb=wb, d_model=d_model, feature_major=False, q_scale=DK_A ** -0.5)


def _proj_in(x_prompt, x_sample, g, w_in, bg, qg_t, kg_t, gm, *, tm, tm_sample):
    b, t, d = x_prompt.shape
    n = x_sample.shape[1]
    wa = H_A * HEAD_W
    wb = H_B * HEAD_W
    assert t % tm == 0 and n % tm_sample == 0 and wa == wb and x_sample.shape[0] == 1
    nt = t // tm
    n_p, n_s = b * nt, n // tm_sample
    p_tile = lambda s: jnp.minimum(s, n_p - 1)
    s_tile = lambda s: jnp.clip(s - n_p, 0, n_s - 1)
    full = lambda a: pl.BlockSpec(a.shape, lambda s: (0,) * a.ndim)

    def group_specs(bsz, length, rows, tile, feature_major):
        nt_g = length // rows
        row = lambda w: pl.BlockSpec((1, rows, w), lambda s: (tile(s) // nt_g, tile(s) % nt_g, 0))
        act = lambda w, dt: jax.ShapeDtypeStruct((bsz, length, w), dt)
        if feature_major:
            fm_shape = lambda dt: jax.ShapeDtypeStruct((bsz, wa, length), dt)
            fm_spec = pl.BlockSpec((1, wa, rows), lambda s: (tile(s) // nt_g, 0, tile(s) % nt_g))
        else:
            fm_shape, fm_spec = (lambda dt: act(wa, dt)), row(wa)
        shapes = [fm_shape(BF16),
                  fm_shape(F32), act(wa, BF16),
                  jax.ShapeDtypeStruct((bsz * length * H_A, HEAD_W), F32),
                  fm_shape(BF16),
                  act(len(MIX_SLABS) * wa, BF16),
                  act(2 * d, BF16)]
        specs = [fm_spec, fm_spec, row(wa),
                 pl.BlockSpec((rows * H_A, HEAD_W), lambda s: (tile(s), 0)),
                 fm_spec, row(len(MIX_SLABS) * wa), row(2 * d)]
        return row(d), shapes, specs

    xp_spec, p_shapes, p_specs = group_specs(b, t, tm, p_tile, True)
    xs_spec, s_shapes, s_specs = group_specs(1, n, tm_sample, s_tile, False)
    assert len(p_shapes) == N_PROJ_OUTS
    outs = pl.pallas_call(
        functools.partial(_proj_in_both_kernel, n_prompt_steps=n_p, wa=wa, wb=wb, d_model=d),
        grid=(n_p + n_s,),
        in_specs=[xp_spec, xs_spec, full(g), pl.BlockSpec(memory_space=pl.ANY),
                  full(bg), full(qg_t), full(kg_t), full(gm)],
        out_specs=p_specs + s_specs,
        out_shape=p_shapes + s_shapes,
        scratch_shapes=[pltpu.VMEM(w_in.shape, w_in.dtype),
                        pltpu.SemaphoreType.DMA((len(W_IN_USE_ORDER),))],
        compiler_params=pltpu.CompilerParams(
            dimension_semantics=("arbitrary",), vmem_limit_bytes=VMEM_LIMIT),
        name="proj_in",
    )(x_prompt, x_sample, g, w_in, bg, qg_t, kg_t, gm)
    return outs[:N_PROJ_OUTS], outs[N_PROJ_OUTS:]


def _split_halves(q):
    lane = lax.broadcasted_iota(jnp.int32, q.shape, 1)
    zero = jnp.zeros_like(q)
    return jnp.concatenate([jnp.where(lane < DK_A, q, zero), jnp.where(lane >= DK_A, q, zero)], axis=0)


def _softmax_update(carry, s, v):
    m, acc = carry
    m_new = jnp.maximum(m, jnp.max(s, axis=-1, keepdims=True))
    p = jnp.exp(s - m_new).astype(BF16)
    v1 = jnp.concatenate([v, jnp.ones_like(v)], axis=1)
    return m_new, jnp.exp(m - m_new) * acc + _dot(p, v1)


def _diff_combine(acc, lam, t):
    dv = acc.shape[1] // 2
    o = acc[:, :dv] / acc[:, dv:]
    return o[:t] - lam * o[t:]


def _lane_rms(o):
    return o * lax.rsqrt(jnp.mean(o * o, axis=-1, keepdims=True) + EPS)


def _retention_stage1(q, k, v, r, dmat, xi, zeta, cdecay):
    s = (_nt_dot(q, k) * dmat).astype(BF16)
    cross = _dot(q, r.astype(BF16)) * xi
    kz = (k.astype(F32) * zeta).astype(BF16)
    r_new = cdecay * r + _tn_dot(kz, v)
    return s, cross, r_new


def _retention_stage2(s, cross, v):
    return _dot(s, v) + cross


def _mixer_prompt_kernel(scal_ref, qt_ref, k_ref, vt_ref, mix_ref,
                         dbias_ref, kpos_ref, kslope_ref, dmat_ref, xi_ref, zeta_ref, sg_ref,
                         x_ref, gate_ref, woa_ref, wob_ref, wout_ref,
                         y_ref, rout_ref, r_scr, acc_scr, oa_scr, ob_scr, merged_scr, *s_scrs,
                         tq, kblk, nb, nq, n_tiles):
    chains = [(bi, j) for bi in range(nb) for j in range(H_A)]
    hps = len(chains)
    nsub = tq // kblk
    t = pl.program_id(0)
    live = t < n_tiles
    qi = lax.rem(jnp.minimum(t, n_tiles - 1), nq)
    lam = scal_ref[0]
    lanes = lambda j: slice(j * HEAD_W, (j + 1) * HEAD_W)

    def split_halves_t(qt):
        feat = lax.broadcasted_iota(jnp.int32, qt.shape, 0)
        zero = jnp.zeros_like(qt)
        return jnp.concatenate([jnp.where(feat < DK_A, qt, zero), jnp.where(feat >= DK_A, qt, zero)], axis=1)

    qst = [jnp.concatenate([split_halves_t(qt_ref[bi, lanes(j), :]), kslope_ref[j]], axis=0)
           for bi, j in chains]

    head = lambda c: chains[c][1]

    def keys(c, kb):
        bi, j = chains[c]
        k = k_ref[bi, pl.ds(pl.multiple_of(kb * kblk, kblk), kblk), lanes(j)]
        return jnp.concatenate([k, kpos_ref[...]], axis=1)

    def values1(c, kb):
        bi, j = chains[c]
        vt = vt_ref[bi, lanes(j), pl.ds(pl.multiple_of(kb * kblk, kblk), kblk)]
        return jnp.concatenate([vt, jnp.ones((SUM_ROWS, kblk), BF16)], axis=0)

    def stage_scores(c, kb, slot):
        s = _dot(keys(c, kb), qst[c])
        s_scrs[c][slot] = s
        return jnp.max(s, axis=0, keepdims=True)

    def softmax_step(m, s, s_max, shift):
        m_new = jnp.maximum(m, s_max + shift)
        return m_new, jnp.exp2(m - m_new), jnp.exp2(s - (m_new - shift)).astype(BF16)

    def body(kb, carry):
        slot = lax.rem(kb, 2)
        new = []
        for c in range(hps):
            smax_cur, m = carry[c]
            s_cur = s_scrs[c][slot]
            smax_next = stage_scores(c, kb + 1, 1 - slot)
            shift = ((kb - nsub * qi) * kblk).astype(F32) * scal_ref[1 + head(c)]
            m, alpha, p = softmax_step(m, s_cur, smax_cur, shift)
            acc_scr[c] = alpha * acc_scr[c] + _dot(values1(c, kb), p)
            new.append((smax_next, m))
        return tuple(new)

    @pl.when(t == 0)
    def _():
        oa_scr[...] = jnp.zeros_like(oa_scr)
        ob_scr[...] = jnp.zeros_like(ob_scr)

    @pl.when(qi == 0)
    def _():
        r_scr[...] = jnp.zeros_like(r_scr)

    acc_scr[...] = jnp.zeros_like(acc_scr)
    init = tuple((stage_scores(c, 0, 0), jnp.full((1, 2 * tq), -jnp.inf, F32)) for c in range(hps))
    d_model = x_ref.shape[2]
    pa = _dot(oa_scr[...], woa_ref[...]).astype(BF16)
    pb = _dot(ob_scr[...], wob_ref[...]).astype(BF16)
    for bi in range(nb):
        rows = slice(bi * tq, (bi + 1) * tq)
        merged_scr[rows] = gate_ref[bi, :, :d_model] * pa[rows] + gate_ref[bi, :, d_model:] * pb[rows]
    n_before = nsub * qi
    carry = lax.fori_loop(0, jnp.where(live, n_before, 0), body, init)

    def retention_stage1(sub, state):
        rows = slice(sub * kblk, (sub + 1) * kblk)
        return [_retention_stage1(
            _slab(mix_ref, "qb", j, bi)[rows], _slab(mix_ref, "kb", j, bi)[rows], _slab(mix_ref, "vb", j, bi)[rows],
            state[c], dmat_ref[j], xi_ref[j], zeta_ref[j], scal_ref[1 + H_A + j])
            for c, (bi, j) in enumerate(chains)]

    def retention_stage2(sub, ret):
        rows = slice(sub * kblk, (sub + 1) * kblk)
        for c, (bi, j) in enumerate(chains):
            ob = _retention_stage2(ret[c][0], ret[c][1], _slab(mix_ref, "vb", j, bi)[rows])
            ob = (_lane_rms(ob) * _slab(mix_ref, "zb", j, bi)[rows].astype(F32)).astype(BF16)
            ob_scr[bi * tq + sub * kblk:bi * tq + (sub + 1) * kblk, lanes(j)] = ob

    def cat(pieces):
        pieces = [p for p in pieces if p.shape[1] > 0]
        return pieces[0] if len(pieces) == 1 else jnp.concatenate(pieces, axis=1)

    def diagonal_block(dsub, c, m):
        kb = n_before + dsub
        w = tq - dsub * kblk
        cols = (slice(dsub * kblk, tq), slice(tq + dsub * kblk, 2 * tq))
        if dsub == 0:
            s = s_scrs[c][lax.rem(n_before, 2)]
        else:
            s = _dot(keys(c, kb), cat([qst[c][:, q] for q in cols]))
        dbias = dbias_ref[head(c)]
        s = cat([s[:, :kblk] + dbias, s[:, kblk:w], s[:, w:w + kblk] + dbias, s[:, w + kblk:]])
        shift = (dsub * kblk) * scal_ref[1 + head(c)]
        m_part, alpha, p = softmax_step(cat([m[:, q] for q in cols]), s,
                                        jnp.max(s, axis=0, keepdims=True), shift)
        pv = _dot(values1(c, kb), p)
        for h, q in enumerate(cols):
            acc_scr[c, :, q] = alpha[:, h * w:(h + 1) * w] * acc_scr[c, :, q] + pv[:, h * w:(h + 1) * w]
        return cat([m[:, :dsub * kblk], m_part[:, :w], m[:, tq:tq + dsub * kblk], m_part[:, w:]])

    def final_projection(cols):
        proj = _dot(merged_scr[...], wout_ref[:, cols])
        for bi in range(nb):
            y_ref[bi, :, cols] = x_ref[bi, :, cols] + proj[bi * tq:(bi + 1) * tq]

    out_cols = [slice(c, c + MXU_TILE) for c in range(0, d_model, MXU_TILE)]
    every = max(1, (hps * nsub) // len(out_cols))
    state = [r_scr[c] for c in range(hps)]
    ms = [carry[c][1] for c in range(hps)]
    for sub in range(nsub):
        ret = retention_stage1(sub, state)
        state = [r[2] for r in ret]
        for c in range(hps):
            if (sub * hps + c) % every == 0 and out_cols:
                final_projection(out_cols.pop(0))
            ms[c] = diagonal_block(sub, c, ms[c])
        retention_stage2(sub, ret)
    while out_cols:
        final_projection(out_cols.pop(0))
    for c, (bi, j) in enumerate(chains):
        r_new = jnp.where(live, state[c], r_scr[c])
        r_scr[c] = r_new
        rout_ref[bi, j] = r_new
    for c, (bi, j) in enumerate(chains):
        inv_l = 1.0 / acc_scr[c, HEAD_W:HEAD_W + 1]
        inv_l = jnp.concatenate([inv_l[:, :tq], lam * inv_l[:, tq:]], axis=1)
        ot = acc_scr[c, :HEAD_W] * inv_l
        oat = ot[:, :tq] - ot[:, tq:]
        oat = oat * lax.rsqrt(jnp.mean(oat * oat, axis=0, keepdims=True) + EPS)
        oa = (oat.T * sg_ref[0:1, :]) * sg_ref[1:2, :]
        oa = (oa * _slab(mix_ref, "za", j, bi)[...].astype(F32)).astype(BF16)
        oa_scr[bi * tq:(bi + 1) * tq, lanes(j)] = oa


def _mixer_prompt(scal, qat, kab, vat, mix, dbias, kpos, kslope, dmat, xi, zeta, sg, x, gates, woa, wob, wout):
    b, t, w = kab.shape
    d = x.shape[2]
    tq, kblk, nb = Q_TILE, KEY_BLOCK, SEQS_PER_STEP
    assert t % tq == 0 and tq % kblk == 0 and kblk % CHUNK == 0 and H_A == H_B and w == H_A * HEAD_W
    assert dbias.shape[1:] == (kblk, kblk) and d % (tq // kblk) == 0 and b % nb == 0
    nq = t // tq
    n_tiles = (b // nb) * nq
    cur = lambda s: jnp.minimum(s, n_tiles - 1)
    prev = lambda s: jnp.maximum(s - 1, 0)
    tile = pl.BlockSpec((nb, tq, mix.shape[2]), lambda s: (cur(s) // nq, cur(s) % nq, 0))
    tile_t = pl.BlockSpec((nb, w, tq), lambda s: (cur(s) // nq, 0, cur(s) % nq))
    whole = pl.BlockSpec((nb, t, w), lambda s: (cur(s) // nq, 0, 0))
    whole_t = pl.BlockSpec((nb, w, t), lambda s: (cur(s) // nq, 0, 0))
    rows = lambda a: pl.BlockSpec((nb, tq, a.shape[2]), lambda s: (prev(s) // nq, prev(s) % nq, 0))
    full = lambda a: pl.BlockSpec(a.shape, lambda s: (0,) * a.ndim, pipeline_mode=pl.Buffered(1))
    return pl.pallas_call(
        functools.partial(_mixer_prompt_kernel, tq=tq, kblk=kblk, nb=nb, nq=nq, n_tiles=n_tiles),
        grid=(n_tiles + 1,),
        in_specs=[pl.BlockSpec(memory_space=pltpu.SMEM),
                  tile_t, whole, whole_t, tile,
                  full(dbias), full(kpos), full(kslope), full(dmat), full(xi), full(zeta), full(sg),
                  rows(x), rows(gates), full(woa), full(wob), full(wout)],
        out_specs=[rows(x),
                   pl.BlockSpec((nb, H_B, DK_B, DV_B), lambda s: (cur(s) // nq, 0, 0, 0))],
        out_shape=[jax.ShapeDtypeStruct((b, t, d), F32),
                   jax.ShapeDtypeStruct((b, H_B, DK_B, DV_B), F32)],
        scratch_shapes=[pltpu.VMEM((nb * H_B, DK_B, DV_B), F32),
                        pltpu.VMEM((nb * H_A, HEAD_W + SUM_ROWS, 2 * tq), F32),
                        pltpu.VMEM((nb * tq, w), BF16), pltpu.VMEM((nb * tq, w), BF16),
                        pltpu.VMEM((nb * tq, d), BF16)]
                       + [pltpu.VMEM((2, kblk, 2 * tq), F32)] * (nb * H_A),
        compiler_params=pltpu.CompilerParams(
            dimension_semantics=("arbitrary",), vmem_limit_bytes=VMEM_LIMIT),
        name="mixer_prompt",
    )(scal, qat, kab, vat, mix, dbias, kpos, kslope, dmat, xi, zeta, sg, x, gates, woa, wob, wout)


def _mixer_sample_kernel(scal_ref, qa_ref, kc_ref, vc_ref, kn_ref, vn_ref, mix_ref, r_ref,
                         abias_ref, dmat_ref, xi_ref, zeta_ref, sg_ref,
                         oa_ref, ob_ref, rout_ref, m_scr, acc_scr, *, n, past, tk, sub):
    kbi = pl.program_id(1)
    lam = scal_ref[0]
    lanes = lambda j: slice(j * HEAD_W, (j + 1) * HEAD_W)

    @pl.when(kbi == 0)
    def _():
        m_scr[...] = jnp.full(m_scr.shape, -jnp.inf, F32)
        acc_scr[...] = jnp.zeros_like(acc_scr)

    col = lax.broadcasted_iota(jnp.int32, (1, sub), 1)
    qs = [_split_halves(qa_ref[0, :, lanes(j)]) for j in range(H_A)]

    def scores(j, c):
        kk = kc_ref[0, lanes(j), c * sub:(c + 1) * sub].astype(BF16)
        kpos = (col + (kbi * tk + (c * sub - past))).astype(F32)
        return _dot(qs[j], kk) + kpos * scal_ref[1 + j]

    blocks = [(j, c) for c in range(tk // sub) for j in range(H_A)]
    carry = [(m_scr[j], acc_scr[j]) for j in range(H_A)]
    staged = [scores(*blk) for blk in blocks[:SCORE_LOOKAHEAD]]
    for i, (j, c) in enumerate(blocks):
        if i + SCORE_LOOKAHEAD < len(blocks):
            staged.append(scores(*blocks[i + SCORE_LOOKAHEAD]))
        vv = vc_ref[0, pl.ds(c * sub * H_A + j, sub, stride=H_A), :].astype(BF16)
        carry[j] = _softmax_update(carry[j], staged[i], vv)
    for j in range(H_A):
        m_scr[j], acc_scr[j] = carry[j]

    @pl.when(kbi == pl.num_programs(1) - 1)
    def _():
        s_new, ret = [], []
        for j in range(H_A):
            bias = abias_ref[j]
            s_new.append(_nt_dot(qs[j], kn_ref[0, :, lanes(j)]) + jnp.concatenate([bias, bias], axis=0))
        for j in range(H_A):
            s_ret, cross, r_new = _retention_stage1(
                _slab(mix_ref, "qb", j)[...], _slab(mix_ref, "kb", j)[...], _slab(mix_ref, "vb", j)[...],
                r_ref[0, j], dmat_ref[j], xi_ref[j], zeta_ref[j], scal_ref[1 + H_A + j])
            rout_ref[0, j] = r_new
            ret.append((s_ret, cross))
        for j in range(H_A):
            _, acc = _softmax_update((m_scr[j], acc_scr[j]), s_new[j], vn_ref[0, :, lanes(j)])
            oa = _diff_combine(acc, lam, n)
            oa = (_lane_rms(oa) * sg_ref[0:1, :]) * sg_ref[1:2, :]
            oa_ref[0, :, lanes(j)] = (oa * _slab(mix_ref, "za", j)[...].astype(F32)).astype(BF16)
        for j in range(H_A):
            ob = _retention_stage2(*ret[j], _slab(mix_ref, "vb", j)[...])
            ob_ref[0, :, lanes(j)] = (_lane_rms(ob) * _slab(mix_ref, "zb", j)[...].astype(F32)).astype(BF16)


def _mixer_sample(scal, qa, kc, vc, kab, vab, mix, r0, abias, dmat, xi, zeta, sg):
    b, n, w = qa.shape
    past = kc.shape[2]
    tk, sub = SAMPLE_KEY_TILE, SAMPLE_KEY_BLOCK
    assert past % tk == 0 and tk % sub == 0 and past % CHUNK == 0 and n <= CHUNK and H_A == H_B
    tile = pl.BlockSpec((1, n, w), lambda bi, ki: (bi, 0, 0))
    mix_tile = pl.BlockSpec((1, n, mix.shape[2]), lambda bi, ki: (bi, 0, 0))
    cache_k = pl.BlockSpec((1, w, tk), lambda bi, ki: (bi, 0, ki))
    cache_v = pl.BlockSpec((1, tk * H_A, HEAD_W), lambda bi, ki: (bi, ki, 0))
    state = pl.BlockSpec((1, H_B, DK_B, DV_B), lambda bi, ki: (bi, 0, 0, 0))
    full = lambda a: pl.BlockSpec(a.shape, lambda bi, ki: (0,) * a.ndim)
    return pl.pallas_call(
        functools.partial(_mixer_sample_kernel, n=n, past=past, tk=tk, sub=sub),
        grid=(b, past // tk),
        in_specs=[pl.BlockSpec(memory_space=pltpu.SMEM),
                  tile, cache_k, cache_v, tile, tile, mix_tile, state,
                  full(abias), full(dmat), full(xi), full(zeta), full(sg)],
        out_specs=[tile, tile, state],
        out_shape=[jax.ShapeDtypeStruct((b, n, H_A * HEAD_W), BF16),
                   jax.ShapeDtypeStruct((b, n, H_B * HEAD_W), BF16),
                   jax.ShapeDtypeStruct((b, H_B, DK_B, DV_B), F32)],
        scratch_shapes=[pltpu.VMEM((H_A, 2 * n, 1), F32),
                        pltpu.VMEM((H_A, 2 * n, 2 * DV_A), F32)],
        compiler_params=pltpu.CompilerParams(
            dimension_semantics=("arbitrary", "arbitrary"), vmem_limit_bytes=VMEM_LIMIT),
        name="mixer_sample",
    )(scal, qa, kc, vc, kab, vab, mix, r0, abias, dmat, xi, zeta, sg)


def _proj_out_kernel(x_ref, oa_ref, ob_ref, gate_ref, woa_ref, wob_ref, wout_ref, y_ref):
    d = x_ref.shape[1]
    m = (gate_ref[:, :d].astype(F32) * _dot(oa_ref[...], woa_ref[...])
         + gate_ref[:, d:].astype(F32) * _dot(ob_ref[...], wob_ref[...]))
    y_ref[...] = x_ref[...] + _dot(m.astype(BF16), wout_ref[...])


def _proj_out(x2, oa, ob, gates, woa, wob, wout):
    n, d = x2.shape
    tm = min(ROW_TILE, n)
    assert n % tm == 0
    row = lambda w: pl.BlockSpec((tm, w), lambda i: (i, 0))
    full = lambda a: pl.BlockSpec(a.shape, lambda i: (0,) * a.ndim)
    return pl.pallas_call(
        _proj_out_kernel,
        grid=(n // tm,),
        in_specs=[row(d), row(oa.shape[1]), row(ob.shape[1]), row(gates.shape[1]),
                  full(woa), full(wob), full(wout)],
        out_specs=row(d),
        out_shape=jax.ShapeDtypeStruct((n, d), F32),
        compiler_params=pltpu.CompilerParams(
            dimension_semantics=("arbitrary",), vmem_limit_bytes=VMEM_LIMIT),
        name="proj_out",
    )(x2, oa, ob, gates, woa, wob, wout)


def _block_tables(t, slopes, log_g):
    f32 = np.float32
    i = np.arange(t, dtype=np.int32)
    allowed = (i[None, :] // CHUNK) <= (i[:, None] // CHUNK)
    dist = np.abs(i[:, None] - i[None, :]).astype(f32)
    jf = i.astype(f32)
    abias = -slopes[:, None, None] * dist[None] + slopes[:, None, None] * jf[None, :, None]
    abias = np.where(allowed[None], abias, f32(-np.inf)).astype(f32)
    dmat = np.where(allowed[None], np.exp(dist[None] * log_g[:, None, None]), f32(0.0)).astype(f32)
    xi = np.exp((jf + f32(1.0))[None] * log_g[:, None]).astype(f32)
    zeta = np.exp((f32(t) - f32(1.0) - jf)[None] * log_g[:, None]).astype(f32)
    bc = lambda a: np.ascontiguousarray(np.broadcast_to(a[:, :, None], a.shape + (HEAD_W,)))
    return abias, dmat, bc(xi), bc(zeta)


N_SLOPE_PARTS = 3


def _alibi_features(t, width, slopes):
    assert t <= 256
    bf16 = jnp.bfloat16
    kpos = np.zeros((t, HEAD_W), np.float32)
    kpos[:, :N_SLOPE_PARTS] = np.arange(t, dtype=np.float32)[:, None]
    kslope = np.zeros((len(slopes), HEAD_W, width), np.float32)
    rest = slopes.astype(np.float32)
    for part in range(N_SLOPE_PARTS):
        piece = rest.astype(bf16).astype(np.float32)
        kslope[:, part, :] = piece[:, None]
        rest = rest - piece
    return kpos.astype(bf16), kslope.astype(bf16)


def _key_bias_table(t, slopes):
    kb = (slopes[:, None] * np.arange(t, dtype=np.float32)[None]).astype(np.float32)
    return np.ascontiguousarray(np.broadcast_to(kb[:, :, None], kb.shape + (HEAD_W,)))


def _layer(x_prompt, x_sample, ck, cv, r0, norm_g, w_in, b_gate, qn_g, kn_g,
           lam_q1, lam_k1, lam_q2, lam_k2, subln_g, w_oa, w_ob, w_out, lam_init):
    bp, tp, d = x_prompt.shape
    bs, ts, _ = x_sample.shape
    past = ck.shape[1]
    slopes = (2.0 ** (-8.0 * np.arange(1, H_A + 1, dtype=np.float32) / H_A)).astype(np.float32)
    log_g = np.log(1.0 - 2.0 ** (-5.0 - np.arange(H_B, dtype=np.float32))).astype(np.float32)
    lam = (jnp.exp(jnp.sum(lam_q1 * lam_k1).astype(F32))
           - jnp.exp(jnp.sum(lam_q2 * lam_k2).astype(F32)) + lam_init)

    woa_bf, wob_bf, wout_bf = w_oa.astype(BF16), w_ob.astype(BF16), w_out.astype(BF16)
    g = norm_g.reshape(1, d)
    qg_t = jnp.tile(qn_g, 2 * H_A).reshape(1, H_A * HEAD_W)
    kg_t = jnp.tile(kn_g, 2 * H_A).reshape(1, H_A * HEAD_W)
    grp = np.arange(MXU_TILE, dtype=np.int32) // DK_A
    gm = jnp.asarray((grp[:, None] == grp[None, :]) * (1.0 / DK_A), dtype=BF16)
    sg = jnp.stack([subln_g, jnp.full((DV_A,), 1.0 - lam_init, F32)])

    def scal(n_block, logit_scale):
        consts = np.concatenate([slopes * logit_scale, np.exp(np.float32(n_block) * log_g)]).astype(np.float32)
        return jnp.concatenate([lam.reshape(1).astype(F32), jnp.asarray(consts)])

    prompt_acts, sample_acts = _proj_in(
        x_prompt, x_sample.reshape(1, bs * ts, d), g, w_in, b_gate, qg_t, kg_t, gm,
        tm=ROW_TILE, tm_sample=min(SAMPLE_ROW_TILE, bs * ts))

    qat, kaf, kab, vaf, vat, mix, gates = prompt_acts
    abias, dmat, xi, zeta = _block_tables(KEY_BLOCK, slopes, log_g)
    kbias = _key_bias_table(KEY_BLOCK, slopes)
    dbias = np.ascontiguousarray(np.swapaxes(abias, 1, 2) - kbias[:, :, :1])
    flat = lambda a: a.reshape(-1, a.shape[-1])
    log2e = np.float32(LOG2E)
    kpos, kslope = _alibi_features(KEY_BLOCK, 2 * Q_TILE, slopes * log2e)
    y_p, ret_p = _mixer_prompt(scal(float(KEY_BLOCK), log2e), qat, kab, vat, mix,
                               dbias * log2e, kpos, kslope, dmat, xi, zeta, sg,
                               x_prompt, gates, woa_bf, wob_bf, wout_bf)
    k_p = jnp.transpose(kaf.reshape(bp, H_A, 2, DK_A, tp), (0, 4, 1, 2, 3))
    v_p = vaf.reshape(bp, tp, H_A, DV_A)

    qa, kaf, kab, vaf, vab, mix, gates = sample_acts
    r3 = lambda a: a.reshape(bs, ts, a.shape[-1])
    tabs = _block_tables(ts, slopes, log_g)
    ck_t = jnp.transpose(ck, (0, 2, 3, 4, 1)).reshape(bs, H_A * HEAD_W, past)
    oa, ob, ret_s = _mixer_sample(scal(float(ts), np.float32(1.0)), r3(qa), ck_t, cv.reshape(bs, past * H_A, HEAD_W),
                                  r3(kab), r3(vab), r3(mix), r0, *tabs, sg)
    y_s = _proj_out(flat(x_sample), flat(oa), flat(ob), flat(gates),
                    woa_bf, wob_bf, wout_bf).reshape(bs, ts, d)
    k_s = kaf.reshape(bs, ts, H_A, 2, DK_A)
    v_s = vaf.reshape(bs, ts, H_A, DV_A)
    return y_p, y_s, k_p, v_p, ret_p, k_s, v_s, ret_s


def kernel(x_prompt, x_sample, cache_k_diff, cache_v_diff, state_ret, norm_g, w_in, b_gate, qn_g, kn_g,
           lam_q1, lam_k1, lam_q2, lam_k2, subln_g, w_o_diff, w_o_ret, w_out):
    depth = w_in.shape[0]
    hp, hs = x_prompt, x_sample
    outs = [[] for _ in range(6)]
    for l in range(depth):
        lam_init = 0.8 - 0.6 * math.exp(-0.3 * l)
        hp, hs, k_p, v_p, ret_p, k_s, v_s, ret_s = _layer(
            hp, hs, cache_k_diff[l], cache_v_diff[l], state_ret[l], norm_g[l], w_in[l], b_gate[l],
            qn_g[l], kn_g[l], lam_q1[l], lam_k1[l], lam_q2[l], lam_k2[l], subln_g[l],
            w_o_diff[l], w_o_ret[l], w_out[l], lam_init)
        for lst, a in zip(outs, (k_p, v_p, ret_p, k_s, v_s, ret_s)):
            lst.append(a)
    return (hp, hs) + tuple(jnp.stack(o) for o in outs)
```

```python
import functools
import math

import jax
import jax.numpy as jnp
import numpy as np
from jax import lax
from jax.experimental import pallas as pl
from jax.experimental.pallas import tpu as pltpu

F32 = jnp.float32
BF16 = jnp.bfloat16

CHUNK = 64
H_A = 4
DK_A = 64
DV_A = 2 * DK_A
H_B = 4
DK_B = 128
DV_B = 128
HEAD_W = 128
EPS = 1e-6
LOG2E = math.log2(math.e)

MXU_TILE = 256
ROW_TILE = 512
SAMPLE_ROW_TILE = 256
Q_TILE = 256
KEY_BLOCK = 256
SEQS_PER_STEP = 2
SUM_ROWS = 16
SAMPLE_KEY_TILE = 4096
SAMPLE_KEY_BLOCK = 1024
SCORE_LOOKAHEAD = 8
V7X_VMEM_BYTES = 64 * 1024 * 1024
VMEM_LIMIT = V7X_VMEM_BYTES - 3 * 1024 * 1024


def _nt_dot(a, b):
    return lax.dot_general(a, b, (((1,), (1,)), ((), ())), preferred_element_type=F32)


def _tn_dot(a, b):
    return lax.dot_general(a, b, (((0,), (0,)), ((), ())), preferred_element_type=F32)


def _dot(a, b):
    return jnp.dot(a, b, preferred_element_type=F32)


def _sigmoid(x):
    return 1.0 / (1.0 + jnp.exp2(x * (-LOG2E)))


MIX_SLABS = ("za", "qb", "kb", "vb", "zb")


def _slab(mix_ref, name, head=None, seq=0):
    lo = MIX_SLABS.index(name) * H_A * HEAD_W
    if head is None:
        return mix_ref.at[seq, :, lo:lo + H_A * HEAD_W]
    return mix_ref.at[seq, :, lo + head * HEAD_W:lo + (head + 1) * HEAD_W]


W_IN_SEGMENTS = ("qa", "ka", "va", "za", "qb", "kb", "vb", "zb", "ga", "gb")
W_IN_USE_ORDER = ("ga", "gb", "qa", "ka", "va", "za", "zb", "kb", "qb", "vb")


def _w_in_columns(name, wa, wb, d_model):
    widths = (wa, wa, wa, wa, wb, wb, wb, wb, d_model, d_model)
    i = W_IN_SEGMENTS.index(name)
    return slice(sum(widths[:i]), sum(widths[:i + 1]))


def _proj_in_kernel(x_ref, g_ref, w_ref, bg_ref, qg_ref, kg_ref, gm_ref,
                    qa_ref, kaf_ref, kab_ref, vaf_ref, vab_ref, mix_ref, gate_ref,
                    *, wa, wb, d_model, feature_major, q_scale, before_segment=None):
    x = x_ref[0]
    tm = x.shape[0]
    ms = jnp.mean(x * x, axis=-1, keepdims=True)
    xn = (x * lax.rsqrt(ms + EPS)) * g_ref[...]
    used = []

    def seg(name):
        used.append(name)
        if before_segment is not None:
            before_segment(name)
        return _dot(xn, w_ref[:, _w_in_columns(name, wa, wb, d_model)])

    def group_norm(h, gain):
        sq = (h * h).astype(BF16)
        ms = jnp.concatenate([_dot(sq[:, c:c + MXU_TILE], gm_ref[...]) for c in range(0, wa, MXU_TILE)], axis=1)
        return (h * lax.rsqrt(ms + EPS)) * gain

    def silu(h):
        return h * _sigmoid(h)

    maybe_t = (lambda a: a.T) if feature_major else (lambda a: a)
    gate_ref[0, :, :d_model] = _sigmoid(seg("ga") + bg_ref[0:1, :]).astype(BF16)
    gate_ref[0, :, d_model:] = _sigmoid(seg("gb") + bg_ref[1:2, :]).astype(BF16)
    group_gain = lambda gain_ref: jnp.tile(gain_ref[...], (1, wa // DK_A))
    qa_ref[0] = maybe_t(group_norm(seg("qa"), group_gain(qg_ref)) * q_scale).astype(BF16)
    ka = group_norm(seg("ka"), group_gain(kg_ref))
    kaf_ref[0] = maybe_t(ka)
    kab_ref[0] = ka.astype(BF16)
    va = seg("va")
    vab_ref[0] = maybe_t(va).astype(BF16)
    for h in range(H_A):
        vaf_ref[pl.ds(h, tm, stride=H_A), :] = va[:, h * HEAD_W:(h + 1) * HEAD_W]
    _slab(mix_ref, "za")[...] = silu(seg("za")).astype(BF16)
    _slab(mix_ref, "zb")[...] = silu(seg("zb")).astype(BF16)
    _slab(mix_ref, "kb")[...] = (seg("kb") * (DK_B ** -0.5)).astype(BF16)
    _slab(mix_ref, "qb")[...] = seg("qb").astype(BF16)
    _slab(mix_ref, "vb")[...] = seg("vb").astype(BF16)
    assert tuple(used) == W_IN_USE_ORDER


N_PROJ_OUTS = 7


def _proj_in_both_kernel(xp_ref, xs_ref, g_ref, w_hbm_ref, bg_ref, qg_ref, kg_ref, gm_ref, *refs,
                         n_prompt_steps, wa, wb, d_model):
    out_refs, (w_ref, w_sem) = refs[:-2], refs[-2:]
    step = pl.program_id(0)
    consts = (g_ref, w_ref, bg_ref, qg_ref, kg_ref, gm_ref)
    prompt_body = functools.partial(
        _proj_in_kernel, xp_ref, *consts, *out_refs[:N_PROJ_OUTS], wa=wa, wb=wb, d_model=d_model,
        feature_major=True, q_scale=DK_A ** -0.5 * LOG2E)

    def segment_copy(name):
        cols = _w_in_columns(name, wa, wb, d_model)
        return pltpu.make_async_copy(w_hbm_ref.at[:, cols], w_ref.at[:, cols],
                                     w_sem.at[W_IN_USE_ORDER.index(name)])

    @pl.when(step == 0)
    def _():
        for name in W_IN_USE_ORDER:
            segment_copy(name).start()
        prompt_body(before_segment=lambda name: segment_copy(name).wait())

    @pl.when((step > 0) & (step < n_prompt_steps))
    def _():
        prompt_body()

    @pl.when(step >= n_prompt_steps)
    def _():
        _proj_in_kernel(xs_ref, *consts, *out_refs[N_PROJ_OUTS:],
                        wa=wa, wb=wb, d_model=d_model, feature_major=False, q_scale=DK_A ** -0.5)


def _proj_in(x_prompt, x_sample, g, w_in, bg, qg, kg, gm, *, tm, tm_sample):
    b, t, d = x_prompt.shape
    n = x_sample.shape[1]
    wa = H_A * HEAD_W
    wb = H_B * HEAD_W
    assert t % tm == 0 and n % tm_sample == 0 and wa == wb and x_sample.shape[0] == 1
    nt = t // tm
    n_p, n_s = b * nt, n // tm_sample
    p_tile = lambda s: jnp.minimum(s, n_p - 1)
    s_tile = lambda s: jnp.clip(s - n_p, 0, n_s - 1)
    full = lambda a: pl.BlockSpec(a.shape, lambda s: (0,) * a.ndim)

    def group_specs(bsz, length, rows, tile, feature_major):
        nt_g = length // rows
        row = lambda w: pl.BlockSpec((1, rows, w), lambda s: (tile(s) // nt_g, tile(s) % nt_g, 0))
        act = lambda w, dt: jax.ShapeDtypeStruct((bsz, length, w), dt)
        if feature_major:
            fm_shape = lambda dt: jax.ShapeDtypeStruct((bsz, wa, length), dt)
            fm_spec = pl.BlockSpec((1, wa, rows), lambda s: (tile(s) // nt_g, 0, tile(s) % nt_g))
        else:
            fm_shape, fm_spec = (lambda dt: act(wa, dt)), row(wa)
        shapes = [fm_shape(BF16),
                  fm_shape(F32), act(wa, BF16),
                  jax.ShapeDtypeStruct((bsz * length * H_A, HEAD_W), F32),
                  fm_shape(BF16),
                  act(len(MIX_SLABS) * wa, BF16),
                  act(2 * d, BF16)]
        specs = [fm_spec, fm_spec, row(wa),
                 pl.BlockSpec((rows * H_A, HEAD_W), lambda s: (tile(s), 0)),
                 fm_spec, row(len(MIX_SLABS) * wa), row(2 * d)]
        return row(d), shapes, specs

    xp_spec, p_shapes, p_specs = group_specs(b, t, tm, p_tile, True)
    xs_spec, s_shapes, s_specs = group_specs(1, n, tm_sample, s_tile, False)
    assert len(p_shapes) == N_PROJ_OUTS
    outs = pl.pallas_call(
        functools.partial(_proj_in_both_kernel, n_prompt_steps=n_p, wa=wa, wb=wb, d_model=d),
        grid=(n_p + n_s,),
        in_specs=[xp_spec, xs_spec, full(g), pl.BlockSpec(memory_space=pl.ANY),
                  full(bg), full(qg), full(kg), full(gm)],
        out_specs=p_specs + s_specs,
        out_shape=p_shapes + s_shapes,
        scratch_shapes=[pltpu.VMEM(w_in.shape, w_in.dtype),
                        pltpu.SemaphoreType.DMA((len(W_IN_USE_ORDER),))],
        compiler_params=pltpu.CompilerParams(
            dimension_semantics=("arbitrary",), vmem_limit_bytes=VMEM_LIMIT),
        name="proj_in",
    )(x_prompt, x_sample, g, w_in, bg, qg, kg, gm)
    return outs[:N_PROJ_OUTS], outs[N_PROJ_OUTS:]


def _split_halves(q):
    lane = lax.broadcasted_iota(jnp.int32, q.shape, 1)
    zero = jnp.zeros_like(q)
    return jnp.concatenate([jnp.where(lane < DK_A, q, zero), jnp.where(lane >= DK_A, q, zero)], axis=0)


def _softmax_update(carry, s, v):
    m, acc = carry
    m_new = jnp.maximum(m, jnp.max(s, axis=-1, keepdims=True))
    p = jnp.exp(s - m_new).astype(BF16)
    v1 = jnp.concatenate([v, jnp.ones_like(v)], axis=1)
    return m_new, jnp.exp(m - m_new) * acc + _dot(p, v1)


def _diff_combine(acc, lam, t):
    dv = acc.shape[1] // 2
    o = acc[:, :dv] / acc[:, dv:]
    return o[:t] - lam * o[t:]


def _lambda(lam_refs, lam_init):
    q1, k1, q2, k2 = (r[...] for r in lam_refs)
    inner = lambda a, b: jnp.sum(a * b, axis=-1, keepdims=True)
    return jnp.exp(inner(q1, k1)) - jnp.exp(inner(q2, k2)) + lam_init


def _lane_rms(o):
    return o * lax.rsqrt(jnp.mean(o * o, axis=-1, keepdims=True) + EPS)


def _retention_stage1(q, k, v, r, dmat, xi, zeta, cdecay):
    s = (_nt_dot(q, k) * dmat).astype(BF16)
    cross = _dot(q, r.astype(BF16)) * xi
    kz = (k.astype(F32) * zeta).astype(BF16)
    r_new = cdecay * r + _tn_dot(kz, v)
    return s, cross, r_new


def _retention_stage2(s, cross, v):
    return _dot(s, v) + cross


def _mixer_prompt_kernel(lq1_ref, lk1_ref, lq2_ref, lk2_ref, qt_ref, k_ref, vt_ref, mix_ref,
                         dbias_ref, kpos_ref, kslope_ref, dmat_ref, xi_ref, zeta_ref, sg_ref,
                         x_ref, gate_ref, woa_ref, wob_ref, wout_ref,
                         y_ref, rout_ref, r_scr, acc_scr, oa_scr, ob_scr, merged_scr, *s_scrs,
                         tq, kblk, nb, nq, n_tiles, slopes, cdecay, lam_init):
    chains = [(bi, j) for bi in range(nb) for j in range(H_A)]
    hps = len(chains)
    nsub = tq // kblk
    t = pl.program_id(0)
    live = t < n_tiles
    qi = lax.rem(jnp.minimum(t, n_tiles - 1), nq)
    lam = _lambda((lq1_ref, lk1_ref, lq2_ref, lk2_ref), lam_init)
    lanes = lambda j: slice(j * HEAD_W, (j + 1) * HEAD_W)

    def split_halves_t(qt):
        feat = lax.broadcasted_iota(jnp.int32, qt.shape, 0)
        zero = jnp.zeros_like(qt)
        return jnp.concatenate([jnp.where(feat < DK_A, qt, zero), jnp.where(feat >= DK_A, qt, zero)], axis=1)

    qst = [jnp.concatenate([split_halves_t(qt_ref[bi, lanes(j), :]), kslope_ref[j]], axis=0)
           for bi, j in chains]

    head = lambda c: chains[c][1]

    def keys(c, kb):
        bi, j = chains[c]
        k = k_ref[bi, pl.ds(pl.multiple_of(kb * kblk, kblk), kblk), lanes(j)]
        return jnp.concatenate([k, kpos_ref[...]], axis=1)

    def values1(c, kb):
        bi, j = chains[c]
        vt = vt_ref[bi, lanes(j), pl.ds(pl.multiple_of(kb * kblk, kblk), kblk)]
        return jnp.concatenate([vt, jnp.ones((SUM_ROWS, kblk), BF16)], axis=0)

    def stage_scores(c, kb, slot):
        s = _dot(keys(c, kb), qst[c])
        s_scrs[c][slot] = s
        return jnp.max(s, axis=0, keepdims=True)

    def softmax_step(m, s, s_max, shift):
        m_new = jnp.maximum(m, s_max + shift)
        return m_new, jnp.exp2(m - m_new), jnp.exp2(s - (m_new - shift)).astype(BF16)

    def body(kb, carry):
        slot = lax.rem(kb, 2)
        new = []
        for c in range(hps):
            smax_cur, m = carry[c]
            s_cur = s_scrs[c][slot]
            smax_next = stage_scores(c, kb + 1, 1 - slot)
            shift = ((kb - nsub * qi) * kblk).astype(F32) * slopes[head(c)]
            m, alpha, p = softmax_step(m, s_cur, smax_cur, shift)
            acc_scr[c] = alpha * acc_scr[c] + _dot(values1(c, kb), p)
            new.append((smax_next, m))
        return tuple(new)

    @pl.when(t == 0)
    def _():
        oa_scr[...] = jnp.zeros_like(oa_scr)
        ob_scr[...] = jnp.zeros_like(ob_scr)

    @pl.when(qi == 0)
    def _():
        r_scr[...] = jnp.zeros_like(r_scr)

    acc_scr[...] = jnp.zeros_like(acc_scr)
    init = tuple((stage_scores(c, 0, 0), jnp.full((1, 2 * tq), -jnp.inf, F32)) for c in range(hps))
    d_model = x_ref.shape[2]
    pa = _dot(oa_scr[...], woa_ref[...]).astype(BF16)
    pb = _dot(ob_scr[...], wob_ref[...]).astype(BF16)
    for bi in range(nb):
        rows = slice(bi * tq, (bi + 1) * tq)
        merged_scr[rows] = gate_ref[bi, :, :d_model] * pa[rows] + gate_ref[bi, :, d_model:] * pb[rows]
    n_before = nsub * qi
    carry = lax.fori_loop(0, jnp.where(live, n_before, 0), body, init)

    def retention_stage1(sub, state):
        rows = slice(sub * kblk, (sub + 1) * kblk)
        return [_retention_stage1(
            _slab(mix_ref, "qb", j, bi)[rows], _slab(mix_ref, "kb", j, bi)[rows], _slab(mix_ref, "vb", j, bi)[rows],
            state[c], dmat_ref[j], xi_ref[j], zeta_ref[j], cdecay[j])
            for c, (bi, j) in enumerate(chains)]

    def retention_stage2(sub, ret):
        rows = slice(sub * kblk, (sub + 1) * kblk)
        for c, (bi, j) in enumerate(chains):
            ob = _retention_stage2(ret[c][0], ret[c][1], _slab(mix_ref, "vb", j, bi)[rows])
            ob = (_lane_rms(ob) * _slab(mix_ref, "zb", j, bi)[rows].astype(F32)).astype(BF16)
            ob_scr[bi * tq + sub * kblk:bi * tq + (sub + 1) * kblk, lanes(j)] = ob

    def cat(pieces):
        pieces = [p for p in pieces if p.shape[1] > 0]
        return pieces[0] if len(pieces) == 1 else jnp.concatenate(pieces, axis=1)

    def diagonal_block(dsub, c, m):
        kb = n_before + dsub
        w = tq - dsub * kblk
        cols = (slice(dsub * kblk, tq), slice(tq + dsub * kblk, 2 * tq))
        if dsub == 0:
            s = s_scrs[c][lax.rem(n_before, 2)]
        else:
            s = _dot(keys(c, kb), cat([qst[c][:, q] for q in cols]))
        dbias = dbias_ref[head(c)]
        s = cat([s[:, :kblk] + dbias, s[:, kblk:w], s[:, w:w + kblk] + dbias, s[:, w + kblk:]])
        shift = (dsub * kblk) * slopes[head(c)]
        m_part, alpha, p = softmax_step(cat([m[:, q] for q in cols]), s,
                                        jnp.max(s, axis=0, keepdims=True), shift)
        pv = _dot(values1(c, kb), p)
        for h, q in enumerate(cols):
            acc_scr[c, :, q] = alpha[:, h * w:(h + 1) * w] * acc_scr[c, :, q] + pv[:, h * w:(h + 1) * w]
        return cat([m[:, :dsub * kblk], m_part[:, :w], m[:, tq:tq + dsub * kblk], m_part[:, w:]])

    def final_projection(cols):
        proj = _dot(merged_scr[...], wout_ref[:, cols])
        for bi in range(nb):
            y_ref[bi, :, cols] = x_ref[bi, :, cols] + proj[bi * tq:(bi + 1) * tq]

    out_cols = [slice(c, c + MXU_TILE) for c in range(0, d_model, MXU_TILE)]
    every = max(1, (hps * nsub) // len(out_cols))
    state = [r_scr[c] for c in range(hps)]
    ms = [carry[c][1] for c in range(hps)]
    for sub in range(nsub):
        ret = retention_stage1(sub, state)
        state = [r[2] for r in ret]
        for c in range(hps):
            if (sub * hps + c) % every == 0 and out_cols:
                final_projection(out_cols.pop(0))
            ms[c] = diagonal_block(sub, c, ms[c])
        retention_stage2(sub, ret)
    while out_cols:
        final_projection(out_cols.pop(0))
    for c, (bi, j) in enumerate(chains):
        r_new = jnp.where(live, state[c], r_scr[c])
        r_scr[c] = r_new
        rout_ref[bi, j] = r_new
    for c, (bi, j) in enumerate(chains):
        inv_l = 1.0 / acc_scr[c, HEAD_W:HEAD_W + 1]
        inv_l = jnp.concatenate([inv_l[:, :tq], lam * inv_l[:, tq:]], axis=1)
        ot = acc_scr[c, :HEAD_W] * inv_l
        oat = ot[:, :tq] - ot[:, tq:]
        oat = oat * lax.rsqrt(jnp.mean(oat * oat, axis=0, keepdims=True) + EPS)
        oa = (oat.T * sg_ref[...]) * (1.0 - lam_init)
        oa = (oa * _slab(mix_ref, "za", j, bi)[...].astype(F32)).astype(BF16)
        oa_scr[bi * tq:(bi + 1) * tq, lanes(j)] = oa


def _mixer_prompt(lam_vecs, qat, kab, vat, mix, dbias, kpos, kslope, dmat, xi, zeta, sg, x, gates, woa, wob, wout,
                  *, slopes, cdecay, lam_init):
    b, t, w = kab.shape
    d = x.shape[2]
    tq, kblk, nb = Q_TILE, KEY_BLOCK, SEQS_PER_STEP
    assert t % tq == 0 and tq % kblk == 0 and kblk % CHUNK == 0 and H_A == H_B and w == H_A * HEAD_W
    assert dbias.shape[1:] == (kblk, kblk) and d % (tq // kblk) == 0 and b % nb == 0
    nq = t // tq
    n_tiles = (b // nb) * nq
    cur = lambda s: jnp.minimum(s, n_tiles - 1)
    prev = lambda s: jnp.maximum(s - 1, 0)
    tile = pl.BlockSpec((nb, tq, mix.shape[2]), lambda s: (cur(s) // nq, cur(s) % nq, 0))
    tile_t = pl.BlockSpec((nb, w, tq), lambda s: (cur(s) // nq, 0, cur(s) % nq))
    whole = pl.BlockSpec((nb, t, w), lambda s: (cur(s) // nq, 0, 0))
    whole_t = pl.BlockSpec((nb, w, t), lambda s: (cur(s) // nq, 0, 0))
    rows = lambda a: pl.BlockSpec((nb, tq, a.shape[2]), lambda s: (prev(s) // nq, prev(s) % nq, 0))
    full = lambda a: pl.BlockSpec(a.shape, lambda s: (0,) * a.ndim, pipeline_mode=pl.Buffered(1))
    return pl.pallas_call(
        functools.partial(_mixer_prompt_kernel, tq=tq, kblk=kblk, nb=nb, nq=nq, n_tiles=n_tiles,
                          slopes=slopes, cdecay=cdecay, lam_init=lam_init),
        grid=(n_tiles + 1,),
        in_specs=[full(v) for v in lam_vecs] + [
                  tile_t, whole, whole_t, tile,
                  full(dbias), full(kpos), full(kslope), full(dmat), full(xi), full(zeta), full(sg),
                  rows(x), rows(gates), full(woa), full(wob), full(wout)],
        out_specs=[rows(x),
                   pl.BlockSpec((nb, H_B, DK_B, DV_B), lambda s: (cur(s) // nq, 0, 0, 0))],
        out_shape=[jax.ShapeDtypeStruct((b, t, d), F32),
                   jax.ShapeDtypeStruct((b, H_B, DK_B, DV_B), F32)],
        scratch_shapes=[pltpu.VMEM((nb * H_B, DK_B, DV_B), F32),
                        pltpu.VMEM((nb * H_A, HEAD_W + SUM_ROWS, 2 * tq), F32),
                        pltpu.VMEM((nb * tq, w), BF16), pltpu.VMEM((nb * tq, w), BF16),
                        pltpu.VMEM((nb * tq, d), BF16)]
                       + [pltpu.VMEM((2, kblk, 2 * tq), F32)] * (nb * H_A),
        compiler_params=pltpu.CompilerParams(
            dimension_semantics=("arbitrary",), vmem_limit_bytes=VMEM_LIMIT),
        name="mixer_prompt",
    )(*lam_vecs, qat, kab, vat, mix, dbias, kpos, kslope, dmat, xi, zeta, sg, x, gates, woa, wob, wout)


def _mixer_sample_kernel(lq1_ref, lk1_ref, lq2_ref, lk2_ref,
                         qa_ref, kc_ref, vc_ref, kn_ref, vn_ref, mix_ref, r_ref,
                         abias_ref, dmat_ref, xi_ref, zeta_ref, sg_ref,
                         oa_ref, ob_ref, rout_ref, m_scr, acc_scr,
                         *, n, past, tk, sub, slopes, cdecay, lam_init):
    kbi = pl.program_id(1)
    lanes = lambda j: slice(j * HEAD_W, (j + 1) * HEAD_W)

    @pl.when(kbi == 0)
    def _():
        m_scr[...] = jnp.full(m_scr.shape, -jnp.inf, F32)
        acc_scr[...] = jnp.zeros_like(acc_scr)

    col = lax.broadcasted_iota(jnp.int32, (1, sub), 1)
    qs = [_split_halves(qa_ref[0, :, lanes(j)]) for j in range(H_A)]

    def scores(j, c):
        kk = kc_ref[0, lanes(j), c * sub:(c + 1) * sub].astype(BF16)
        kpos = (col + (kbi * tk + (c * sub - past))).astype(F32)
        return _dot(qs[j], kk) + kpos * slopes[j]

    blocks = [(j, c) for c in range(tk // sub) for j in range(H_A)]
    carry = [(m_scr[j], acc_scr[j]) for j in range(H_A)]
    staged = [scores(*blk) for blk in blocks[:SCORE_LOOKAHEAD]]
    for i, (j, c) in enumerate(blocks):
        if i + SCORE_LOOKAHEAD < len(blocks):
            staged.append(scores(*blocks[i + SCORE_LOOKAHEAD]))
        vv = vc_ref[0, pl.ds(c * sub * H_A + j, sub, stride=H_A), :].astype(BF16)
        carry[j] = _softmax_update(carry[j], staged[i], vv)
    for j in range(H_A):
        m_scr[j], acc_scr[j] = carry[j]

    @pl.when(kbi == pl.num_programs(1) - 1)
    def _():
        lam = _lambda((lq1_ref, lk1_ref, lq2_ref, lk2_ref), lam_init)
        s_new, ret = [], []
        for j in range(H_A):
            bias = abias_ref[j]
            s_new.append(_nt_dot(qs[j], kn_ref[0, :, lanes(j)]) + jnp.concatenate([bias, bias], axis=0))
        for j in range(H_A):
            s_ret, cross, r_new = _retention_stage1(
                _slab(mix_ref, "qb", j)[...], _slab(mix_ref, "kb", j)[...], _slab(mix_ref, "vb", j)[...],
                r_ref[0, j], dmat_ref[j], xi_ref[j], zeta_ref[j], cdecay[j])
            rout_ref[0, j] = r_new
            ret.append((s_ret, cross))
        for j in range(H_A):
            _, acc = _softmax_update((m_scr[j], acc_scr[j]), s_new[j], vn_ref[0, :, lanes(j)])
            oa = _diff_combine(acc, lam, n)
            oa = (_lane_rms(oa) * sg_ref[...]) * (1.0 - lam_init)
            oa_ref[0, :, lanes(j)] = (oa * _slab(mix_ref, "za", j)[...].astype(F32)).astype(BF16)
        for j in range(H_A):
            ob = _retention_stage2(*ret[j], _slab(mix_ref, "vb", j)[...])
            ob_ref[0, :, lanes(j)] = (_lane_rms(ob) * _slab(mix_ref, "zb", j)[...].astype(F32)).astype(BF16)


def _mixer_sample(lam_vecs, qa, kc, vc, kab, vab, mix, r0, abias, dmat, xi, zeta, sg, *, slopes, cdecay, lam_init):
    b, n, w = qa.shape
    past = kc.shape[2]
    tk, sub = SAMPLE_KEY_TILE, SAMPLE_KEY_BLOCK
    assert past % tk == 0 and tk % sub == 0 and past % CHUNK == 0 and n <= CHUNK and H_A == H_B
    tile = pl.BlockSpec((1, n, w), lambda bi, ki: (bi, 0, 0))
    mix_tile = pl.BlockSpec((1, n, mix.shape[2]), lambda bi, ki: (bi, 0, 0))
    cache_k = pl.BlockSpec((1, w, tk), lambda bi, ki: (bi, 0, ki))
    cache_v = pl.BlockSpec((1, tk * H_A, HEAD_W), lambda bi, ki: (bi, ki, 0))
    state = pl.BlockSpec((1, H_B, DK_B, DV_B), lambda bi, ki: (bi, 0, 0, 0))
    full = lambda a: pl.BlockSpec(a.shape, lambda bi, ki: (0,) * a.ndim)
    return pl.pallas_call(
        functools.partial(_mixer_sample_kernel, n=n, past=past, tk=tk, sub=sub,
                          slopes=slopes, cdecay=cdecay, lam_init=lam_init),
        grid=(b, past // tk),
        in_specs=[full(v) for v in lam_vecs] + [
                  tile, cache_k, cache_v, tile, tile, mix_tile, state,
                  full(abias), full(dmat), full(xi), full(zeta), full(sg)],
        out_specs=[tile, tile, state],
        out_shape=[jax.ShapeDtypeStruct((b, n, H_A * HEAD_W), BF16),
                   jax.ShapeDtypeStruct((b, n, H_B * HEAD_W), BF16),
                   jax.ShapeDtypeStruct((b, H_B, DK_B, DV_B), F32)],
        scratch_shapes=[pltpu.VMEM((H_A, 2 * n, 1), F32),
                        pltpu.VMEM((H_A, 2 * n, 2 * DV_A), F32)],
        compiler_params=pltpu.CompilerParams(
            dimension_semantics=("arbitrary", "arbitrary"), vmem_limit_bytes=VMEM_LIMIT),
        name="mixer_sample",
    )(*lam_vecs, qa, kc, vc, kab, vab, mix, r0, abias, dmat, xi, zeta, sg)


def _proj_out_kernel(x_ref, oa_ref, ob_ref, gate_ref, woa_ref, wob_ref, wout_ref, y_ref):
    d = x_ref.shape[1]
    m = (gate_ref[:, :d].astype(F32) * _dot(oa_ref[...], woa_ref[...])
         + gate_ref[:, d:].astype(F32) * _dot(ob_ref[...], wob_ref[...]))
    y_ref[...] = x_ref[...] + _dot(m.astype(BF16), wout_ref[...])


def _proj_out(x2, oa, ob, gates, woa, wob, wout):
    n, d = x2.shape
    tm = min(ROW_TILE, n)
    assert n % tm == 0
    row = lambda w: pl.BlockSpec((tm, w), lambda i: (i, 0))
    full = lambda a: pl.BlockSpec(a.shape, lambda i: (0,) * a.ndim)
    return pl.pallas_call(
        _proj_out_kernel,
        grid=(n // tm,),
        in_specs=[row(d), row(oa.shape[1]), row(ob.shape[1]), row(gates.shape[1]),
                  full(woa), full(wob), full(wout)],
        out_specs=row(d),
        out_shape=jax.ShapeDtypeStruct((n, d), F32),
        compiler_params=pltpu.CompilerParams(
            dimension_semantics=("arbitrary",), vmem_limit_bytes=VMEM_LIMIT),
        name="proj_out",
    )(x2, oa, ob, gates, woa, wob, wout)


def _block_tables(t, slopes, log_g):
    f32 = np.float32
    i = np.arange(t, dtype=np.int32)
    allowed = (i[None, :] // CHUNK) <= (i[:, None] // CHUNK)
    dist = np.abs(i[:, None] - i[None, :]).astype(f32)
    jf = i.astype(f32)
    abias = -slopes[:, None, None] * dist[None] + slopes[:, None, None] * jf[None, :, None]
    abias = np.where(allowed[None], abias, f32(-np.inf)).astype(f32)
    dmat = np.where(allowed[None], np.exp(dist[None] * log_g[:, None, None]), f32(0.0)).astype(f32)
    xi = np.exp((jf + f32(1.0))[None] * log_g[:, None]).astype(f32)
    zeta = np.exp((f32(t) - f32(1.0) - jf)[None] * log_g[:, None]).astype(f32)
    bc = lambda a: np.ascontiguousarray(np.broadcast_to(a[:, :, None], a.shape + (HEAD_W,)))
    return abias, dmat, bc(xi), bc(zeta)


N_SLOPE_PARTS = 3


def _alibi_features(t, width, slopes):
    assert t <= 256
    bf16 = jnp.bfloat16
    kpos = np.zeros((t, HEAD_W), np.float32)
    kpos[:, :N_SLOPE_PARTS] = np.arange(t, dtype=np.float32)[:, None]
    kslope = np.zeros((len(slopes), HEAD_W, width), np.float32)
    rest = slopes.astype(np.float32)
    for part in range(N_SLOPE_PARTS):
        piece = rest.astype(bf16).astype(np.float32)
        kslope[:, part, :] = piece[:, None]
        rest = rest - piece
    return kpos.astype(bf16), kslope.astype(bf16)


def _key_bias_table(t, slopes):
    kb = (slopes[:, None] * np.arange(t, dtype=np.float32)[None]).astype(np.float32)
    return np.ascontiguousarray(np.broadcast_to(kb[:, :, None], kb.shape + (HEAD_W,)))


def _layer(x_prompt, x_sample, ck, cv, r0, norm_g, w_in, b_gate, qn_g, kn_g,
           lam_q1, lam_k1, lam_q2, lam_k2, subln_g, w_oa, w_ob, w_out, lam_init):
    bp, tp, d = x_prompt.shape
    bs, ts, _ = x_sample.shape
    past = ck.shape[1]
    slopes = (2.0 ** (-8.0 * np.arange(1, H_A + 1, dtype=np.float32) / H_A)).astype(np.float32)
    log_g = np.log(1.0 - 2.0 ** (-5.0 - np.arange(H_B, dtype=np.float32))).astype(np.float32)
    row = lambda v: v.reshape(1, -1)
    lam_vecs = (row(lam_q1), row(lam_k1), row(lam_q2), row(lam_k2))
    floats = lambda a: tuple(float(v) for v in np.asarray(a, np.float32))
    cdecay = lambda n_block: floats(np.exp(np.float32(n_block) * log_g))

    woa_bf, wob_bf, wout_bf = w_oa.astype(BF16), w_ob.astype(BF16), w_out.astype(BF16)
    grp = np.arange(MXU_TILE, dtype=np.int32) // DK_A
    gm = jnp.asarray((grp[:, None] == grp[None, :]) * (1.0 / DK_A), dtype=BF16)
    sg = row(subln_g)

    prompt_acts, sample_acts = _proj_in(
        x_prompt, x_sample.reshape(1, bs * ts, d), row(norm_g), w_in, b_gate, row(qn_g), row(kn_g), gm,
        tm=ROW_TILE, tm_sample=min(SAMPLE_ROW_TILE, bs * ts))

    qat, kaf, kab, vaf, vat, mix, gates = prompt_acts
    abias, dmat, xi, zeta = _block_tables(KEY_BLOCK, slopes, log_g)
    kbias = _key_bias_table(KEY_BLOCK, slopes)
    dbias = np.ascontiguousarray(np.swapaxes(abias, 1, 2) - kbias[:, :, :1])
    flat = lambda a: a.reshape(-1, a.shape[-1])
    log2e = np.float32(LOG2E)
    kpos, kslope = _alibi_features(KEY_BLOCK, 2 * Q_TILE, slopes * log2e)
    y_p, ret_p = _mixer_prompt(lam_vecs, qat, kab, vat, mix,
                               dbias * log2e, kpos, kslope, dmat, xi, zeta, sg,
                               x_prompt, gates, woa_bf, wob_bf, wout_bf,
                               slopes=floats(slopes * log2e), cdecay=cdecay(KEY_BLOCK), lam_init=lam_init)
    k_p = jnp.transpose(kaf.reshape(bp, H_A, 2, DK_A, tp), (0, 4, 1, 2, 3))
    v_p = vaf.reshape(bp, tp, H_A, DV_A)

    qa, kaf, kab, vaf, vab, mix, gates = sample_acts
    r3 = lambda a: a.reshape(bs, ts, a.shape[-1])
    tabs = _block_tables(ts, slopes, log_g)
    ck_t = jnp.transpose(ck, (0, 2, 3, 4, 1)).reshape(bs, H_A * HEAD_W, past)
    oa, ob, ret_s = _mixer_sample(lam_vecs, r3(qa), ck_t, cv.reshape(bs, past * H_A, HEAD_W),
                                  r3(kab), r3(vab), r3(mix), r0, *tabs, sg,
                                  slopes=floats(slopes), cdecay=cdecay(ts), lam_init=lam_init)
    y_s = _proj_out(flat(x_sample), flat(oa), flat(ob), flat(gates),
                    woa_bf, wob_bf, wout_bf).reshape(bs, ts, d)
    k_s = kaf.reshape(bs, ts, H_A, 2, DK_A)
    v_s = vaf.reshape(bs, ts, H_A, DV_A)
    return y_p, y_s, k_p, v_p, ret_p, k_s, v_s, ret_s


def kernel(x_prompt, x_sample, cache_k_diff, cache_v_diff, state_ret, norm_g, w_in, b_gate, qn_g, kn_g,
           lam_q1, lam_k1, lam_q2, lam_k2, subln_g, w_o_diff, w_o_ret, w_out):
    depth = w_in.shape[0]
    hp, hs = x_prompt, x_sample
    outs = [[] for _ in range(6)]
    for l in range(depth):
        lam_init = 0.8 - 0.6 * math.exp(-0.3 * l)
        hp, hs, k_p, v_p, ret_p, k_s, v_s, ret_s = _layer(
            hp, hs, cache_k_diff[l], cache_v_diff[l], state_ret[l], norm_g[l], w_in[l], b_gate[l],
            qn_g[l], kn_g[l], lam_q1[l], lam_k1[l], lam_q2[l], lam_k2[l], subln_g[l],
            w_o_diff[l], w_o_ret[l], w_out[l], lam_init)
        for lst, a in zip(outs, (k_p, v_p, ret_p, k_s, v_s, ret_s)):
            lst.append(a)
    return (hp, hs) + tuple(jnp.stack(o) for o in outs)
```

```python
import functools
import math

import jax
import jax.numpy as jnp
import numpy as np
from jax import lax
from jax.experimental import pallas as pl
from jax.experimental.pallas import tpu as pltpu

F32 = jnp.float32
BF16 = jnp.bfloat16

CHUNK = 64
H_A = 4
DK_A = 64
DV_A = 2 * DK_A
H_B = 4
DK_B = 128
DV_B = 128
HEAD_W = 128
EPS = 1e-6
LOG2E = math.log2(math.e)

MXU_TILE = 256
ROW_TILE = 512
SAMPLE_ROW_TILE = 256
Q_TILE = 256
KEY_BLOCK = 256
SEQS_PER_STEP = 2
SUM_ROWS = 16
SAMPLE_KEY_TILE = 4096
SAMPLE_KEY_BLOCK = 1024
SCORE_LOOKAHEAD = 8
V7X_VMEM_BYTES = 64 * 1024 * 1024
VMEM_LIMIT = V7X_VMEM_BYTES - 3 * 1024 * 1024


def _nt_dot(a, b):
    return lax.dot_general(a, b, (((1,), (1,)), ((), ())), preferred_element_type=F32)


def _tn_dot(a, b):
    return lax.dot_general(a, b, (((0,), (0,)), ((), ())), preferred_element_type=F32)


def _dot(a, b):
    return jnp.dot(a, b, preferred_element_type=F32)


def _sigmoid(x):
    return 1.0 / (1.0 + jnp.exp2(x * (-LOG2E)))


MIX_SLABS = ("za", "qb", "kb", "vb", "zb")


def _slab(mix_ref, name, head=None, seq=0):
    lo = MIX_SLABS.index(name) * H_A * HEAD_W
    if head is None:
        return mix_ref.at[seq, :, lo:lo + H_A * HEAD_W]
    return mix_ref.at[seq, :, lo + head * HEAD_W:lo + (head + 1) * HEAD_W]


W_IN_SEGMENTS = ("qa", "ka", "va", "za", "qb", "kb", "vb", "zb", "ga", "gb")
W_IN_USE_ORDER = ("ga", "gb", "qa", "ka", "va", "za", "zb", "kb", "qb", "vb")


def _w_in_columns(name, wa, wb, d_model):
    widths = (wa, wa, wa, wa, wb, wb, wb, wb, d_model, d_model)
    i = W_IN_SEGMENTS.index(name)
    return slice(sum(widths[:i]), sum(widths[:i + 1]))


def _proj_in_kernel(x_ref, g_ref, w_ref, bg_ref, qg_ref, kg_ref, gm_ref,
                    qa_ref, kaf_ref, kab_ref, vaf_ref, vab_ref, mix_ref, gate_ref,
                    *, wa, wb, d_model, feature_major, q_scale, before_segment=None):
    x = x_ref[0]
    tm = x.shape[0]
    ms = jnp.mean(x * x, axis=-1, keepdims=True)
    xn = (x * lax.rsqrt(ms + EPS)) * g_ref[...]
    used = []

    def seg(name):
        used.append(name)
        if before_segment is not None:
            before_segment(name)
        return _dot(xn, w_ref[:, _w_in_columns(name, wa, wb, d_model)])

    def group_norm(h, gain):
        sq = (h * h).astype(BF16)
        ms = jnp.concatenate([_dot(sq[:, c:c + MXU_TILE], gm_ref[...]) for c in range(0, wa, MXU_TILE)], axis=1)
        return (h * lax.rsqrt(ms + EPS)) * gain

    def silu(h):
        return h * _sigmoid(h)

    maybe_t = (lambda a: a.T) if feature_major else (lambda a: a)
    gate_ref[0, :, :d_model] = _sigmoid(seg("ga") + bg_ref[0:1, :]).astype(BF16)
    gate_ref[0, :, d_model:] = _sigmoid(seg("gb") + bg_ref[1:2, :]).astype(BF16)
    group_gain = lambda gain_ref: jnp.tile(gain_ref[...], (1, wa // DK_A))
    qa_ref[0] = maybe_t(group_norm(seg("qa"), group_gain(qg_ref)) * q_scale).astype(BF16)
    ka = group_norm(seg("ka"), group_gain(kg_ref))
    if feature_major:
        kaf_ref[0] = ka.T
    else:
        groups = wa // DK_A
        for grp in range(groups):
            kaf_ref[pl.ds(grp, tm, stride=groups), :] = ka[:, grp * DK_A:(grp + 1) * DK_A]
    kab_ref[0] = ka.astype(BF16)
    va = seg("va")
    vab_ref[0] = maybe_t(va).astype(BF16)
    for h in range(H_A):
        vaf_ref[pl.ds(h, tm, stride=H_A), :] = va[:, h * HEAD_W:(h + 1) * HEAD_W]
    _slab(mix_ref, "za")[...] = silu(seg("za")).astype(BF16)
    _slab(mix_ref, "zb")[...] = silu(seg("zb")).astype(BF16)
    _slab(mix_ref, "kb")[...] = (seg("kb") * (DK_B ** -0.5)).astype(BF16)
    _slab(mix_ref, "qb")[...] = seg("qb").astype(BF16)
    _slab(mix_ref, "vb")[...] = seg("vb").astype(BF16)
    assert tuple(used) == W_IN_USE_ORDER


N_PROJ_OUTS = 7


def _proj_in_both_kernel(xp_ref, xs_ref, g_ref, w_hbm_ref, bg_ref, qg_ref, kg_ref, gm_ref, *refs,
                         n_prompt_steps, n_casts, wa, wb, d_model):
    cast_in, refs = refs[:n_casts], refs[n_casts:]
    out_refs, cast_out, (w_ref, w_sem) = refs[:2 * N_PROJ_OUTS], refs[2 * N_PROJ_OUTS:-2], refs[-2:]
    step = pl.program_id(0)
    consts = (g_ref, w_ref, bg_ref, qg_ref, kg_ref, gm_ref)

    def prompt_body(**kwargs):
        for src, dst in zip(cast_in, cast_out, strict=True):
            dst[...] = src[...].astype(dst.dtype)
        _proj_in_kernel(xp_ref, *consts, *out_refs[:N_PROJ_OUTS], wa=wa, wb=wb, d_model=d_model,
                        feature_major=True,
                        q_scale=DK_A ** -0.5 * LOG2E,
                        **kwargs)

    def segment_copy(name):
        cols = _w_in_columns(name, wa, wb, d_model)
        return pltpu.make_async_copy(w_hbm_ref.at[:, cols], w_ref.at[:, cols],
                                     w_sem.at[W_IN_USE_ORDER.index(name)])

    @pl.when(step == 0)
    def _():
        for name in W_IN_USE_ORDER:
            segment_copy(name).start()
        prompt_body(before_segment=lambda name: segment_copy(name).wait())

    @pl.when((step > 0) & (step < n_prompt_steps))
    def _():
        prompt_body()

    @pl.when(step >= n_prompt_steps)
    def _():
        _proj_in_kernel(xs_ref, *consts, *out_refs[N_PROJ_OUTS:],
                        wa=wa, wb=wb, d_model=d_model, feature_major=False, q_scale=DK_A ** -0.5)


def _proj_in(x_prompt, x_sample, g, w_in, bg, qg, kg, gm, to_bf16, *, tm, tm_sample):
    b, t, d = x_prompt.shape
    n = x_sample.shape[1]
    wa = H_A * HEAD_W
    wb = H_B * HEAD_W
    assert t % tm == 0 and n % tm_sample == 0 and wa == wb and x_sample.shape[0] == 1
    nt = t // tm
    n_p, n_s = b * nt, n // tm_sample
    bf16_rows = 16
    assert all(a.ndim == 2 and a.shape[0] % (n_p * bf16_rows) == 0 for a in to_bf16)
    p_tile = lambda s: jnp.minimum(s, n_p - 1)
    s_tile = lambda s: jnp.clip(s - n_p, 0, n_s - 1)
    full = lambda a: pl.BlockSpec(a.shape, lambda s: (0,) * a.ndim)

    def group_specs(bsz, length, rows, tile, feature_major):
        nt_g = length // rows
        row = lambda w: pl.BlockSpec((1, rows, w), lambda s: (tile(s) // nt_g, tile(s) % nt_g, 0))
        act = lambda w, dt: jax.ShapeDtypeStruct((bsz, length, w), dt)
        grouped = lambda groups, lanes: jax.ShapeDtypeStruct((bsz * length * groups, lanes), F32)
        grouped_spec = lambda groups, lanes: pl.BlockSpec((rows * groups, lanes), lambda s: (tile(s), 0))
        if feature_major:
            fm_shape = lambda dt: jax.ShapeDtypeStruct((bsz, wa, length), dt)
            fm_spec = pl.BlockSpec((1, wa, rows), lambda s: (tile(s) // nt_g, 0, tile(s) % nt_g))
            kf_shape, kf_spec = fm_shape(F32), fm_spec
        else:
            fm_shape, fm_spec = (lambda dt: act(wa, dt)), row(wa)
            kf_shape, kf_spec = grouped(wa // DK_A, DK_A), grouped_spec(wa // DK_A, DK_A)
        shapes = [fm_shape(BF16),
                  kf_shape, act(wa, BF16),
                  grouped(H_A, HEAD_W),
                  fm_shape(BF16),
                  act(len(MIX_SLABS) * wa, BF16),
                  act(2 * d, BF16)]
        specs = [fm_spec, kf_spec, row(wa), grouped_spec(H_A, HEAD_W),
                 fm_spec, row(len(MIX_SLABS) * wa), row(2 * d)]
        return row(d), shapes, specs

    cast_specs = [pl.BlockSpec((a.shape[0] // n_p, a.shape[1]), lambda s: (p_tile(s), 0)) for a in to_bf16]
    cast_shapes = [jax.ShapeDtypeStruct(a.shape, BF16) for a in to_bf16]

    xp_spec, p_shapes, p_specs = group_specs(b, t, tm, p_tile, True)
    xs_spec, s_shapes, s_specs = group_specs(1, n, tm_sample, s_tile, False)
    assert len(p_shapes) == N_PROJ_OUTS
    outs = pl.pallas_call(
        functools.partial(_proj_in_both_kernel, n_prompt_steps=n_p, n_casts=len(to_bf16),
                          wa=wa, wb=wb, d_model=d),
        grid=(n_p + n_s,),
        in_specs=[xp_spec, xs_spec, full(g), pl.BlockSpec(memory_space=pl.ANY),
                  full(bg), full(qg), full(kg), full(gm)] + cast_specs,
        out_specs=p_specs + s_specs + cast_specs,
        out_shape=p_shapes + s_shapes + cast_shapes,
        scratch_shapes=[pltpu.VMEM(w_in.shape, w_in.dtype),
                        pltpu.SemaphoreType.DMA((len(W_IN_USE_ORDER),))],
        compiler_params=pltpu.CompilerParams(
            dimension_semantics=("arbitrary",), vmem_limit_bytes=VMEM_LIMIT),
        name="proj_in",
    )(x_prompt, x_sample, g, w_in, bg, qg, kg, gm, *to_bf16)
    return outs[:N_PROJ_OUTS], outs[N_PROJ_OUTS:2 * N_PROJ_OUTS], outs[2 * N_PROJ_OUTS:]


def _split_halves(q):
    lane = lax.broadcasted_iota(jnp.int32, q.shape, 1)
    zero = jnp.zeros_like(q)
    return jnp.concatenate([jnp.where(lane < DK_A, q, zero), jnp.where(lane >= DK_A, q, zero)], axis=0)


def _softmax_update(carry, s, v):
    m, acc = carry
    m_new = jnp.maximum(m, jnp.max(s, axis=-1, keepdims=True))
    p = jnp.exp(s - m_new).astype(BF16)
    v1 = jnp.concatenate([v, jnp.ones_like(v)], axis=1)
    return m_new, jnp.exp(m - m_new) * acc + _dot(p, v1)


def _diff_combine(acc, lam, t):
    dv = acc.shape[1] // 2
    o = acc[:, :dv] / acc[:, dv:]
    return o[:t] - lam * o[t:]


def _lambda(lam_refs, lam_init):
    q1, k1, q2, k2 = (r[...] for r in lam_refs)
    inner = lambda a, b: jnp.sum(a * b, axis=-1, keepdims=True)
    return jnp.exp(inner(q1, k1)) - jnp.exp(inner(q2, k2)) + lam_init


def _lane_rms(o):
    return o * lax.rsqrt(jnp.mean(o * o, axis=-1, keepdims=True) + EPS)


def _retention_stage1(q, k, v, r, dmat, xi, zeta, cdecay):
    s = (_nt_dot(q, k) * dmat).astype(BF16)
    cross = _dot(q, r.astype(BF16)) * xi
    kz = (k.astype(F32) * zeta).astype(BF16)
    r_new = cdecay * r + _tn_dot(kz, v)
    return s, cross, r_new


def _retention_stage2(s, cross, v):
    return _dot(s, v) + cross


def _mixer_prompt_kernel(lq1_ref, lk1_ref, lq2_ref, lk2_ref, qt_ref, k_ref, vt_ref, mix_ref,
                         dbias_ref, kpos_ref, kslope_ref, dmat_ref, xi_ref, zeta_ref, sg_ref,
                         x_ref, gate_ref, woa_ref, wob_ref, wout_ref,
                         y_ref, rout_ref, r_scr, acc_scr, oa_scr, ob_scr, merged_scr, *s_scrs,
                         tq, kblk, nb, nq, n_tiles, slopes, cdecay, lam_init):
    chains = [(bi, j) for bi in range(nb) for j in range(H_A)]
    hps = len(chains)
    nsub = tq // kblk
    t = pl.program_id(0)
    live = t < n_tiles
    qi = lax.rem(jnp.minimum(t, n_tiles - 1), nq)
    lam = _lambda((lq1_ref, lk1_ref, lq2_ref, lk2_ref), lam_init)
    lanes = lambda j: slice(j * HEAD_W, (j + 1) * HEAD_W)

    def split_halves_t(qt):
        feat = lax.broadcasted_iota(jnp.int32, qt.shape, 0)
        zero = jnp.zeros_like(qt)
        return jnp.concatenate([jnp.where(feat < DK_A, qt, zero), jnp.where(feat >= DK_A, qt, zero)], axis=1)

    qst = [jnp.concatenate([split_halves_t(qt_ref[bi, lanes(j), :]), kslope_ref[j]], axis=0)
           for bi, j in chains]

    head = lambda c: chains[c][1]

    def keys(c, kb):
        bi, j = chains[c]
        k = k_ref[bi, pl.ds(pl.multiple_of(kb * kblk, kblk), kblk), lanes(j)]
        return jnp.concatenate([k, kpos_ref[...]], axis=1)

    def values1(c, kb):
        bi, j = chains[c]
        vt = vt_ref[bi, lanes(j), pl.ds(pl.multiple_of(kb * kblk, kblk), kblk)]
        return jnp.concatenate([vt, jnp.ones((SUM_ROWS, kblk), BF16)], axis=0)

    def stage_scores(c, kb, slot):
        s = _dot(keys(c, kb), qst[c])
        s_scrs[c][slot] = s
        return jnp.max(s, axis=0, keepdims=True)

    def softmax_step(m, s, s_max, shift):
        m_new = jnp.maximum(m, s_max + shift)
        return m_new, jnp.exp2(m - m_new), jnp.exp2(s - (m_new - shift)).astype(BF16)

    def body(kb, carry):
        slot = lax.rem(kb, 2)
        new = []
        for c in range(hps):
            smax_cur, m = carry[c]
            s_cur = s_scrs[c][slot]
            smax_next = stage_scores(c, kb + 1, 1 - slot)
            shift = ((kb - nsub * qi) * kblk).astype(F32) * slopes[head(c)]
            m, alpha, p = softmax_step(m, s_cur, smax_cur, shift)
            acc_scr[c] = alpha * acc_scr[c] + _dot(values1(c, kb), p)
            new.append((smax_next, m))
        return tuple(new)

    @pl.when(t == 0)
    def _():
        oa_scr[...] = jnp.zeros_like(oa_scr)
        ob_scr[...] = jnp.zeros_like(ob_scr)

    @pl.when(qi == 0)
    def _():
        r_scr[...] = jnp.zeros_like(r_scr)

    acc_scr[...] = jnp.zeros_like(acc_scr)
    init = tuple((stage_scores(c, 0, 0), jnp.full((1, 2 * tq), -jnp.inf, F32)) for c in range(hps))
    d_model = x_ref.shape[2]
    pa = _dot(oa_scr[...], woa_ref[...]).astype(BF16)
    pb = _dot(ob_scr[...], wob_ref[...]).astype(BF16)
    for bi in range(nb):
        rows = slice(bi * tq, (bi + 1) * tq)
        merged_scr[rows] = gate_ref[bi, :, :d_model] * pa[rows] + gate_ref[bi, :, d_model:] * pb[rows]
    n_before = nsub * qi
    carry = lax.fori_loop(0, jnp.where(live, n_before, 0), body, init)

    def retention_stage1(sub, state):
        rows = slice(sub * kblk, (sub + 1) * kblk)
        return [_retention_stage1(
            _slab(mix_ref, "qb", j, bi)[rows], _slab(mix_ref, "kb", j, bi)[rows], _slab(mix_ref, "vb", j, bi)[rows],
            state[c], dmat_ref[j], xi_ref[j], zeta_ref[j], cdecay[j])
            for c, (bi, j) in enumerate(chains)]

    def retention_stage2(sub, ret):
        rows = slice(sub * kblk, (sub + 1) * kblk)
        for c, (bi, j) in enumerate(chains):
            ob = _retention_stage2(ret[c][0], ret[c][1], _slab(mix_ref, "vb", j, bi)[rows])
            ob = (_lane_rms(ob) * _slab(mix_ref, "zb", j, bi)[rows].astype(F32)).astype(BF16)
            ob_scr[bi * tq + sub * kblk:bi * tq + (sub + 1) * kblk, lanes(j)] = ob

    def cat(pieces):
        pieces = [p for p in pieces if p.shape[1] > 0]
        return pieces[0] if len(pieces) == 1 else jnp.concatenate(pieces, axis=1)

    def diagonal_block(dsub, c, m):
        kb = n_before + dsub
        w = tq - dsub * kblk
        cols = (slice(dsub * kblk, tq), slice(tq + dsub * kblk, 2 * tq))
        if dsub == 0:
            s = s_scrs[c][lax.rem(n_before, 2)]
        else:
            s = _dot(keys(c, kb), cat([qst[c][:, q] for q in cols]))
        dbias = dbias_ref[head(c)]
        s = cat([s[:, :kblk] + dbias, s[:, kblk:w], s[:, w:w + kblk] + dbias, s[:, w + kblk:]])
        shift = (dsub * kblk) * slopes[head(c)]
        m_part, alpha, p = softmax_step(cat([m[:, q] for q in cols]), s,
                                        jnp.max(s, axis=0, keepdims=True), shift)
        pv = _dot(values1(c, kb), p)
        for h, q in enumerate(cols):
            acc_scr[c, :, q] = alpha[:, h * w:(h + 1) * w] * acc_scr[c, :, q] + pv[:, h * w:(h + 1) * w]
        return cat([m[:, :dsub * kblk], m_part[:, :w], m[:, tq:tq + dsub * kblk], m_part[:, w:]])

    def final_projection(cols):
        proj = _dot(merged_scr[...], wout_ref[:, cols])
        for bi in range(nb):
            y_ref[bi, :, cols] = x_ref[bi, :, cols] + proj[bi * tq:(bi + 1) * tq]

    out_cols = [slice(c, c + MXU_TILE) for c in range(0, d_model, MXU_TILE)]
    every = max(1, (hps * nsub) // len(out_cols))
    state = [r_scr[c] for c in range(hps)]
    ms = [carry[c][1] for c in range(hps)]
    for sub in range(nsub):
        ret = retention_stage1(sub, state)
        state = [r[2] for r in ret]
        for c in range(hps):
            if (sub * hps + c) % every == 0 and out_cols:
                final_projection(out_cols.pop(0))
            ms[c] = diagonal_block(sub, c, ms[c])
        retention_stage2(sub, ret)
    while out_cols:
        final_projection(out_cols.pop(0))
    for c, (bi, j) in enumerate(chains):
        r_new = jnp.where(live, state[c], r_scr[c])
        r_scr[c] = r_new
        rout_ref[bi, j] = r_new
    for c, (bi, j) in enumerate(chains):
        inv_l = 1.0 / acc_scr[c, HEAD_W:HEAD_W + 1]
        inv_l = jnp.concatenate([inv_l[:, :tq], lam * inv_l[:, tq:]], axis=1)
        ot = acc_scr[c, :HEAD_W] * inv_l
        oat = ot[:, :tq] - ot[:, tq:]
        oat = oat * lax.rsqrt(jnp.mean(oat * oat, axis=0, keepdims=True) + EPS)
        oa = (oat.T * sg_ref[...]) * (1.0 - lam_init)
        oa = (oa * _slab(mix_ref, "za", j, bi)[...].astype(F32)).astype(BF16)
        oa_scr[bi * tq:(bi + 1) * tq, lanes(j)] = oa


def _mixer_prompt(lam_vecs, qat, kab, vat, mix, dbias, kpos, kslope, dmat, xi, zeta, sg, x, gates, woa, wob, wout,
                  *, slopes, cdecay, lam_init):
    b, t, w = kab.shape
    d = x.shape[2]
    tq, kblk, nb = Q_TILE, KEY_BLOCK, SEQS_PER_STEP
    assert t % tq == 0 and tq % kblk == 0 and kblk % CHUNK == 0 and H_A == H_B and w == H_A * HEAD_W
    assert dbias.shape[1:] == (kblk, kblk) and d % (tq // kblk) == 0 and b % nb == 0
    nq = t // tq
    n_tiles = (b // nb) * nq
    cur = lambda s: jnp.minimum(s, n_tiles - 1)
    prev = lambda s: jnp.maximum(s - 1, 0)
    tile = pl.BlockSpec((nb, tq, mix.shape[2]), lambda s: (cur(s) // nq, cur(s) % nq, 0))
    tile_t = pl.BlockSpec((nb, w, tq), lambda s: (cur(s) // nq, 0, cur(s) % nq))
    whole = pl.BlockSpec((nb, t, w), lambda s: (cur(s) // nq, 0, 0))
    whole_t = pl.BlockSpec((nb, w, t), lambda s: (cur(s) // nq, 0, 0))
    rows = lambda a: pl.BlockSpec((nb, tq, a.shape[2]), lambda s: (prev(s) // nq, prev(s) % nq, 0))
    full = lambda a: pl.BlockSpec(a.shape, lambda s: (0,) * a.ndim, pipeline_mode=pl.Buffered(1))
    return pl.pallas_call(
        functools.partial(_mixer_prompt_kernel, tq=tq, kblk=kblk, nb=nb, nq=nq, n_tiles=n_tiles,
                          slopes=slopes, cdecay=cdecay, lam_init=lam_init),
        grid=(n_tiles + 1,),
        in_specs=[full(v) for v in lam_vecs] + [
                  tile_t, whole, whole_t, tile,
                  full(dbias), full(kpos), full(kslope), full(dmat), full(xi), full(zeta), full(sg),
                  rows(x), rows(gates), full(woa), full(wob), full(wout)],
        out_specs=[rows(x),
                   pl.BlockSpec((nb, H_B, DK_B, DV_B), lambda s: (cur(s) // nq, 0, 0, 0))],
        out_shape=[jax.ShapeDtypeStruct((b, t, d), F32),
                   jax.ShapeDtypeStruct((b, H_B, DK_B, DV_B), F32)],
        scratch_shapes=[pltpu.VMEM((nb * H_B, DK_B, DV_B), F32),
                        pltpu.VMEM((nb * H_A, HEAD_W + SUM_ROWS, 2 * tq), F32),
                        pltpu.VMEM((nb * tq, w), BF16), pltpu.VMEM((nb * tq, w), BF16),
                        pltpu.VMEM((nb * tq, d), BF16)]
                       + [pltpu.VMEM((2, kblk, 2 * tq), F32)] * (nb * H_A),
        compiler_params=pltpu.CompilerParams(
            dimension_semantics=("arbitrary",), vmem_limit_bytes=VMEM_LIMIT),
        name="mixer_prompt",
    )(*lam_vecs, qat, kab, vat, mix, dbias, kpos, kslope, dmat, xi, zeta, sg, x, gates, woa, wob, wout)


def _mixer_sample_kernel(lq1_ref, lk1_ref, lq2_ref, lk2_ref,
                         qa_ref, kc_ref, vc_ref, kn_ref, vn_ref, mix_ref, r_ref,
                         abias_ref, dmat_ref, xi_ref, zeta_ref, sg_ref,
                         oa_ref, ob_ref, rout_ref, m_scr, acc_scr,
                         *, n, past, tk, sub, slopes, cdecay, lam_init):
    kbi = pl.program_id(1)
    lanes = lambda j: slice(j * HEAD_W, (j + 1) * HEAD_W)

    @pl.when(kbi == 0)
    def _():
        m_scr[...] = jnp.full(m_scr.shape, -jnp.inf, F32)
        acc_scr[...] = jnp.zeros_like(acc_scr)

    col = lax.broadcasted_iota(jnp.int32, (1, sub), 1)
    qs = [_split_halves(qa_ref[0, :, lanes(j)]) for j in range(H_A)]

    def scores(j, c):
        kk = kc_ref[0, lanes(j), c * sub:(c + 1) * sub].astype(BF16)
        kpos = (col + (kbi * tk + (c * sub - past))).astype(F32)
        return _dot(qs[j], kk) + kpos * slopes[j]

    blocks = [(j, c) for c in range(tk // sub) for j in range(H_A)]
    carry = [(m_scr[j], acc_scr[j]) for j in range(H_A)]
    staged = [scores(*blk) for blk in blocks[:SCORE_LOOKAHEAD]]
    for i, (j, c) in enumerate(blocks):
        if i + SCORE_LOOKAHEAD < len(blocks):
            staged.append(scores(*blocks[i + SCORE_LOOKAHEAD]))
        vv = vc_ref[0, pl.ds(c * sub * H_A + j, sub, stride=H_A), :].astype(BF16)
        carry[j] = _softmax_update(carry[j], staged[i], vv)
    for j in range(H_A):
        m_scr[j], acc_scr[j] = carry[j]

    @pl.when(kbi == pl.num_programs(1) - 1)
    def _():
        lam = _lambda((lq1_ref, lk1_ref, lq2_ref, lk2_ref), lam_init)
        s_new, ret = [], []
        for j in range(H_A):
            bias = abias_ref[j]
            s_new.append(_nt_dot(qs[j], kn_ref[0, :, lanes(j)]) + jnp.concatenate([bias, bias], axis=0))
        for j in range(H_A):
            s_ret, cross, r_new = _retention_stage1(
                _slab(mix_ref, "qb", j)[...], _slab(mix_ref, "kb", j)[...], _slab(mix_ref, "vb", j)[...],
                r_ref[0, j], dmat_ref[j], xi_ref[j], zeta_ref[j], cdecay[j])
            rout_ref[0, j] = r_new
            ret.append((s_ret, cross))
        for j in range(H_A):
            _, acc = _softmax_update((m_scr[j], acc_scr[j]), s_new[j], vn_ref[0, :, lanes(j)])
            oa = _diff_combine(acc, lam, n)
            oa = (_lane_rms(oa) * sg_ref[...]) * (1.0 - lam_init)
            oa_ref[0, :, lanes(j)] = (oa * _slab(mix_ref, "za", j)[...].astype(F32)).astype(BF16)
        for j in range(H_A):
            ob = _retention_stage2(*ret[j], _slab(mix_ref, "vb", j)[...])
            ob_ref[0, :, lanes(j)] = (_lane_rms(ob) * _slab(mix_ref, "zb", j)[...].astype(F32)).astype(BF16)


def _mixer_sample(lam_vecs, qa, kc, vc, kab, vab, mix, r0, abias, dmat, xi, zeta, sg, *, slopes, cdecay, lam_init):
    b, n, w = qa.shape
    past = kc.shape[2]
    tk, sub = SAMPLE_KEY_TILE, SAMPLE_KEY_BLOCK
    assert past % tk == 0 and tk % sub == 0 and past % CHUNK == 0 and n <= CHUNK and H_A == H_B
    tile = pl.BlockSpec((1, n, w), lambda bi, ki: (bi, 0, 0))
    mix_tile = pl.BlockSpec((1, n, mix.shape[2]), lambda bi, ki: (bi, 0, 0))
    cache_k = pl.BlockSpec((1, w, tk), lambda bi, ki: (bi, 0, ki))
    cache_v = pl.BlockSpec((1, tk * H_A, HEAD_W), lambda bi, ki: (bi, ki, 0))
    state = pl.BlockSpec((1, H_B, DK_B, DV_B), lambda bi, ki: (bi, 0, 0, 0))
    full = lambda a: pl.BlockSpec(a.shape, lambda bi, ki: (0,) * a.ndim)
    return pl.pallas_call(
        functools.partial(_mixer_sample_kernel, n=n, past=past, tk=tk, sub=sub,
                          slopes=slopes, cdecay=cdecay, lam_init=lam_init),
        grid=(b, past // tk),
        in_specs=[full(v) for v in lam_vecs] + [
                  tile, cache_k, cache_v, tile, tile, mix_tile, state,
                  full(abias), full(dmat), full(xi), full(zeta), full(sg)],
        out_specs=[tile, tile, state],
        out_shape=[jax.ShapeDtypeStruct((b, n, H_A * HEAD_W), BF16),
                   jax.ShapeDtypeStruct((b, n, H_B * HEAD_W), BF16),
                   jax.ShapeDtypeStruct((b, H_B, DK_B, DV_B), F32)],
        scratch_shapes=[pltpu.VMEM((H_A, 2 * n, 1), F32),
                        pltpu.VMEM((H_A, 2 * n, 2 * DV_A), F32)],
        compiler_params=pltpu.CompilerParams(
            dimension_semantics=("arbitrary", "arbitrary"), vmem_limit_bytes=VMEM_LIMIT),
        name="mixer_sample",
    )(*lam_vecs, qa, kc, vc, kab, vab, mix, r0, abias, dmat, xi, zeta, sg)


def _proj_out_kernel(x_ref, oa_ref, ob_ref, gate_ref, woa_ref, wob_ref, wout_ref, y_ref):
    d = x_ref.shape[1]
    m = (gate_ref[:, :d].astype(F32) * _dot(oa_ref[...], woa_ref[...])
         + gate_ref[:, d:].astype(F32) * _dot(ob_ref[...], wob_ref[...]))
    y_ref[...] = x_ref[...] + _dot(m.astype(BF16), wout_ref[...])


def _proj_out(x2, oa, ob, gates, woa, wob, wout):
    n, d = x2.shape
    tm = min(ROW_TILE, n)
    assert n % tm == 0
    row = lambda w: pl.BlockSpec((tm, w), lambda i: (i, 0))
    full = lambda a: pl.BlockSpec(a.shape, lambda i: (0,) * a.ndim)
    return pl.pallas_call(
        _proj_out_kernel,
        grid=(n // tm,),
        in_specs=[row(d), row(oa.shape[1]), row(ob.shape[1]), row(gates.shape[1]),
                  full(woa), full(wob), full(wout)],
        out_specs=row(d),
        out_shape=jax.ShapeDtypeStruct((n, d), F32),
        compiler_params=pltpu.CompilerParams(
            dimension_semantics=("arbitrary",), vmem_limit_bytes=VMEM_LIMIT),
        name="proj_out",
    )(x2, oa, ob, gates, woa, wob, wout)


def _block_tables(t, slopes, log_g):
    f32 = np.float32
    i = np.arange(t, dtype=np.int32)
    allowed = (i[None, :] // CHUNK) <= (i[:, None] // CHUNK)
    dist = np.abs(i[:, None] - i[None, :]).astype(f32)
    jf = i.astype(f32)
    abias = -slopes[:, None, None] * dist[None] + slopes[:, None, None] * jf[None, :, None]
    abias = np.where(allowed[None], abias, f32(-np.inf)).astype(f32)
    dmat = np.where(allowed[None], np.exp(dist[None] * log_g[:, None, None]), f32(0.0)).astype(f32)
    xi = np.exp((jf + f32(1.0))[None] * log_g[:, None]).astype(f32)
    zeta = np.exp((f32(t) - f32(1.0) - jf)[None] * log_g[:, None]).astype(f32)
    bc = lambda a: np.ascontiguousarray(np.broadcast_to(a[:, :, None], a.shape + (HEAD_W,)))
    return abias, dmat, bc(xi), bc(zeta)


N_SLOPE_PARTS = 3


def _alibi_features(t, width, slopes):
    assert t <= 256
    bf16 = jnp.bfloat16
    kpos = np.zeros((t, HEAD_W), np.float32)
    kpos[:, :N_SLOPE_PARTS] = np.arange(t, dtype=np.float32)[:, None]
    kslope = np.zeros((len(slopes), HEAD_W, width), np.float32)
    rest = slopes.astype(np.float32)
    for part in range(N_SLOPE_PARTS):
        piece = rest.astype(bf16).astype(np.float32)
        kslope[:, part, :] = piece[:, None]
        rest = rest - piece
    return kpos.astype(bf16), kslope.astype(bf16)


def _key_bias_table(t, slopes):
    kb = (slopes[:, None] * np.arange(t, dtype=np.float32)[None]).astype(np.float32)
    return np.ascontiguousarray(np.broadcast_to(kb[:, :, None], kb.shape + (HEAD_W,)))


def _layer(x_prompt, x_sample, ck, cv, r0, norm_g, w_in, b_gate, qn_g, kn_g,
           lam_q1, lam_k1, lam_q2, lam_k2, subln_g, w_oa, w_ob, w_out, lam_init):
    bp, tp, d = x_prompt.shape
    bs, ts, _ = x_sample.shape
    past = ck.shape[1]
    slopes = (2.0 ** (-8.0 * np.arange(1, H_A + 1, dtype=np.float32) / H_A)).astype(np.float32)
    log_g = np.log(1.0 - 2.0 ** (-5.0 - np.arange(H_B, dtype=np.float32))).astype(np.float32)
    row = lambda v: v.reshape(1, -1)
    lam_vecs = (row(lam_q1), row(lam_k1), row(lam_q2), row(lam_k2))
    floats = lambda a: tuple(float(v) for v in np.asarray(a, np.float32))
    cdecay = lambda n_block: floats(np.exp(np.float32(n_block) * log_g))

    grp = np.arange(MXU_TILE, dtype=np.int32) // DK_A
    gm = jnp.asarray((grp[:, None] == grp[None, :]) * (1.0 / DK_A), dtype=BF16)
    sg = row(subln_g)

    prompt_acts, sample_acts, (woa_bf, wob_bf, wout_bf) = _proj_in(
        x_prompt, x_sample.reshape(1, bs * ts, d), row(norm_g), w_in, b_gate, row(qn_g), row(kn_g), gm,
        (w_oa, w_ob, w_out), tm=ROW_TILE, tm_sample=min(SAMPLE_ROW_TILE, bs * ts))

    qat, kaf, kab, vaf, vat, mix, gates = prompt_acts
    abias, dmat, xi, zeta = _block_tables(KEY_BLOCK, slopes, log_g)
    kbias = _key_bias_table(KEY_BLOCK, slopes)
    dbias = np.ascontiguousarray(np.swapaxes(abias, 1, 2) - kbias[:, :, :1])
    flat = lambda a: a.reshape(-1, a.shape[-1])
    log2e = np.float32(LOG2E)
    kpos, kslope = _alibi_features(KEY_BLOCK, 2 * Q_TILE, slopes * log2e)
    y_p, ret_p = _mixer_prompt(lam_vecs, qat, kab, vat, mix,
                               dbias * log2e, kpos, kslope, dmat, xi, zeta, sg,
                               x_prompt, gates, woa_bf, wob_bf, wout_bf,
                               slopes=floats(slopes * log2e), cdecay=cdecay(KEY_BLOCK), lam_init=lam_init)
    k_p = jnp.transpose(kaf.reshape(bp, H_A, 2, DK_A, tp), (0, 4, 1, 2, 3))
    v_p = vaf.reshape(bp, tp, H_A, DV_A)

    qa, kaf, kab, vaf, vab, mix, gates = sample_acts
    r3 = lambda a: a.reshape(bs, ts, a.shape[-1])
    tabs = _block_tables(ts, slopes, log_g)
    ck_t = jnp.transpose(ck, (0, 2, 3, 4, 1)).reshape(bs, H_A * HEAD_W, past)
    oa, ob, ret_s = _mixer_sample(lam_vecs, r3(qa), ck_t, cv.reshape(bs, past * H_A, HEAD_W),
                                  r3(kab), r3(vab), r3(mix), r0, *tabs, sg,
                                  slopes=floats(slopes), cdecay=cdecay(ts), lam_init=lam_init)
    y_s = _proj_out(flat(x_sample), flat(oa), flat(ob), flat(gates),
                    woa_bf, wob_bf, wout_bf).reshape(bs, ts, d)
    k_s = kaf.reshape(bs, ts, H_A, 2, DK_A)
    v_s = vaf.reshape(bs, ts, H_A, DV_A)
    return y_p, y_s, k_p, v_p, ret_p, k_s, v_s, ret_s


def kernel(x_prompt, x_sample, cache_k_diff, cache_v_diff, state_ret, norm_g, w_in, b_gate, qn_g, kn_g,
           lam_q1, lam_k1, lam_q2, lam_k2, subln_g, w_o_diff, w_o_ret, w_out):
    depth = w_in.shape[0]
    hp, hs = x_prompt, x_sample
    outs = [[] for _ in range(6)]
    for l in range(depth):
        lam_init = 0.8 - 0.6 * math.exp(-0.3 * l)
        hp, hs, k_p, v_p, ret_p, k_s, v_s, ret_s = _layer(
            hp, hs, cache_k_diff[l], cache_v_diff[l], state_ret[l], norm_g[l], w_in[l], b_gate[l],
            qn_g[l], kn_g[l], lam_q1[l], lam_k1[l], lam_q2[l], lam_k2[l], subln_g[l],
            w_o_diff[l], w_o_ret[l], w_out[l], lam_init)
        for lst, a in zip(outs, (k_p, v_p, ret_p, k_s, v_s, ret_s)):
            lst.append(a)
    return (hp, hs) + tuple(jnp.stack(o) for o in outs)
```

```python
import functools
import math

import jax
import jax.numpy as jnp
import numpy as np
from jax import lax
from jax.experimental import pallas as pl
from jax.experimental.pallas import tpu as pltpu

F32 = jnp.float32
BF16 = jnp.bfloat16

CHUNK = 64
H_A = 4
DK_A = 64
DV_A = 2 * DK_A
H_B = 4
DK_B = 128
DV_B = 128
HEAD_W = 128
EPS = 1e-6
LOG2E = math.log2(math.e)

MXU_TILE = 256
ROW_TILE = 512
SAMPLE_ROW_TILE = 256
Q_TILE = 256
KEY_BLOCK = 256
SEQS_PER_STEP = 2
SUM_ROWS = 16
SAMPLE_KEY_TILE = 4096
SAMPLE_KEY_BLOCK = 1024
SCORE_LOOKAHEAD = 8
V7X_VMEM_BYTES = 64 * 1024 * 1024
VMEM_LIMIT = V7X_VMEM_BYTES - 3 * 1024 * 1024


def _nt_dot(a, b):
    return lax.dot_general(a, b, (((1,), (1,)), ((), ())), preferred_element_type=F32)


def _tn_dot(a, b):
    return lax.dot_general(a, b, (((0,), (0,)), ((), ())), preferred_element_type=F32)


def _dot(a, b):
    return jnp.dot(a, b, preferred_element_type=F32)


def _sigmoid(x):
    return 1.0 / (1.0 + jnp.exp2(x * (-LOG2E)))


MIX_SLABS = ("za", "qb", "kb", "vb", "zb")


def _slab(mix_ref, name, head=None, seq=0):
    lo = MIX_SLABS.index(name) * H_A * HEAD_W
    if head is None:
        return mix_ref.at[seq, :, lo:lo + H_A * HEAD_W]
    return mix_ref.at[seq, :, lo + head * HEAD_W:lo + (head + 1) * HEAD_W]


W_IN_SEGMENTS = ("qa", "ka", "va", "za", "qb", "kb", "vb", "zb", "ga", "gb")
W_IN_USE_ORDER = ("ga", "gb", "qa", "ka", "va", "za", "zb", "kb", "qb", "vb")


def _w_in_columns(name, wa, wb, d_model):
    widths = (wa, wa, wa, wa, wb, wb, wb, wb, d_model, d_model)
    i = W_IN_SEGMENTS.index(name)
    return slice(sum(widths[:i]), sum(widths[:i + 1]))


def _proj_in_kernel(x_ref, g_ref, w_ref, bg_ref, qg_ref, kg_ref, gm_ref,
                    qa_ref, kaf_ref, kab_ref, vaf_ref, vab_ref, mix_ref, gate_ref,
                    *, wa, wb, d_model, feature_major, q_scale, before_segment=None):
    x = x_ref[0]
    tm = x.shape[0]
    ms = jnp.mean(x * x, axis=-1, keepdims=True)
    xn = (x * lax.rsqrt(ms + EPS)) * g_ref[...]
    used = []

    def seg(name):
        used.append(name)
        if before_segment is not None:
            before_segment(name)
        return _dot(xn, w_ref[:, _w_in_columns(name, wa, wb, d_model)])

    def group_norm(h, gain):
        sq = (h * h).astype(BF16)
        ms = jnp.concatenate([_dot(sq[:, c:c + MXU_TILE], gm_ref[...]) for c in range(0, wa, MXU_TILE)], axis=1)
        return (h * lax.rsqrt(ms + EPS)) * gain

    def silu(h):
        return h * _sigmoid(h)

    maybe_t = (lambda a: a.T) if feature_major else (lambda a: a)
    gate_ref[0, :, :d_model] = _sigmoid(seg("ga") + bg_ref[0:1, :]).astype(BF16)
    gate_ref[0, :, d_model:] = _sigmoid(seg("gb") + bg_ref[1:2, :]).astype(BF16)
    group_gain = lambda gain_ref: jnp.tile(gain_ref[...], (1, wa // DK_A))
    qa_ref[0] = maybe_t(group_norm(seg("qa"), group_gain(qg_ref)) * q_scale).astype(BF16)
    ka = group_norm(seg("ka"), group_gain(kg_ref))
    if feature_major:
        kaf_ref[0] = ka.T
    else:
        groups = wa // DK_A
        for grp in range(groups):
            kaf_ref[pl.ds(grp, tm, stride=groups), :] = ka[:, grp * DK_A:(grp + 1) * DK_A]
    kab_ref[0] = ka.astype(BF16)
    va = seg("va")
    vab_ref[0] = maybe_t(va).astype(BF16)
    for h in range(H_A):
        vaf_ref[pl.ds(h, tm, stride=H_A), :] = va[:, h * HEAD_W:(h + 1) * HEAD_W]
    _slab(mix_ref, "za")[...] = silu(seg("za")).astype(BF16)
    _slab(mix_ref, "zb")[...] = silu(seg("zb")).astype(BF16)
    _slab(mix_ref, "kb")[...] = (seg("kb") * (DK_B ** -0.5)).astype(BF16)
    _slab(mix_ref, "qb")[...] = seg("qb").astype(BF16)
    _slab(mix_ref, "vb")[...] = seg("vb").astype(BF16)
    assert tuple(used) == W_IN_USE_ORDER


N_PROJ_OUTS = 7


def _proj_in_both_kernel(xp_ref, xs_ref, g_ref, w_hbm_ref, bg_ref, qg_ref, kg_ref, gm_ref, *refs,
                         n_prompt_steps, n_casts, wa, wb, d_model):
    cast_in, refs = refs[:n_casts], refs[n_casts:]
    out_refs, cast_out, (w_ref, w_sem) = refs[:2 * N_PROJ_OUTS], refs[2 * N_PROJ_OUTS:-2], refs[-2:]
    step = pl.program_id(0)
    consts = (g_ref, w_ref, bg_ref, qg_ref, kg_ref, gm_ref)

    def prompt_body(**kwargs):
        for src, dst in zip(cast_in, cast_out, strict=True):
            dst[...] = src[...].astype(dst.dtype)
        _proj_in_kernel(xp_ref, *consts, *out_refs[:N_PROJ_OUTS], wa=wa, wb=wb, d_model=d_model,
                        feature_major=True,
                        q_scale=DK_A ** -0.5 * LOG2E,
                        **kwargs)

    def segment_copy(name):
        cols = _w_in_columns(name, wa, wb, d_model)
        return pltpu.make_async_copy(w_hbm_ref.at[:, cols], w_ref.at[:, cols],
                                     w_sem.at[W_IN_USE_ORDER.index(name)])

    @pl.when(step == 0)
    def _():
        for name in W_IN_USE_ORDER:
            segment_copy(name).start()
        prompt_body(before_segment=lambda name: segment_copy(name).wait())

    @pl.when((step > 0) & (step < n_prompt_steps))
    def _():
        prompt_body()

    @pl.when(step >= n_prompt_steps)
    def _():
        _proj_in_kernel(xs_ref, *consts, *out_refs[N_PROJ_OUTS:],
                        wa=wa, wb=wb, d_model=d_model, feature_major=False, q_scale=DK_A ** -0.5)


def _proj_in(x_prompt, x_sample, g, w_in, bg, qg, kg, gm, to_bf16, *, tm, tm_sample):
    b, t, d = x_prompt.shape
    n = x_sample.shape[1]
    wa = H_A * HEAD_W
    wb = H_B * HEAD_W
    assert t % tm == 0 and n % tm_sample == 0 and wa == wb and x_sample.shape[0] == 1
    nt = t // tm
    n_p, n_s = b * nt, n // tm_sample
    bf16_rows = 16
    assert all(a.ndim == 2 and a.shape[0] % (n_p * bf16_rows) == 0 for a in to_bf16)
    p_tile = lambda s: jnp.minimum(s, n_p - 1)
    s_tile = lambda s: jnp.clip(s - n_p, 0, n_s - 1)
    full = lambda a: pl.BlockSpec(a.shape, lambda s: (0,) * a.ndim)

    def group_specs(bsz, length, rows, tile, feature_major):
        nt_g = length // rows
        row = lambda w: pl.BlockSpec((1, rows, w), lambda s: (tile(s) // nt_g, tile(s) % nt_g, 0))
        act = lambda w, dt: jax.ShapeDtypeStruct((bsz, length, w), dt)
        grouped = lambda groups, lanes: jax.ShapeDtypeStruct((bsz * length * groups, lanes), F32)
        grouped_spec = lambda groups, lanes: pl.BlockSpec((rows * groups, lanes), lambda s: (tile(s), 0))
        if feature_major:
            fm_shape = lambda dt: jax.ShapeDtypeStruct((bsz, wa, length), dt)
            fm_spec = pl.BlockSpec((1, wa, rows), lambda s: (tile(s) // nt_g, 0, tile(s) % nt_g))
            kf_shape, kf_spec = fm_shape(F32), fm_spec
        else:
            fm_shape, fm_spec = (lambda dt: act(wa, dt)), row(wa)
            kf_shape, kf_spec = grouped(wa // DK_A, DK_A), grouped_spec(wa // DK_A, DK_A)
        shapes = [fm_shape(BF16),
                  kf_shape, act(wa, BF16),
                  grouped(H_A, HEAD_W),
                  fm_shape(BF16),
                  act(len(MIX_SLABS) * wa, BF16),
                  act(2 * d, BF16)]
        specs = [fm_spec, kf_spec, row(wa), grouped_spec(H_A, HEAD_W),
                 fm_spec, row(len(MIX_SLABS) * wa), row(2 * d)]
        return row(d), shapes, specs

    cast_specs = [pl.BlockSpec((a.shape[0] // n_p, a.shape[1]), lambda s: (p_tile(s), 0)) for a in to_bf16]
    cast_shapes = [jax.ShapeDtypeStruct(a.shape, BF16) for a in to_bf16]

    xp_spec, p_shapes, p_specs = group_specs(b, t, tm, p_tile, True)
    xs_spec, s_shapes, s_specs = group_specs(1, n, tm_sample, s_tile, False)
    assert len(p_shapes) == N_PROJ_OUTS
    outs = pl.pallas_call(
        functools.partial(_proj_in_both_kernel, n_prompt_steps=n_p, n_casts=len(to_bf16),
                          wa=wa, wb=wb, d_model=d),
        grid=(n_p + n_s,),
        in_specs=[xp_spec, xs_spec, full(g), pl.BlockSpec(memory_space=pl.ANY),
                  full(bg), full(qg), full(kg), full(gm)] + cast_specs,
        out_specs=p_specs + s_specs + cast_specs,
        out_shape=p_shapes + s_shapes + cast_shapes,
        scratch_shapes=[pltpu.VMEM(w_in.shape, w_in.dtype),
                        pltpu.SemaphoreType.DMA((len(W_IN_USE_ORDER),))],
        compiler_params=pltpu.CompilerParams(
            dimension_semantics=("arbitrary",), vmem_limit_bytes=VMEM_LIMIT),
        name="proj_in",
    )(x_prompt, x_sample, g, w_in, bg, qg, kg, gm, *to_bf16)
    return outs[:N_PROJ_OUTS], outs[N_PROJ_OUTS:2 * N_PROJ_OUTS], outs[2 * N_PROJ_OUTS:]


def _split_halves(q):
    lane = lax.broadcasted_iota(jnp.int32, q.shape, 1)
    zero = jnp.zeros_like(q)
    return jnp.concatenate([jnp.where(lane < DK_A, q, zero), jnp.where(lane >= DK_A, q, zero)], axis=0)


def _softmax_update(carry, s, v):
    m, acc = carry
    m_new = jnp.maximum(m, jnp.max(s, axis=-1, keepdims=True))
    p = jnp.exp(s - m_new).astype(BF16)
    v1 = jnp.concatenate([v, jnp.ones_like(v)], axis=1)
    return m_new, jnp.exp(m - m_new) * acc + _dot(p, v1)


def _diff_combine(acc, lam, t):
    dv = acc.shape[1] // 2
    o = acc[:, :dv] / acc[:, dv:]
    return o[:t] - lam * o[t:]


def _lambda(lam_refs, lam_init):
    q1, k1, q2, k2 = (r[...] for r in lam_refs)
    inner = lambda a, b: jnp.sum(a * b, axis=-1, keepdims=True)
    return jnp.exp(inner(q1, k1)) - jnp.exp(inner(q2, k2)) + lam_init


def _lane_rms(o):
    return o * lax.rsqrt(jnp.mean(o * o, axis=-1, keepdims=True) + EPS)


def _retention_stage1(q, k, v, r, dmat, xi, zeta, cdecay):
    s = (_nt_dot(q, k) * dmat).astype(BF16)
    cross = _dot(q, r.astype(BF16)) * xi
    kz = (k.astype(F32) * zeta).astype(BF16)
    r_new = cdecay * r + _tn_dot(kz, v)
    return s, cross, r_new


def _retention_stage2(s, cross, v):
    return _dot(s, v) + cross


def _mixer_prompt_kernel(lq1_ref, lk1_ref, lq2_ref, lk2_ref, qt_ref, k_ref, vt_ref, mix_ref,
                         dbias_ref, kpos_ref, kslope_ref, dmat_ref, xi_ref, zeta_ref, sg_ref,
                         x_ref, gate_ref, woa_ref, wob_ref, wout_ref,
                         y_ref, rout_ref, r_scr, acc_scr, oa_scr, ob_scr, merged_scr, *s_scrs,
                         tq, kblk, nb, nq, n_tiles, slopes, cdecay, lam_init):
    chains = [(bi, j) for bi in range(nb) for j in range(H_A)]
    hps = len(chains)
    nsub = tq // kblk
    t = pl.program_id(0)
    live = t < n_tiles
    qi = lax.rem(jnp.minimum(t, n_tiles - 1), nq)
    lam = _lambda((lq1_ref, lk1_ref, lq2_ref, lk2_ref), lam_init)
    lanes = lambda j: slice(j * HEAD_W, (j + 1) * HEAD_W)

    def split_halves_t(qt):
        feat = lax.broadcasted_iota(jnp.int32, qt.shape, 0)
        zero = jnp.zeros_like(qt)
        return jnp.concatenate([jnp.where(feat < DK_A, qt, zero), jnp.where(feat >= DK_A, qt, zero)], axis=1)

    qst = [jnp.concatenate([split_halves_t(qt_ref[bi, lanes(j), :]), kslope_ref[j]], axis=0)
           for bi, j in chains]

    head = lambda c: chains[c][1]

    def keys(c, kb):
        bi, j = chains[c]
        k = k_ref[bi, pl.ds(pl.multiple_of(kb * kblk, kblk), kblk), lanes(j)]
        return jnp.concatenate([k, kpos_ref[...]], axis=1)

    def values1(c, kb):
        bi, j = chains[c]
        vt = vt_ref[bi, lanes(j), pl.ds(pl.multiple_of(kb * kblk, kblk), kblk)]
        return jnp.concatenate([vt, jnp.ones((SUM_ROWS, kblk), BF16)], axis=0)

    def stage_scores(c, kb, slot):
        s = _dot(keys(c, kb), qst[c])
        s_scrs[c][slot] = s
        return jnp.max(s, axis=0, keepdims=True)

    def softmax_step(m, s, s_max, shift):
        m_new = jnp.maximum(m, s_max + shift)
        return m_new, jnp.exp2(m - m_new), jnp.exp2(s - (m_new - shift)).astype(BF16)

    def body(kb, carry):
        slot = lax.rem(kb, 2)
        new = []
        for c in range(hps):
            smax_cur, m = carry[c]
            s_cur = s_scrs[c][slot]
            smax_next = stage_scores(c, kb + 1, 1 - slot)
            shift = ((kb - nsub * qi) * kblk).astype(F32) * slopes[head(c)]
            m, alpha, p = softmax_step(m, s_cur, smax_cur, shift)
            acc_scr[c] = alpha * acc_scr[c] + _dot(values1(c, kb), p)
            new.append((smax_next, m))
        return tuple(new)

    @pl.when(t == 0)
    def _():
        oa_scr[...] = jnp.zeros_like(oa_scr)
        ob_scr[...] = jnp.zeros_like(ob_scr)

    @pl.when(qi == 0)
    def _():
        r_scr[...] = jnp.zeros_like(r_scr)

    acc_scr[...] = jnp.zeros_like(acc_scr)
    init = tuple((stage_scores(c, 0, 0), jnp.full((1, 2 * tq), -jnp.inf, F32)) for c in range(hps))
    d_model = x_ref.shape[2]
    pa = _dot(oa_scr[...], woa_ref[...]).astype(BF16)
    pb = _dot(ob_scr[...], wob_ref[...]).astype(BF16)
    for bi in range(nb):
        rows = slice(bi * tq, (bi + 1) * tq)
        merged_scr[rows] = gate_ref[bi, :, :d_model] * pa[rows] + gate_ref[bi, :, d_model:] * pb[rows]
    n_before = nsub * qi
    carry = lax.fori_loop(0, jnp.where(live, n_before, 0), body, init)

    def retention_stage1(sub, state):
        rows = slice(sub * kblk, (sub + 1) * kblk)
        return [_retention_stage1(
            _slab(mix_ref, "qb", j, bi)[rows], _slab(mix_ref, "kb", j, bi)[rows], _slab(mix_ref, "vb", j, bi)[rows],
            state[c], dmat_ref[j], xi_ref[j], zeta_ref[j], cdecay[j])
            for c, (bi, j) in enumerate(chains)]

    def retention_stage2(sub, ret):
        rows = slice(sub * kblk, (sub + 1) * kblk)
        for c, (bi, j) in enumerate(chains):
            ob = _retention_stage2(ret[c][0], ret[c][1], _slab(mix_ref, "vb", j, bi)[rows])
            ob = (_lane_rms(ob) * _slab(mix_ref, "zb", j, bi)[rows].astype(F32)).astype(BF16)
            ob_scr[bi * tq + sub * kblk:bi * tq + (sub + 1) * kblk, lanes(j)] = ob

    def cat(pieces):
        pieces = [p for p in pieces if p.shape[1] > 0]
        return pieces[0] if len(pieces) == 1 else jnp.concatenate(pieces, axis=1)

    def diagonal_block(dsub, c, m):
        kb = n_before + dsub
        w = tq - dsub * kblk
        cols = (slice(dsub * kblk, tq), slice(tq + dsub * kblk, 2 * tq))
        if dsub == 0:
            s = s_scrs[c][lax.rem(n_before, 2)]
        else:
            s = _dot(keys(c, kb), cat([qst[c][:, q] for q in cols]))
        dbias = dbias_ref[head(c)]
        s = cat([s[:, :kblk] + dbias, s[:, kblk:w], s[:, w:w + kblk] + dbias, s[:, w + kblk:]])
        shift = (dsub * kblk) * slopes[head(c)]
        m_part, alpha, p = softmax_step(cat([m[:, q] for q in cols]), s,
                                        jnp.max(s, axis=0, keepdims=True), shift)
        pv = _dot(values1(c, kb), p)
        for h, q in enumerate(cols):
            acc_scr[c, :, q] = alpha[:, h * w:(h + 1) * w] * acc_scr[c, :, q] + pv[:, h * w:(h + 1) * w]
        return cat([m[:, :dsub * kblk], m_part[:, :w], m[:, tq:tq + dsub * kblk], m_part[:, w:]])

    def final_projection(cols):
        proj = _dot(merged_scr[...], wout_ref[:, cols])
        for bi in range(nb):
            y_ref[bi, :, cols] = x_ref[bi, :, cols] + proj[bi * tq:(bi + 1) * tq]

    out_cols = [slice(c, c + MXU_TILE) for c in range(0, d_model, MXU_TILE)]
    every = max(1, (hps * nsub) // len(out_cols))
    state = [r_scr[c] for c in range(hps)]
    ms = [carry[c][1] for c in range(hps)]
    for sub in range(nsub):
        ret = retention_stage1(sub, state)
        state = [r[2] for r in ret]
        for c in range(hps):
            if (sub * hps + c) % every == 0 and out_cols:
                final_projection(out_cols.pop(0))
            ms[c] = diagonal_block(sub, c, ms[c])
        retention_stage2(sub, ret)
    while out_cols:
        final_projection(out_cols.pop(0))
    for c, (bi, j) in enumerate(chains):
        r_new = jnp.where(live, state[c], r_scr[c])
        r_scr[c] = r_new
        rout_ref[bi, j] = r_new
    for c, (bi, j) in enumerate(chains):
        inv_l = 1.0 / acc_scr[c, HEAD_W:HEAD_W + 1]
        inv_l = jnp.concatenate([inv_l[:, :tq], lam * inv_l[:, tq:]], axis=1)
        ot = acc_scr[c, :HEAD_W] * inv_l
        oat = ot[:, :tq] - ot[:, tq:]
        oat = oat * lax.rsqrt(jnp.mean(oat * oat, axis=0, keepdims=True) + EPS)
        oa = (oat.T * sg_ref[...]) * (1.0 - lam_init)
        oa = (oa * _slab(mix_ref, "za", j, bi)[...].astype(F32)).astype(BF16)
        oa_scr[bi * tq:(bi + 1) * tq, lanes(j)] = oa


def _mixer_prompt(lam_vecs, qat, kab, vat, mix, dbias, kpos, kslope, dmat, xi, zeta, sg, x, gates, woa, wob, wout,
                  *, slopes, cdecay, lam_init):
    b, t, w = kab.shape
    d = x.shape[2]
    tq, kblk, nb = Q_TILE, KEY_BLOCK, SEQS_PER_STEP
    assert t % tq == 0 and tq % kblk == 0 and kblk % CHUNK == 0 and H_A == H_B and w == H_A * HEAD_W
    assert dbias.shape[1:] == (kblk, kblk) and d % (tq // kblk) == 0 and b % nb == 0
    nq = t // tq
    n_tiles = (b // nb) * nq
    cur = lambda s: jnp.minimum(s, n_tiles - 1)
    prev = lambda s: jnp.maximum(s - 1, 0)
    tile = pl.BlockSpec((nb, tq, mix.shape[2]), lambda s: (cur(s) // nq, cur(s) % nq, 0))
    tile_t = pl.BlockSpec((nb, w, tq), lambda s: (cur(s) // nq, 0, cur(s) % nq))
    whole = pl.BlockSpec((nb, t, w), lambda s: (cur(s) // nq, 0, 0))
    whole_t = pl.BlockSpec((nb, w, t), lambda s: (cur(s) // nq, 0, 0))
    rows = lambda a: pl.BlockSpec((nb, tq, a.shape[2]), lambda s: (prev(s) // nq, prev(s) % nq, 0))
    full = lambda a: pl.BlockSpec(a.shape, lambda s: (0,) * a.ndim, pipeline_mode=pl.Buffered(1))
    return pl.pallas_call(
        functools.partial(_mixer_prompt_kernel, tq=tq, kblk=kblk, nb=nb, nq=nq, n_tiles=n_tiles,
                          slopes=slopes, cdecay=cdecay, lam_init=lam_init),
        grid=(n_tiles + 1,),
        in_specs=[full(v) for v in lam_vecs] + [
                  tile_t, whole, whole_t, tile,
                  full(dbias), full(kpos), full(kslope), full(dmat), full(xi), full(zeta), full(sg),
                  rows(x), rows(gates), full(woa), full(wob), full(wout)],
        out_specs=[rows(x),
                   pl.BlockSpec((nb, H_B, DK_B, DV_B), lambda s: (cur(s) // nq, 0, 0, 0))],
        out_shape=[jax.ShapeDtypeStruct((b, t, d), F32),
                   jax.ShapeDtypeStruct((b, H_B, DK_B, DV_B), F32)],
        scratch_shapes=[pltpu.VMEM((nb * H_B, DK_B, DV_B), F32),
                        pltpu.VMEM((nb * H_A, HEAD_W + SUM_ROWS, 2 * tq), F32),
                        pltpu.VMEM((nb * tq, w), BF16), pltpu.VMEM((nb * tq, w), BF16),
                        pltpu.VMEM((nb * tq, d), BF16)]
                       + [pltpu.VMEM((2, kblk, 2 * tq), F32)] * (nb * H_A),
        compiler_params=pltpu.CompilerParams(
            dimension_semantics=("arbitrary",), vmem_limit_bytes=VMEM_LIMIT),
        name="mixer_prompt",
    )(*lam_vecs, qat, kab, vat, mix, dbias, kpos, kslope, dmat, xi, zeta, sg, x, gates, woa, wob, wout)


def _mixer_sample_kernel(lq1_ref, lk1_ref, lq2_ref, lk2_ref,
                         qa_ref, kc_ref, vc_ref, kn_ref, vn_ref, mix_ref, r_ref,
                         abias_ref, dmat_ref, xi_ref, zeta_ref, sg_ref,
                         oa_ref, ob_ref, rout_ref, m_scr, acc_scr,
                         *, n, past, tk, sub, slopes, cdecay, lam_init):
    kbi = pl.program_id(1)
    lanes = lambda j: slice(j * HEAD_W, (j + 1) * HEAD_W)

    @pl.when(kbi == 0)
    def _():
        m_scr[...] = jnp.full(m_scr.shape, -jnp.inf, F32)
        acc_scr[...] = jnp.zeros_like(acc_scr)

    col = lax.broadcasted_iota(jnp.int32, (1, sub), 1)
    qs = [_split_halves(qa_ref[0, :, lanes(j)]) for j in range(H_A)]

    def scores(j, c):
        kk = kc_ref[0, lanes(j), c * sub:(c + 1) * sub].astype(BF16)
        kpos = (col + (kbi * tk + (c * sub - past))).astype(F32)
        return _dot(qs[j], kk) + kpos * slopes[j]

    blocks = [(j, c) for c in range(tk // sub) for j in range(H_A)]
    carry = [(m_scr[j], acc_scr[j]) for j in range(H_A)]
    staged = [scores(*blk) for blk in blocks[:SCORE_LOOKAHEAD]]
    for i, (j, c) in enumerate(blocks):
        if i + SCORE_LOOKAHEAD < len(blocks):
            staged.append(scores(*blocks[i + SCORE_LOOKAHEAD]))
        vv = vc_ref[0, pl.ds(c * sub * H_A + j, sub, stride=H_A), :].astype(BF16)
        carry[j] = _softmax_update(carry[j], staged[i], vv)
    for j in range(H_A):
        m_scr[j], acc_scr[j] = carry[j]

    @pl.when(kbi == pl.num_programs(1) - 1)
    def _():
        lam = _lambda((lq1_ref, lk1_ref, lq2_ref, lk2_ref), lam_init)
        s_new, ret = [], []
        for j in range(H_A):
            bias = abias_ref[j]
            s_new.append(_nt_dot(qs[j], kn_ref[0, :, lanes(j)]) + jnp.concatenate([bias, bias], axis=0))
        for j in range(H_A):
            s_ret, cross, r_new = _retention_stage1(
                _slab(mix_ref, "qb", j)[...], _slab(mix_ref, "kb", j)[...], _slab(mix_ref, "vb", j)[...],
                r_ref[0, j], dmat_ref[j], xi_ref[j], zeta_ref[j], cdecay[j])
            rout_ref[0, j] = r_new
            ret.append((s_ret, cross))
        for j in range(H_A):
            _, acc = _softmax_update((m_scr[j], acc_scr[j]), s_new[j], vn_ref[0, :, lanes(j)])
            oa = _diff_combine(acc, lam, n)
            oa = (_lane_rms(oa) * sg_ref[...]) * (1.0 - lam_init)
            oa_ref[0, :, lanes(j)] = (oa * _slab(mix_ref, "za", j)[...].astype(F32)).astype(BF16)
        for j in range(H_A):
            ob = _retention_stage2(*ret[j], _slab(mix_ref, "vb", j)[...])
            ob_ref[0, :, lanes(j)] = (_lane_rms(ob) * _slab(mix_ref, "zb", j)[...].astype(F32)).astype(BF16)


def _mixer_sample(lam_vecs, qa, kc, vc, kab, vab, mix, r0, abias, dmat, xi, zeta, sg, *, slopes, cdecay, lam_init):
    b, n, w = qa.shape
    past = kc.shape[2]
    tk, sub = SAMPLE_KEY_TILE, SAMPLE_KEY_BLOCK
    assert past % tk == 0 and tk % sub == 0 and past % CHUNK == 0 and n <= CHUNK and H_A == H_B
    tile = pl.BlockSpec((1, n, w), lambda bi, ki: (bi, 0, 0))
    mix_tile = pl.BlockSpec((1, n, mix.shape[2]), lambda bi, ki: (bi, 0, 0))
    cache_k = pl.BlockSpec((1, w, tk), lambda bi, ki: (bi, 0, ki))
    cache_v = pl.BlockSpec((1, tk * H_A, HEAD_W), lambda bi, ki: (bi, ki, 0))
    state = pl.BlockSpec((1, H_B, DK_B, DV_B), lambda bi, ki: (bi, 0, 0, 0))
    full = lambda a: pl.BlockSpec(a.shape, lambda bi, ki: (0,) * a.ndim)
    return pl.pallas_call(
        functools.partial(_mixer_sample_kernel, n=n, past=past, tk=tk, sub=sub,
                          slopes=slopes, cdecay=cdecay, lam_init=lam_init),
        grid=(b, past // tk),
        in_specs=[full(v) for v in lam_vecs] + [
                  tile, cache_k, cache_v, tile, tile, mix_tile, state,
                  full(abias), full(dmat), full(xi), full(zeta), full(sg)],
        out_specs=[tile, tile, state],
        out_shape=[jax.ShapeDtypeStruct((b, n, H_A * HEAD_W), BF16),
                   jax.ShapeDtypeStruct((b, n, H_B * HEAD_W), BF16),
                   jax.ShapeDtypeStruct((b, H_B, DK_B, DV_B), F32)],
        scratch_shapes=[pltpu.VMEM((H_A, 2 * n, 1), F32),
                        pltpu.VMEM((H_A, 2 * n, 2 * DV_A), F32)],
        compiler_params=pltpu.CompilerParams(
            dimension_semantics=("arbitrary", "arbitrary"), vmem_limit_bytes=VMEM_LIMIT),
        name="mixer_sample",
    )(*lam_vecs, qa, kc, vc, kab, vab, mix, r0, abias, dmat, xi, zeta, sg)


def _proj_out_kernel(x_ref, oa_ref, ob_ref, gate_ref, woa_ref, wob_ref, wout_ref, y_ref):
    d = x_ref.shape[1]
    m = (gate_ref[:, :d].astype(F32) * _dot(oa_ref[...], woa_ref[...])
         + gate_ref[:, d:].astype(F32) * _dot(ob_ref[...], wob_ref[...]))
    y_ref[...] = x_ref[...] + _dot(m.astype(BF16), wout_ref[...])


def _proj_out(x2, oa, ob, gates, woa, wob, wout):
    n, d = x2.shape
    tm = min(ROW_TILE, n)
    assert n % tm == 0
    row = lambda w: pl.BlockSpec((tm, w), lambda i: (i, 0))
    full = lambda a: pl.BlockSpec(a.shape, lambda i: (0,) * a.ndim)
    return pl.pallas_call(
        _proj_out_kernel,
        grid=(n // tm,),
        in_specs=[row(d), row(oa.shape[1]), row(ob.shape[1]), row(gates.shape[1]),
                  full(woa), full(wob), full(wout)],
        out_specs=row(d),
        out_shape=jax.ShapeDtypeStruct((n, d), F32),
        compiler_params=pltpu.CompilerParams(
            dimension_semantics=("arbitrary",), vmem_limit_bytes=VMEM_LIMIT),
        name="proj_out",
    )(x2, oa, ob, gates, woa, wob, wout)


def _block_tables(t, slopes, log_g):
    f32 = np.float32
    i = np.arange(t, dtype=np.int32)
    allowed = (i[None, :] // CHUNK) <= (i[:, None] // CHUNK)
    dist = np.abs(i[:, None] - i[None, :]).astype(f32)
    jf = i.astype(f32)
    abias = -slopes[:, None, None] * dist[None] + slopes[:, None, None] * jf[None, :, None]
    abias = np.where(allowed[None], abias, f32(-np.inf)).astype(f32)
    dmat = np.where(allowed[None], np.exp(dist[None] * log_g[:, None, None]), f32(0.0)).astype(f32)
    xi = np.exp((jf + f32(1.0))[None] * log_g[:, None]).astype(f32)
    zeta = np.exp((f32(t) - f32(1.0) - jf)[None] * log_g[:, None]).astype(f32)
    bc = lambda a: np.ascontiguousarray(np.broadcast_to(a[:, :, None], a.shape + (HEAD_W,)))
    return abias, dmat, bc(xi), bc(zeta)


N_SLOPE_PARTS = 3


def _alibi_features(t, width, slopes):
    assert t <= 256
    bf16 = jnp.bfloat16
    kpos = np.zeros((t, HEAD_W), np.float32)
    kpos[:, :N_SLOPE_PARTS] = np.arange(t, dtype=np.float32)[:, None]
    kslope = np.zeros((len(slopes), HEAD_W, width), np.float32)
    rest = slopes.astype(np.float32)
    for part in range(N_SLOPE_PARTS):
        piece = rest.astype(bf16).astype(np.float32)
        kslope[:, part, :] = piece[:, None]
        rest = rest - piece
    return kpos.astype(bf16), kslope.astype(bf16)


def _key_bias_table(t, slopes):
    kb = (slopes[:, None] * np.arange(t, dtype=np.float32)[None]).astype(np.float32)
    return np.ascontiguousarray(np.broadcast_to(kb[:, :, None], kb.shape + (HEAD_W,)))


def _layer(x_prompt, x_sample, ck, cv, r0, norm_g, w_in, b_gate, qn_g, kn_g,
           lam_q1, lam_k1, lam_q2, lam_k2, subln_g, w_oa, w_ob, w_out, lam_init):
    bp, tp, d = x_prompt.shape
    bs, ts, _ = x_sample.shape
    past = ck.shape[1]
    slopes = (2.0 ** (-8.0 * np.arange(1, H_A + 1, dtype=np.float32) / H_A)).astype(np.float32)
    log_g = np.log(1.0 - 2.0 ** (-5.0 - np.arange(H_B, dtype=np.float32))).astype(np.float32)
    row = lambda v: v.reshape(1, -1)
    lam_vecs = (row(lam_q1), row(lam_k1), row(lam_q2), row(lam_k2))
    floats = lambda a: tuple(float(v) for v in np.asarray(a, np.float32))
    cdecay = lambda n_block: floats(np.exp(np.float32(n_block) * log_g))

    grp = np.arange(MXU_TILE, dtype=np.int32) // DK_A
    gm = jnp.asarray((grp[:, None] == grp[None, :]) * (1.0 / DK_A), dtype=BF16)
    sg = row(subln_g)

    woa_bf, wob_bf, wout_bf = w_oa.astype(BF16), w_ob.astype(BF16), w_out.astype(BF16)
    prompt_acts, sample_acts, _ = _proj_in(
        x_prompt, x_sample.reshape(1, bs * ts, d), row(norm_g), w_in, b_gate, row(qn_g), row(kn_g), gm,
        (), tm=ROW_TILE, tm_sample=min(SAMPLE_ROW_TILE, bs * ts))

    qat, kaf, kab, vaf, vat, mix, gates = prompt_acts
    abias, dmat, xi, zeta = _block_tables(KEY_BLOCK, slopes, log_g)
    kbias = _key_bias_table(KEY_BLOCK, slopes)
    dbias = np.ascontiguousarray(np.swapaxes(abias, 1, 2) - kbias[:, :, :1])
    flat = lambda a: a.reshape(-1, a.shape[-1])
    log2e = np.float32(LOG2E)
    kpos, kslope = _alibi_features(KEY_BLOCK, 2 * Q_TILE, slopes * log2e)
    y_p, ret_p = _mixer_prompt(lam_vecs, qat, kab, vat, mix,
                               dbias * log2e, kpos, kslope, dmat, xi, zeta, sg,
                               x_prompt, gates, woa_bf, wob_bf, wout_bf,
                               slopes=floats(slopes * log2e), cdecay=cdecay(KEY_BLOCK), lam_init=lam_init)
    k_p = jnp.transpose(kaf.reshape(bp, H_A, 2, DK_A, tp), (0, 4, 1, 2, 3))
    v_p = vaf.reshape(bp, tp, H_A, DV_A)

    qa, kaf, kab, vaf, vab, mix, gates = sample_acts
    r3 = lambda a: a.reshape(bs, ts, a.shape[-1])
    tabs = _block_tables(ts, slopes, log_g)
    ck_t = jnp.transpose(ck, (0, 2, 3, 4, 1)).reshape(bs, H_A * HEAD_W, past)
    oa, ob, ret_s = _mixer_sample(lam_vecs, r3(qa), ck_t, cv.reshape(bs, past * H_A, HEAD_W),
                                  r3(kab), r3(vab), r3(mix), r0, *tabs, sg,
                                  slopes=floats(slopes), cdecay=cdecay(ts), lam_init=lam_init)
    y_s = _proj_out(flat(x_sample), flat(oa), flat(ob), flat(gates),
                    woa_bf, wob_bf, wout_bf).reshape(bs, ts, d)
    k_s = kaf.reshape(bs, ts, H_A, 2, DK_A)
    v_s = vaf.reshape(bs, ts, H_A, DV_A)
    return y_p, y_s, k_p, v_p, ret_p, k_s, v_s, ret_s


def kernel(x_prompt, x_sample, cache_k_diff, cache_v_diff, state_ret, norm_g, w_in, b_gate, qn_g, kn_g,
           lam_q1, lam_k1, lam_q2, lam_k2, subln_g, w_o_diff, w_o_ret, w_out):
    depth = w_in.shape[0]
    hp, hs = x_prompt, x_sample
    outs = [[] for _ in range(6)]
    for l in range(depth):
        lam_init = 0.8 - 0.6 * math.exp(-0.3 * l)
        hp, hs, k_p, v_p, ret_p, k_s, v_s, ret_s = _layer(
            hp, hs, cache_k_diff[l], cache_v_diff[l], state_ret[l], norm_g[l], w_in[l], b_gate[l],
            qn_g[l], kn_g[l], lam_q1[l], lam_k1[l], lam_q2[l], lam_k2[l], subln_g[l],
            w_o_diff[l], w_o_ret[l], w_out[l], lam_init)
        for lst, a in zip(outs, (k_p, v_p, ret_p, k_s, v_s, ret_s)):
            lst.append(a)
    return (hp, hs) + tuple(jnp.stack(o) for o in outs)
```

```python
import functools
import math

import jax
import jax.numpy as jnp
import numpy as np
from jax import lax
from jax.experimental import pallas as pl
from jax.experimental.pallas import tpu as pltpu

F32 = jnp.float32
BF16 = jnp.bfloat16

CHUNK = 64
H_A = 4
DK_A = 64
DV_A = 2 * DK_A
H_B = 4
DK_B = 128
DV_B = 128
HEAD_W = 128
EPS = 1e-6
LOG2E = math.log2(math.e)

MXU_TILE = 256
ROW_TILE = 512
SAMPLE_ROW_TILE = 256
Q_TILE = 256
KEY_BLOCK = 256
SEQS_PER_STEP = 2
SUM_ROWS = 16
SAMPLE_KEY_TILE = 4096
SAMPLE_KEY_BLOCK = 1024
SCORE_LOOKAHEAD = 8
V7X_VMEM_BYTES = 64 * 1024 * 1024
VMEM_LIMIT = V7X_VMEM_BYTES - 3 * 1024 * 1024


def _nt_dot(a, b):
    return lax.dot_general(a, b, (((1,), (1,)), ((), ())), preferred_element_type=F32)


def _tn_dot(a, b):
    return lax.dot_general(a, b, (((0,), (0,)), ((), ())), preferred_element_type=F32)


def _dot(a, b):
    return jnp.dot(a, b, preferred_element_type=F32)


def _sigmoid(x):
    return 1.0 / (1.0 + jnp.exp2(x * (-LOG2E)))


MIX_SLABS = ("za", "qb", "kb", "vb", "zb")


def _slab(mix_ref, name, head=None, seq=0):
    lo = MIX_SLABS.index(name) * H_A * HEAD_W
    if head is None:
        return mix_ref.at[seq, :, lo:lo + H_A * HEAD_W]
    return mix_ref.at[seq, :, lo + head * HEAD_W:lo + (head + 1) * HEAD_W]


W_IN_SEGMENTS = ("qa", "ka", "va", "za", "qb", "kb", "vb", "zb", "ga", "gb")
W_IN_USE_ORDER = ("ga", "gb", "qa", "ka", "va", "za", "zb", "kb", "qb", "vb")


def _w_in_columns(name, wa, wb, d_model):
    widths = (wa, wa, wa, wa, wb, wb, wb, wb, d_model, d_model)
    i = W_IN_SEGMENTS.index(name)
    return slice(sum(widths[:i]), sum(widths[:i + 1]))


def _proj_in_kernel(x_ref, g_ref, w_ref, bg_ref, qg_ref, kg_ref, gm_ref,
                    qa_ref, kaf_ref, kab_ref, vaf_ref, vab_ref, mix_ref, gate_ref,
                    *, wa, wb, d_model, feature_major, q_scale, before_segment=None):
    x = x_ref[0]
    tm = x.shape[0]
    ms = jnp.mean(x * x, axis=-1, keepdims=True)
    xn = (x * lax.rsqrt(ms + EPS)) * g_ref[...]
    used = []

    def seg(name):
        used.append(name)
        if before_segment is not None:
            before_segment(name)
        return _dot(xn, w_ref[:, _w_in_columns(name, wa, wb, d_model)])

    def group_norm(h, gain):
        sq = (h * h).astype(BF16)
        ms = jnp.concatenate([_dot(sq[:, c:c + MXU_TILE], gm_ref[...]) for c in range(0, wa, MXU_TILE)], axis=1)
        return (h * lax.rsqrt(ms + EPS)) * gain

    def silu(h):
        return h * _sigmoid(h)

    maybe_t = (lambda a: a.T) if feature_major else (lambda a: a)
    gate_ref[0, :, :d_model] = _sigmoid(seg("ga") + bg_ref[0:1, :]).astype(BF16)
    gate_ref[0, :, d_model:] = _sigmoid(seg("gb") + bg_ref[1:2, :]).astype(BF16)
    group_gain = lambda gain_ref: jnp.tile(gain_ref[...], (1, wa // DK_A))
    qa_ref[0] = maybe_t(group_norm(seg("qa"), group_gain(qg_ref)) * q_scale).astype(BF16)
    ka = group_norm(seg("ka"), group_gain(kg_ref))
    if feature_major:
        kaf_ref[0] = ka.T
    else:
        groups = wa // DK_A
        for grp in range(groups):
            kaf_ref[pl.ds(grp, tm, stride=groups), :] = ka[:, grp * DK_A:(grp + 1) * DK_A]
    kab_ref[0] = ka.astype(BF16)
    va = seg("va")
    vab_ref[0] = maybe_t(va).astype(BF16)
    for h in range(H_A):
        vaf_ref[pl.ds(h, tm, stride=H_A), :] = va[:, h * HEAD_W:(h + 1) * HEAD_W]
    _slab(mix_ref, "za")[...] = silu(seg("za")).astype(BF16)
    _slab(mix_ref, "zb")[...] = silu(seg("zb")).astype(BF16)
    _slab(mix_ref, "kb")[...] = (seg("kb") * (DK_B ** -0.5)).astype(BF16)
    _slab(mix_ref, "qb")[...] = seg("qb").astype(BF16)
    _slab(mix_ref, "vb")[...] = seg("vb").astype(BF16)
    assert tuple(used) == W_IN_USE_ORDER


N_PROJ_OUTS = 7


def _proj_in_both_kernel(xp_ref, xs_ref, g_ref, w_hbm_ref, bg_ref, qg_ref, kg_ref, gm_ref, *refs,
                         n_prompt_steps, wa, wb, d_model):
    out_refs, (w_ref, w_sem) = refs[:-2], refs[-2:]
    step = pl.program_id(0)
    consts = (g_ref, w_ref, bg_ref, qg_ref, kg_ref, gm_ref)
    prompt_body = functools.partial(
        _proj_in_kernel, xp_ref, *consts, *out_refs[:N_PROJ_OUTS], wa=wa, wb=wb, d_model=d_model,
        feature_major=True, q_scale=DK_A ** -0.5 * LOG2E)

    def segment_copy(name):
        cols = _w_in_columns(name, wa, wb, d_model)
        return pltpu.make_async_copy(w_hbm_ref.at[:, cols], w_ref.at[:, cols],
                                     w_sem.at[W_IN_USE_ORDER.index(name)])

    @pl.when(step == 0)
    def _():
        for name in W_IN_USE_ORDER:
            segment_copy(name).start()
        prompt_body(before_segment=lambda name: segment_copy(name).wait())

    @pl.when((step > 0) & (step < n_prompt_steps))
    def _():
        prompt_body()

    @pl.when(step >= n_prompt_steps)
    def _():
        _proj_in_kernel(xs_ref, *consts, *out_refs[N_PROJ_OUTS:],
                        wa=wa, wb=wb, d_model=d_model, feature_major=False, q_scale=DK_A ** -0.5)


def _proj_in(x_prompt, x_sample, g, w_in, bg, qg, kg, gm, *, tm, tm_sample):
    b, t, d = x_prompt.shape
    n = x_sample.shape[1]
    wa = H_A * HEAD_W
    wb = H_B * HEAD_W
    assert t % tm == 0 and n % tm_sample == 0 and wa == wb and x_sample.shape[0] == 1
    nt = t // tm
    n_p, n_s = b * nt, n // tm_sample
    p_tile = lambda s: jnp.minimum(s, n_p - 1)
    s_tile = lambda s: jnp.clip(s - n_p, 0, n_s - 1)
    full = lambda a: pl.BlockSpec(a.shape, lambda s: (0,) * a.ndim)

    def group_specs(bsz, length, rows, tile, feature_major):
        nt_g = length // rows
        row = lambda w: pl.BlockSpec((1, rows, w), lambda s: (tile(s) // nt_g, tile(s) % nt_g, 0))
        act = lambda w, dt: jax.ShapeDtypeStruct((bsz, length, w), dt)
        grouped = lambda groups, lanes: jax.ShapeDtypeStruct((bsz * length * groups, lanes), F32)
        grouped_spec = lambda groups, lanes: pl.BlockSpec((rows * groups, lanes), lambda s: (tile(s), 0))
        if feature_major:
            fm_shape = lambda dt: jax.ShapeDtypeStruct((bsz, wa, length), dt)
            fm_spec = pl.BlockSpec((1, wa, rows), lambda s: (tile(s) // nt_g, 0, tile(s) % nt_g))
            kf_shape, kf_spec = fm_shape(F32), fm_spec
        else:
            fm_shape, fm_spec = (lambda dt: act(wa, dt)), row(wa)
            kf_shape, kf_spec = grouped(wa // DK_A, DK_A), grouped_spec(wa // DK_A, DK_A)
        shapes = [fm_shape(BF16),
                  kf_shape, act(wa, BF16),
                  grouped(H_A, HEAD_W),
                  fm_shape(BF16),
                  act(len(MIX_SLABS) * wa, BF16),
                  act(2 * d, BF16)]
        specs = [fm_spec, kf_spec, row(wa), grouped_spec(H_A, HEAD_W),
                 fm_spec, row(len(MIX_SLABS) * wa), row(2 * d)]
        return row(d), shapes, specs

    xp_spec, p_shapes, p_specs = group_specs(b, t, tm, p_tile, True)
    xs_spec, s_shapes, s_specs = group_specs(1, n, tm_sample, s_tile, False)
    assert len(p_shapes) == N_PROJ_OUTS
    outs = pl.pallas_call(
        functools.partial(_proj_in_both_kernel, n_prompt_steps=n_p, wa=wa, wb=wb, d_model=d),
        grid=(n_p + n_s,),
        in_specs=[xp_spec, xs_spec, full(g), pl.BlockSpec(memory_space=pl.ANY),
                  full(bg), full(qg), full(kg), full(gm)],
        out_specs=p_specs + s_specs,
        out_shape=p_shapes + s_shapes,
        scratch_shapes=[pltpu.VMEM(w_in.shape, w_in.dtype),
                        pltpu.SemaphoreType.DMA((len(W_IN_USE_ORDER),))],
        compiler_params=pltpu.CompilerParams(
            dimension_semantics=("arbitrary",), vmem_limit_bytes=VMEM_LIMIT),
        name="proj_in",
    )(x_prompt, x_sample, g, w_in, bg, qg, kg, gm)
    return outs[:N_PROJ_OUTS], outs[N_PROJ_OUTS:]


def _split_halves(q):
    lane = lax.broadcasted_iota(jnp.int32, q.shape, 1)
    zero = jnp.zeros_like(q)
    return jnp.concatenate([jnp.where(lane < DK_A, q, zero), jnp.where(lane >= DK_A, q, zero)], axis=0)


def _softmax_update(carry, s, v):
    m, acc = carry
    m_new = jnp.maximum(m, jnp.max(s, axis=-1, keepdims=True))
    p = jnp.exp(s - m_new).astype(BF16)
    v1 = jnp.concatenate([v, jnp.ones_like(v)], axis=1)
    return m_new, jnp.exp(m - m_new) * acc + _dot(p, v1)


def _diff_combine(acc, lam, t):
    dv = acc.shape[1] // 2
    o = acc[:, :dv] / acc[:, dv:]
    return o[:t] - lam * o[t:]


def _lambda(lam_refs, lam_init):
    q1, k1, q2, k2 = (r[...] for r in lam_refs)
    inner = lambda a, b: jnp.sum(a * b, axis=-1, keepdims=True)
    return jnp.exp(inner(q1, k1)) - jnp.exp(inner(q2, k2)) + lam_init


def _lane_rms(o):
    return o * lax.rsqrt(jnp.mean(o * o, axis=-1, keepdims=True) + EPS)


def _retention_stage1(q, k, v, r, dmat, xi, zeta, cdecay):
    s = (_nt_dot(q, k) * dmat).astype(BF16)
    cross = _dot(q, r.astype(BF16)) * xi
    kz = (k.astype(F32) * zeta).astype(BF16)
    r_new = cdecay * r + _tn_dot(kz, v)
    return s, cross, r_new


def _retention_stage2(s, cross, v):
    return _dot(s, v) + cross


def _branch_projections(oa_scr, ob_scr, gate_ref, woa_ref, wob_ref, merged_scr, nb, tq):
    d_model = woa_ref.shape[1]
    pa = _dot(oa_scr[...], woa_ref[...]).astype(BF16)
    pb = _dot(ob_scr[...], wob_ref[...]).astype(BF16)
    for bi in range(nb):
        rows = slice(bi * tq, (bi + 1) * tq)
        merged_scr[rows] = gate_ref[bi, :, :d_model] * pa[rows] + gate_ref[bi, :, d_model:] * pb[rows]


def _final_projection(merged_scr, wout_ref, x_ref, y_ref, cols, nb, tq):
    proj = _dot(merged_scr[...], wout_ref[:, cols])
    for bi in range(nb):
        y_ref[bi, :, cols] = x_ref[bi, :, cols] + proj[bi * tq:(bi + 1) * tq]


def _mixer_prompt_kernel(*refs, tq, nb, n_tiles, **params):
    t = pl.program_id(0)

    @pl.when(t < n_tiles)
    def _():
        _mixer_prompt_tile(*refs, tq=tq, nb=nb, **params)

    @pl.when(t == n_tiles)
    def _():
        (x_ref, gate_ref, woa_ref, wob_ref, wout_ref, y_ref, _, _, _,
         oa_scr, ob_scr, merged_scr) = refs[N_MIXER_TILE_INPUTS:N_MIXER_TILE_INPUTS + 12]
        _branch_projections(oa_scr, ob_scr, gate_ref, woa_ref, wob_ref, merged_scr, nb, tq)
        _final_projection(merged_scr, wout_ref, x_ref, y_ref, slice(None), nb, tq)


N_MIXER_TILE_INPUTS = 15


def _mixer_prompt_tile(lq1_ref, lk1_ref, lq2_ref, lk2_ref, qt_ref, k_ref, vt_ref, mix_ref,
                       dbias_ref, kpos_ref, kslope_ref, dmat_ref, xi_ref, zeta_ref, sg_ref,
                       x_ref, gate_ref, woa_ref, wob_ref, wout_ref,
                       y_ref, rout_ref, r_scr, acc_scr, oa_scr, ob_scr, merged_scr, *s_scrs,
                       tq, kblk, nb, nq, slopes, cdecay, lam_init):
    chains = [(bi, j) for bi in range(nb) for j in range(H_A)]
    hps = len(chains)
    nsub = tq // kblk
    t = pl.program_id(0)
    qi = lax.rem(t, nq)
    lam = _lambda((lq1_ref, lk1_ref, lq2_ref, lk2_ref), lam_init)
    lanes = lambda j: slice(j * HEAD_W, (j + 1) * HEAD_W)

    def split_halves_t(qt):
        feat = lax.broadcasted_iota(jnp.int32, qt.shape, 0)
        zero = jnp.zeros_like(qt)
        return jnp.concatenate([jnp.where(feat < DK_A, qt, zero), jnp.where(feat >= DK_A, qt, zero)], axis=1)

    qst = [jnp.concatenate([split_halves_t(qt_ref[bi, lanes(j), :]), kslope_ref[j]], axis=0)
           for bi, j in chains]

    head = lambda c: chains[c][1]

    def keys(c, kb):
        bi, j = chains[c]
        k = k_ref[bi, pl.ds(pl.multiple_of(kb * kblk, kblk), kblk), lanes(j)]
        return jnp.concatenate([k, kpos_ref[...]], axis=1)

    def values1(c, kb):
        bi, j = chains[c]
        vt = vt_ref[bi, lanes(j), pl.ds(pl.multiple_of(kb * kblk, kblk), kblk)]
        return jnp.concatenate([vt, jnp.ones((SUM_ROWS, kblk), BF16)], axis=0)

    def stage_scores(c, kb, slot):
        s = _dot(keys(c, kb), qst[c])
        s_scrs[c][slot] = s
        return jnp.max(s, axis=0, keepdims=True)

    def softmax_step(m, s, s_max, shift):
        m_new = jnp.maximum(m, s_max + shift)
        return m_new, jnp.exp2(m - m_new), jnp.exp2(s - (m_new - shift)).astype(BF16)

    def body(kb, carry):
        slot = lax.rem(kb, 2)
        new = []
        for c in range(hps):
            smax_cur, m = carry[c]
            s_cur = s_scrs[c][slot]
            smax_next = stage_scores(c, kb + 1, 1 - slot)
            shift = ((kb - nsub * qi) * kblk).astype(F32) * slopes[head(c)]
            m, alpha, p = softmax_step(m, s_cur, smax_cur, shift)
            acc_scr[c] = alpha * acc_scr[c] + _dot(values1(c, kb), p)
            new.append((smax_next, m))
        return tuple(new)

    @pl.when(t == 0)
    def _():
        oa_scr[...] = jnp.zeros_like(oa_scr)
        ob_scr[...] = jnp.zeros_like(ob_scr)

    @pl.when(qi == 0)
    def _():
        r_scr[...] = jnp.zeros_like(r_scr)

    acc_scr[...] = jnp.zeros_like(acc_scr)
    init = tuple((stage_scores(c, 0, 0), jnp.full((1, 2 * tq), -jnp.inf, F32)) for c in range(hps))
    _branch_projections(oa_scr, ob_scr, gate_ref, woa_ref, wob_ref, merged_scr, nb, tq)
    n_before = nsub * qi
    carry = lax.fori_loop(0, n_before, body, init)

    def retention_stage1(sub, state):
        rows = slice(sub * kblk, (sub + 1) * kblk)
        return [_retention_stage1(
            _slab(mix_ref, "qb", j, bi)[rows], _slab(mix_ref, "kb", j, bi)[rows], _slab(mix_ref, "vb", j, bi)[rows],
            state[c], dmat_ref[j], xi_ref[j], zeta_ref[j], cdecay[j])
            for c, (bi, j) in enumerate(chains)]

    def retention_stage2(sub, ret):
        rows = slice(sub * kblk, (sub + 1) * kblk)
        for c, (bi, j) in enumerate(chains):
            ob = _retention_stage2(ret[c][0], ret[c][1], _slab(mix_ref, "vb", j, bi)[rows])
            ob = (_lane_rms(ob) * _slab(mix_ref, "zb", j, bi)[rows].astype(F32)).astype(BF16)
            ob_scr[bi * tq + sub * kblk:bi * tq + (sub + 1) * kblk, lanes(j)] = ob

    def cat(pieces):
        pieces = [p for p in pieces if p.shape[1] > 0]
        return pieces[0] if len(pieces) == 1 else jnp.concatenate(pieces, axis=1)

    def diagonal_block(dsub, c, m):
        kb = n_before + dsub
        w = tq - dsub * kblk
        cols = (slice(dsub * kblk, tq), slice(tq + dsub * kblk, 2 * tq))
        if dsub == 0:
            s = s_scrs[c][lax.rem(n_before, 2)]
        else:
            s = _dot(keys(c, kb), cat([qst[c][:, q] for q in cols]))
        dbias = dbias_ref[head(c)]
        s = cat([s[:, :kblk] + dbias, s[:, kblk:w], s[:, w:w + kblk] + dbias, s[:, w + kblk:]])
        shift = (dsub * kblk) * slopes[head(c)]
        m_part, alpha, p = softmax_step(cat([m[:, q] for q in cols]), s,
                                        jnp.max(s, axis=0, keepdims=True), shift)
        pv = _dot(values1(c, kb), p)
        for h, q in enumerate(cols):
            acc_scr[c, :, q] = alpha[:, h * w:(h + 1) * w] * acc_scr[c, :, q] + pv[:, h * w:(h + 1) * w]
        return cat([m[:, :dsub * kblk], m_part[:, :w], m[:, tq:tq + dsub * kblk], m_part[:, w:]])

    final_projection = functools.partial(_final_projection, merged_scr, wout_ref, x_ref, y_ref, nb=nb, tq=tq)

    out_cols = [slice(c, c + MXU_TILE) for c in range(0, x_ref.shape[2], MXU_TILE)]
    every = max(1, (hps * nsub) // len(out_cols))
    state = [r_scr[c] for c in range(hps)]
    ms = [carry[c][1] for c in range(hps)]
    for sub in range(nsub):
        ret = retention_stage1(sub, state)
        state = [r[2] for r in ret]
        for c in range(hps):
            if (sub * hps + c) % every == 0 and out_cols:
                final_projection(out_cols.pop(0))
            ms[c] = diagonal_block(sub, c, ms[c])
        retention_stage2(sub, ret)
    while out_cols:
        final_projection(out_cols.pop(0))
    for c, (bi, j) in enumerate(chains):
        r_scr[c] = state[c]
        rout_ref[bi, j] = state[c]
    for c, (bi, j) in enumerate(chains):
        inv_l = 1.0 / acc_scr[c, HEAD_W:HEAD_W + 1]
        inv_l = jnp.concatenate([inv_l[:, :tq], lam * inv_l[:, tq:]], axis=1)
        ot = acc_scr[c, :HEAD_W] * inv_l
        oat = ot[:, :tq] - ot[:, tq:]
        oat = oat * lax.rsqrt(jnp.mean(oat * oat, axis=0, keepdims=True) + EPS)
        oa = (oat.T * sg_ref[...]) * (1.0 - lam_init)
        oa = (oa * _slab(mix_ref, "za", j, bi)[...].astype(F32)).astype(BF16)
        oa_scr[bi * tq:(bi + 1) * tq, lanes(j)] = oa


def _mixer_prompt(lam_vecs, qat, kab, vat, mix, dbias, kpos, kslope, dmat, xi, zeta, sg, x, gates, woa, wob, wout,
                  *, slopes, cdecay, lam_init):
    b, t, w = kab.shape
    d = x.shape[2]
    tq, kblk, nb = Q_TILE, KEY_BLOCK, SEQS_PER_STEP
    assert t % tq == 0 and tq % kblk == 0 and kblk % CHUNK == 0 and H_A == H_B and w == H_A * HEAD_W
    assert dbias.shape[1:] == (kblk, kblk) and d % (tq // kblk) == 0 and b % nb == 0
    nq = t // tq
    n_tiles = (b // nb) * nq
    cur = lambda s: jnp.minimum(s, n_tiles - 1)
    prev = lambda s: jnp.maximum(s - 1, 0)
    tile = pl.BlockSpec((nb, tq, mix.shape[2]), lambda s: (cur(s) // nq, cur(s) % nq, 0))
    tile_t = pl.BlockSpec((nb, w, tq), lambda s: (cur(s) // nq, 0, cur(s) % nq))
    whole = pl.BlockSpec((nb, t, w), lambda s: (cur(s) // nq, 0, 0))
    whole_t = pl.BlockSpec((nb, w, t), lambda s: (cur(s) // nq, 0, 0))
    rows = lambda a: pl.BlockSpec((nb, tq, a.shape[2]), lambda s: (prev(s) // nq, prev(s) % nq, 0))
    full = lambda a: pl.BlockSpec(a.shape, lambda s: (0,) * a.ndim, pipeline_mode=pl.Buffered(1))
    return pl.pallas_call(
        functools.partial(_mixer_prompt_kernel, tq=tq, kblk=kblk, nb=nb, nq=nq, n_tiles=n_tiles,
                          slopes=slopes, cdecay=cdecay, lam_init=lam_init),
        grid=(n_tiles + 1,),
        in_specs=[full(v) for v in lam_vecs] + [
                  tile_t, whole, whole_t, tile,
                  full(dbias), full(kpos), full(kslope), full(dmat), full(xi), full(zeta), full(sg),
                  rows(x), rows(gates), full(woa), full(wob), full(wout)],
        out_specs=[rows(x),
                   pl.BlockSpec((nb, H_B, DK_B, DV_B), lambda s: (cur(s) // nq, 0, 0, 0))],
        out_shape=[jax.ShapeDtypeStruct((b, t, d), F32),
                   jax.ShapeDtypeStruct((b, H_B, DK_B, DV_B), F32)],
        scratch_shapes=[pltpu.VMEM((nb * H_B, DK_B, DV_B), F32),
                        pltpu.VMEM((nb * H_A, HEAD_W + SUM_ROWS, 2 * tq), F32),
                        pltpu.VMEM((nb * tq, w), BF16), pltpu.VMEM((nb * tq, w), BF16),
                        pltpu.VMEM((nb * tq, d), BF16)]
                       + [pltpu.VMEM((2, kblk, 2 * tq), F32)] * (nb * H_A),
        compiler_params=pltpu.CompilerParams(
            dimension_semantics=("arbitrary",), vmem_limit_bytes=VMEM_LIMIT),
        name="mixer_prompt",
    )(*lam_vecs, qat, kab, vat, mix, dbias, kpos, kslope, dmat, xi, zeta, sg, x, gates, woa, wob, wout)


def _mixer_sample_kernel(lq1_ref, lk1_ref, lq2_ref, lk2_ref,
                         qa_ref, kc_ref, vc_ref, kn_ref, vn_ref, mix_ref, r_ref,
                         abias_ref, dmat_ref, xi_ref, zeta_ref, sg_ref,
                         oa_ref, ob_ref, rout_ref, m_scr, acc_scr,
                         *, n, past, tk, sub, slopes, cdecay, lam_init):
    kbi = pl.program_id(1)
    lanes = lambda j: slice(j * HEAD_W, (j + 1) * HEAD_W)

    @pl.when(kbi == 0)
    def _():
        m_scr[...] = jnp.full(m_scr.shape, -jnp.inf, F32)
        acc_scr[...] = jnp.zeros_like(acc_scr)

    col = lax.broadcasted_iota(jnp.int32, (1, sub), 1)
    qs = [_split_halves(qa_ref[0, :, lanes(j)]) for j in range(H_A)]

    def scores(j, c):
        kk = kc_ref[0, lanes(j), c * sub:(c + 1) * sub].astype(BF16)
        kpos = (col + (kbi * tk + (c * sub - past))).astype(F32)
        return _dot(qs[j], kk) + kpos * slopes[j]

    blocks = [(j, c) for c in range(tk // sub) for j in range(H_A)]
    carry = [(m_scr[j], acc_scr[j]) for j in range(H_A)]
    staged = [scores(*blk) for blk in blocks[:SCORE_LOOKAHEAD]]
    for i, (j, c) in enumerate(blocks):
        if i + SCORE_LOOKAHEAD < len(blocks):
            staged.append(scores(*blocks[i + SCORE_LOOKAHEAD]))
        vv = vc_ref[0, pl.ds(c * sub * H_A + j, sub, stride=H_A), :].astype(BF16)
        carry[j] = _softmax_update(carry[j], staged[i], vv)
    for j in range(H_A):
        m_scr[j], acc_scr[j] = carry[j]

    @pl.when(kbi == pl.num_programs(1) - 1)
    def _():
        lam = _lambda((lq1_ref, lk1_ref, lq2_ref, lk2_ref), lam_init)
        s_new, ret = [], []
        for j in range(H_A):
            bias = abias_ref[j]
            s_new.append(_nt_dot(qs[j], kn_ref[0, :, lanes(j)]) + jnp.concatenate([bias, bias], axis=0))
        for j in range(H_A):
            s_ret, cross, r_new = _retention_stage1(
                _slab(mix_ref, "qb", j)[...], _slab(mix_ref, "kb", j)[...], _slab(mix_ref, "vb", j)[...],
                r_ref[0, j], dmat_ref[j], xi_ref[j], zeta_ref[j], cdecay[j])
            rout_ref[0, j] = r_new
            ret.append((s_ret, cross))
        for j in range(H_A):
            _, acc = _softmax_update((m_scr[j], acc_scr[j]), s_new[j], vn_ref[0, :, lanes(j)])
            oa = _diff_combine(acc, lam, n)
            oa = (_lane_rms(oa) * sg_ref[...]) * (1.0 - lam_init)
            oa_ref[0, :, lanes(j)] = (oa * _slab(mix_ref, "za", j)[...].astype(F32)).astype(BF16)
        for j in range(H_A):
            ob = _retention_stage2(*ret[j], _slab(mix_ref, "vb", j)[...])
            ob_ref[0, :, lanes(j)] = (_lane_rms(ob) * _slab(mix_ref, "zb", j)[...].astype(F32)).astype(BF16)


def _mixer_sample(lam_vecs, qa, kc, vc, kab, vab, mix, r0, abias, dmat, xi, zeta, sg, *, slopes, cdecay, lam_init):
    b, n, w = qa.shape
    past = kc.shape[2]
    tk, sub = SAMPLE_KEY_TILE, SAMPLE_KEY_BLOCK
    assert past % tk == 0 and tk % sub == 0 and past % CHUNK == 0 and n <= CHUNK and H_A == H_B
    tile = pl.BlockSpec((1, n, w), lambda bi, ki: (bi, 0, 0))
    mix_tile = pl.BlockSpec((1, n, mix.shape[2]), lambda bi, ki: (bi, 0, 0))
    cache_k = pl.BlockSpec((1, w, tk), lambda bi, ki: (bi, 0, ki))
    cache_v = pl.BlockSpec((1, tk * H_A, HEAD_W), lambda bi, ki: (bi, ki, 0))
    state = pl.BlockSpec((1, H_B, DK_B, DV_B), lambda bi, ki: (bi, 0, 0, 0))
    full = lambda a: pl.BlockSpec(a.shape, lambda bi, ki: (0,) * a.ndim)
    return pl.pallas_call(
        functools.partial(_mixer_sample_kernel, n=n, past=past, tk=tk, sub=sub,
                          slopes=slopes, cdecay=cdecay, lam_init=lam_init),
        grid=(b, past // tk),
        in_specs=[full(v) for v in lam_vecs] + [
                  tile, cache_k, cache_v, tile, tile, mix_tile, state,
                  full(abias), full(dmat), full(xi), full(zeta), full(sg)],
        out_specs=[tile, tile, state],
        out_shape=[jax.ShapeDtypeStruct((b, n, H_A * HEAD_W), BF16),
                   jax.ShapeDtypeStruct((b, n, H_B * HEAD_W), BF16),
                   jax.ShapeDtypeStruct((b, H_B, DK_B, DV_B), F32)],
        scratch_shapes=[pltpu.VMEM((H_A, 2 * n, 1), F32),
                        pltpu.VMEM((H_A, 2 * n, 2 * DV_A), F32)],
        compiler_params=pltpu.CompilerParams(
            dimension_semantics=("arbitrary", "arbitrary"), vmem_limit_bytes=VMEM_LIMIT),
        name="mixer_sample",
    )(*lam_vecs, qa, kc, vc, kab, vab, mix, r0, abias, dmat, xi, zeta, sg)


def _proj_out_kernel(x_ref, oa_ref, ob_ref, gate_ref, woa_ref, wob_ref, wout_ref, y_ref):
    d = x_ref.shape[1]
    m = (gate_ref[:, :d].astype(F32) * _dot(oa_ref[...], woa_ref[...])
         + gate_ref[:, d:].astype(F32) * _dot(ob_ref[...], wob_ref[...]))
    y_ref[...] = x_ref[...] + _dot(m.astype(BF16), wout_ref[...])


def _proj_out(x2, oa, ob, gates, woa, wob, wout):
    n, d = x2.shape
    tm = min(ROW_TILE, n)
    assert n % tm == 0
    row = lambda w: pl.BlockSpec((tm, w), lambda i: (i, 0))
    full = lambda a: pl.BlockSpec(a.shape, lambda i: (0,) * a.ndim)
    return pl.pallas_call(
        _proj_out_kernel,
        grid=(n // tm,),
        in_specs=[row(d), row(oa.shape[1]), row(ob.shape[1]), row(gates.shape[1]),
                  full(woa), full(wob), full(wout)],
        out_specs=row(d),
        out_shape=jax.ShapeDtypeStruct((n, d), F32),
        compiler_params=pltpu.CompilerParams(
            dimension_semantics=("arbitrary",), vmem_limit_bytes=VMEM_LIMIT),
        name="proj_out",
    )(x2, oa, ob, gates, woa, wob, wout)


def _block_tables(t, slopes, log_g):
    f32 = np.float32
    i = np.arange(t, dtype=np.int32)
    allowed = (i[None, :] // CHUNK) <= (i[:, None] // CHUNK)
    dist = np.abs(i[:, None] - i[None, :]).astype(f32)
    jf = i.astype(f32)
    abias = -slopes[:, None, None] * dist[None] + slopes[:, None, None] * jf[None, :, None]
    abias = np.where(allowed[None], abias, f32(-np.inf)).astype(f32)
    dmat = np.where(allowed[None], np.exp(dist[None] * log_g[:, None, None]), f32(0.0)).astype(f32)
    xi = np.exp((jf + f32(1.0))[None] * log_g[:, None]).astype(f32)
    zeta = np.exp((f32(t) - f32(1.0) - jf)[None] * log_g[:, None]).astype(f32)
    bc = lambda a: np.ascontiguousarray(np.broadcast_to(a[:, :, None], a.shape + (HEAD_W,)))
    return abias, dmat, bc(xi), bc(zeta)


N_SLOPE_PARTS = 3


def _alibi_features(t, width, slopes):
    assert t <= 256
    bf16 = jnp.bfloat16
    kpos = np.zeros((t, HEAD_W), np.float32)
    kpos[:, :N_SLOPE_PARTS] = np.arange(t, dtype=np.float32)[:, None]
    kslope = np.zeros((len(slopes), HEAD_W, width), np.float32)
    rest = slopes.astype(np.float32)
    for part in range(N_SLOPE_PARTS):
        piece = rest.astype(bf16).astype(np.float32)
        kslope[:, part, :] = piece[:, None]
        rest = rest - piece
    return kpos.astype(bf16), kslope.astype(bf16)


def _key_bias_table(t, slopes):
    kb = (slopes[:, None] * np.arange(t, dtype=np.float32)[None]).astype(np.float32)
    return np.ascontiguousarray(np.broadcast_to(kb[:, :, None], kb.shape + (HEAD_W,)))


def _layer(x_prompt, x_sample, ck, cv, r0, norm_g, w_in, b_gate, qn_g, kn_g,
           lam_q1, lam_k1, lam_q2, lam_k2, subln_g, w_oa, w_ob, w_out, lam_init):
    bp, tp, d = x_prompt.shape
    bs, ts, _ = x_sample.shape
    past = ck.shape[1]
    slopes = (2.0 ** (-8.0 * np.arange(1, H_A + 1, dtype=np.float32) / H_A)).astype(np.float32)
    log_g = np.log(1.0 - 2.0 ** (-5.0 - np.arange(H_B, dtype=np.float32))).astype(np.float32)
    row = lambda v: v.reshape(1, -1)
    lam_vecs = (row(lam_q1), row(lam_k1), row(lam_q2), row(lam_k2))
    floats = lambda a: tuple(float(v) for v in np.asarray(a, np.float32))
    cdecay = lambda n_block: floats(np.exp(np.float32(n_block) * log_g))

    grp = np.arange(MXU_TILE, dtype=np.int32) // DK_A
    gm = jnp.asarray((grp[:, None] == grp[None, :]) * (1.0 / DK_A), dtype=BF16)
    sg = row(subln_g)

    woa_bf, wob_bf, wout_bf = w_oa.astype(BF16), w_ob.astype(BF16), w_out.astype(BF16)
    prompt_acts, sample_acts = _proj_in(
        x_prompt, x_sample.reshape(1, bs * ts, d), row(norm_g), w_in, b_gate, row(qn_g), row(kn_g), gm,
        tm=ROW_TILE, tm_sample=min(SAMPLE_ROW_TILE, bs * ts))

    qat, kaf, kab, vaf, vat, mix, gates = prompt_acts
    abias, dmat, xi, zeta = _block_tables(KEY_BLOCK, slopes, log_g)
    kbias = _key_bias_table(KEY_BLOCK, slopes)
    dbias = np.ascontiguousarray(np.swapaxes(abias, 1, 2) - kbias[:, :, :1])
    flat = lambda a: a.reshape(-1, a.shape[-1])
    log2e = np.float32(LOG2E)
    kpos, kslope = _alibi_features(KEY_BLOCK, 2 * Q_TILE, slopes * log2e)
    y_p, ret_p = _mixer_prompt(lam_vecs, qat, kab, vat, mix,
                               dbias * log2e, kpos, kslope, dmat, xi, zeta, sg,
                               x_prompt, gates, woa_bf, wob_bf, wout_bf,
                               slopes=floats(slopes * log2e), cdecay=cdecay(KEY_BLOCK), lam_init=lam_init)
    k_p = jnp.transpose(kaf.reshape(bp, H_A, 2, DK_A, tp), (0, 4, 1, 2, 3))
    v_p = vaf.reshape(bp, tp, H_A, DV_A)

    qa, kaf, kab, vaf, vab, mix, gates = sample_acts
    r3 = lambda a: a.reshape(bs, ts, a.shape[-1])
    tabs = _block_tables(ts, slopes, log_g)
    ck_t = jnp.transpose(ck, (0, 2, 3, 4, 1)).reshape(bs, H_A * HEAD_W, past)
    oa, ob, ret_s = _mixer_sample(lam_vecs, r3(qa), ck_t, cv.reshape(bs, past * H_A, HEAD_W),
                                  r3(kab), r3(vab), r3(mix), r0, *tabs, sg,
                                  slopes=floats(slopes), cdecay=cdecay(ts), lam_init=lam_init)
    y_s = _proj_out(flat(x_sample), flat(oa), flat(ob), flat(gates),
                    woa_bf, wob_bf, wout_bf).reshape(bs, ts, d)
    k_s = kaf.reshape(bs, ts, H_A, 2, DK_A)
    v_s = vaf.reshape(bs, ts, H_A, DV_A)
    return y_p, y_s, k_p, v_p, ret_p, k_s, v_s, ret_s


def kernel(x_prompt, x_sample, cache_k_diff, cache_v_diff, state_ret, norm_g, w_in, b_gate, qn_g, kn_g,
           lam_q1, lam_k1, lam_q2, lam_k2, subln_g, w_o_diff, w_o_ret, w_out):
    depth = w_in.shape[0]
    hp, hs = x_prompt, x_sample
    outs = [[] for _ in range(6)]
    for l in range(depth):
        lam_init = 0.8 - 0.6 * math.exp(-0.3 * l)
        hp, hs, k_p, v_p, ret_p, k_s, v_s, ret_s = _layer(
            hp, hs, cache_k_diff[l], cache_v_diff[l], state_ret[l], norm_g[l], w_in[l], b_gate[l],
            qn_g[l], kn_g[l], lam_q1[l], lam_k1[l], lam_q2[l], lam_k2[l], subln_g[l],
            w_o_diff[l], w_o_ret[l], w_out[l], lam_init)
        for lst, a in zip(outs, (k_p, v_p, ret_p, k_s, v_s, ret_s)):
            lst.append(a)
    return (hp, hs) + tuple(jnp.stack(o) for o in outs)
```

```python
import functools
import math

import jax
import jax.numpy as jnp
import numpy as np
from jax import lax
from jax.experimental import pallas as pl
from jax.experimental.pallas import tpu as pltpu

F32 = jnp.float32
BF16 = jnp.bfloat16

CHUNK = 64
H_A = 4
DK_A = 64
DV_A = 2 * DK_A
H_B = 4
DK_B = 128
DV_B = 128
HEAD_W = 128
EPS = 1e-6
LOG2E = math.log2(math.e)

MXU_TILE = 256
ROW_TILE = 512
SAMPLE_ROW_TILE = 256
Q_TILE = 256
KEY_BLOCK = 256
SEQS_PER_STEP = 2
SUM_ROWS = 16
SAMPLE_KEY_TILE = 4096
SAMPLE_KEY_BLOCK = 1024
SCORE_LOOKAHEAD = 8
V7X_VMEM_BYTES = 64 * 1024 * 1024
VMEM_LIMIT = V7X_VMEM_BYTES - 3 * 1024 * 1024


def _nt_dot(a, b):
    return lax.dot_general(a, b, (((1,), (1,)), ((), ())), preferred_element_type=F32)


def _tn_dot(a, b):
    return lax.dot_general(a, b, (((0,), (0,)), ((), ())), preferred_element_type=F32)


def _dot(a, b):
    return jnp.dot(a, b, preferred_element_type=F32)


def _sigmoid(x):
    return 1.0 / (1.0 + jnp.exp2(x * (-LOG2E)))


MIX_SLABS = ("za", "qb", "kb", "vb", "zb")


def _slab(mix_ref, name, head=None, seq=0):
    lo = MIX_SLABS.index(name) * H_A * HEAD_W
    if head is None:
        return mix_ref.at[seq, :, lo:lo + H_A * HEAD_W]
    return mix_ref.at[seq, :, lo + head * HEAD_W:lo + (head + 1) * HEAD_W]


W_IN_SEGMENTS = ("qa", "ka", "va", "za", "qb", "kb", "vb", "zb", "ga", "gb")
W_IN_USE_ORDER = ("ga", "gb", "qa", "ka", "va", "za", "zb", "kb", "qb", "vb")


def _w_in_columns(name, wa, wb, d_model):
    widths = (wa, wa, wa, wa, wb, wb, wb, wb, d_model, d_model)
    i = W_IN_SEGMENTS.index(name)
    return slice(sum(widths[:i]), sum(widths[:i + 1]))


def _proj_in_kernel(x_ref, g_ref, w_ref, bg_ref, qg_ref, kg_ref, gm_ref,
                    qa_ref, kaf_ref, kab_ref, vaf_ref, vab_ref, mix_ref, gate_ref,
                    *, wa, wb, d_model, feature_major, q_scale, before_segment=None):
    x = x_ref[0]
    tm = x.shape[0]
    ms = jnp.mean(x * x, axis=-1, keepdims=True)
    xn = (x * lax.rsqrt(ms + EPS)) * g_ref[...]
    used = []

    def seg(name):
        used.append(name)
        if before_segment is not None:
            before_segment(name)
        return _dot(xn, w_ref[:, _w_in_columns(name, wa, wb, d_model)])

    def group_norm(h, gain):
        sq = (h * h).astype(BF16)
        ms = jnp.concatenate([_dot(sq[:, c:c + MXU_TILE], gm_ref[...]) for c in range(0, wa, MXU_TILE)], axis=1)
        return (h * lax.rsqrt(ms + EPS)) * gain

    def silu(h):
        return h * _sigmoid(h)

    maybe_t = (lambda a: a.T) if feature_major else (lambda a: a)
    gate_ref[0, :, :d_model] = _sigmoid(seg("ga") + bg_ref[0:1, :]).astype(BF16)
    gate_ref[0, :, d_model:] = _sigmoid(seg("gb") + bg_ref[1:2, :]).astype(BF16)
    group_gain = lambda gain_ref: jnp.tile(gain_ref[...], (1, wa // DK_A))
    qa_ref[0] = maybe_t(group_norm(seg("qa"), group_gain(qg_ref)) * q_scale).astype(BF16)
    ka = group_norm(seg("ka"), group_gain(kg_ref))
    if feature_major:
        kaf_ref[0] = ka.T
    else:
        groups = wa // DK_A
        for grp in range(groups):
            kaf_ref[pl.ds(grp, tm, stride=groups), :] = ka[:, grp * DK_A:(grp + 1) * DK_A]
    kab_ref[0] = ka.astype(BF16)
    va = seg("va")
    vab_ref[0] = maybe_t(va).astype(BF16)
    for h in range(H_A):
        vaf_ref[pl.ds(h, tm, stride=H_A), :] = va[:, h * HEAD_W:(h + 1) * HEAD_W]
    _slab(mix_ref, "za")[...] = silu(seg("za")).astype(BF16)
    _slab(mix_ref, "zb")[...] = silu(seg("zb")).astype(BF16)
    _slab(mix_ref, "kb")[...] = (seg("kb") * (DK_B ** -0.5)).astype(BF16)
    _slab(mix_ref, "qb")[...] = seg("qb").astype(BF16)
    _slab(mix_ref, "vb")[...] = seg("vb").astype(BF16)
    assert tuple(used) == W_IN_USE_ORDER


N_PROJ_OUTS = 7


def _proj_in_both_kernel(xp_ref, xs_ref, g_ref, w_hbm_ref, bg_ref, qg_ref, kg_ref, gm_ref, *refs,
                         n_prompt_steps, wa, wb, d_model):
    out_refs, (w_ref, w_sem) = refs[:-2], refs[-2:]
    step = pl.program_id(0)
    consts = (g_ref, w_ref, bg_ref, qg_ref, kg_ref, gm_ref)
    prompt_body = functools.partial(
        _proj_in_kernel, xp_ref, *consts, *out_refs[:N_PROJ_OUTS], wa=wa, wb=wb, d_model=d_model,
        feature_major=True, q_scale=DK_A ** -0.5 * LOG2E)

    def segment_copy(name):
        cols = _w_in_columns(name, wa, wb, d_model)
        return pltpu.make_async_copy(w_hbm_ref.at[:, cols], w_ref.at[:, cols],
                                     w_sem.at[W_IN_USE_ORDER.index(name)])

    @pl.when(step == 0)
    def _():
        for name in W_IN_USE_ORDER:
            segment_copy(name).start()
        prompt_body(before_segment=lambda name: segment_copy(name).wait())

    @pl.when((step > 0) & (step < n_prompt_steps))
    def _():
        prompt_body()

    @pl.when(step >= n_prompt_steps)
    def _():
        _proj_in_kernel(xs_ref, *consts, *out_refs[N_PROJ_OUTS:],
                        wa=wa, wb=wb, d_model=d_model, feature_major=False, q_scale=DK_A ** -0.5)


def _proj_in(x_prompt, x_sample, g, w_in, bg, qg, kg, gm, *, tm, tm_sample):
    b, t, d = x_prompt.shape
    n = x_sample.shape[1]
    wa = H_A * HEAD_W
    wb = H_B * HEAD_W
    assert t % tm == 0 and n % tm_sample == 0 and wa == wb and x_sample.shape[0] == 1
    nt = t // tm
    n_p, n_s = b * nt, n // tm_sample
    p_tile = lambda s: jnp.minimum(s, n_p - 1)
    s_tile = lambda s: jnp.clip(s - n_p, 0, n_s - 1)
    full = lambda a: pl.BlockSpec(a.shape, lambda s: (0,) * a.ndim)

    def group_specs(bsz, length, rows, tile, feature_major):
        nt_g = length // rows
        row = lambda w: pl.BlockSpec((1, rows, w), lambda s: (tile(s) // nt_g, tile(s) % nt_g, 0))
        act = lambda w, dt: jax.ShapeDtypeStruct((bsz, length, w), dt)
        grouped = lambda groups, lanes: jax.ShapeDtypeStruct((bsz * length * groups, lanes), F32)
        grouped_spec = lambda groups, lanes: pl.BlockSpec((rows * groups, lanes), lambda s: (tile(s), 0))
        if feature_major:
            fm_shape = lambda dt: jax.ShapeDtypeStruct((bsz, wa, length), dt)
            fm_spec = pl.BlockSpec((1, wa, rows), lambda s: (tile(s) // nt_g, 0, tile(s) % nt_g))
            kf_shape, kf_spec = fm_shape(F32), fm_spec
        else:
            fm_shape, fm_spec = (lambda dt: act(wa, dt)), row(wa)
            kf_shape, kf_spec = grouped(wa // DK_A, DK_A), grouped_spec(wa // DK_A, DK_A)
        shapes = [fm_shape(BF16),
                  kf_shape, act(wa, BF16),
                  grouped(H_A, HEAD_W),
                  fm_shape(BF16),
                  act(len(MIX_SLABS) * wa, BF16),
                  act(2 * d, BF16)]
        specs = [fm_spec, kf_spec, row(wa), grouped_spec(H_A, HEAD_W),
                 fm_spec, row(len(MIX_SLABS) * wa), row(2 * d)]
        return row(d), shapes, specs

    xp_spec, p_shapes, p_specs = group_specs(b, t, tm, p_tile, True)
    xs_spec, s_shapes, s_specs = group_specs(1, n, tm_sample, s_tile, False)
    assert len(p_shapes) == N_PROJ_OUTS
    outs = pl.pallas_call(
        functools.partial(_proj_in_both_kernel, n_prompt_steps=n_p, wa=wa, wb=wb, d_model=d),
        grid=(n_p + n_s,),
        in_specs=[xp_spec, xs_spec, full(g), pl.BlockSpec(memory_space=pl.ANY),
                  full(bg), full(qg), full(kg), full(gm)],
        out_specs=p_specs + s_specs,
        out_shape=p_shapes + s_shapes,
        scratch_shapes=[pltpu.VMEM(w_in.shape, w_in.dtype),
                        pltpu.SemaphoreType.DMA((len(W_IN_USE_ORDER),))],
        compiler_params=pltpu.CompilerParams(
            dimension_semantics=("arbitrary",), vmem_limit_bytes=VMEM_LIMIT),
        name="proj_in",
    )(x_prompt, x_sample, g, w_in, bg, qg, kg, gm)
    return outs[:N_PROJ_OUTS], outs[N_PROJ_OUTS:]


def _split_halves(q):
    lane = lax.broadcasted_iota(jnp.int32, q.shape, 1)
    zero = jnp.zeros_like(q)
    return jnp.concatenate([jnp.where(lane < DK_A, q, zero), jnp.where(lane >= DK_A, q, zero)], axis=0)


def _softmax_update(carry, s, v):
    m, acc = carry
    m_new = jnp.maximum(m, jnp.max(s, axis=-1, keepdims=True))
    p = jnp.exp(s - m_new).astype(BF16)
    v1 = jnp.concatenate([v, jnp.ones_like(v)], axis=1)
    return m_new, jnp.exp(m - m_new) * acc + _dot(p, v1)


def _diff_combine(acc, lam, t):
    dv = acc.shape[1] // 2
    o = acc[:, :dv] / acc[:, dv:]
    return o[:t] - lam * o[t:]


def _lambda(lam_refs, lam_init):
    q1, k1, q2, k2 = (r[...] for r in lam_refs)
    inner = lambda a, b: jnp.sum(a * b, axis=-1, keepdims=True)
    return jnp.exp(inner(q1, k1)) - jnp.exp(inner(q2, k2)) + lam_init


def _lane_rms(o):
    return o * lax.rsqrt(jnp.mean(o * o, axis=-1, keepdims=True) + EPS)


def _retention_stage1(q, k, v, r, dmat, xi, zeta, cdecay):
    s = (_nt_dot(q, k) * dmat).astype(BF16)
    cross = _dot(q, r.astype(BF16)) * xi
    kz = (k.astype(F32) * zeta).astype(BF16)
    r_new = cdecay * r + _tn_dot(kz, v)
    return s, cross, r_new


def _retention_stage2(s, cross, v):
    return _dot(s, v) + cross


def _branch_projections(oa_scr, ob_scr, gate_ref, woa_ref, wob_ref, merged_scr, nb, tq):
    d_model = woa_ref.shape[1]
    pa = _dot(oa_scr[...], woa_ref[...]).astype(BF16)
    pb = _dot(ob_scr[...], wob_ref[...]).astype(BF16)
    for bi in range(nb):
        rows = slice(bi * tq, (bi + 1) * tq)
        merged_scr[rows] = gate_ref[bi, :, :d_model] * pa[rows] + gate_ref[bi, :, d_model:] * pb[rows]


def _final_projection(merged_scr, wout_ref, x_ref, y_ref, cols, nb, tq):
    proj = _dot(merged_scr[...], wout_ref[:, cols])
    for bi in range(nb):
        y_ref[bi, :, cols] = x_ref[bi, :, cols] + proj[bi * tq:(bi + 1) * tq]


def _mixer_prompt_kernel(*refs, tq, nb, n_tiles, **params):
    t = pl.program_id(0)

    @pl.when(t < n_tiles)
    def _():
        _mixer_prompt_tile(*refs, tq=tq, nb=nb, **params)

    @pl.when(t == n_tiles)
    def _():
        (x_ref, gate_ref, woa_ref, wob_ref, wout_ref, y_ref, _, _, _,
         oa_scr, ob_scr, merged_scr) = refs[N_MIXER_TILE_INPUTS:N_MIXER_TILE_INPUTS + 12]
        _branch_projections(oa_scr, ob_scr, gate_ref, woa_ref, wob_ref, merged_scr, nb, tq)
        _final_projection(merged_scr, wout_ref, x_ref, y_ref, slice(None), nb, tq)


N_MIXER_TILE_INPUTS = 15


def _mixer_prompt_tile(lq1_ref, lk1_ref, lq2_ref, lk2_ref, qt_ref, k_ref, vt_ref, mix_ref,
                       dbias_ref, kpos_ref, kslope_ref, dmat_ref, xi_ref, zeta_ref, sg_ref,
                       x_ref, gate_ref, woa_ref, wob_ref, wout_ref,
                       y_ref, rout_ref, r_scr, acc_scr, oa_scr, ob_scr, merged_scr, *s_scrs,
                       tq, kblk, nb, nq, slopes, cdecay, lam_init):
    chains = [(bi, j) for bi in range(nb) for j in range(H_A)]
    hps = len(chains)
    nsub = tq // kblk
    t = pl.program_id(0)
    qi = lax.rem(t, nq)
    lam = _lambda((lq1_ref, lk1_ref, lq2_ref, lk2_ref), lam_init)
    lanes = lambda j: slice(j * HEAD_W, (j + 1) * HEAD_W)

    def split_halves_t(qt):
        feat = lax.broadcasted_iota(jnp.int32, qt.shape, 0)
        zero = jnp.zeros_like(qt)
        return jnp.concatenate([jnp.where(feat < DK_A, qt, zero), jnp.where(feat >= DK_A, qt, zero)], axis=1)

    qst = [jnp.concatenate([split_halves_t(qt_ref[bi, lanes(j), :]), kslope_ref[j]], axis=0)
           for bi, j in chains]

    head = lambda c: chains[c][1]

    def keys(c, kb):
        bi, j = chains[c]
        k = k_ref[bi, pl.ds(pl.multiple_of(kb * kblk, kblk), kblk), lanes(j)]
        return jnp.concatenate([k, kpos_ref[...]], axis=1)

    def values1(c, kb):
        bi, j = chains[c]
        vt = vt_ref[bi, lanes(j), pl.ds(pl.multiple_of(kb * kblk, kblk), kblk)]
        return jnp.concatenate([vt, jnp.ones((SUM_ROWS, kblk), BF16)], axis=0)

    def stage_scores(c, kb, slot):
        s = _dot(keys(c, kb), qst[c])
        s_scrs[c][slot] = s
        return jnp.max(s, axis=0, keepdims=True)

    def softmax_step(m, s, s_max, shift):
        m_new = jnp.maximum(m, s_max + shift)
        return m_new, jnp.exp2(m - m_new), jnp.exp2(s - (m_new - shift)).astype(BF16)

    def body(kb, carry):
        slot = lax.rem(kb, 2)
        new = []
        for c in range(hps):
            smax_cur, m = carry[c]
            s_cur = s_scrs[c][slot]
            smax_next = stage_scores(c, kb + 1, 1 - slot)
            shift = ((kb - nsub * qi) * kblk).astype(F32) * slopes[head(c)]
            m, alpha, p = softmax_step(m, s_cur, smax_cur, shift)
            acc_scr[c] = alpha * acc_scr[c] + _dot(values1(c, kb), p)
            new.append((smax_next, m))
        return tuple(new)

    @pl.when(t == 0)
    def _():
        oa_scr[...] = jnp.zeros_like(oa_scr)
        ob_scr[...] = jnp.zeros_like(ob_scr)

    @pl.when(qi == 0)
    def _():
        r_scr[...] = jnp.zeros_like(r_scr)

    acc_scr[...] = jnp.zeros_like(acc_scr)
    init = tuple((stage_scores(c, 0, 0), jnp.full((1, 2 * tq), -jnp.inf, F32)) for c in range(hps))
    _branch_projections(oa_scr, ob_scr, gate_ref, woa_ref, wob_ref, merged_scr, nb, tq)
    n_before = nsub * qi
    carry = lax.fori_loop(0, n_before, body, init)

    def retention_stage1(sub, state):
        rows = slice(sub * kblk, (sub + 1) * kblk)
        return [_retention_stage1(
            _slab(mix_ref, "qb", j, bi)[rows], _slab(mix_ref, "kb", j, bi)[rows], _slab(mix_ref, "vb", j, bi)[rows],
            state[c], dmat_ref[j], xi_ref[j], zeta_ref[j], cdecay[j])
            for c, (bi, j) in enumerate(chains)]

    def retention_stage2(sub, ret):
        rows = slice(sub * kblk, (sub + 1) * kblk)
        for c, (bi, j) in enumerate(chains):
            ob = _retention_stage2(ret[c][0], ret[c][1], _slab(mix_ref, "vb", j, bi)[rows])
            ob = (_lane_rms(ob) * _slab(mix_ref, "zb", j, bi)[rows].astype(F32)).astype(BF16)
            ob_scr[bi * tq + sub * kblk:bi * tq + (sub + 1) * kblk, lanes(j)] = ob

    def cat(pieces):
        pieces = [p for p in pieces if p.shape[1] > 0]
        return pieces[0] if len(pieces) == 1 else jnp.concatenate(pieces, axis=1)

    def diagonal_block(dsub, c, m):
        kb = n_before + dsub
        w = tq - dsub * kblk
        cols = (slice(dsub * kblk, tq), slice(tq + dsub * kblk, 2 * tq))
        if dsub == 0:
            s = s_scrs[c][lax.rem(n_before, 2)]
        else:
            s = _dot(keys(c, kb), cat([qst[c][:, q] for q in cols]))
        dbias = dbias_ref[head(c)]
        s = cat([s[:, :kblk] + dbias, s[:, kblk:w], s[:, w:w + kblk] + dbias, s[:, w + kblk:]])
        shift = (dsub * kblk) * slopes[head(c)]
        m_part, alpha, p = softmax_step(cat([m[:, q] for q in cols]), s,
                                        jnp.max(s, axis=0, keepdims=True), shift)
        pv = _dot(values1(c, kb), p)
        for h, q in enumerate(cols):
            acc_scr[c, :, q] = alpha[:, h * w:(h + 1) * w] * acc_scr[c, :, q] + pv[:, h * w:(h + 1) * w]
        return cat([m[:, :dsub * kblk], m_part[:, :w], m[:, tq:tq + dsub * kblk], m_part[:, w:]])

    final_projection = functools.partial(_final_projection, merged_scr, wout_ref, x_ref, y_ref, nb=nb, tq=tq)

    out_cols = [slice(c, c + MXU_TILE) for c in range(0, x_ref.shape[2], MXU_TILE)]
    every = max(1, (hps * nsub) // len(out_cols))
    state = [r_scr[c] for c in range(hps)]
    ms = [carry[c][1] for c in range(hps)]
    for sub in range(nsub):
        ret = retention_stage1(sub, state)
        state = [r[2] for r in ret]
        for c in range(hps):
            if (sub * hps + c) % every == 0 and out_cols:
                final_projection(out_cols.pop(0))
            ms[c] = diagonal_block(sub, c, ms[c])
        retention_stage2(sub, ret)
    while out_cols:
        final_projection(out_cols.pop(0))
    for c, (bi, j) in enumerate(chains):
        r_scr[c] = state[c]
        rout_ref[bi, j] = state[c]
    for c, (bi, j) in enumerate(chains):
        inv_l = 1.0 / acc_scr[c, HEAD_W:HEAD_W + 1]
        inv_l = jnp.concatenate([inv_l[:, :tq], lam * inv_l[:, tq:]], axis=1)
        ot = acc_scr[c, :HEAD_W] * inv_l
        oat = ot[:, :tq] - ot[:, tq:]
        oat = oat * lax.rsqrt(jnp.mean(oat * oat, axis=0, keepdims=True) + EPS)
        oa = (oat.T * sg_ref[...]) * (1.0 - lam_init)
        oa = (oa * _slab(mix_ref, "za", j, bi)[...].astype(F32)).astype(BF16)
        oa_scr[bi * tq:(bi + 1) * tq, lanes(j)] = oa


def _mixer_prompt(lam_vecs, qat, kab, vat, mix, dbias, kpos, kslope, dmat, xi, zeta, sg, x, gates, woa, wob, wout,
                  *, slopes, cdecay, lam_init):
    b, t, w = kab.shape
    d = x.shape[2]
    tq, kblk, nb = Q_TILE, KEY_BLOCK, SEQS_PER_STEP
    assert t % tq == 0 and tq % kblk == 0 and kblk % CHUNK == 0 and H_A == H_B and w == H_A * HEAD_W
    assert dbias.shape[1:] == (kblk, kblk) and d % (tq // kblk) == 0 and b % nb == 0
    nq = t // tq
    n_tiles = (b // nb) * nq
    cur = lambda s: jnp.minimum(s, n_tiles - 1)
    prev = lambda s: jnp.maximum(s - 1, 0)
    tile = pl.BlockSpec((nb, tq, mix.shape[2]), lambda s: (cur(s) // nq, cur(s) % nq, 0))
    tile_t = pl.BlockSpec((nb, w, tq), lambda s: (cur(s) // nq, 0, cur(s) % nq))
    whole = pl.BlockSpec((nb, t, w), lambda s: (cur(s) // nq, 0, 0))
    whole_t = pl.BlockSpec((nb, w, t), lambda s: (cur(s) // nq, 0, 0))
    rows = lambda a: pl.BlockSpec((nb, tq, a.shape[2]), lambda s: (prev(s) // nq, prev(s) % nq, 0))
    full = lambda a: pl.BlockSpec(a.shape, lambda s: (0,) * a.ndim, pipeline_mode=pl.Buffered(1))
    return pl.pallas_call(
        functools.partial(_mixer_prompt_kernel, tq=tq, kblk=kblk, nb=nb, nq=nq, n_tiles=n_tiles,
                          slopes=slopes, cdecay=cdecay, lam_init=lam_init),
        grid=(n_tiles + 1,),
        in_specs=[full(v) for v in lam_vecs] + [
                  tile_t, whole, whole_t, tile,
                  full(dbias), full(kpos), full(kslope), full(dmat), full(xi), full(zeta), full(sg),
                  rows(x), rows(gates), full(woa), full(wob), full(wout)],
        out_specs=[rows(x),
                   pl.BlockSpec((nb, H_B, DK_B, DV_B), lambda s: (cur(s) // nq, 0, 0, 0))],
        out_shape=[jax.ShapeDtypeStruct((b, t, d), F32),
                   jax.ShapeDtypeStruct((b, H_B, DK_B, DV_B), F32)],
        scratch_shapes=[pltpu.VMEM((nb * H_B, DK_B, DV_B), F32),
                        pltpu.VMEM((nb * H_A, HEAD_W + SUM_ROWS, 2 * tq), F32),
                        pltpu.VMEM((nb * tq, w), BF16), pltpu.VMEM((nb * tq, w), BF16),
                        pltpu.VMEM((nb * tq, d), BF16)]
                       + [pltpu.VMEM((2, kblk, 2 * tq), F32)] * (nb * H_A),
        compiler_params=pltpu.CompilerParams(
            dimension_semantics=("arbitrary",), vmem_limit_bytes=VMEM_LIMIT),
        name="mixer_prompt",
    )(*lam_vecs, qat, kab, vat, mix, dbias, kpos, kslope, dmat, xi, zeta, sg, x, gates, woa, wob, wout)


def _mixer_sample_kernel(lq1_ref, lk1_ref, lq2_ref, lk2_ref,
                         qa_ref, kc_ref, vc_ref, kn_ref, vn_ref, mix_ref, r_ref,
                         abias_ref, dmat_ref, xi_ref, zeta_ref, sg_ref,
                         oa_ref, ob_ref, rout_ref, m_scr, acc_scr,
                         *, n, past, tk, sub, slopes, cdecay, lam_init):
    kbi = pl.program_id(1)
    lanes = lambda j: slice(j * HEAD_W, (j + 1) * HEAD_W)

    @pl.when(kbi == 0)
    def _():
        m_scr[...] = jnp.full(m_scr.shape, -jnp.inf, F32)
        acc_scr[...] = jnp.zeros_like(acc_scr)

    col = lax.broadcasted_iota(jnp.int32, (1, sub), 1)
    qs = [_split_halves(qa_ref[0, :, lanes(j)]) for j in range(H_A)]

    def scores(j, c):
        kk = kc_ref[0, lanes(j), c * sub:(c + 1) * sub].astype(BF16)
        kpos = (col + (kbi * tk + (c * sub - past))).astype(F32)
        return _dot(qs[j], kk) + kpos * slopes[j]

    blocks = [(j, c) for c in range(tk // sub) for j in range(H_A)]
    carry = [(m_scr[j], acc_scr[j]) for j in range(H_A)]
    staged = [scores(*blk) for blk in blocks[:SCORE_LOOKAHEAD]]
    for i, (j, c) in enumerate(blocks):
        if i + SCORE_LOOKAHEAD < len(blocks):
            staged.append(scores(*blocks[i + SCORE_LOOKAHEAD]))
        vv = vc_ref[0, pl.ds(c * sub * H_A + j, sub, stride=H_A), :].astype(BF16)
        carry[j] = _softmax_update(carry[j], staged[i], vv)
    for j in range(H_A):
        m_scr[j], acc_scr[j] = carry[j]

    @pl.when(kbi == pl.num_programs(1) - 1)
    def _():
        lam = _lambda((lq1_ref, lk1_ref, lq2_ref, lk2_ref), lam_init)
        s_new, ret = [], []
        for j in range(H_A):
            bias = abias_ref[j]
            s_new.append(_nt_dot(qs[j], kn_ref[0, :, lanes(j)]) + jnp.concatenate([bias, bias], axis=0))
        for j in range(H_A):
            s_ret, cross, r_new = _retention_stage1(
                _slab(mix_ref, "qb", j)[...], _slab(mix_ref, "kb", j)[...], _slab(mix_ref, "vb", j)[...],
                r_ref[0, j], dmat_ref[j], xi_ref[j], zeta_ref[j], cdecay[j])
            rout_ref[0, j] = r_new
            ret.append((s_ret, cross))
        for j in range(H_A):
            _, acc = _softmax_update((m_scr[j], acc_scr[j]), s_new[j], vn_ref[0, :, lanes(j)])
            oa = _diff_combine(acc, lam, n)
            oa = (_lane_rms(oa) * sg_ref[...]) * (1.0 - lam_init)
            oa_ref[0, :, lanes(j)] = (oa * _slab(mix_ref, "za", j)[...].astype(F32)).astype(BF16)
        for j in range(H_A):
            ob = _retention_stage2(*ret[j], _slab(mix_ref, "vb", j)[...])
            ob_ref[0, :, lanes(j)] = (_lane_rms(ob) * _slab(mix_ref, "zb", j)[...].astype(F32)).astype(BF16)


def _mixer_sample(lam_vecs, qa, kc, vc, kab, vab, mix, r0, abias, dmat, xi, zeta, sg, *, slopes, cdecay, lam_init):
    b, n, w = qa.shape
    past = kc.shape[2]
    tk, sub = SAMPLE_KEY_TILE, SAMPLE_KEY_BLOCK
    assert past % tk == 0 and tk % sub == 0 and past % CHUNK == 0 and n <= CHUNK and H_A == H_B
    tile = pl.BlockSpec((1, n, w), lambda bi, ki: (bi, 0, 0))
    mix_tile = pl.BlockSpec((1, n, mix.shape[2]), lambda bi, ki: (bi, 0, 0))
    cache_k = pl.BlockSpec((1, w, tk), lambda bi, ki: (bi, 0, ki))
    cache_v = pl.BlockSpec((1, tk * H_A, HEAD_W), lambda bi, ki: (bi, ki, 0))
    state = pl.BlockSpec((1, H_B, DK_B, DV_B), lambda bi, ki: (bi, 0, 0, 0))
    full = lambda a: pl.BlockSpec(a.shape, lambda bi, ki: (0,) * a.ndim)
    return pl.pallas_call(
        functools.partial(_mixer_sample_kernel, n=n, past=past, tk=tk, sub=sub,
                          slopes=slopes, cdecay=cdecay, lam_init=lam_init),
        grid=(b, past // tk),
        in_specs=[full(v) for v in lam_vecs] + [
                  tile, cache_k, cache_v, tile, tile, mix_tile, state,
                  full(abias), full(dmat), full(xi), full(zeta), full(sg)],
        out_specs=[tile, tile, state],
        out_shape=[jax.ShapeDtypeStruct((b, n, H_A * HEAD_W), BF16),
                   jax.ShapeDtypeStruct((b, n, H_B * HEAD_W), BF16),
                   jax.ShapeDtypeStruct((b, H_B, DK_B, DV_B), F32)],
        scratch_shapes=[pltpu.VMEM((H_A, 2 * n, 1), F32),
                        pltpu.VMEM((H_A, 2 * n, 2 * DV_A), F32)],
        compiler_params=pltpu.CompilerParams(
            dimension_semantics=("arbitrary", "arbitrary"), vmem_limit_bytes=VMEM_LIMIT),
        name="mixer_sample",
    )(*lam_vecs, qa, kc, vc, kab, vab, mix, r0, abias, dmat, xi, zeta, sg)


def _proj_out_kernel(x_ref, oa_ref, ob_ref, gate_ref, woa_ref, wob_ref, wout_ref, y_ref):
    d = x_ref.shape[1]
    m = (gate_ref[:, :d].astype(F32) * _dot(oa_ref[...], woa_ref[...])
         + gate_ref[:, d:].astype(F32) * _dot(ob_ref[...], wob_ref[...]))
    y_ref[...] = x_ref[...] + _dot(m.astype(BF16), wout_ref[...])


def _proj_out(x2, oa, ob, gates, woa, wob, wout):
    n, d = x2.shape
    tm = min(SAMPLE_ROW_TILE, n)
    assert n % tm == 0
    row = lambda w: pl.BlockSpec((tm, w), lambda i: (i, 0))
    full = lambda a: pl.BlockSpec(a.shape, lambda i: (0,) * a.ndim)
    return pl.pallas_call(
        _proj_out_kernel,
        grid=(n // tm,),
        in_specs=[row(d), row(oa.shape[1]), row(ob.shape[1]), row(gates.shape[1]),
                  full(woa), full(wob), full(wout)],
        out_specs=row(d),
        out_shape=jax.ShapeDtypeStruct((n, d), F32),
        compiler_params=pltpu.CompilerParams(
            dimension_semantics=("arbitrary",), vmem_limit_bytes=VMEM_LIMIT),
        name="proj_out",
    )(x2, oa, ob, gates, woa, wob, wout)


def _block_tables(t, slopes, log_g):
    f32 = np.float32
    i = np.arange(t, dtype=np.int32)
    allowed = (i[None, :] // CHUNK) <= (i[:, None] // CHUNK)
    dist = np.abs(i[:, None] - i[None, :]).astype(f32)
    jf = i.astype(f32)
    abias = -slopes[:, None, None] * dist[None] + slopes[:, None, None] * jf[None, :, None]
    abias = np.where(allowed[None], abias, f32(-np.inf)).astype(f32)
    dmat = np.where(allowed[None], np.exp(dist[None] * log_g[:, None, None]), f32(0.0)).astype(f32)
    xi = np.exp((jf + f32(1.0))[None] * log_g[:, None]).astype(f32)
    zeta = np.exp((f32(t) - f32(1.0) - jf)[None] * log_g[:, None]).astype(f32)
    bc = lambda a: np.ascontiguousarray(np.broadcast_to(a[:, :, None], a.shape + (HEAD_W,)))
    return abias, dmat, bc(xi), bc(zeta)


N_SLOPE_PARTS = 3


def _alibi_features(t, width, slopes):
    assert t <= 256
    bf16 = jnp.bfloat16
    kpos = np.zeros((t, HEAD_W), np.float32)
    kpos[:, :N_SLOPE_PARTS] = np.arange(t, dtype=np.float32)[:, None]
    kslope = np.zeros((len(slopes), HEAD_W, width), np.float32)
    rest = slopes.astype(np.float32)
    for part in range(N_SLOPE_PARTS):
        piece = rest.astype(bf16).astype(np.float32)
        kslope[:, part, :] = piece[:, None]
        rest = rest - piece
    return kpos.astype(bf16), kslope.astype(bf16)


def _key_bias_table(t, slopes):
    kb = (slopes[:, None] * np.arange(t, dtype=np.float32)[None]).astype(np.float32)
    return np.ascontiguousarray(np.broadcast_to(kb[:, :, None], kb.shape + (HEAD_W,)))


def _layer(x_prompt, x_sample, ck, cv, r0, norm_g, w_in, b_gate, qn_g, kn_g,
           lam_q1, lam_k1, lam_q2, lam_k2, subln_g, w_oa, w_ob, w_out, lam_init):
    bp, tp, d = x_prompt.shape
    bs, ts, _ = x_sample.shape
    past = ck.shape[1]
    slopes = (2.0 ** (-8.0 * np.arange(1, H_A + 1, dtype=np.float32) / H_A)).astype(np.float32)
    log_g = np.log(1.0 - 2.0 ** (-5.0 - np.arange(H_B, dtype=np.float32))).astype(np.float32)
    row = lambda v: v.reshape(1, -1)
    lam_vecs = (row(lam_q1), row(lam_k1), row(lam_q2), row(lam_k2))
    floats = lambda a: tuple(float(v) for v in np.asarray(a, np.float32))
    cdecay = lambda n_block: floats(np.exp(np.float32(n_block) * log_g))

    grp = np.arange(MXU_TILE, dtype=np.int32) // DK_A
    gm = jnp.asarray((grp[:, None] == grp[None, :]) * (1.0 / DK_A), dtype=BF16)
    sg = row(subln_g)

    woa_bf, wob_bf, wout_bf = w_oa.astype(BF16), w_ob.astype(BF16), w_out.astype(BF16)
    prompt_acts, sample_acts = _proj_in(
        x_prompt, x_sample.reshape(1, bs * ts, d), row(norm_g), w_in, b_gate, row(qn_g), row(kn_g), gm,
        tm=ROW_TILE, tm_sample=min(SAMPLE_ROW_TILE, bs * ts))

    qat, kaf, kab, vaf, vat, mix, gates = prompt_acts
    abias, dmat, xi, zeta = _block_tables(KEY_BLOCK, slopes, log_g)
    kbias = _key_bias_table(KEY_BLOCK, slopes)
    dbias = np.ascontiguousarray(np.swapaxes(abias, 1, 2) - kbias[:, :, :1])
    flat = lambda a: a.reshape(-1, a.shape[-1])
    log2e = np.float32(LOG2E)
    kpos, kslope = _alibi_features(KEY_BLOCK, 2 * Q_TILE, slopes * log2e)
    y_p, ret_p = _mixer_prompt(lam_vecs, qat, kab, vat, mix,
                               dbias * log2e, kpos, kslope, dmat, xi, zeta, sg,
                               x_prompt, gates, woa_bf, wob_bf, wout_bf,
                               slopes=floats(slopes * log2e), cdecay=cdecay(KEY_BLOCK), lam_init=lam_init)
    k_p = jnp.transpose(kaf.reshape(bp, H_A, 2, DK_A, tp), (0, 4, 1, 2, 3))
    v_p = vaf.reshape(bp, tp, H_A, DV_A)

    qa, kaf, kab, vaf, vab, mix, gates = sample_acts
    r3 = lambda a: a.reshape(bs, ts, a.shape[-1])
    tabs = _block_tables(ts, slopes, log_g)
    ck_t = jnp.transpose(ck, (0, 2, 3, 4, 1)).reshape(bs, H_A * HEAD_W, past)
    oa, ob, ret_s = _mixer_sample(lam_vecs, r3(qa), ck_t, cv.reshape(bs, past * H_A, HEAD_W),
                                  r3(kab), r3(vab), r3(mix), r0, *tabs, sg,
                                  slopes=floats(slopes), cdecay=cdecay(ts), lam_init=lam_init)
    y_s = _proj_out(flat(x_sample), flat(oa), flat(ob), flat(gates),
                    woa_bf, wob_bf, wout_bf).reshape(bs, ts, d)
    k_s = kaf.reshape(bs, ts, H_A, 2, DK_A)
    v_s = vaf.reshape(bs, ts, H_A, DV_A)
    return y_p, y_s, k_p, v_p, ret_p, k_s, v_s, ret_s


def kernel(x_prompt, x_sample, cache_k_diff, cache_v_diff, state_ret, norm_g, w_in, b_gate, qn_g, kn_g,
           lam_q1, lam_k1, lam_q2, lam_k2, subln_g, w_o_diff, w_o_ret, w_out):
    depth = w_in.shape[0]
    hp, hs = x_prompt, x_sample
    outs = [[] for _ in range(6)]
    for l in range(depth):
        lam_init = 0.8 - 0.6 * math.exp(-0.3 * l)
        hp, hs, k_p, v_p, ret_p, k_s, v_s, ret_s = _layer(
            hp, hs, cache_k_diff[l], cache_v_diff[l], state_ret[l], norm_g[l], w_in[l], b_gate[l],
            qn_g[l], kn_g[l], lam_q1[l], lam_k1[l], lam_q2[l], lam_k2[l], subln_g[l],
            w_o_diff[l], w_o_ret[l], w_out[l], lam_init)
        for lst, a in zip(outs, (k_p, v_p, ret_p, k_s, v_s, ret_s)):
            lst.append(a)
    return (hp, hs) + tuple(jnp.stack(o) for o in outs)
```

```python
import functools
import math

import jax
import jax.numpy as jnp
import numpy as np
from jax import lax
from jax.experimental import pallas as pl
from jax.experimental.pallas import tpu as pltpu

F32 = jnp.float32
BF16 = jnp.bfloat16

CHUNK = 64
H_A = 4
DK_A = 64
DV_A = 2 * DK_A
H_B = 4
DK_B = 128
DV_B = 128
HEAD_W = 128
EPS = 1e-6
LOG2E = math.log2(math.e)

MXU_TILE = 256
ROW_TILE = 512
SAMPLE_ROW_TILE = 256
Q_TILE = 256
KEY_BLOCK = 256
SEQS_PER_STEP = 2
SUM_ROWS = 16
SAMPLE_KEY_BLOCK = 1024
SAMPLE_RING = 4
V7X_VMEM_BYTES = 64 * 1024 * 1024
VMEM_LIMIT = V7X_VMEM_BYTES - 3 * 1024 * 1024


def _nt_dot(a, b):
    return lax.dot_general(a, b, (((1,), (1,)), ((), ())), preferred_element_type=F32)


def _tn_dot(a, b):
    return lax.dot_general(a, b, (((0,), (0,)), ((), ())), preferred_element_type=F32)


def _dot(a, b):
    return jnp.dot(a, b, preferred_element_type=F32)


def _sigmoid(x):
    return 1.0 / (1.0 + jnp.exp2(x * (-LOG2E)))


MIX_SLABS = ("za", "qb", "kb", "vb", "zb")


def _slab(mix_ref, name, head=None, seq=0):
    lo = MIX_SLABS.index(name) * H_A * HEAD_W
    if head is None:
        return mix_ref.at[seq, :, lo:lo + H_A * HEAD_W]
    return mix_ref.at[seq, :, lo + head * HEAD_W:lo + (head + 1) * HEAD_W]


W_IN_SEGMENTS = ("qa", "ka", "va", "za", "qb", "kb", "vb", "zb", "ga", "gb")
W_IN_USE_ORDER = ("ga", "gb", "qa", "ka", "va", "za", "zb", "kb", "qb", "vb")


def _w_in_columns(name, wa, wb, d_model):
    widths = (wa, wa, wa, wa, wb, wb, wb, wb, d_model, d_model)
    i = W_IN_SEGMENTS.index(name)
    return slice(sum(widths[:i]), sum(widths[:i + 1]))


def _proj_in_kernel(x_ref, g_ref, w_ref, bg_ref, qg_ref, kg_ref, gm_ref,
                    qa_ref, kaf_ref, kab_ref, vaf_ref, vab_ref, mix_ref, gate_ref,
                    *, wa, wb, d_model, feature_major, q_scale, before_segment=None):
    x = x_ref[0]
    tm = x.shape[0]
    ms = jnp.mean(x * x, axis=-1, keepdims=True)
    xn = (x * lax.rsqrt(ms + EPS)) * g_ref[...]
    used = []

    def seg(name):
        used.append(name)
        if before_segment is not None:
            before_segment(name)
        return _dot(xn, w_ref[:, _w_in_columns(name, wa, wb, d_model)])

    def group_norm(h, gain):
        sq = (h * h).astype(BF16)
        ms = jnp.concatenate([_dot(sq[:, c:c + MXU_TILE], gm_ref[...]) for c in range(0, wa, MXU_TILE)], axis=1)
        return (h * lax.rsqrt(ms + EPS)) * gain

    def silu(h):
        return h * _sigmoid(h)

    maybe_t = (lambda a: a.T) if feature_major else (lambda a: a)
    gate_ref[0, :, :d_model] = _sigmoid(seg("ga") + bg_ref[0:1, :]).astype(BF16)
    gate_ref[0, :, d_model:] = _sigmoid(seg("gb") + bg_ref[1:2, :]).astype(BF16)
    group_gain = lambda gain_ref: jnp.tile(gain_ref[...], (1, wa // DK_A))
    qa_ref[0] = maybe_t(group_norm(seg("qa"), group_gain(qg_ref)) * q_scale).astype(BF16)
    ka = group_norm(seg("ka"), group_gain(kg_ref))
    if feature_major:
        kaf_ref[0] = ka.T
    else:
        groups = wa // DK_A
        for grp in range(groups):
            kaf_ref[pl.ds(grp, tm, stride=groups), :] = ka[:, grp * DK_A:(grp + 1) * DK_A]
    kab_ref[0] = ka.astype(BF16)
    va = seg("va")
    vab_ref[0] = maybe_t(va).astype(BF16)
    for h in range(H_A):
        vaf_ref[pl.ds(h, tm, stride=H_A), :] = va[:, h * HEAD_W:(h + 1) * HEAD_W]
    _slab(mix_ref, "za")[...] = silu(seg("za")).astype(BF16)
    _slab(mix_ref, "zb")[...] = silu(seg("zb")).astype(BF16)
    _slab(mix_ref, "kb")[...] = (seg("kb") * (DK_B ** -0.5)).astype(BF16)
    _slab(mix_ref, "qb")[...] = seg("qb").astype(BF16)
    _slab(mix_ref, "vb")[...] = seg("vb").astype(BF16)
    assert tuple(used) == W_IN_USE_ORDER


N_PROJ_OUTS = 7


def _proj_in_both_kernel(xp_ref, xs_ref, g_ref, w_hbm_ref, bg_ref, qg_ref, kg_ref, gm_ref, *refs,
                         n_prompt_steps, wa, wb, d_model):
    out_refs, (w_ref, w_sem) = refs[:-2], refs[-2:]
    step = pl.program_id(0)
    consts = (g_ref, w_ref, bg_ref, qg_ref, kg_ref, gm_ref)
    prompt_body = functools.partial(
        _proj_in_kernel, xp_ref, *consts, *out_refs[:N_PROJ_OUTS], wa=wa, wb=wb, d_model=d_model,
        feature_major=True, q_scale=DK_A ** -0.5 * LOG2E)

    def segment_copy(name):
        cols = _w_in_columns(name, wa, wb, d_model)
        return pltpu.make_async_copy(w_hbm_ref.at[:, cols], w_ref.at[:, cols],
                                     w_sem.at[W_IN_USE_ORDER.index(name)])

    @pl.when(step == 0)
    def _():
        for name in W_IN_USE_ORDER:
            segment_copy(name).start()
        prompt_body(before_segment=lambda name: segment_copy(name).wait())

    @pl.when((step > 0) & (step < n_prompt_steps))
    def _():
        prompt_body()

    @pl.when(step >= n_prompt_steps)
    def _():
        _proj_in_kernel(xs_ref, *consts, *out_refs[N_PROJ_OUTS:],
                        wa=wa, wb=wb, d_model=d_model, feature_major=False, q_scale=DK_A ** -0.5)


def _proj_in(x_prompt, x_sample, g, w_in, bg, qg, kg, gm, *, tm, tm_sample):
    b, t, d = x_prompt.shape
    n = x_sample.shape[1]
    wa = H_A * HEAD_W
    wb = H_B * HEAD_W
    assert t % tm == 0 and n % tm_sample == 0 and wa == wb and x_sample.shape[0] == 1
    nt = t // tm
    n_p, n_s = b * nt, n // tm_sample
    p_tile = lambda s: jnp.minimum(s, n_p - 1)
    s_tile = lambda s: jnp.clip(s - n_p, 0, n_s - 1)
    full = lambda a: pl.BlockSpec(a.shape, lambda s: (0,) * a.ndim)

    def group_specs(bsz, length, rows, tile, feature_major):
        nt_g = length // rows
        row = lambda w: pl.BlockSpec((1, rows, w), lambda s: (tile(s) // nt_g, tile(s) % nt_g, 0))
        act = lambda w, dt: jax.ShapeDtypeStruct((bsz, length, w), dt)
        grouped = lambda groups, lanes: jax.ShapeDtypeStruct((bsz * length * groups, lanes), F32)
        grouped_spec = lambda groups, lanes: pl.BlockSpec((rows * groups, lanes), lambda s: (tile(s), 0))
        if feature_major:
            fm_shape = lambda dt: jax.ShapeDtypeStruct((bsz, wa, length), dt)
            fm_spec = pl.BlockSpec((1, wa, rows), lambda s: (tile(s) // nt_g, 0, tile(s) % nt_g))
            kf_shape, kf_spec = fm_shape(F32), fm_spec
        else:
            fm_shape, fm_spec = (lambda dt: act(wa, dt)), row(wa)
            kf_shape, kf_spec = grouped(wa // DK_A, DK_A), grouped_spec(wa // DK_A, DK_A)
        shapes = [fm_shape(BF16),
                  kf_shape, act(wa, BF16),
                  grouped(H_A, HEAD_W),
                  fm_shape(BF16),
                  act(len(MIX_SLABS) * wa, BF16),
                  act(2 * d, BF16)]
        specs = [fm_spec, kf_spec, row(wa), grouped_spec(H_A, HEAD_W),
                 fm_spec, row(len(MIX_SLABS) * wa), row(2 * d)]
        return row(d), shapes, specs

    xp_spec, p_shapes, p_specs = group_specs(b, t, tm, p_tile, True)
    xs_spec, s_shapes, s_specs = group_specs(1, n, tm_sample, s_tile, False)
    assert len(p_shapes) == N_PROJ_OUTS
    outs = pl.pallas_call(
        functools.partial(_proj_in_both_kernel, n_prompt_steps=n_p, wa=wa, wb=wb, d_model=d),
        grid=(n_p + n_s,),
        in_specs=[xp_spec, xs_spec, full(g), pl.BlockSpec(memory_space=pl.ANY),
                  full(bg), full(qg), full(kg), full(gm)],
        out_specs=p_specs + s_specs,
        out_shape=p_shapes + s_shapes,
        scratch_shapes=[pltpu.VMEM(w_in.shape, w_in.dtype),
                        pltpu.SemaphoreType.DMA((len(W_IN_USE_ORDER),))],
        compiler_params=pltpu.CompilerParams(
            dimension_semantics=("arbitrary",), vmem_limit_bytes=VMEM_LIMIT),
        name="proj_in",
    )(x_prompt, x_sample, g, w_in, bg, qg, kg, gm)
    return outs[:N_PROJ_OUTS], outs[N_PROJ_OUTS:]


def _split_halves(q):
    lane = lax.broadcasted_iota(jnp.int32, q.shape, 1)
    zero = jnp.zeros_like(q)
    return jnp.concatenate([jnp.where(lane < DK_A, q, zero), jnp.where(lane >= DK_A, q, zero)], axis=0)


def _softmax_update(carry, s, v):
    m, acc = carry
    m_new = jnp.maximum(m, jnp.max(s, axis=-1, keepdims=True))
    p = jnp.exp(s - m_new).astype(BF16)
    v1 = jnp.concatenate([v, jnp.ones_like(v)], axis=1)
    return m_new, jnp.exp(m - m_new) * acc + _dot(p, v1)


def _diff_combine(acc, lam, t):
    dv = acc.shape[1] // 2
    o = acc[:, :dv] / acc[:, dv:]
    return o[:t] - lam * o[t:]


def _lambda(lam_refs, lam_init):
    q1, k1, q2, k2 = (r[...] for r in lam_refs)
    inner = lambda a, b: jnp.sum(a * b, axis=-1, keepdims=True)
    return jnp.exp(inner(q1, k1)) - jnp.exp(inner(q2, k2)) + lam_init


def _lane_rms(o):
    return o * lax.rsqrt(jnp.mean(o * o, axis=-1, keepdims=True) + EPS)


def _retention_stage1(q, k, v, r, dmat, xi, zeta, cdecay):
    s = (_nt_dot(q, k) * dmat).astype(BF16)
    cross = _dot(q, r.astype(BF16)) * xi
    kz = (k.astype(F32) * zeta).astype(BF16)
    r_new = cdecay * r + _tn_dot(kz, v)
    return s, cross, r_new


def _retention_stage2(s, cross, v):
    return _dot(s, v) + cross


def _branch_projections(oa_scr, ob_scr, gate_ref, woa_ref, wob_ref, merged_scr, nb, tq):
    d_model = woa_ref.shape[1]
    pa = _dot(oa_scr[...], woa_ref[...]).astype(BF16)
    pb = _dot(ob_scr[...], wob_ref[...]).astype(BF16)
    for bi in range(nb):
        rows = slice(bi * tq, (bi + 1) * tq)
        merged_scr[rows] = gate_ref[bi, :, :d_model] * pa[rows] + gate_ref[bi, :, d_model:] * pb[rows]


def _final_projection(merged_scr, wout_ref, x_ref, y_ref, cols, nb, tq):
    proj = _dot(merged_scr[...], wout_ref[:, cols])
    for bi in range(nb):
        y_ref[bi, :, cols] = x_ref[bi, :, cols] + proj[bi * tq:(bi + 1) * tq]


def _mixer_prompt_kernel(*refs, tq, nb, n_tiles, **params):
    t = pl.program_id(0)

    @pl.when(t < n_tiles)
    def _():
        _mixer_prompt_tile(*refs, tq=tq, nb=nb, **params)

    @pl.when(t == n_tiles)
    def _():
        (x_ref, gate_ref, woa_ref, wob_ref, wout_ref, y_ref, _, _, _,
         oa_scr, ob_scr, merged_scr) = refs[N_MIXER_TILE_INPUTS:N_MIXER_TILE_INPUTS + 12]
        _branch_projections(oa_scr, ob_scr, gate_ref, woa_ref, wob_ref, merged_scr, nb, tq)
        _final_projection(merged_scr, wout_ref, x_ref, y_ref, slice(None), nb, tq)


N_MIXER_TILE_INPUTS = 15


def _mixer_prompt_tile(lq1_ref, lk1_ref, lq2_ref, lk2_ref, qt_ref, k_ref, vt_ref, mix_ref,
                       dbias_ref, kpos_ref, kslope_ref, dmat_ref, xi_ref, zeta_ref, sg_ref,
                       x_ref, gate_ref, woa_ref, wob_ref, wout_ref,
                       y_ref, rout_ref, r_scr, acc_scr, oa_scr, ob_scr, merged_scr, *s_scrs,
                       tq, kblk, nb, nq, slopes, cdecay, lam_init):
    chains = [(bi, j) for bi in range(nb) for j in range(H_A)]
    hps = len(chains)
    nsub = tq // kblk
    t = pl.program_id(0)
    qi = lax.rem(t, nq)
    lam = _lambda((lq1_ref, lk1_ref, lq2_ref, lk2_ref), lam_init)
    lanes = lambda j: slice(j * HEAD_W, (j + 1) * HEAD_W)

    def split_halves_t(qt):
        feat = lax.broadcasted_iota(jnp.int32, qt.shape, 0)
        zero = jnp.zeros_like(qt)
        return jnp.concatenate([jnp.where(feat < DK_A, qt, zero), jnp.where(feat >= DK_A, qt, zero)], axis=1)

    qst = [jnp.concatenate([split_halves_t(qt_ref[bi, lanes(j), :]), kslope_ref[j]], axis=0)
           for bi, j in chains]

    head = lambda c: chains[c][1]

    def keys(c, kb):
        bi, j = chains[c]
        k = k_ref[bi, pl.ds(pl.multiple_of(kb * kblk, kblk), kblk), lanes(j)]
        return jnp.concatenate([k, kpos_ref[...]], axis=1)

    def values1(c, kb):
        bi, j = chains[c]
        vt = vt_ref[bi, lanes(j), pl.ds(pl.multiple_of(kb * kblk, kblk), kblk)]
        return jnp.concatenate([vt, jnp.ones((SUM_ROWS, kblk), BF16)], axis=0)

    def stage_scores(c, kb, slot):
        s = _dot(keys(c, kb), qst[c])
        s_scrs[c][slot] = s
        return jnp.max(s, axis=0, keepdims=True)

    def softmax_step(m, s, s_max, shift):
        m_new = jnp.maximum(m, s_max + shift)
        return m_new, jnp.exp2(m - m_new), jnp.exp2(s - (m_new - shift)).astype(BF16)

    def body(kb, carry):
        slot = lax.rem(kb, 2)
        new = []
        for c in range(hps):
            smax_cur, m = carry[c]
            s_cur = s_scrs[c][slot]
            smax_next = stage_scores(c, kb + 1, 1 - slot)
            shift = ((kb - nsub * qi) * kblk).astype(F32) * slopes[head(c)]
            m, alpha, p = softmax_step(m, s_cur, smax_cur, shift)
            acc_scr[c] = alpha * acc_scr[c] + _dot(values1(c, kb), p)
            new.append((smax_next, m))
        return tuple(new)

    @pl.when(t == 0)
    def _():
        oa_scr[...] = jnp.zeros_like(oa_scr)
        ob_scr[...] = jnp.zeros_like(ob_scr)

    @pl.when(qi == 0)
    def _():
        r_scr[...] = jnp.zeros_like(r_scr)

    acc_scr[...] = jnp.zeros_like(acc_scr)
    init = tuple((stage_scores(c, 0, 0), jnp.full((1, 2 * tq), -jnp.inf, F32)) for c in range(hps))
    _branch_projections(oa_scr, ob_scr, gate_ref, woa_ref, wob_ref, merged_scr, nb, tq)
    n_before = nsub * qi
    carry = lax.fori_loop(0, n_before, body, init)

    def retention_stage1(sub, state):
        rows = slice(sub * kblk, (sub + 1) * kblk)
        return [_retention_stage1(
            _slab(mix_ref, "qb", j, bi)[rows], _slab(mix_ref, "kb", j, bi)[rows], _slab(mix_ref, "vb", j, bi)[rows],
            state[c], dmat_ref[j], xi_ref[j], zeta_ref[j], cdecay[j])
            for c, (bi, j) in enumerate(chains)]

    def retention_stage2(sub, ret):
        rows = slice(sub * kblk, (sub + 1) * kblk)
        for c, (bi, j) in enumerate(chains):
            ob = _retention_stage2(ret[c][0], ret[c][1], _slab(mix_ref, "vb", j, bi)[rows])
            ob = (_lane_rms(ob) * _slab(mix_ref, "zb", j, bi)[rows].astype(F32)).astype(BF16)
            ob_scr[bi * tq + sub * kblk:bi * tq + (sub + 1) * kblk, lanes(j)] = ob

    def cat(pieces):
        pieces = [p for p in pieces if p.shape[1] > 0]
        return pieces[0] if len(pieces) == 1 else jnp.concatenate(pieces, axis=1)

    def diagonal_block(dsub, c, m):
        kb = n_before + dsub
        w = tq - dsub * kblk
        cols = (slice(dsub * kblk, tq), slice(tq + dsub * kblk, 2 * tq))
        if dsub == 0:
            s = s_scrs[c][lax.rem(n_before, 2)]
        else:
            s = _dot(keys(c, kb), cat([qst[c][:, q] for q in cols]))
        dbias = dbias_ref[head(c)]
        s = cat([s[:, :kblk] + dbias, s[:, kblk:w], s[:, w:w + kblk] + dbias, s[:, w + kblk:]])
        shift = (dsub * kblk) * slopes[head(c)]
        m_part, alpha, p = softmax_step(cat([m[:, q] for q in cols]), s,
                                        jnp.max(s, axis=0, keepdims=True), shift)
        pv = _dot(values1(c, kb), p)
        for h, q in enumerate(cols):
            acc_scr[c, :, q] = alpha[:, h * w:(h + 1) * w] * acc_scr[c, :, q] + pv[:, h * w:(h + 1) * w]
        return cat([m[:, :dsub * kblk], m_part[:, :w], m[:, tq:tq + dsub * kblk], m_part[:, w:]])

    final_projection = functools.partial(_final_projection, merged_scr, wout_ref, x_ref, y_ref, nb=nb, tq=tq)

    out_cols = [slice(c, c + MXU_TILE) for c in range(0, x_ref.shape[2], MXU_TILE)]
    every = max(1, (hps * nsub) // len(out_cols))
    state = [r_scr[c] for c in range(hps)]
    ms = [carry[c][1] for c in range(hps)]
    for sub in range(nsub):
        ret = retention_stage1(sub, state)
        state = [r[2] for r in ret]
        for c in range(hps):
            if (sub * hps + c) % every == 0 and out_cols:
                final_projection(out_cols.pop(0))
            ms[c] = diagonal_block(sub, c, ms[c])
        retention_stage2(sub, ret)
    while out_cols:
        final_projection(out_cols.pop(0))
    for c, (bi, j) in enumerate(chains):
        r_scr[c] = state[c]
        rout_ref[bi, j] = state[c]
    for c, (bi, j) in enumerate(chains):
        inv_l = 1.0 / acc_scr[c, HEAD_W:HEAD_W + 1]
        inv_l = jnp.concatenate([inv_l[:, :tq], lam * inv_l[:, tq:]], axis=1)
        ot = acc_scr[c, :HEAD_W] * inv_l
        oat = ot[:, :tq] - ot[:, tq:]
        oat = oat * lax.rsqrt(jnp.mean(oat * oat, axis=0, keepdims=True) + EPS)
        oa = (oat.T * sg_ref[...]) * (1.0 - lam_init)
        oa = (oa * _slab(mix_ref, "za", j, bi)[...].astype(F32)).astype(BF16)
        oa_scr[bi * tq:(bi + 1) * tq, lanes(j)] = oa


def _mixer_prompt(lam_vecs, qat, kab, vat, mix, dbias, kpos, kslope, dmat, xi, zeta, sg, x, gates, woa, wob, wout,
                  *, slopes, cdecay, lam_init):
    b, t, w = kab.shape
    d = x.shape[2]
    tq, kblk, nb = Q_TILE, KEY_BLOCK, SEQS_PER_STEP
    assert t % tq == 0 and tq % kblk == 0 and kblk % CHUNK == 0 and H_A == H_B and w == H_A * HEAD_W
    assert dbias.shape[1:] == (kblk, kblk) and d % (tq // kblk) == 0 and b % nb == 0
    nq = t // tq
    n_tiles = (b // nb) * nq
    cur = lambda s: jnp.minimum(s, n_tiles - 1)
    prev = lambda s: jnp.maximum(s - 1, 0)
    tile = pl.BlockSpec((nb, tq, mix.shape[2]), lambda s: (cur(s) // nq, cur(s) % nq, 0))
    tile_t = pl.BlockSpec((nb, w, tq), lambda s: (cur(s) // nq, 0, cur(s) % nq))
    whole = pl.BlockSpec((nb, t, w), lambda s: (cur(s) // nq, 0, 0))
    whole_t = pl.BlockSpec((nb, w, t), lambda s: (cur(s) // nq, 0, 0))
    rows = lambda a: pl.BlockSpec((nb, tq, a.shape[2]), lambda s: (prev(s) // nq, prev(s) % nq, 0))
    full = lambda a: pl.BlockSpec(a.shape, lambda s: (0,) * a.ndim, pipeline_mode=pl.Buffered(1))
    return pl.pallas_call(
        functools.partial(_mixer_prompt_kernel, tq=tq, kblk=kblk, nb=nb, nq=nq, n_tiles=n_tiles,
                          slopes=slopes, cdecay=cdecay, lam_init=lam_init),
        grid=(n_tiles + 1,),
        in_specs=[full(v) for v in lam_vecs] + [
                  tile_t, whole, whole_t, tile,
                  full(dbias), full(kpos), full(kslope), full(dmat), full(xi), full(zeta), full(sg),
                  rows(x), rows(gates), full(woa), full(wob), full(wout)],
        out_specs=[rows(x),
                   pl.BlockSpec((nb, H_B, DK_B, DV_B), lambda s: (cur(s) // nq, 0, 0, 0))],
        out_shape=[jax.ShapeDtypeStruct((b, t, d), F32),
                   jax.ShapeDtypeStruct((b, H_B, DK_B, DV_B), F32)],
        scratch_shapes=[pltpu.VMEM((nb * H_B, DK_B, DV_B), F32),
                        pltpu.VMEM((nb * H_A, HEAD_W + SUM_ROWS, 2 * tq), F32),
                        pltpu.VMEM((nb * tq, w), BF16), pltpu.VMEM((nb * tq, w), BF16),
                        pltpu.VMEM((nb * tq, d), BF16)]
                       + [pltpu.VMEM((2, kblk, 2 * tq), F32)] * (nb * H_A),
        compiler_params=pltpu.CompilerParams(
            dimension_semantics=("arbitrary",), vmem_limit_bytes=VMEM_LIMIT),
        name="mixer_prompt",
    )(*lam_vecs, qat, kab, vat, mix, dbias, kpos, kslope, dmat, xi, zeta, sg, x, gates, woa, wob, wout)


def _mixer_sample_kernel(lq1_ref, lk1_ref, lq2_ref, lk2_ref,
                         qa_ref, kc_ref, vc_ref, kn_ref, vn_ref, mix_ref, r_ref,
                         abias_ref, dmat_ref, xi_ref, zeta_ref, sg_ref,
                         oa_ref, ob_ref, rout_ref, kbuf, vbuf, sem,
                         *, n, past, sub, slopes, cdecay, lam_init):
    bi = pl.program_id(0)
    nchunk = past // sub
    ahead = SAMPLE_RING - 1
    lanes = lambda j: slice(j * HEAD_W, (j + 1) * HEAD_W)
    slot = lambda c: c % SAMPLE_RING

    def chunk_copies(stream, c):
        return (pltpu.make_async_copy(kc_ref.at[stream, :, pl.ds(c * sub, sub)], kbuf.at[slot(c)],
                                      sem.at[0, slot(c)]),
                pltpu.make_async_copy(vc_ref.at[stream, pl.ds(c * sub * H_A, sub * H_A), :], vbuf.at[slot(c)],
                                      sem.at[1, slot(c)]))

    def start(stream, c):
        for copy in chunk_copies(stream, c):
            copy.start()

    def wait(c):
        for copy in chunk_copies(bi, c):
            copy.wait()

    @pl.when(bi == 0)
    def _():
        for c in range(ahead):
            start(0, c)

    col = lax.broadcasted_iota(jnp.int32, (1, sub), 1)
    qs = [_split_halves(qa_ref[0, :, lanes(j)]) for j in range(H_A)]

    def scores(c):
        kpos = (col + (c * sub - past)).astype(F32)
        return [_dot(qs[j], kbuf[slot(c), lanes(j), :].astype(BF16)) + kpos * slopes[j] for j in range(H_A)]

    carry = [(jnp.full((2 * n, 1), -jnp.inf, F32), jnp.zeros((2 * n, 2 * DV_A), F32)) for _ in range(H_A)]
    wait(0)
    staged = scores(0)
    for c in range(nchunk):
        if c + ahead < nchunk:
            start(bi, c + ahead)
        else:
            @pl.when(bi + 1 < pl.num_programs(0))
            def _():
                start(bi + 1, c + ahead - nchunk)
        staged, current = None, staged
        if c + 1 < nchunk:
            wait(c + 1)
            staged = scores(c + 1)
        for j in range(H_A):
            vv = vbuf[slot(c), pl.ds(j, sub, stride=H_A), :].astype(BF16)
            carry[j] = _softmax_update(carry[j], current[j], vv)

    lam = _lambda((lq1_ref, lk1_ref, lq2_ref, lk2_ref), lam_init)
    s_new, ret = [], []
    for j in range(H_A):
        bias = abias_ref[j]
        s_new.append(_nt_dot(qs[j], kn_ref[0, :, lanes(j)]) + jnp.concatenate([bias, bias], axis=0))
    for j in range(H_A):
        s_ret, cross, r_new = _retention_stage1(
            _slab(mix_ref, "qb", j)[...], _slab(mix_ref, "kb", j)[...], _slab(mix_ref, "vb", j)[...],
            r_ref[0, j], dmat_ref[j], xi_ref[j], zeta_ref[j], cdecay[j])
        rout_ref[0, j] = r_new
        ret.append((s_ret, cross))
    for j in range(H_A):
        _, acc = _softmax_update(carry[j], s_new[j], vn_ref[0, :, lanes(j)])
        oa = _diff_combine(acc, lam, n)
        oa = (_lane_rms(oa) * sg_ref[...]) * (1.0 - lam_init)
        oa_ref[0, :, lanes(j)] = (oa * _slab(mix_ref, "za", j)[...].astype(F32)).astype(BF16)
    for j in range(H_A):
        ob = _retention_stage2(*ret[j], _slab(mix_ref, "vb", j)[...])
        ob_ref[0, :, lanes(j)] = (_lane_rms(ob) * _slab(mix_ref, "zb", j)[...].astype(F32)).astype(BF16)


def _mixer_sample(lam_vecs, qa, kc, vc, kab, vab, mix, r0, abias, dmat, xi, zeta, sg, *, slopes, cdecay, lam_init):
    b, n, w = qa.shape
    past = kc.shape[2]
    sub = SAMPLE_KEY_BLOCK
    assert past % (sub * SAMPLE_RING) == 0 and past % CHUNK == 0 and n <= CHUNK and H_A == H_B
    tile = pl.BlockSpec((1, n, w), lambda bi: (bi, 0, 0))
    mix_tile = pl.BlockSpec((1, n, mix.shape[2]), lambda bi: (bi, 0, 0))
    in_hbm = pl.BlockSpec(memory_space=pl.ANY)
    state = pl.BlockSpec((1, H_B, DK_B, DV_B), lambda bi: (bi, 0, 0, 0))
    full = lambda a: pl.BlockSpec(a.shape, lambda bi: (0,) * a.ndim)
    return pl.pallas_call(
        functools.partial(_mixer_sample_kernel, n=n, past=past, sub=sub,
                          slopes=slopes, cdecay=cdecay, lam_init=lam_init),
        grid=(b,),
        in_specs=[full(v) for v in lam_vecs] + [
                  tile, in_hbm, in_hbm, tile, tile, mix_tile, state,
                  full(abias), full(dmat), full(xi), full(zeta), full(sg)],
        out_specs=[tile, tile, state],
        out_shape=[jax.ShapeDtypeStruct((b, n, H_A * HEAD_W), BF16),
                   jax.ShapeDtypeStruct((b, n, H_B * HEAD_W), BF16),
                   jax.ShapeDtypeStruct((b, H_B, DK_B, DV_B), F32)],
        scratch_shapes=[pltpu.VMEM((SAMPLE_RING, w, sub), F32),
                        pltpu.VMEM((SAMPLE_RING, sub * H_A, HEAD_W), F32),
                        pltpu.SemaphoreType.DMA((2, SAMPLE_RING))],
        compiler_params=pltpu.CompilerParams(
            dimension_semantics=("arbitrary",), vmem_limit_bytes=VMEM_LIMIT),
        name="mixer_sample",
    )(*lam_vecs, qa, kc, vc, kab, vab, mix, r0, abias, dmat, xi, zeta, sg)


def _proj_out_kernel(x_ref, oa_ref, ob_ref, gate_ref, woa_ref, wob_ref, wout_ref, y_ref):
    d = x_ref.shape[1]
    m = (gate_ref[:, :d].astype(F32) * _dot(oa_ref[...], woa_ref[...])
         + gate_ref[:, d:].astype(F32) * _dot(ob_ref[...], wob_ref[...]))
    y_ref[...] = x_ref[...] + _dot(m.astype(BF16), wout_ref[...])


def _proj_out(x2, oa, ob, gates, woa, wob, wout):
    n, d = x2.shape
    tm = min(ROW_TILE, n)
    assert n % tm == 0
    row = lambda w: pl.BlockSpec((tm, w), lambda i: (i, 0))
    full = lambda a: pl.BlockSpec(a.shape, lambda i: (0,) * a.ndim)
    return pl.pallas_call(
        _proj_out_kernel,
        grid=(n // tm,),
        in_specs=[row(d), row(oa.shape[1]), row(ob.shape[1]), row(gates.shape[1]),
                  full(woa), full(wob), full(wout)],
        out_specs=row(d),
        out_shape=jax.ShapeDtypeStruct((n, d), F32),
        compiler_params=pltpu.CompilerParams(
            dimension_semantics=("arbitrary",), vmem_limit_bytes=VMEM_LIMIT),
        name="proj_out",
    )(x2, oa, ob, gates, woa, wob, wout)


def _block_tables(t, slopes, log_g):
    f32 = np.float32
    i = np.arange(t, dtype=np.int32)
    allowed = (i[None, :] // CHUNK) <= (i[:, None] // CHUNK)
    dist = np.abs(i[:, None] - i[None, :]).astype(f32)
    jf = i.astype(f32)
    abias = -slopes[:, None, None] * dist[None] + slopes[:, None, None] * jf[None, :, None]
    abias = np.where(allowed[None], abias, f32(-np.inf)).astype(f32)
    dmat = np.where(allowed[None], np.exp(dist[None] * log_g[:, None, None]), f32(0.0)).astype(f32)
    xi = np.exp((jf + f32(1.0))[None] * log_g[:, None]).astype(f32)
    zeta = np.exp((f32(t) - f32(1.0) - jf)[None] * log_g[:, None]).astype(f32)
    bc = lambda a: np.ascontiguousarray(np.broadcast_to(a[:, :, None], a.shape + (HEAD_W,)))
    return abias, dmat, bc(xi), bc(zeta)


N_SLOPE_PARTS = 3


def _alibi_features(t, width, slopes):
    assert t <= 256
    bf16 = jnp.bfloat16
    kpos = np.zeros((t, HEAD_W), np.float32)
    kpos[:, :N_SLOPE_PARTS] = np.arange(t, dtype=np.float32)[:, None]
    kslope = np.zeros((len(slopes), HEAD_W, width), np.float32)
    rest = slopes.astype(np.float32)
    for part in range(N_SLOPE_PARTS):
        piece = rest.astype(bf16).astype(np.float32)
        kslope[:, part, :] = piece[:, None]
        rest = rest - piece
    return kpos.astype(bf16), kslope.astype(bf16)


def _key_bias_table(t, slopes):
    kb = (slopes[:, None] * np.arange(t, dtype=np.float32)[None]).astype(np.float32)
    return np.ascontiguousarray(np.broadcast_to(kb[:, :, None], kb.shape + (HEAD_W,)))


def _layer(x_prompt, x_sample, ck, cv, r0, norm_g, w_in, b_gate, qn_g, kn_g,
           lam_q1, lam_k1, lam_q2, lam_k2, subln_g, w_oa, w_ob, w_out, lam_init):
    bp, tp, d = x_prompt.shape
    bs, ts, _ = x_sample.shape
    past = ck.shape[1]
    slopes = (2.0 ** (-8.0 * np.arange(1, H_A + 1, dtype=np.float32) / H_A)).astype(np.float32)
    log_g = np.log(1.0 - 2.0 ** (-5.0 - np.arange(H_B, dtype=np.float32))).astype(np.float32)
    row = lambda v: v.reshape(1, -1)
    lam_vecs = (row(lam_q1), row(lam_k1), row(lam_q2), row(lam_k2))
    floats = lambda a: tuple(float(v) for v in np.asarray(a, np.float32))
    cdecay = lambda n_block: floats(np.exp(np.float32(n_block) * log_g))

    grp = np.arange(MXU_TILE, dtype=np.int32) // DK_A
    gm = jnp.asarray((grp[:, None] == grp[None, :]) * (1.0 / DK_A), dtype=BF16)
    sg = row(subln_g)

    woa_bf, wob_bf, wout_bf = w_oa.astype(BF16), w_ob.astype(BF16), w_out.astype(BF16)
    prompt_acts, sample_acts = _proj_in(
        x_prompt, x_sample.reshape(1, bs * ts, d), row(norm_g), w_in, b_gate, row(qn_g), row(kn_g), gm,
        tm=ROW_TILE, tm_sample=min(SAMPLE_ROW_TILE, bs * ts))

    qat, kaf, kab, vaf, vat, mix, gates = prompt_acts
    abias, dmat, xi, zeta = _block_tables(KEY_BLOCK, slopes, log_g)
    kbias = _key_bias_table(KEY_BLOCK, slopes)
    dbias = np.ascontiguousarray(np.swapaxes(abias, 1, 2) - kbias[:, :, :1])
    flat = lambda a: a.reshape(-1, a.shape[-1])
    log2e = np.float32(LOG2E)
    kpos, kslope = _alibi_features(KEY_BLOCK, 2 * Q_TILE, slopes * log2e)
    y_p, ret_p = _mixer_prompt(lam_vecs, qat, kab, vat, mix,
                               dbias * log2e, kpos, kslope, dmat, xi, zeta, sg,
                               x_prompt, gates, woa_bf, wob_bf, wout_bf,
                               slopes=floats(slopes * log2e), cdecay=cdecay(KEY_BLOCK), lam_init=lam_init)
    k_p = jnp.transpose(kaf.reshape(bp, H_A, 2, DK_A, tp), (0, 4, 1, 2, 3))
    v_p = vaf.reshape(bp, tp, H_A, DV_A)

    qa, kaf, kab, vaf, vab, mix, gates = sample_acts
    r3 = lambda a: a.reshape(bs, ts, a.shape[-1])
    tabs = _block_tables(ts, slopes, log_g)
    ck_t = jnp.transpose(ck, (0, 2, 3, 4, 1)).reshape(bs, H_A * HEAD_W, past)
    oa, ob, ret_s = _mixer_sample(lam_vecs, r3(qa), ck_t, cv.reshape(bs, past * H_A, HEAD_W),
                                  r3(kab), r3(vab), r3(mix), r0, *tabs, sg,
                                  slopes=floats(slopes), cdecay=cdecay(ts), lam_init=lam_init)
    y_s = _proj_out(flat(x_sample), flat(oa), flat(ob), flat(gates),
                    woa_bf, wob_bf, wout_bf).reshape(bs, ts, d)
    k_s = kaf.reshape(bs, ts, H_A, 2, DK_A)
    v_s = vaf.reshape(bs, ts, H_A, DV_A)
    return y_p, y_s, k_p, v_p, ret_p, k_s, v_s, ret_s


def kernel(x_prompt, x_sample, cache_k_diff, cache_v_diff, state_ret, norm_g, w_in, b_gate, qn_g, kn_g,
           lam_q1, lam_k1, lam_q2, lam_k2, subln_g, w_o_diff, w_o_ret, w_out):
    depth = w_in.shape[0]
    hp, hs = x_prompt, x_sample
    outs = [[] for _ in range(6)]
    for l in range(depth):
        lam_init = 0.8 - 0.6 * math.exp(-0.3 * l)
        hp, hs, k_p, v_p, ret_p, k_s, v_s, ret_s = _layer(
            hp, hs, cache_k_diff[l], cache_v_diff[l], state_ret[l], norm_g[l], w_in[l], b_gate[l],
            qn_g[l], kn_g[l], lam_q1[l], lam_k1[l], lam_q2[l], lam_k2[l], subln_g[l],
            w_o_diff[l], w_o_ret[l], w_out[l], lam_init)
        for lst, a in zip(outs, (k_p, v_p, ret_p, k_s, v_s, ret_s)):
            lst.append(a)
    return (hp, hs) + tuple(jnp.stack(o) for o in outs)
```

```python
import functools
import math

import jax
import jax.numpy as jnp
import numpy as np
from jax import lax
from jax.experimental import pallas as pl
from jax.experimental.pallas import tpu as pltpu

F32 = jnp.float32
BF16 = jnp.bfloat16

CHUNK = 64
H_A = 4
DK_A = 64
DV_A = 2 * DK_A
H_B = 4
DK_B = 128
DV_B = 128
HEAD_W = 128
EPS = 1e-6
LOG2E = math.log2(math.e)

MXU_TILE = 256
ROW_TILE = 512
SAMPLE_ROW_TILE = 256
Q_TILE = 256
KEY_BLOCK = 256
SEQS_PER_STEP = 2
SUM_ROWS = 16
SAMPLE_KEY_BLOCK = 1024
SAMPLE_RING = 4
V7X_VMEM_BYTES = 64 * 1024 * 1024
VMEM_LIMIT = V7X_VMEM_BYTES - 3 * 1024 * 1024


def _nt_dot(a, b):
    return lax.dot_general(a, b, (((1,), (1,)), ((), ())), preferred_element_type=F32)


def _tn_dot(a, b):
    return lax.dot_general(a, b, (((0,), (0,)), ((), ())), preferred_element_type=F32)


def _dot(a, b):
    return jnp.dot(a, b, preferred_element_type=F32)


def _sigmoid(x):
    return 1.0 / (1.0 + jnp.exp2(x * (-LOG2E)))


MIX_SLABS = ("za", "qb", "kb", "vb", "zb")


def _slab(mix_ref, name, head=None, seq=0):
    lo = MIX_SLABS.index(name) * H_A * HEAD_W
    if head is None:
        return mix_ref.at[seq, :, lo:lo + H_A * HEAD_W]
    return mix_ref.at[seq, :, lo + head * HEAD_W:lo + (head + 1) * HEAD_W]


W_IN_SEGMENTS = ("qa", "ka", "va", "za", "qb", "kb", "vb", "zb", "ga", "gb")
W_IN_USE_ORDER = ("ga", "gb", "qa", "ka", "va", "za", "zb", "kb", "qb", "vb")


def _w_in_columns(name, wa, wb, d_model):
    widths = (wa, wa, wa, wa, wb, wb, wb, wb, d_model, d_model)
    i = W_IN_SEGMENTS.index(name)
    return slice(sum(widths[:i]), sum(widths[:i + 1]))


def _proj_in_kernel(x_ref, g_ref, w_ref, bg_ref, qg_ref, kg_ref, gm_ref,
                    qa_ref, kaf_ref, kab_ref, vaf_ref, vab_ref, mix_ref, gate_ref,
                    *, wa, wb, d_model, feature_major, q_scale, before_segment=None):
    x = x_ref[0]
    tm = x.shape[0]
    ms = jnp.mean(x * x, axis=-1, keepdims=True)
    xn = (x * lax.rsqrt(ms + EPS)) * g_ref[...]
    used = []

    def seg(name):
        used.append(name)
        if before_segment is not None:
            before_segment(name)
        return _dot(xn, w_ref[:, _w_in_columns(name, wa, wb, d_model)])

    def group_norm(h, gain):
        sq = (h * h).astype(BF16)
        ms = jnp.concatenate([_dot(sq[:, c:c + MXU_TILE], gm_ref[...]) for c in range(0, wa, MXU_TILE)], axis=1)
        return (h * lax.rsqrt(ms + EPS)) * gain

    def silu(h):
        return h * _sigmoid(h)

    maybe_t = (lambda a: a.T) if feature_major else (lambda a: a)
    gate_ref[0, :, :d_model] = _sigmoid(seg("ga") + bg_ref[0:1, :]).astype(BF16)
    gate_ref[0, :, d_model:] = _sigmoid(seg("gb") + bg_ref[1:2, :]).astype(BF16)
    group_gain = lambda gain_ref: jnp.tile(gain_ref[...], (1, wa // DK_A))
    qa_ref[0] = maybe_t(group_norm(seg("qa"), group_gain(qg_ref)) * q_scale).astype(BF16)
    ka = group_norm(seg("ka"), group_gain(kg_ref))
    if feature_major:
        kaf_ref[0] = ka.T
    else:
        groups = wa // DK_A
        for grp in range(groups):
            kaf_ref[pl.ds(grp, tm, stride=groups), :] = ka[:, grp * DK_A:(grp + 1) * DK_A]
    kab_ref[0] = ka.astype(BF16)
    va = seg("va")
    vab_ref[0] = maybe_t(va).astype(BF16)
    for h in range(H_A):
        vaf_ref[pl.ds(h, tm, stride=H_A), :] = va[:, h * HEAD_W:(h + 1) * HEAD_W]
    _slab(mix_ref, "za")[...] = silu(seg("za")).astype(BF16)
    _slab(mix_ref, "zb")[...] = silu(seg("zb")).astype(BF16)
    _slab(mix_ref, "kb")[...] = (seg("kb") * (DK_B ** -0.5)).astype(BF16)
    _slab(mix_ref, "qb")[...] = seg("qb").astype(BF16)
    _slab(mix_ref, "vb")[...] = seg("vb").astype(BF16)
    assert tuple(used) == W_IN_USE_ORDER


N_PROJ_OUTS = 7


def _proj_in_both_kernel(xp_ref, xs_ref, g_ref, w_hbm_ref, bg_ref, qg_ref, kg_ref, gm_ref, *refs,
                         n_prompt_steps, wa, wb, d_model):
    out_refs, (w_ref, w_sem) = refs[:-2], refs[-2:]
    step = pl.program_id(0)
    consts = (g_ref, w_ref, bg_ref, qg_ref, kg_ref, gm_ref)
    prompt_body = functools.partial(
        _proj_in_kernel, xp_ref, *consts, *out_refs[:N_PROJ_OUTS], wa=wa, wb=wb, d_model=d_model,
        feature_major=True, q_scale=DK_A ** -0.5 * LOG2E)

    def segment_copy(name):
        cols = _w_in_columns(name, wa, wb, d_model)
        return pltpu.make_async_copy(w_hbm_ref.at[:, cols], w_ref.at[:, cols],
                                     w_sem.at[W_IN_USE_ORDER.index(name)])

    @pl.when(step == 0)
    def _():
        for name in W_IN_USE_ORDER:
            segment_copy(name).start()
        prompt_body(before_segment=lambda name: segment_copy(name).wait())

    @pl.when((step > 0) & (step < n_prompt_steps))
    def _():
        prompt_body()

    @pl.when(step >= n_prompt_steps)
    def _():
        _proj_in_kernel(xs_ref, *consts, *out_refs[N_PROJ_OUTS:],
                        wa=wa, wb=wb, d_model=d_model, feature_major=False, q_scale=DK_A ** -0.5)


def _proj_in(x_prompt, x_sample, g, w_in, bg, qg, kg, gm, *, tm, tm_sample):
    b, t, d = x_prompt.shape
    n = x_sample.shape[1]
    wa = H_A * HEAD_W
    wb = H_B * HEAD_W
    assert t % tm == 0 and n % tm_sample == 0 and wa == wb and x_sample.shape[0] == 1
    nt = t // tm
    n_p, n_s = b * nt, n // tm_sample
    p_tile = lambda s: jnp.minimum(s, n_p - 1)
    s_tile = lambda s: jnp.clip(s - n_p, 0, n_s - 1)
    full = lambda a: pl.BlockSpec(a.shape, lambda s: (0,) * a.ndim)

    def group_specs(bsz, length, rows, tile, feature_major):
        nt_g = length // rows
        row = lambda w: pl.BlockSpec((1, rows, w), lambda s: (tile(s) // nt_g, tile(s) % nt_g, 0))
        act = lambda w, dt: jax.ShapeDtypeStruct((bsz, length, w), dt)
        grouped = lambda groups, lanes: jax.ShapeDtypeStruct((bsz * length * groups, lanes), F32)
        grouped_spec = lambda groups, lanes: pl.BlockSpec((rows * groups, lanes), lambda s: (tile(s), 0))
        if feature_major:
            fm_shape = lambda dt: jax.ShapeDtypeStruct((bsz, wa, length), dt)
            fm_spec = pl.BlockSpec((1, wa, rows), lambda s: (tile(s) // nt_g, 0, tile(s) % nt_g))
            kf_shape, kf_spec = fm_shape(F32), fm_spec
        else:
            fm_shape, fm_spec = (lambda dt: act(wa, dt)), row(wa)
            kf_shape, kf_spec = grouped(wa // DK_A, DK_A), grouped_spec(wa // DK_A, DK_A)
        shapes = [fm_shape(BF16),
                  kf_shape, act(wa, BF16),
                  grouped(H_A, HEAD_W),
                  fm_shape(BF16),
                  act(len(MIX_SLABS) * wa, BF16),
                  act(2 * d, BF16)]
        specs = [fm_spec, kf_spec, row(wa), grouped_spec(H_A, HEAD_W),
                 fm_spec, row(len(MIX_SLABS) * wa), row(2 * d)]
        return row(d), shapes, specs

    xp_spec, p_shapes, p_specs = group_specs(b, t, tm, p_tile, True)
    xs_spec, s_shapes, s_specs = group_specs(1, n, tm_sample, s_tile, False)
    assert len(p_shapes) == N_PROJ_OUTS
    outs = pl.pallas_call(
        functools.partial(_proj_in_both_kernel, n_prompt_steps=n_p, wa=wa, wb=wb, d_model=d),
        grid=(n_p + n_s,),
        in_specs=[xp_spec, xs_spec, full(g), pl.BlockSpec(memory_space=pl.ANY),
                  full(bg), full(qg), full(kg), full(gm)],
        out_specs=p_specs + s_specs,
        out_shape=p_shapes + s_shapes,
        scratch_shapes=[pltpu.VMEM(w_in.shape, w_in.dtype),
                        pltpu.SemaphoreType.DMA((len(W_IN_USE_ORDER),))],
        compiler_params=pltpu.CompilerParams(
            dimension_semantics=("arbitrary",), vmem_limit_bytes=VMEM_LIMIT),
        name="proj_in",
    )(x_prompt, x_sample, g, w_in, bg, qg, kg, gm)
    return outs[:N_PROJ_OUTS], outs[N_PROJ_OUTS:]


def _split_halves(q):
    lane = lax.broadcasted_iota(jnp.int32, q.shape, 1)
    zero = jnp.zeros_like(q)
    return jnp.concatenate([jnp.where(lane < DK_A, q, zero), jnp.where(lane >= DK_A, q, zero)], axis=0)


def _softmax_update(carry, s, v):
    m, acc = carry
    m_new = jnp.maximum(m, jnp.max(s, axis=-1, keepdims=True))
    p = jnp.exp(s - m_new).astype(BF16)
    v1 = jnp.concatenate([v, jnp.ones_like(v)], axis=1)
    return m_new, jnp.exp(m - m_new) * acc + _dot(p, v1)


def _diff_combine(acc, lam, t):
    dv = acc.shape[1] // 2
    o = acc[:, :dv] / acc[:, dv:]
    return o[:t] - lam * o[t:]


def _lambda(lam_refs, lam_init):
    q1, k1, q2, k2 = (r[...] for r in lam_refs)
    inner = lambda a, b: jnp.sum(a * b, axis=-1, keepdims=True)
    return jnp.exp(inner(q1, k1)) - jnp.exp(inner(q2, k2)) + lam_init


def _lane_rms(o):
    return o * lax.rsqrt(jnp.mean(o * o, axis=-1, keepdims=True) + EPS)


def _retention_stage1(q, k, v, r, dmat, xi, zeta, cdecay):
    s = (_nt_dot(q, k) * dmat).astype(BF16)
    cross = _dot(q, r.astype(BF16)) * xi
    kz = (k.astype(F32) * zeta).astype(BF16)
    r_new = cdecay * r + _tn_dot(kz, v)
    return s, cross, r_new


def _retention_stage2(s, cross, v):
    return _dot(s, v) + cross


def _branch_projections(oa_scr, ob_scr, gate_ref, woa_ref, wob_ref, merged_scr, nb, tq):
    d_model = woa_ref.shape[1]
    pa = _dot(oa_scr[...], woa_ref[...]).astype(BF16)
    pb = _dot(ob_scr[...], wob_ref[...]).astype(BF16)
    for bi in range(nb):
        rows = slice(bi * tq, (bi + 1) * tq)
        merged_scr[rows] = gate_ref[bi, :, :d_model] * pa[rows] + gate_ref[bi, :, d_model:] * pb[rows]


def _final_projection(merged_scr, wout_ref, x_ref, y_ref, cols, nb, tq):
    proj = _dot(merged_scr[...], wout_ref[:, cols])
    for bi in range(nb):
        y_ref[bi, :, cols] = x_ref[bi, :, cols] + proj[bi * tq:(bi + 1) * tq]


def _mixer_prompt_kernel(*refs, tq, nb, n_tiles, **params):
    t = pl.program_id(0)

    @pl.when(t < n_tiles)
    def _():
        _mixer_prompt_tile(*refs, tq=tq, nb=nb, **params)

    @pl.when(t == n_tiles)
    def _():
        (x_ref, gate_ref, woa_ref, wob_ref, wout_ref, y_ref, _, _, _,
         oa_scr, ob_scr, merged_scr) = refs[N_MIXER_TILE_INPUTS:N_MIXER_TILE_INPUTS + 12]
        _branch_projections(oa_scr, ob_scr, gate_ref, woa_ref, wob_ref, merged_scr, nb, tq)
        _final_projection(merged_scr, wout_ref, x_ref, y_ref, slice(None), nb, tq)


N_MIXER_TILE_INPUTS = 15


def _mixer_prompt_tile(lq1_ref, lk1_ref, lq2_ref, lk2_ref, qt_ref, k_ref, vt_ref, mix_ref,
                       dbias_ref, kpos_ref, kslope_ref, dmat_ref, xi_ref, zeta_ref, sg_ref,
                       x_ref, gate_ref, woa_ref, wob_ref, wout_ref,
                       y_ref, rout_ref, r_scr, acc_scr, oa_scr, ob_scr, merged_scr, *s_scrs,
                       tq, kblk, nb, nq, slopes, cdecay, lam_init):
    chains = [(bi, j) for bi in range(nb) for j in range(H_A)]
    hps = len(chains)
    nsub = tq // kblk
    t = pl.program_id(0)
    qi = lax.rem(t, nq)
    lam = _lambda((lq1_ref, lk1_ref, lq2_ref, lk2_ref), lam_init)
    lanes = lambda j: slice(j * HEAD_W, (j + 1) * HEAD_W)

    def split_halves_t(qt):
        feat = lax.broadcasted_iota(jnp.int32, qt.shape, 0)
        zero = jnp.zeros_like(qt)
        return jnp.concatenate([jnp.where(feat < DK_A, qt, zero), jnp.where(feat >= DK_A, qt, zero)], axis=1)

    qst = [jnp.concatenate([split_halves_t(qt_ref[bi, lanes(j), :]), kslope_ref[j]], axis=0)
           for bi, j in chains]

    head = lambda c: chains[c][1]

    def keys(c, kb):
        bi, j = chains[c]
        k = k_ref[bi, pl.ds(pl.multiple_of(kb * kblk, kblk), kblk), lanes(j)]
        return jnp.concatenate([k, kpos_ref[...]], axis=1)

    def values1(c, kb):
        bi, j = chains[c]
        vt = vt_ref[bi, lanes(j), pl.ds(pl.multiple_of(kb * kblk, kblk), kblk)]
        return jnp.concatenate([vt, jnp.ones((SUM_ROWS, kblk), BF16)], axis=0)

    def stage_scores(c, kb, slot):
        s = _dot(keys(c, kb), qst[c])
        s_scrs[c][slot] = s
        return jnp.max(s, axis=0, keepdims=True)

    def softmax_step(m, s, s_max, shift):
        m_new = jnp.maximum(m, s_max + shift)
        return m_new, jnp.exp2(m - m_new), jnp.exp2(s - (m_new - shift)).astype(BF16)

    def body(kb, carry):
        slot = lax.rem(kb, 2)
        new = []
        for c in range(hps):
            smax_cur, m = carry[c]
            s_cur = s_scrs[c][slot]
            smax_next = stage_scores(c, kb + 1, 1 - slot)
            shift = ((kb - nsub * qi) * kblk).astype(F32) * slopes[head(c)]
            m, alpha, p = softmax_step(m, s_cur, smax_cur, shift)
            acc_scr[c] = alpha * acc_scr[c] + _dot(values1(c, kb), p)
            new.append((smax_next, m))
        return tuple(new)

    @pl.when(t == 0)
    def _():
        oa_scr[...] = jnp.zeros_like(oa_scr)
        ob_scr[...] = jnp.zeros_like(ob_scr)

    @pl.when(qi == 0)
    def _():
        r_scr[...] = jnp.zeros_like(r_scr)

    acc_scr[...] = jnp.zeros_like(acc_scr)
    init = tuple((stage_scores(c, 0, 0), jnp.full((1, 2 * tq), -jnp.inf, F32)) for c in range(hps))
    _branch_projections(oa_scr, ob_scr, gate_ref, woa_ref, wob_ref, merged_scr, nb, tq)
    n_before = nsub * qi
    carry = lax.fori_loop(0, n_before, body, init)

    def retention_stage1(sub, state):
        rows = slice(sub * kblk, (sub + 1) * kblk)
        return [_retention_stage1(
            _slab(mix_ref, "qb", j, bi)[rows], _slab(mix_ref, "kb", j, bi)[rows], _slab(mix_ref, "vb", j, bi)[rows],
            state[c], dmat_ref[j], xi_ref[j], zeta_ref[j], cdecay[j])
            for c, (bi, j) in enumerate(chains)]

    def retention_stage2(sub, ret):
        rows = slice(sub * kblk, (sub + 1) * kblk)
        for c, (bi, j) in enumerate(chains):
            ob = _retention_stage2(ret[c][0], ret[c][1], _slab(mix_ref, "vb", j, bi)[rows])
            ob = (_lane_rms(ob) * _slab(mix_ref, "zb", j, bi)[rows].astype(F32)).astype(BF16)
            ob_scr[bi * tq + sub * kblk:bi * tq + (sub + 1) * kblk, lanes(j)] = ob

    def cat(pieces):
        pieces = [p for p in pieces if p.shape[1] > 0]
        return pieces[0] if len(pieces) == 1 else jnp.concatenate(pieces, axis=1)

    def diagonal_block(dsub, c, m):
        kb = n_before + dsub
        w = tq - dsub * kblk
        cols = (slice(dsub * kblk, tq), slice(tq + dsub * kblk, 2 * tq))
        if dsub == 0:
            s = s_scrs[c][lax.rem(n_before, 2)]
        else:
            s = _dot(keys(c, kb), cat([qst[c][:, q] for q in cols]))
        dbias = dbias_ref[head(c)]
        s = cat([s[:, :kblk] + dbias, s[:, kblk:w], s[:, w:w + kblk] + dbias, s[:, w + kblk:]])
        shift = (dsub * kblk) * slopes[head(c)]
        m_part, alpha, p = softmax_step(cat([m[:, q] for q in cols]), s,
                                        jnp.max(s, axis=0, keepdims=True), shift)
        pv = _dot(values1(c, kb), p)
        for h, q in enumerate(cols):
            acc_scr[c, :, q] = alpha[:, h * w:(h + 1) * w] * acc_scr[c, :, q] + pv[:, h * w:(h + 1) * w]
        return cat([m[:, :dsub * kblk], m_part[:, :w], m[:, tq:tq + dsub * kblk], m_part[:, w:]])

    final_projection = functools.partial(_final_projection, merged_scr, wout_ref, x_ref, y_ref, nb=nb, tq=tq)

    out_cols = [slice(c, c + MXU_TILE) for c in range(0, x_ref.shape[2], MXU_TILE)]
    every = max(1, (hps * nsub) // len(out_cols))
    state = [r_scr[c] for c in range(hps)]
    ms = [carry[c][1] for c in range(hps)]
    for sub in range(nsub):
        ret = retention_stage1(sub, state)
        state = [r[2] for r in ret]
        for c in range(hps):
            if (sub * hps + c) % every == 0 and out_cols:
                final_projection(out_cols.pop(0))
            ms[c] = diagonal_block(sub, c, ms[c])
        retention_stage2(sub, ret)
    while out_cols:
        final_projection(out_cols.pop(0))
    for c, (bi, j) in enumerate(chains):
        r_scr[c] = state[c]
        rout_ref[bi, j] = state[c]
    for c, (bi, j) in enumerate(chains):
        inv_l = 1.0 / acc_scr[c, HEAD_W:HEAD_W + 1]
        inv_l = jnp.concatenate([inv_l[:, :tq], lam * inv_l[:, tq:]], axis=1)
        ot = acc_scr[c, :HEAD_W] * inv_l
        oat = ot[:, :tq] - ot[:, tq:]
        oat = oat * lax.rsqrt(jnp.mean(oat * oat, axis=0, keepdims=True) + EPS)
        oa = (oat.T * sg_ref[...]) * (1.0 - lam_init)
        oa = (oa * _slab(mix_ref, "za", j, bi)[...].astype(F32)).astype(BF16)
        oa_scr[bi * tq:(bi + 1) * tq, lanes(j)] = oa


def _mixer_prompt(lam_vecs, qat, kab, vat, mix, dbias, kpos, kslope, dmat, xi, zeta, sg, x, gates, woa, wob, wout,
                  *, slopes, cdecay, lam_init):
    b, t, w = kab.shape
    d = x.shape[2]
    tq, kblk, nb = Q_TILE, KEY_BLOCK, SEQS_PER_STEP
    assert t % tq == 0 and tq % kblk == 0 and kblk % CHUNK == 0 and H_A == H_B and w == H_A * HEAD_W
    assert dbias.shape[1:] == (kblk, kblk) and d % (tq // kblk) == 0 and b % nb == 0
    nq = t // tq
    n_tiles = (b // nb) * nq
    cur = lambda s: jnp.minimum(s, n_tiles - 1)
    prev = lambda s: jnp.maximum(s - 1, 0)
    tile = pl.BlockSpec((nb, tq, mix.shape[2]), lambda s: (cur(s) // nq, cur(s) % nq, 0))
    tile_t = pl.BlockSpec((nb, w, tq), lambda s: (cur(s) // nq, 0, cur(s) % nq))
    whole = pl.BlockSpec((nb, t, w), lambda s: (cur(s) // nq, 0, 0))
    whole_t = pl.BlockSpec((nb, w, t), lambda s: (cur(s) // nq, 0, 0))
    rows = lambda a: pl.BlockSpec((nb, tq, a.shape[2]), lambda s: (prev(s) // nq, prev(s) % nq, 0))
    full = lambda a: pl.BlockSpec(a.shape, lambda s: (0,) * a.ndim, pipeline_mode=pl.Buffered(1))
    return pl.pallas_call(
        functools.partial(_mixer_prompt_kernel, tq=tq, kblk=kblk, nb=nb, nq=nq, n_tiles=n_tiles,
                          slopes=slopes, cdecay=cdecay, lam_init=lam_init),
        grid=(n_tiles + 1,),
        in_specs=[full(v) for v in lam_vecs] + [
                  tile_t, whole, whole_t, tile,
                  full(dbias), full(kpos), full(kslope), full(dmat), full(xi), full(zeta), full(sg),
                  rows(x), rows(gates), full(woa), full(wob), full(wout)],
        out_specs=[rows(x),
                   pl.BlockSpec((nb, H_B, DK_B, DV_B), lambda s: (cur(s) // nq, 0, 0, 0))],
        out_shape=[jax.ShapeDtypeStruct((b, t, d), F32),
                   jax.ShapeDtypeStruct((b, H_B, DK_B, DV_B), F32)],
        scratch_shapes=[pltpu.VMEM((nb * H_B, DK_B, DV_B), F32),
                        pltpu.VMEM((nb * H_A, HEAD_W + SUM_ROWS, 2 * tq), F32),
                        pltpu.VMEM((nb * tq, w), BF16), pltpu.VMEM((nb * tq, w), BF16),
                        pltpu.VMEM((nb * tq, d), BF16)]
                       + [pltpu.VMEM((2, kblk, 2 * tq), F32)] * (nb * H_A),
        compiler_params=pltpu.CompilerParams(
            dimension_semantics=("arbitrary",), vmem_limit_bytes=VMEM_LIMIT),
        name="mixer_prompt",
    )(*lam_vecs, qat, kab, vat, mix, dbias, kpos, kslope, dmat, xi, zeta, sg, x, gates, woa, wob, wout)


def _mixer_sample_kernel(lq1_ref, lk1_ref, lq2_ref, lk2_ref,
                         qa_ref, kc_ref, vc_ref, kn_ref, vn_ref, mix_ref, r_ref,
                         abias_ref, dmat_ref, xi_ref, zeta_ref, sg_ref,
                         oa_ref, ob_ref, rout_ref, kbuf, vbuf, sem,
                         *, n, past, sub, slopes, cdecay, lam_init):
    bi = pl.program_id(0)
    nchunk = past // sub
    ahead = SAMPLE_RING - 1
    lanes = lambda j: slice(j * HEAD_W, (j + 1) * HEAD_W)
    slot = lambda c: c % SAMPLE_RING

    def chunk_copies(stream, c):
        return (pltpu.make_async_copy(kc_ref.at[stream, :, pl.ds(c * sub, sub)], kbuf.at[slot(c)],
                                      sem.at[0, slot(c)]),
                pltpu.make_async_copy(vc_ref.at[stream, pl.ds(c * sub * H_A, sub * H_A), :], vbuf.at[slot(c)],
                                      sem.at[1, slot(c)]))

    def start(stream, c):
        for priority, copy in enumerate(chunk_copies(stream, c)):
            copy.start(priority=priority)

    def wait(c):
        for copy in chunk_copies(bi, c):
            copy.wait()

    @pl.when(bi == 0)
    def _():
        for c in range(ahead):
            start(0, c)

    col = lax.broadcasted_iota(jnp.int32, (1, sub), 1)
    qs = [_split_halves(qa_ref[0, :, lanes(j)]) for j in range(H_A)]

    def scores(c):
        kpos = (col + (c * sub - past)).astype(F32)
        return [_dot(qs[j], kbuf[slot(c), lanes(j), :].astype(BF16)) + kpos * slopes[j] for j in range(H_A)]

    carry = [(jnp.full((2 * n, 1), -jnp.inf, F32), jnp.zeros((2 * n, 2 * DV_A), F32)) for _ in range(H_A)]
    wait(0)
    staged = scores(0)
    for c in range(nchunk):
        if c + ahead < nchunk:
            start(bi, c + ahead)
        else:
            @pl.when(bi + 1 < pl.num_programs(0))
            def _():
                start(bi + 1, c + ahead - nchunk)
        staged, current = None, staged
        if c + 1 < nchunk:
            wait(c + 1)
            staged = scores(c + 1)
        for j in range(H_A):
            vv = vbuf[slot(c), pl.ds(j, sub, stride=H_A), :].astype(BF16)
            carry[j] = _softmax_update(carry[j], current[j], vv)

    lam = _lambda((lq1_ref, lk1_ref, lq2_ref, lk2_ref), lam_init)
    s_new, ret = [], []
    for j in range(H_A):
        bias = abias_ref[j]
        s_new.append(_nt_dot(qs[j], kn_ref[0, :, lanes(j)]) + jnp.concatenate([bias, bias], axis=0))
    for j in range(H_A):
        s_ret, cross, r_new = _retention_stage1(
            _slab(mix_ref, "qb", j)[...], _slab(mix_ref, "kb", j)[...], _slab(mix_ref, "vb", j)[...],
            r_ref[0, j], dmat_ref[j], xi_ref[j], zeta_ref[j], cdecay[j])
        rout_ref[0, j] = r_new
        ret.append((s_ret, cross))
    for j in range(H_A):
        _, acc = _softmax_update(carry[j], s_new[j], vn_ref[0, :, lanes(j)])
        oa = _diff_combine(acc, lam, n)
        oa = (_lane_rms(oa) * sg_ref[...]) * (1.0 - lam_init)
        oa_ref[0, :, lanes(j)] = (oa * _slab(mix_ref, "za", j)[...].astype(F32)).astype(BF16)
    for j in range(H_A):
        ob = _retention_stage2(*ret[j], _slab(mix_ref, "vb", j)[...])
        ob_ref[0, :, lanes(j)] = (_lane_rms(ob) * _slab(mix_ref, "zb", j)[...].astype(F32)).astype(BF16)


def _mixer_sample(lam_vecs, qa, kc, vc, kab, vab, mix, r0, abias, dmat, xi, zeta, sg, *, slopes, cdecay, lam_init):
    b, n, w = qa.shape
    past = kc.shape[2]
    sub = SAMPLE_KEY_BLOCK
    assert past % (sub * SAMPLE_RING) == 0 and past % CHUNK == 0 and n <= CHUNK and H_A == H_B
    tile = pl.BlockSpec((1, n, w), lambda bi: (bi, 0, 0))
    mix_tile = pl.BlockSpec((1, n, mix.shape[2]), lambda bi: (bi, 0, 0))
    in_hbm = pl.BlockSpec(memory_space=pl.ANY)
    state = pl.BlockSpec((1, H_B, DK_B, DV_B), lambda bi: (bi, 0, 0, 0))
    full = lambda a: pl.BlockSpec(a.shape, lambda bi: (0,) * a.ndim)
    return pl.pallas_call(
        functools.partial(_mixer_sample_kernel, n=n, past=past, sub=sub,
                          slopes=slopes, cdecay=cdecay, lam_init=lam_init),
        grid=(b,),
        in_specs=[full(v) for v in lam_vecs] + [
                  tile, in_hbm, in_hbm, tile, tile, mix_tile, state,
                  full(abias), full(dmat), full(xi), full(zeta), full(sg)],
        out_specs=[tile, tile, state],
        out_shape=[jax.ShapeDtypeStruct((b, n, H_A * HEAD_W), BF16),
                   jax.ShapeDtypeStruct((b, n, H_B * HEAD_W), BF16),
                   jax.ShapeDtypeStruct((b, H_B, DK_B, DV_B), F32)],
        scratch_shapes=[pltpu.VMEM((SAMPLE_RING, w, sub), F32),
                        pltpu.VMEM((SAMPLE_RING, sub * H_A, HEAD_W), F32),
                        pltpu.SemaphoreType.DMA((2, SAMPLE_RING))],
        compiler_params=pltpu.CompilerParams(
            dimension_semantics=("arbitrary",), vmem_limit_bytes=VMEM_LIMIT),
        name="mixer_sample",
    )(*lam_vecs, qa, kc, vc, kab, vab, mix, r0, abias, dmat, xi, zeta, sg)


def _proj_out_kernel(x_ref, oa_ref, ob_ref, gate_ref, woa_ref, wob_ref, wout_ref, y_ref):
    d = x_ref.shape[1]
    m = (gate_ref[:, :d].astype(F32) * _dot(oa_ref[...], woa_ref[...])
         + gate_ref[:, d:].astype(F32) * _dot(ob_ref[...], wob_ref[...]))
    y_ref[...] = x_ref[...] + _dot(m.astype(BF16), wout_ref[...])


def _proj_out(x2, oa, ob, gates, woa, wob, wout):
    n, d = x2.shape
    tm = min(ROW_TILE, n)
    assert n % tm == 0
    row = lambda w: pl.BlockSpec((tm, w), lambda i: (i, 0))
    full = lambda a: pl.BlockSpec(a.shape, lambda i: (0,) * a.ndim)
    return pl.pallas_call(
        _proj_out_kernel,
        grid=(n // tm,),
        in_specs=[row(d), row(oa.shape[1]), row(ob.shape[1]), row(gates.shape[1]),
                  full(woa), full(wob), full(wout)],
        out_specs=row(d),
        out_shape=jax.ShapeDtypeStruct((n, d), F32),
        compiler_params=pltpu.CompilerParams(
            dimension_semantics=("arbitrary",), vmem_limit_bytes=VMEM_LIMIT),
        name="proj_out",
    )(x2, oa, ob, gates, woa, wob, wout)


def _block_tables(t, slopes, log_g):
    f32 = np.float32
    i = np.arange(t, dtype=np.int32)
    allowed = (i[None, :] // CHUNK) <= (i[:, None] // CHUNK)
    dist = np.abs(i[:, None] - i[None, :]).astype(f32)
    jf = i.astype(f32)
    abias = -slopes[:, None, None] * dist[None] + slopes[:, None, None] * jf[None, :, None]
    abias = np.where(allowed[None], abias, f32(-np.inf)).astype(f32)
    dmat = np.where(allowed[None], np.exp(dist[None] * log_g[:, None, None]), f32(0.0)).astype(f32)
    xi = np.exp((jf + f32(1.0))[None] * log_g[:, None]).astype(f32)
    zeta = np.exp((f32(t) - f32(1.0) - jf)[None] * log_g[:, None]).astype(f32)
    bc = lambda a: np.ascontiguousarray(np.broadcast_to(a[:, :, None], a.shape + (HEAD_W,)))
    return abias, dmat, bc(xi), bc(zeta)


N_SLOPE_PARTS = 3


def _alibi_features(t, width, slopes):
    assert t <= 256
    bf16 = jnp.bfloat16
    kpos = np.zeros((t, HEAD_W), np.float32)
    kpos[:, :N_SLOPE_PARTS] = np.arange(t, dtype=np.float32)[:, None]
    kslope = np.zeros((len(slopes), HEAD_W, width), np.float32)
    rest = slopes.astype(np.float32)
    for part in range(N_SLOPE_PARTS):
        piece = rest.astype(bf16).astype(np.float32)
        kslope[:, part, :] = piece[:, None]
        rest = rest - piece
    return kpos.astype(bf16), kslope.astype(bf16)


def _key_bias_table(t, slopes):
    kb = (slopes[:, None] * np.arange(t, dtype=np.float32)[None]).astype(np.float32)
    return np.ascontiguousarray(np.broadcast_to(kb[:, :, None], kb.shape + (HEAD_W,)))


def _layer(x_prompt, x_sample, ck, cv, r0, norm_g, w_in, b_gate, qn_g, kn_g,
           lam_q1, lam_k1, lam_q2, lam_k2, subln_g, w_oa, w_ob, w_out, lam_init):
    bp, tp, d = x_prompt.shape
    bs, ts, _ = x_sample.shape
    past = ck.shape[1]
    slopes = (2.0 ** (-8.0 * np.arange(1, H_A + 1, dtype=np.float32) / H_A)).astype(np.float32)
    log_g = np.log(1.0 - 2.0 ** (-5.0 - np.arange(H_B, dtype=np.float32))).astype(np.float32)
    row = lambda v: v.reshape(1, -1)
    lam_vecs = (row(lam_q1), row(lam_k1), row(lam_q2), row(lam_k2))
    floats = lambda a: tuple(float(v) for v in np.asarray(a, np.float32))
    cdecay = lambda n_block: floats(np.exp(np.float32(n_block) * log_g))

    grp = np.arange(MXU_TILE, dtype=np.int32) // DK_A
    gm = jnp.asarray((grp[:, None] == grp[None, :]) * (1.0 / DK_A), dtype=BF16)
    sg = row(subln_g)

    woa_bf, wob_bf, wout_bf = w_oa.astype(BF16), w_ob.astype(BF16), w_out.astype(BF16)
    prompt_acts, sample_acts = _proj_in(
        x_prompt, x_sample.reshape(1, bs * ts, d), row(norm_g), w_in, b_gate, row(qn_g), row(kn_g), gm,
        tm=ROW_TILE, tm_sample=min(SAMPLE_ROW_TILE, bs * ts))

    qat, kaf, kab, vaf, vat, mix, gates = prompt_acts
    abias, dmat, xi, zeta = _block_tables(KEY_BLOCK, slopes, log_g)
    kbias = _key_bias_table(KEY_BLOCK, slopes)
    dbias = np.ascontiguousarray(np.swapaxes(abias, 1, 2) - kbias[:, :, :1])
    flat = lambda a: a.reshape(-1, a.shape[-1])
    log2e = np.float32(LOG2E)
    kpos, kslope = _alibi_features(KEY_BLOCK, 2 * Q_TILE, slopes * log2e)
    y_p, ret_p = _mixer_prompt(lam_vecs, qat, kab, vat, mix,
                               dbias * log2e, kpos, kslope, dmat, xi, zeta, sg,
                               x_prompt, gates, woa_bf, wob_bf, wout_bf,
                               slopes=floats(slopes * log2e), cdecay=cdecay(KEY_BLOCK), lam_init=lam_init)
    k_p = jnp.transpose(kaf.reshape(bp, H_A, 2, DK_A, tp), (0, 4, 1, 2, 3))
    v_p = vaf.reshape(bp, tp, H_A, DV_A)

    qa, kaf, kab, vaf, vab, mix, gates = sample_acts
    r3 = lambda a: a.reshape(bs, ts, a.shape[-1])
    tabs = _block_tables(ts, slopes, log_g)
    ck_t = jnp.transpose(ck, (0, 2, 3, 4, 1)).reshape(bs, H_A * HEAD_W, past)
    oa, ob, ret_s = _mixer_sample(lam_vecs, r3(qa), ck_t, cv.reshape(bs, past * H_A, HEAD_W),
                                  r3(kab), r3(vab), r3(mix), r0, *tabs, sg,
                                  slopes=floats(slopes), cdecay=cdecay(ts), lam_init=lam_init)
    y_s = _proj_out(flat(x_sample), flat(oa), flat(ob), flat(gates),
                    woa_bf, wob_bf, wout_bf).reshape(bs, ts, d)
    k_s = kaf.reshape(bs, ts, H_A, 2, DK_A)
    v_s = vaf.reshape(bs, ts, H_A, DV_A)
    return y_p, y_s, k_p, v_p, ret_p, k_s, v_s, ret_s


def kernel(x_prompt, x_sample, cache_k_diff, cache_v_diff, state_ret, norm_g, w_in, b_gate, qn_g, kn_g,
           lam_q1, lam_k1, lam_q2, lam_k2, subln_g, w_o_diff, w_o_ret, w_out):
    depth = w_in.shape[0]
    hp, hs = x_prompt, x_sample
    outs = [[] for _ in range(6)]
    for l in range(depth):
        lam_init = 0.8 - 0.6 * math.exp(-0.3 * l)
        hp, hs, k_p, v_p, ret_p, k_s, v_s, ret_s = _layer(
            hp, hs, cache_k_diff[l], cache_v_diff[l], state_ret[l], norm_g[l], w_in[l], b_gate[l],
            qn_g[l], kn_g[l], lam_q1[l], lam_k1[l], lam_q2[l], lam_k2[l], subln_g[l],
            w_o_diff[l], w_o_ret[l], w_out[l], lam_init)
        for lst, a in zip(outs, (k_p, v_p, ret_p, k_s, v_s, ret_s)):
            lst.append(a)
    return (hp, hs) + tuple(jnp.stack(o) for o in outs)
```
